```python
import math
import jax
import jax.numpy as jnp
from jax import lax
import numpy as np

D_MODEL = 1024
BATCH = 32
SEQ = 256
DEPTH = 2
DEC_BATCH = 2
DEC_SEQ = 4096
PAST_LEN = 256

F32 = jnp.float32
GRID_W = 64
D_MIX = 512
A_HEADS = 8
A_HD = 64
A_LORA_W = 64
A_LORA_A = 64
A_LORA_G = 128
B_HEADS = 8
B_HD = 64
WIN_R = 8
WIN_C = 16
C_GROUP = 16
C_GROUPS = D_MIX // C_GROUP
C_STATE = 64
N_EXPERTS = 64
TOP_K = 8
N_GROUPS = 8
TOPK_GROUPS = 4
D_EXPERT = 128
D_SHARED = 128
ROUTED_SCALE = 2.5
LN_EPS = 1e-5
GN_EPS = 64e-5
NEG = -1e30
Q_BLOCK = 128
DN_ALPHA = (2 * DEPTH) ** 0.25
DN_BETA = (8 * DEPTH) ** -0.25
A_COLS = 3 * D_MIX + 2 * A_LORA_W + 2 * A_LORA_A + A_LORA_G
IN_COLS = A_COLS + 3 * D_MIX + D_MIX + 3 * D_MODEL

kernel_name = 'hybrid_rwkv7_natten_s5_moe_diffusion_step'


def split_cols(z, sizes):
    return jnp.split(z, [int(i) for i in np.cumsum(sizes)[:-1]], axis=-1)


def layer_norm(x, g, b):
    xf = x.astype(F32)
    mu = jnp.mean(xf, axis=-1, keepdims=True)
    var = jnp.mean(jnp.square(xf - mu), axis=-1, keepdims=True)
    return ((xf - mu) * lax.rsqrt(var + LN_EPS) * g + b).astype(x.dtype)


def modulation(cond, w_ada, b_ada):
    m = jax.nn.silu(cond) @ w_ada + b_ada
    return jnp.split(m[:, None, :], 6, axis=-1)


def token_shift_mix(z, mu):
    zp = jnp.pad(z, ((0, 0), (1, 1), (0, 0)))
    return z + mu * (0.5 * (zp[:, :-2] + zp[:, 2:]) - z)


def mixer_inputs(h, p):
    z = h @ p['w_in']
    za, zq, zk, zv, zu, zg = split_cols(z, [A_COLS, D_MIX, D_MIX, D_MIX, D_MIX, 3 * D_MODEL])
    return token_shift_mix(za, p['rwkv_mu']), zq, zk, zv, zu, zg


def rwkv_prep(za, p):
    bsz, t = za.shape[:2]
    r, k, v, lw, la, lg = split_cols(za.astype(F32), [D_MIX, D_MIX, D_MIX, 2 * A_LORA_W, 2 * A_LORA_A, A_LORA_G])
    lw = lw.reshape(bsz, t, 2, A_LORA_W)
    la = la.reshape(bsz, t, 2, A_LORA_A)
    w = p['rwkv_w0'] + jnp.einsum('btnr,nrc->btnc', jnp.tanh(lw), p['rwkv_w2'])
    decay = jnp.exp(-jnp.exp(-jax.nn.softplus(-w) - 0.5))
    a = jax.nn.sigmoid(p['rwkv_a0'] + jnp.einsum('btnr,nrc->btnc', la, p['rwkv_a2']))
    g = jax.nn.sigmoid(lg) @ p['rwkv_g2']
    kk = (k * p['rwkv_kk']).reshape(bsz, t, A_HEADS, A_HD)
    kk = kk / jnp.maximum(jnp.linalg.norm(kk, axis=-1, keepdims=True), 1e-12)
    k_dir = k[:, :, None, :] * (1.0 + (a - 1.0) * p['rwkv_ka'])

    def to_heads(y):
        return y.reshape(y.shape[:-1] + (A_HEADS, A_HD))
    return (to_heads(r), to_heads(decay), to_heads(k_dir), to_heads(v),
            kk, kk[:, :, None] * to_heads(a), g)


def rwkv_scan(r, w, k, v, kk, b, s0, reverse):
    def step(s, inp):
        r_t, w_t, k_t, v_t, kk_t, b_t = inp
        sa = jnp.einsum('bhvk,bhk->bhv', s, kk_t)
        s = s * w_t[:, :, None, :] - sa[..., None] * b_t[:, :, None, :] + v_t[..., None] * k_t[:, :, None, :]
        return s, jnp.einsum('bhvk,bhk->bhv', s, r_t)
    xs = tuple(jnp.moveaxis(y, 1, 0) for y in (r, w, k, v, kk, b))
    s_fin, y = lax.scan(step, s0, xs, reverse=reverse)
    return jnp.moveaxis(y, 0, 1), s_fin


def rwkv_branch(za, p, s0):
    bsz, t = za.shape[:2]
    r, decay, k_dir, v, kk, b_dir, g = rwkv_prep(za, p)
    s0 = s0.astype(F32)
    y_f, s_f = rwkv_scan(r, decay[:, :, 0], k_dir[:, :, 0], v, kk, b_dir[:, :, 0], s0[:, 0], False)
    y_b, s_b = rwkv_scan(r, decay[:, :, 1], k_dir[:, :, 1], v, kk, b_dir[:, :, 1], s0[:, 1], True)
    y = jnp.stack([y_f, y_b], axis=2)
    mu = jnp.mean(y, axis=-1, keepdims=True)
    var = jnp.mean(jnp.square(y - mu), axis=-1, keepdims=True)
    gn = (y - mu) * lax.rsqrt(var + GN_EPS) * p['rwkv_gn_g'].reshape(A_HEADS, A_HD) + p['rwkv_gn_b'].reshape(A_HEADS, A_HD)
    bonus = jnp.sum(r[:, :, None] * k_dir * p['rwkv_rk'].reshape(A_HEADS, A_HD), axis=-1, keepdims=True) * v[:, :, None]
    out = jnp.sum(gn + bonus, axis=2).reshape(bsz, t, D_MIX) * g
    return out, jnp.stack([s_f, s_b], axis=1)


def linear_scan(a, b):
    a_seq = jnp.broadcast_to(a, b.shape)

    def combine(e1, e2):
        return e1[0] * e2[0], e2[0] * e1[1] + e2[1]
    return lax.associative_scan(combine, (a_seq, b), axis=1)[1]


def s5_branch(u, p, x0):
    bsz, t = u.shape[:2]
    a = lax.complex(p['s5_a_re'].astype(F32), p['s5_a_im'].astype(F32))
    dt = jnp.exp(p['s5_log_dt'].astype(F32))[..., None]
    a_bar = jnp.exp(dt * a)
    b_mat = lax.complex(p['s5_b_re'].astype(F32), p['s5_b_im'].astype(F32))
    b_bar = ((a_bar - 1.0) / a)[..., None] * b_mat
    c_mat = lax.complex(p['s5_c_re'].astype(F32), p['s5_c_im'].astype(F32))
    uf = u.astype(F32)
    ug = uf.reshape(bsz, t, C_GROUPS, C_GROUP).astype(jnp.complex64)
    bu = jnp.einsum('btgh,ngph->nbtgp', ug, b_bar)
    bu_f = bu[0].at[:, 0].add(a_bar[0] * x0[:, 0])
    bu_b = bu[1].at[:, -1].add(a_bar[1] * x0[:, 1])
    x_f = linear_scan(a_bar[0], bu_f)
    x_b = jnp.flip(linear_scan(a_bar[1], jnp.flip(bu_b, axis=1)), axis=1)
    y = jnp.einsum('btgp,ghp->btgh', x_f + x_b, c_mat).real.reshape(bsz, t, D_MIX) + p['s5_d'] * uf
    y = jax.nn.gelu(y)
    y = y * jax.nn.sigmoid(y @ p['s5_w_glu'] + p['s5_b_glu'])
    return y, jnp.stack([x_f[:, -1], x_b[:, 0]], axis=1)


def attn_heads(z):
    bsz, t = z.shape[:2]
    return z.reshape(bsz, t, B_HEADS, B_HD).transpose(0, 2, 1, 3)


def merge_heads(y):
    bsz, nh, t, hd = y.shape
    return y.transpose(0, 2, 1, 3).reshape(bsz, t, nh * hd)


def context_attention(q, k, v):
    bsz, nh, n, hd = q.shape
    qb = q.reshape(bsz, nh, n // Q_BLOCK, Q_BLOCK, hd).transpose(2, 0, 1, 3, 4)

    def block(qi):
        s = jnp.einsum('bhqd,bhkd->bhqk', qi, k).astype(F32) * hd ** -0.5
        return jnp.einsum('bhqk,bhkd->bhqd', jax.nn.softmax(s, axis=-1).astype(v.dtype), v)
    out = lax.map(block, qb)
    return out.transpose(1, 2, 0, 3, 4).reshape(bsz, nh, n, hd)


def na_latent(zq, zk, zv, k_ctx, v_ctx, rpb):
    bsz, t = zq.shape[:2]
    rows = t // GRID_W
    wr = min(WIN_R, rows)

    def to_grid(y):
        return y.reshape(bsz, rows, GRID_W, B_HEADS, B_HD).transpose(0, 3, 1, 2, 4)
    qg, kg, vg = to_grid(zq), to_grid(zk), to_grid(zv)
    cq = np.arange(GRID_W)[:, None]
    ck = np.arange(GRID_W)[None, :]
    cs = np.clip(cq - WIN_C // 2, 0, GRID_W - WIN_C)
    col_bias = np.where((ck >= cs) & (ck < cs + WIN_C), 0.0, NEG).astype(np.float32)
    col_idx = np.clip(ck - cq + WIN_C - 1, 0, 2 * WIN_C - 2)
    rpb_col = rpb.astype(F32)[:, :, col_idx]
    scale = B_HD ** -0.5
    n_loc = wr * GRID_W

    def row_block(r):
        rs = jnp.clip(r - wr // 2, 0, rows - wr)
        kb = lax.dynamic_slice_in_dim(kg, rs, wr, axis=2)
        vb = lax.dynamic_slice_in_dim(vg, rs, wr, axis=2)
        qr = lax.dynamic_index_in_dim(qg, r, axis=2, keepdims=False)
        row_idx = rs + jnp.arange(wr) - r + WIN_R - 1
        bias = jnp.take(rpb_col, row_idx, axis=1).transpose(0, 2, 1, 3) + col_bias[:, None, :]
        s_loc = jnp.einsum('bhqd,bhjkd->bhqjk', qr, kb).astype(F32) * scale + bias
        s_ctx = jnp.einsum('bhqd,bhld->bhql', qr, k_ctx).astype(F32) * scale
        prob = jax.nn.softmax(jnp.concatenate([s_loc.reshape(bsz, B_HEADS, GRID_W, n_loc), s_ctx], axis=-1), axis=-1)
        p_loc = prob[..., :n_loc].reshape(bsz, B_HEADS, GRID_W, wr, GRID_W).astype(vb.dtype)
        p_ctx = prob[..., n_loc:].astype(v_ctx.dtype)
        return (jnp.einsum('bhqjk,bhjkd->bhqd', p_loc, vb)
                + jnp.einsum('bhql,bhld->bhqd', p_ctx, v_ctx))
    out = lax.map(row_block, jnp.arange(rows))
    return out.transpose(1, 0, 3, 2, 4).reshape(bsz, t, B_HEADS * B_HD)


def merge_branches(y_a, y_b, y_c, zg, p):
    y = jnp.stack([y_a, y_b, y_c], axis=2)
    gates = jax.nn.sigmoid(zg.astype(F32)).reshape(zg.shape[:2] + (3, D_MODEL))
    merged = jnp.sum(jnp.einsum('btnc,ncd->btnd', y, p['w_branch']) * gates, axis=2)
    return merged @ p['w_out']


def moe_ffn(h, p):
    shape = h.shape
    x = h.reshape(-1, D_MODEL)
    n = x.shape[0]
    scores = jax.nn.sigmoid((x @ p['router_w']).astype(F32))
    sel = scores + p['router_bias'].astype(F32)
    grp_score = jnp.sum(lax.top_k(sel.reshape(n, N_GROUPS, N_EXPERTS // N_GROUPS), 2)[0], axis=-1)
    _, gidx = lax.top_k(grp_score, TOPK_GROUPS)
    gmask = jnp.sum(jax.nn.one_hot(gidx, N_GROUPS, dtype=F32), axis=1)
    sel = jnp.where(jnp.repeat(gmask, N_EXPERTS // N_GROUPS, axis=1) > 0, sel, NEG)
    _, eidx = lax.top_k(sel, TOP_K)
    w = jnp.take_along_axis(scores, eidx, axis=1)
    w = ROUTED_SCALE * w / jnp.sum(w, axis=-1, keepdims=True)
    gates = jnp.sum(jax.nn.one_hot(eidx, N_EXPERTS, dtype=F32) * w[..., None], axis=1)
    hg = jnp.einsum('nd,edf->nef', x, p['exp_w_gate'])
    hu = jnp.einsum('nd,edf->nef', x, p['exp_w_up'])
    y = jnp.einsum('nef,efd->nd', jax.nn.silu(hg) * hu * gates[..., None].astype(hg.dtype), p['exp_w_down'])
    y_sh = (jax.nn.silu(x @ p['sh_w_gate']) * (x @ p['sh_w_up'])) @ p['sh_w_down']
    return (y + y_sh).reshape(shape)


def context_layer(x, mod, p):
    shift1, scale1, gate1, shift2, scale2, gate2 = mod
    bsz = x.shape[0]
    h = x * (1.0 + scale1) + shift1
    za, zq, zk, zv, zu, zg = mixer_inputs(h, p)
    y_a, s_rwkv = rwkv_branch(za, p, jnp.zeros((bsz, 2, A_HEADS, A_HD, A_HD), F32))
    k_h, v_h = attn_heads(zk), attn_heads(zv)
    y_b = merge_heads(context_attention(attn_heads(zq), k_h, v_h))
    y_c, s_s5 = s5_branch(zu, p, jnp.zeros((bsz, 2, C_GROUPS, C_STATE), jnp.complex64))
    x = layer_norm(DN_ALPHA * x + gate1 * merge_branches(y_a, y_b, y_c, zg, p), p['ln1_g'], p['ln1_b'])
    h = x * (1.0 + scale2) + shift2
    x = layer_norm(DN_ALPHA * x + gate2 * moe_ffn(h, p), p['ln2_g'], p['ln2_b'])
    return x, k_h, v_h, s_rwkv, s_s5


def latent_layer(x, mod, p, k_ctx, v_ctx, s_rwkv, s5_x0):
    shift1, scale1, gate1, shift2, scale2, gate2 = mod
    h = x * (1.0 + scale1) + shift1
    za, zq, zk, zv, zu, zg = mixer_inputs(h, p)
    y_a, _ = rwkv_branch(za, p, s_rwkv)
    y_b = na_latent(zq, zk, zv, k_ctx, v_ctx, p['na_rpb'])
    y_c, _ = s5_branch(zu, p, s5_x0)
    x = layer_norm(DN_ALPHA * x + gate1 * merge_branches(y_a, y_b, y_c, zg, p), p['ln1_g'], p['ln1_b'])
    h = x * (1.0 + scale2) + shift2
    x = layer_norm(DN_ALPHA * x + gate2 * moe_ffn(h, p), p['ln2_g'], p['ln2_b'])
    return x


def setup_inputs(seed: int = 0) -> dict:
    key = jax.random.key(seed)
    ks = iter(jax.random.split(key, 64))

    def nrm(shape, scale):
        return scale * jax.random.normal(next(ks), shape, F32)

    def unif(shape, lo, hi):
        return jax.random.uniform(next(ks), shape, F32, lo, hi)
    L = DEPTH
    col_scale = np.ones((IN_COLS,), np.float32)
    col_scale[2 * D_MIX:3 * D_MIX] = DN_BETA
    col_scale[A_COLS + 2 * D_MIX:A_COLS + 3 * D_MIX] = DN_BETA
    a_im0 = np.pi * np.arange(C_STATE, dtype=np.float32)
    return {
        'x_prompt': nrm((BATCH, SEQ, D_MODEL), 1.0),
        'x_sample': nrm((DEC_BATCH, DEC_SEQ, D_MODEL), 1.0),
        'c': nrm((DEC_BATCH, D_MODEL), 1.0),
        'cache_na_k': nrm((DEC_BATCH, DEPTH, B_HEADS, PAST_LEN, B_HD), 1.0),
        'cache_na_v': nrm((DEC_BATCH, DEPTH, B_HEADS, PAST_LEN, B_HD), 1.0),
        'state_rwkv': nrm((DEC_BATCH, DEPTH, 2, A_HEADS, A_HD, A_HD), 0.5),
        'state_s5_re': nrm((DEC_BATCH, DEPTH, 2, C_GROUPS, C_STATE), 0.5),
        'state_s5_im': nrm((DEC_BATCH, DEPTH, 2, C_GROUPS, C_STATE), 0.5),
        'c_ctx': nrm((D_MODEL,), 1.0),
        'w_ada': nrm((L, D_MODEL, 6 * D_MODEL), D_MODEL ** -0.5),
        'b_ada': nrm((L, 6 * D_MODEL), 0.1),
        'w_in': nrm((L, D_MODEL, IN_COLS), D_MODEL ** -0.5) * col_scale,
        'rwkv_mu': unif((L, A_COLS), 0.0, 1.0),
        'rwkv_w0': unif((L, 2, D_MIX), -5.0, 1.0),
        'rwkv_w2': nrm((L, 2, A_LORA_W, D_MIX), 0.1 * A_LORA_W ** -0.5),
        'rwkv_a0': nrm((L, 2, D_MIX), 0.5),
        'rwkv_a2': nrm((L, 2, A_LORA_A, D_MIX), A_LORA_A ** -0.5),
        'rwkv_g2': nrm((L, A_LORA_G, D_MIX), A_LORA_G ** -0.5),
        'rwkv_kk': 0.85 + nrm((L, D_MIX), 0.05),
        'rwkv_ka': 1.0 + nrm((L, D_MIX), 0.05),
        'rwkv_rk': nrm((L, D_MIX), 0.1),
        'rwkv_gn_g': 1.0 + nrm((L, D_MIX), 0.05),
        'rwkv_gn_b': nrm((L, D_MIX), 0.02),
        'na_rpb': nrm((L, B_HEADS, 2 * WIN_R - 1, 2 * WIN_C - 1), 0.1),
        's5_a_re': -0.5 + nrm((L, 2, C_GROUPS, C_STATE), 0.01),
        's5_a_im': a_im0 + nrm((L, 2, C_GROUPS, C_STATE), 0.01),
        's5_log_dt': unif((L, 2, C_GROUPS), math.log(1e-3), math.log(1e-1)),
        's5_b_re': nrm((L, C_GROUPS, C_STATE, C_GROUP), (2 * C_GROUP) ** -0.5),
        's5_b_im': nrm((L, C_GROUPS, C_STATE, C_GROUP), (2 * C_GROUP) ** -0.5),
        's5_c_re': nrm((L, C_GROUPS, C_GROUP, C_STATE), (2 * C_STATE) ** -0.5),
        's5_c_im': nrm((L, C_GROUPS, C_GROUP, C_STATE), (2 * C_STATE) ** -0.5),
        's5_d': nrm((L, D_MIX), 1.0),
        's5_w_glu': nrm((L, D_MIX, D_MIX), D_MIX ** -0.5),
        's5_b_glu': nrm((L, D_MIX), 0.02),
        'w_branch': nrm((L, 3, D_MIX, D_MODEL), D_MIX ** -0.5),
        'w_out': nrm((L, D_MODEL, D_MODEL), DN_BETA * D_MODEL ** -0.5),
        'ln1_g': 1.0 + nrm((L, D_MODEL), 0.05),
        'ln1_b': nrm((L, D_MODEL), 0.02),
        'router_w': nrm((L, D_MODEL, N_EXPERTS), D_MODEL ** -0.5),
        'router_bias': nrm((L, N_EXPERTS), 0.01),
        'exp_w_gate': nrm((L, N_EXPERTS, D_MODEL, D_EXPERT), D_MODEL ** -0.5),
        'exp_w_up': nrm((L, N_EXPERTS, D_MODEL, D_EXPERT), D_MODEL ** -0.5),
        'exp_w_down': nrm((L, N_EXPERTS, D_EXPERT, D_MODEL), DN_BETA * D_EXPERT ** -0.5),
        'sh_w_gate': nrm((L, D_MODEL, D_SHARED), D_MODEL ** -0.5),
        'sh_w_up': nrm((L, D_MODEL, D_SHARED), D_MODEL ** -0.5),
        'sh_w_down': nrm((L, D_SHARED, D_MODEL), DN_BETA * D_SHARED ** -0.5),
        'ln2_g': 1.0 + nrm((L, D_MODEL), 0.05),
        'ln2_b': nrm((L, D_MODEL), 0.02),
    }


def reference(x_prompt, x_sample, c, cache_na_k, cache_na_v, state_rwkv, state_s5_re, state_s5_im,
              c_ctx, w_ada, b_ada, w_in, rwkv_mu, rwkv_w0, rwkv_w2, rwkv_a0, rwkv_a2, rwkv_g2,
              rwkv_kk, rwkv_ka, rwkv_rk, rwkv_gn_g, rwkv_gn_b, na_rpb, s5_a_re, s5_a_im, s5_log_dt,
              s5_b_re, s5_b_im, s5_c_re, s5_c_im, s5_d, s5_w_glu, s5_b_glu, w_branch, w_out,
              ln1_g, ln1_b, router_w, router_bias, exp_w_gate, exp_w_up, exp_w_down,
              sh_w_gate, sh_w_up, sh_w_down, ln2_g, ln2_b):
    stacked = {
        'w_ada': w_ada, 'b_ada': b_ada, 'w_in': w_in, 'rwkv_mu': rwkv_mu, 'rwkv_w0': rwkv_w0,
        'rwkv_w2': rwkv_w2, 'rwkv_a0': rwkv_a0, 'rwkv_a2': rwkv_a2, 'rwkv_g2': rwkv_g2,
        'rwkv_kk': rwkv_kk, 'rwkv_ka': rwkv_ka, 'rwkv_rk': rwkv_rk, 'rwkv_gn_g': rwkv_gn_g,
        'rwkv_gn_b': rwkv_gn_b, 'na_rpb': na_rpb, 's5_a_re': s5_a_re, 's5_a_im': s5_a_im,
        's5_log_dt': s5_log_dt, 's5_b_re': s5_b_re, 's5_b_im': s5_b_im, 's5_c_re': s5_c_re,
        's5_c_im': s5_c_im, 's5_d': s5_d, 's5_w_glu': s5_w_glu, 's5_b_glu': s5_b_glu,
        'w_branch': w_branch, 'w_out': w_out, 'ln1_g': ln1_g, 'ln1_b': ln1_b,
        'router_w': router_w, 'router_bias': router_bias, 'exp_w_gate': exp_w_gate,
        'exp_w_up': exp_w_up, 'exp_w_down': exp_w_down, 'sh_w_gate': sh_w_gate,
        'sh_w_up': sh_w_up, 'sh_w_down': sh_w_down, 'ln2_g': ln2_g, 'ln2_b': ln2_b,
    }
    xp, xs = x_prompt, x_sample
    new_k, new_v, new_rwkv, new_s5 = [], [], [], []
    for l in range(DEPTH):
        p = {name: arr[l] for name, arr in stacked.items()}
        xp, k_h, v_h, s_rwkv, s_s5 = context_layer(xp, modulation(c_ctx[None, :], p['w_ada'], p['b_ada']), p)
        new_k.append(k_h)
        new_v.append(v_h)
        new_rwkv.append(s_rwkv)
        new_s5.append(s_s5)
        s5_x0 = lax.complex(state_s5_re[:, l].astype(F32), state_s5_im[:, l].astype(F32))
        xs = latent_layer(xs, modulation(c, p['w_ada'], p['b_ada']), p,
                          cache_na_k[:, l], cache_na_v[:, l], state_rwkv[:, l], s5_x0)
    s5_all = jnp.stack(new_s5, axis=1)
    return (xp, xs, jnp.stack(new_k, axis=1), jnp.stack(new_v, axis=1), jnp.stack(new_rwkv, axis=1),
            s5_all.real, s5_all.imag)
```

```python
import functools
import math

import numpy as np
import jax
import jax.numpy as jnp
from jax import lax
from jax.experimental import pallas as pl
from jax.experimental.pallas import tpu as pltpu

F32 = jnp.float32
BF16 = jnp.bfloat16
HIGHEST = lax.Precision.HIGHEST

D_MODEL = 1024
DEPTH = 2
GRID_W = 64
D_MIX = 512
HEADS = 8
HD = 64
LORA = 128
WIN_R = 8
WIN_C = 16
C_GROUP = 16
C_GROUPS = D_MIX // C_GROUP
C_STATE = 64
S5_N = C_GROUPS * C_STATE
S5_LANES = 256
N_EXPERTS = 64
TOP_K = 8
N_GROUPS = 8
TOPK_GROUPS = 4
D_EXPERT = 128
D_SHARED = 128
ROUTED_SCALE = 2.5
LN_EPS = 1e-5
GN_EPS = 64e-5
NEG = -1e30
DN_ALPHA = (2 * DEPTH) ** 0.25
A_COLS = 3 * D_MIX + 3 * LORA
A_PAD = 2048
IN_COLS_P = A_PAD + 4 * D_MIX + 3 * D_MODEL

LANE = 128
SUBLANE = 8
TM = 256
CH = 64
VMEM_LIMIT = 56 * 1024 * 1024


def _cparams(sem, vmem=None):
    return pltpu.CompilerParams(dimension_semantics=sem, vmem_limit_bytes=vmem)


def _sigmoid(x):
    return 1.0 / (1.0 + jnp.exp(-x))


def _dot(a, b, precision=None):
    return jnp.dot(a, b, preferred_element_type=F32, precision=precision)


def _dot_nt(a, b, precision=None):
    return lax.dot_general(a, b, (((1,), (1,)), ((), ())), preferred_element_type=F32, precision=precision)


def _dot_tn(a, b, precision=None):
    return lax.dot_general(a, b, (((0,), (0,)), ((), ())), preferred_element_type=F32, precision=precision)


def _layer_norm(x, g, b):
    mu = jnp.mean(x, axis=-1, keepdims=True)
    xc = x - mu
    var = jnp.mean(xc * xc, axis=-1, keepdims=True)
    return xc * lax.rsqrt(var + LN_EPS) * g + b


def _ada_kernel(c_ref, w_ref, b_ref, o_ref):
    c = c_ref[...]
    s = c * _sigmoid(c)
    o_ref[0] = _dot(s.astype(BF16), w_ref[0]) + b_ref[0]


def ada_modulation(cond, w_ada, b_ada):
    n_l, d, n6 = w_ada.shape
    tn = 1536
    return pl.pallas_call(
        _ada_kernel,
        grid=(n_l, n6 // tn),
        in_specs=[pl.BlockSpec((SUBLANE, d), lambda l, j: (0, 0)),
                  pl.BlockSpec((1, d, tn), lambda l, j: (l, 0, j)),
                  pl.BlockSpec((1, 1, tn), lambda l, j: (l, 0, j))],
        out_specs=pl.BlockSpec((1, SUBLANE, tn), lambda l, j: (l, 0, j)),
        out_shape=jax.ShapeDtypeStruct((n_l, SUBLANE, n6), F32),
        compiler_params=_cparams(("arbitrary", "arbitrary")),
        name="ada_modulation",
    )(cond, w_ada.astype(BF16), b_ada.reshape(n_l, 1, n6))


def _inproj_kernel(x_ref, mod_ref, w_ref, z_ref, h_scr):
    @pl.when(pl.program_id(1) == 0)
    def _():
        m = mod_ref[0]
        h_scr[...] = (x_ref[...] * (1.0 + m[1:2]) + m[0:1]).astype(BF16)
    z_ref[...] = _dot(h_scr[...], w_ref[...])


def in_projection(x, mods, w_in_p, mod_of_tile, tm=512, tn=1024):
    n, d = x.shape
    cols = w_in_p.shape[1]
    return pl.pallas_call(
        _inproj_kernel,
        grid=(n // tm, cols // tn),
        in_specs=[pl.BlockSpec((tm, d), lambda i, j: (i, 0)),
                  pl.BlockSpec((1, SUBLANE, d), lambda i, j: (mod_of_tile(i, tm), 0, 0)),
                  pl.BlockSpec((d, tn), lambda i, j: (0, j))],
        out_specs=pl.BlockSpec((tm, tn), lambda i, j: (i, j)),
        out_shape=jax.ShapeDtypeStruct((n, cols), F32),
        scratch_shapes=[pltpu.VMEM((tm, d), BF16)],
        compiler_params=_cparams(("arbitrary", "arbitrary"), VMEM_LIMIT),
        name="in_projection",
    )(x, mods, w_in_p)


def _rwkv_prep_kernel(seq_tiles_ref, z_ref, zp_ref, zn_ref, mu_ref, w2_ref, a2_ref, g2_ref, w0_ref, a0_ref,
                      kkp_ref, ka_ref, rk_ref, e_ref,
                      r_ref, v_ref, kk_ref, ld_ref, kd_ref, bd_ref, g_ref, bonus_ref):
    i = pl.program_id(0)
    tiles = seq_tiles_ref[i]
    pos = seq_tiles_ref[i + pl.num_programs(0)]
    x = z_ref[...]
    tm = x.shape[0]
    rows = lax.broadcasted_iota(jnp.int32, x.shape, 0)
    prev_row = jnp.where(pos == 0, 0.0, zp_ref[SUBLANE - 1:SUBLANE, :])
    next_row = jnp.where(pos == tiles - 1, 0.0, zn_ref[0:1, :])
    xm1 = jnp.where(rows == 0, prev_row, pltpu.roll(x, 1, axis=0))
    xp1 = jnp.where(rows == tm - 1, next_row, pltpu.roll(x, tm - 1, axis=0))
    za = x + mu_ref[...] * (0.5 * (xm1 + xp1) - x)

    r = za[:, 0:D_MIX]
    k = za[:, D_MIX:2 * D_MIX]
    v = za[:, 2 * D_MIX:3 * D_MIX]
    lw = za[:, 3 * D_MIX:3 * D_MIX + LORA]
    la = za[:, 3 * D_MIX + LORA:3 * D_MIX + 2 * LORA]
    lg = za[:, 3 * D_MIX + 2 * LORA:3 * D_MIX + 3 * LORA]

    w_both = w0_ref[...] + _dot(jnp.tanh(lw).astype(BF16), w2_ref[...])
    a_both = _sigmoid(a0_ref[...] + _dot(la.astype(BF16), a2_ref[...]))
    g_ref[...] = _dot(_sigmoid(lg).astype(BF16), g2_ref[...])

    e = e_ref[...]
    kks = k * kkp_ref[...]
    nrm = jnp.sqrt(_dot(kks * kks, e, HIGHEST))
    kk = kks / jnp.maximum(nrm, 1e-12)
    bonus = jnp.zeros_like(v)
    for h in range(HEADS):
        sl = slice(h * HD, (h + 1) * HD)
        r_ref[h] = r[:, sl]
        v_ref[h] = v[:, sl]
        kk_ref[h] = kk[:, sl]
    for d in range(2):
        w = w_both[:, d * D_MIX:(d + 1) * D_MIX]
        a = a_both[:, d * D_MIX:(d + 1) * D_MIX]
        ld = -math.exp(-0.5) * _sigmoid(w)
        kd = k * (1.0 + (a - 1.0) * ka_ref[...])
        bd = kk * a
        bonus = bonus + _dot(r * kd * rk_ref[...], e, HIGHEST) * v
        for h in range(HEADS):
            sl = slice(h * HD, (h + 1) * HD)
            ld_ref[d, h] = ld[:, sl]
            kd_ref[d, h] = kd[:, sl]
            bd_ref[d, h] = bd[:, sl]
    bonus_ref[...] = bonus


def rwkv_prep(z, seq_tiles, p):
    n = z.shape[0]
    nt = n // TM
    halo = TM // SUBLANE
    nb8 = n // SUBLANE
    hm = jax.ShapeDtypeStruct((HEADS, n, HD), F32)
    hm2 = jax.ShapeDtypeStruct((2, HEADS, n, HD), F32)
    tok = jax.ShapeDtypeStruct((n, D_MIX), F32)
    full = lambda shape: pl.BlockSpec(shape, lambda i, s: (0,) * len(shape))
    hm_spec = pl.BlockSpec((HEADS, TM, HD), lambda i, s: (0, i, 0))
    hm2_spec = pl.BlockSpec((2, HEADS, TM, HD), lambda i, s: (0, 0, i, 0))
    tok_spec = pl.BlockSpec((TM, D_MIX), lambda i, s: (i, 0))
    grid_spec = pltpu.PrefetchScalarGridSpec(
        num_scalar_prefetch=1,
        grid=(nt,),
        in_specs=[pl.BlockSpec((TM, A_PAD), lambda i, s: (i, 0)),
                  pl.BlockSpec((SUBLANE, A_PAD), lambda i, s: (jnp.maximum(i * halo - 1, 0), 0)),
                  pl.BlockSpec((SUBLANE, A_PAD), lambda i, s: (jnp.minimum((i + 1) * halo, nb8 - 1), 0)),
                  full((1, A_PAD)), full((LORA, 2 * D_MIX)), full((LORA, 2 * D_MIX)), full((LORA, D_MIX)),
                  full((1, 2 * D_MIX)), full((1, 2 * D_MIX)), full((1, D_MIX)), full((1, D_MIX)),
                  full((1, D_MIX)), full((D_MIX, D_MIX))],
        out_specs=[hm_spec, hm_spec, hm_spec, hm2_spec, hm2_spec, hm2_spec, tok_spec, tok_spec],
    )
    return pl.pallas_call(
        _rwkv_prep_kernel,
        grid_spec=grid_spec,
        out_shape=[hm, hm, hm, hm2, hm2, hm2, tok, tok],
        compiler_params=_cparams(("arbitrary",), VMEM_LIMIT),
        name="rwkv_prep",
    )(seq_tiles, z, z, z, p['mu'], p['w2bd'], p['a2bd'], p['g2'], p['w0'], p['a0'],
      p['kkp'], p['ka'], p['rk'], p['seg_ones'])


def _chunk_masks():
    t = np.arange(CH)
    fwd_incl = (t[:, None] >= t[None, :])
    out = []
    for incl in (fwd_incl, fwd_incl.T):
        strict = incl & (t[:, None] != t[None, :])
        ms = [incl, strict, strict & ((t[:, None] // 8) == (t[None, :] // 8))]
        for m in (8, 16, 32):
            ms.append(strict & ((t[:, None] // (2 * m)) == (t[None, :] // (2 * m)))
                      & ((t[:, None] // m) != (t[None, :] // m)))
        out.append(np.stack(ms))
    return np.stack(out).astype(np.float32)


def _rwkv_chunk_kernel(has_s0, *refs):
    if has_s0:
        (r_ref, v_ref, kk_ref, ld_ref, kd_ref, bd_ref, m_ref, gng_ref, gnb_ref, s0_ref,
         y_ref, sfin_ref, s_scr, y_scr) = refs
    else:
        (r_ref, v_ref, kk_ref, ld_ref, kd_ref, bd_ref, m_ref, gng_ref, gnb_ref,
         y_ref, sfin_ref, s_scr, y_scr) = refs
    c = pl.program_id(2)
    n_c = pl.num_programs(2)

    @pl.when(c == 0)
    def _():
        if has_s0:
            s_scr[...] = s0_ref[0, 0]
        else:
            s_scr[...] = jnp.zeros_like(s_scr)

    incl = m_ref[0, 0]
    strict = m_ref[0, 1]
    ri = lax.broadcasted_iota(jnp.int32, (CH, CH), 0)
    ci = lax.broadcasted_iota(jnp.int32, (CH, CH), 1)
    eye = (ri == ci).astype(F32)
    hp = HIGHEST

    def head(h, carry):
        R = r_ref[h]
        V = v_ref[h]
        KK = kk_ref[h]
        LD = ld_ref[0, h]
        Kd = kd_ref[0, h]
        Bd = bd_ref[0, h]
        L = _dot(incl, LD, hp)
        Lm = L - LD
        ltot = jnp.sum(LD, axis=0, keepdims=True)
        e_l = jnp.exp(L)
        e_nl = jnp.exp(-L)
        e_rest = jnp.exp(ltot - L)
        Qh = KK * jnp.exp(Lm)
        Rh = R * e_l
        Bt = Bd * e_nl
        Kt = Kd * e_nl
        Bc = Bd * e_rest
        Kc = Kd * e_rest
        Nl = strict * _dot_nt(Qh, Bt, hp)
        Mk = strict * _dot_nt(Qh, Kt, hp)
        Mrb = incl * _dot_nt(Rh, Bt, hp)
        Mrk = incl * _dot_nt(Rh, Kt, hp)
        N8 = m_ref[0, 2] * Nl
        N2 = _dot(N8, N8, hp)
        N4 = _dot(N2, N2, hp)
        W = _dot(_dot(eye - N8, eye + N2, hp), eye + N4, hp)
        for lvl in range(3):
            W = W - _dot(W, _dot(m_ref[0, 3 + lvl] * Nl, W, hp), hp)
        What = _dot(W, Qh, hp)
        U0 = -_dot(W, _dot(Mk, V, hp), hp)
        Rbar = Rh - _dot(Mrb, What, hp)
        Y0 = _dot(Mrb, U0, hp) + _dot(Mrk, V, hp)
        ST = s_scr[h]
        Y = _dot(Rbar, ST, hp) + Y0
        G = eye * jnp.exp(ltot) - _dot_tn(Bc, What, hp)
        Hm = _dot_tn(Bc, U0, hp) + _dot_tn(Kc, V, hp)
        s_scr[h] = _dot(G, ST, hp) + Hm
        mu = jnp.mean(Y, axis=-1, keepdims=True)
        yc = Y - mu
        var = jnp.mean(yc * yc, axis=-1, keepdims=True)
        y_scr[h] = yc * lax.rsqrt(var + GN_EPS) * gng_ref[h] + gnb_ref[h]
        return carry

    lax.fori_loop(0, HEADS, head, 0)
    for h in range(HEADS):
        y_ref[0, :, h * HD:(h + 1) * HD] = y_scr[h]

    @pl.when(c == n_c - 1)
    def _():
        sfin_ref[0, 0] = s_scr[...]


def rwkv_scan(prep, masks, gng, gnb, s0t, tok_base, bsz, t_len):
    r, v, kk, ld, kd, bd = prep
    n_c = t_len // CH
    base_blk = tok_base // CH

    def blk(d, b, c):
        cc = jnp.where(d == 0, c, n_c - 1 - c)
        return b * n_c + cc

    hm_spec = pl.BlockSpec((HEADS, CH, HD), lambda d, b, c: (0, base_blk + blk(d, b, c), 0))
    hm2_spec = pl.BlockSpec((1, HEADS, CH, HD), lambda d, b, c: (d, 0, base_blk + blk(d, b, c), 0))
    in_specs = [hm_spec, hm_spec, hm_spec, hm2_spec, hm2_spec, hm2_spec,
                pl.BlockSpec((1, 6, CH, CH), lambda d, b, c: (d, 0, 0, 0)),
                pl.BlockSpec((HEADS, 1, HD), lambda d, b, c: (0, 0, 0)),
                pl.BlockSpec((HEADS, 1, HD), lambda d, b, c: (0, 0, 0))]
    args = [r, v, kk, ld, kd, bd, masks, gng, gnb]
    if s0t is not None:
        in_specs.append(pl.BlockSpec((1, 1, HEADS, HD, HD), lambda d, b, c: (b, d, 0, 0, 0)))
        args.append(s0t)
    return pl.pallas_call(
        functools.partial(_rwkv_chunk_kernel, s0t is not None),
        grid=(2, bsz, n_c),
        in_specs=in_specs,
        out_specs=[pl.BlockSpec((1, CH, D_MIX), lambda d, b, c: (d, blk(d, b, c), 0)),
                   pl.BlockSpec((1, 1, HEADS, HD, HD), lambda d, b, c: (b, d, 0, 0, 0))],
        out_shape=[jax.ShapeDtypeStruct((2, bsz * t_len, D_MIX), F32),
                   jax.ShapeDtypeStruct((bsz, 2, HEADS, HD, HD), F32)],
        scratch_shapes=[pltpu.VMEM((HEADS, HD, HD), F32), pltpu.VMEM((HEADS, CH, HD), F32)],
        compiler_params=_cparams(("arbitrary", "arbitrary", "arbitrary")),
        name="rwkv_scan",
    )(*args)


def _ctx_attn_kernel(q_ref, k_ref, v_ref, y_ref, ko_ref, vo_ref):
    scale = HD ** -0.5
    for h in range(HEADS):
        sl = slice(h * HD, (h + 1) * HD)
        q = q_ref[:, sl]
        k = k_ref[:, sl]
        v = v_ref[:, sl]
        ko_ref[0, h] = k
        vo_ref[0, h] = v
        s = _dot_nt(q.astype(BF16), k.astype(BF16)) * scale
        m = jnp.max(s, axis=-1, keepdims=True)
        e = jnp.exp(s - m)
        p = e / jnp.sum(e, axis=-1, keepdims=True)
        y_ref[:, sl] = _dot(p.astype(BF16), v.astype(BF16))


def ctx_attention(z, bsz, t_len):
    qb = A_PAD // D_MIX
    return pl.pallas_call(
        _ctx_attn_kernel,
        grid=(bsz,),
        in_specs=[pl.BlockSpec((t_len, D_MIX), lambda b: (b, qb)),
                  pl.BlockSpec((t_len, D_MIX), lambda b: (b, qb + 1)),
                  pl.BlockSpec((t_len, D_MIX), lambda b: (b, qb + 2))],
        out_specs=[pl.BlockSpec((t_len, D_MIX), lambda b: (b, 0)),
                   pl.BlockSpec((1, HEADS, t_len, HD), lambda b: (b, 0, 0, 0)),
                   pl.BlockSpec((1, HEADS, t_len, HD), lambda b: (b, 0, 0, 0))],
        out_shape=[jax.ShapeDtypeStruct((bsz * t_len, D_MIX), F32),
                   jax.ShapeDtypeStruct((bsz, HEADS, t_len, HD), F32),
                   jax.ShapeDtypeStruct((bsz, HEADS, t_len, HD), F32)],
        compiler_params=_cparams(("arbitrary",)),
        name="ctx_attention",
    )(z, z, z)


def _na_kernel(wr, rows, *refs):
    q_ref = refs[0]
    k_refs = refs[1:1 + wr]
    v_refs = refs[1 + wr:1 + 2 * wr]
    kc_ref, vc_ref, bias_ref, y_ref = refs[1 + 2 * wr:]
    r = pl.program_id(1)
    rs = jnp.clip(r - wr // 2, 0, rows - wr)
    base = rs - r + WIN_R - 1
    scale = HD ** -0.5
    kwin = jnp.concatenate([kr[...] for kr in k_refs], axis=0).astype(BF16)
    vwin = jnp.concatenate([vr[...] for vr in v_refs], axis=0).astype(BF16)
    for h in range(HEADS):
        sl = slice(h * HD, (h + 1) * HD)
        q = q_ref[:, sl].astype(BF16)
        s_loc = _dot_nt(q, kwin[:, sl]) * scale + bias_ref[h, base]
        s_ctx = _dot_nt(q, kc_ref[0, 0, h].astype(BF16)) * scale
        m = jnp.maximum(jnp.max(s_loc, axis=-1, keepdims=True), jnp.max(s_ctx, axis=-1, keepdims=True))
        e_loc = jnp.exp(s_loc - m)
        e_ctx = jnp.exp(s_ctx - m)
        den = jnp.sum(e_loc, axis=-1, keepdims=True) + jnp.sum(e_ctx, axis=-1, keepdims=True)
        p_loc = (e_loc / den).astype(BF16)
        p_ctx = (e_ctx / den).astype(BF16)
        y_ref[:, sl] = _dot(p_loc, vwin[:, sl]) + _dot(p_ctx, vc_ref[0, 0, h].astype(BF16))


def na_bias_table(rpb, wr):
    cq = np.arange(GRID_W)[:, None]
    ck = np.arange(GRID_W)[None, :]
    cs = np.clip(cq - WIN_C // 2, 0, GRID_W - WIN_C)
    col_bias = np.where((ck >= cs) & (ck < cs + WIN_C), 0.0, NEG).astype(np.float32)
    col_idx = np.clip(ck - cq + WIN_C - 1, 0, 2 * WIN_C - 2)
    rpb_col = rpb.astype(F32)[:, :, col_idx] + col_bias
    n_base = 2 * WIN_R - wr
    tabs = [jnp.concatenate([rpb_col[:, b + j] for j in range(wr)], axis=-1) for b in range(n_base)]
    return jnp.stack(tabs, axis=1)


def na_attention(z, k_ctx, v_ctx, layer, bias_tab, tok_base, bsz, t_len):
    rows = t_len // GRID_W
    wr = min(WIN_R, rows)
    qb = A_PAD // D_MIX
    base_blk = tok_base // GRID_W
    past = k_ctx.shape[3]

    def win_spec(j, col):
        return pl.BlockSpec(
            (GRID_W, D_MIX),
            lambda b, r: (base_blk + b * rows + jnp.clip(r - wr // 2, 0, rows - wr) + j, col))

    in_specs = ([pl.BlockSpec((GRID_W, D_MIX), lambda b, r: (base_blk + b * rows + r, qb))]
                + [win_spec(j, qb + 1) for j in range(wr)]
                + [win_spec(j, qb + 2) for j in range(wr)]
                + [pl.BlockSpec((1, 1, HEADS, past, HD), lambda b, r: (b, layer, 0, 0, 0)),
                   pl.BlockSpec((1, 1, HEADS, past, HD), lambda b, r: (b, layer, 0, 0, 0)),
                   pl.BlockSpec(bias_tab.shape, lambda b, r: (0, 0, 0, 0))])
    return pl.pallas_call(
        functools.partial(_na_kernel, wr, rows),
        grid=(bsz, rows),
        in_specs=in_specs,
        out_specs=pl.BlockSpec((GRID_W, D_MIX), lambda b, r: (b * rows + r, 0)),
        out_shape=jax.ShapeDtypeStruct((bsz * t_len, D_MIX), F32),
        compiler_params=_cparams(("arbitrary", "arbitrary"), VMEM_LIMIT),
        name="na_attention",
    )(*([z] * (1 + 2 * wr)), k_ctx, v_ctx, bias_tab)


def _s5_kernel(tb, u_ref, bm_ref, cm_ref, ar_ref, ai_ref, x0_ref, y_ref, xf_ref, x_scr, carry_scr):
    t_blk = pl.program_id(1)
    rows = u_ref.shape[2]

    @pl.when(t_blk == 0)
    def _():
        carry_scr[...] = x0_ref[0]

    u = u_ref[0].reshape(tb * rows, D_MIX).astype(BF16)
    x_scr[...] = _dot(u, bm_ref[0]).reshape(tb, rows, 2 * S5_N)
    for lc in range(S5_N // S5_LANES):
        re = slice(lc * S5_LANES, (lc + 1) * S5_LANES)
        im = slice(S5_N + lc * S5_LANES, S5_N + (lc + 1) * S5_LANES)
        ar = ar_ref[0, :, re]
        ai = ai_ref[0, :, re]

        def step(t, x, re=re, im=im, ar=ar, ai=ai):
            xr, xi = x
            nr = ar * xr - ai * xi + x_scr[t, :, re]
            ni = ar * xi + ai * xr + x_scr[t, :, im]
            x_scr[t, :, re] = nr
            x_scr[t, :, im] = ni
            return nr, ni

        xr, xi = lax.fori_loop(0, tb, step, (carry_scr[:, re], carry_scr[:, im]))
        carry_scr[:, re] = xr
        carry_scr[:, im] = xi
    xs = x_scr[...].reshape(tb * rows, 2 * S5_N).astype(BF16)
    y_ref[0] = _dot(xs, cm_ref[...]).reshape(tb, rows, D_MIX)

    @pl.when(t_blk == pl.num_programs(1) - 1)
    def _():
        xf_ref[0] = carry_scr[...]


def s5_scan(u_tm, bmat, cmat, a_re, a_im, x0, tb):
    _, t_len, rows, _ = u_tm.shape
    return pl.pallas_call(
        functools.partial(_s5_kernel, tb),
        grid=(2, t_len // tb),
        in_specs=[pl.BlockSpec((1, tb, rows, D_MIX), lambda d, t: (d, t, 0, 0)),
                  pl.BlockSpec((1, D_MIX, 2 * S5_N), lambda d, t: (d, 0, 0)),
                  pl.BlockSpec((2 * S5_N, D_MIX), lambda d, t: (0, 0)),
                  pl.BlockSpec((1, 1, S5_N), lambda d, t: (d, 0, 0)),
                  pl.BlockSpec((1, 1, S5_N), lambda d, t: (d, 0, 0)),
                  pl.BlockSpec((1, rows, 2 * S5_N), lambda d, t: (d, 0, 0))],
        out_specs=[pl.BlockSpec((1, tb, rows, D_MIX), lambda d, t: (d, t, 0, 0)),
                   pl.BlockSpec((1, rows, 2 * S5_N), lambda d, t: (d, 0, 0))],
        out_shape=[jax.ShapeDtypeStruct(u_tm.shape, F32),
                   jax.ShapeDtypeStruct((2, rows, 2 * S5_N), F32)],
        scratch_shapes=[pltpu.VMEM((tb, rows, 2 * S5_N), F32), pltpu.VMEM((rows, 2 * S5_N), F32)],
        compiler_params=_cparams(("arbitrary", "arbitrary"), VMEM_LIMIT),
        name="s5_scan",
    )(u_tm, bmat, cmat, a_re, a_im, x0)


def _merge_kernel(gn_ref, bonus_ref, g_ref, yb_ref, yc_ref, u_ref, zg0_ref, zg1_ref, zg2_ref, x_ref, mod_ref,
                  s5d_ref, wglu_ref, bglu_ref, wb_ref, wout_ref, lng_ref, lnb_ref, x1_ref, h2_ref):
    ya = (gn_ref[0] + gn_ref[1] + bonus_ref[...]) * g_ref[...]
    yc = yc_ref[...] + s5d_ref[...] * u_ref[...]
    yc = 0.5 * yc * (1.0 + jnp.tanh(math.sqrt(2.0 / math.pi) * (yc + 0.044715 * (yc * yc * yc))))
    yc = yc * _sigmoid(_dot(yc.astype(BF16), wglu_ref[...]) + bglu_ref[...])
    merged = (_dot(ya.astype(BF16), wb_ref[0]) * _sigmoid(zg0_ref[...])
              + _dot(yb_ref[...].astype(BF16), wb_ref[1]) * _sigmoid(zg1_ref[...])
              + _dot(yc.astype(BF16), wb_ref[2]) * _sigmoid(zg2_ref[...]))
    mo = _dot(merged.astype(BF16), wout_ref[...])
    m = mod_ref[0]
    x1 = _layer_norm(DN_ALPHA * x_ref[...] + m[2:3] * mo, lng_ref[...], lnb_ref[...])
    x1_ref[...] = x1
    h2_ref[...] = (x1 * (1.0 + m[4:5]) + m[3:4]).astype(BF16)


def merge_branches(gn, bonus, g, yb, yc, z, x, mods, mod_of_tile, p):
    n = x.shape[0]
    row = lambda w, col=0: pl.BlockSpec((TM, w), lambda i, col=col: (i, col))
    full = lambda shape: pl.BlockSpec(shape, lambda i: (0,) * len(shape))
    gb = (A_PAD + 4 * D_MIX) // D_MODEL
    return pl.pallas_call(
        _merge_kernel,
        grid=(n // TM,),
        in_specs=[pl.BlockSpec((2, TM, D_MIX), lambda i: (0, i, 0)),
                  row(D_MIX), row(D_MIX), row(D_MIX), row(D_MIX),
                  row(D_MIX, A_PAD // D_MIX + 3),
                  row(D_MODEL, gb), row(D_MODEL, gb + 1), row(D_MODEL, gb + 2),
                  row(D_MODEL),
                  pl.BlockSpec((1, SUBLANE, D_MODEL), lambda i: (mod_of_tile(i, TM), 0, 0)),
                  full((1, D_MIX)), full((D_MIX, D_MIX)), full((1, D_MIX)),
                  full((3, D_MIX, D_MODEL)), full((D_MODEL, D_MODEL)),
                  full((1, D_MODEL)), full((1, D_MODEL))],
        out_specs=[row(D_MODEL), row(D_MODEL)],
        out_shape=[jax.ShapeDtypeStruct((n, D_MODEL), F32), jax.ShapeDtypeStruct((n, D_MODEL), BF16)],
        compiler_params=_cparams(("arbitrary",), VMEM_LIMIT),
        name="merge_branches",
    )(gn, bonus, g, yb, yc, z, z, z, z, x, mods,
      p['s5_d'], p['w_glu'], p['b_glu'], p['w_branch'], p['w_out'], p['ln1_g'], p['ln1_b'])


def _first_max(val, idx, big):
    m = jnp.max(jnp.max(val, axis=1, keepdims=True), axis=0, keepdims=True)
    cand = jnp.where(val == m, idx, big)
    first = jnp.min(jnp.min(cand, axis=1, keepdims=True), axis=0, keepdims=True)
    return m, idx == first


def _router_kernel(h_ref, wt_ref, bias_ref, gates_ref):
    per = N_EXPERTS // N_GROUPS
    logits = _dot_nt(wt_ref[...], h_ref[...])
    n = logits.shape[1]
    scores = _sigmoid(logits).reshape(N_GROUPS, per, n)
    sel = scores + bias_ref[...]
    e_idx = (lax.broadcasted_iota(jnp.int32, (N_GROUPS, per, n), 0) * per
             + lax.broadcasted_iota(jnp.int32, (N_GROUPS, per, n), 1))
    in_grp = lax.broadcasted_iota(jnp.int32, (N_GROUPS, per, n), 1)
    m1 = jnp.max(sel, axis=1, keepdims=True)
    first = jnp.min(jnp.where(sel == m1, in_grp, per), axis=1, keepdims=True)
    m2 = jnp.max(jnp.where(in_grp == first, -jnp.inf, sel), axis=1, keepdims=True)
    grp = m1 + m2
    g_idx = lax.broadcasted_iota(jnp.int32, (N_GROUPS, 1, n), 0)
    gmask = jnp.zeros((N_GROUPS, 1, n), F32)
    for _ in range(TOPK_GROUPS):
        _, hit = _first_max(grp, g_idx, N_GROUPS)
        gmask = jnp.where(hit, 1.0, gmask)
        grp = jnp.where(hit, -jnp.inf, grp)
    cur = jnp.where(jnp.broadcast_to(gmask, sel.shape) > 0.0, sel, NEG)
    w = jnp.zeros((N_GROUPS, per, n), F32)
    for _ in range(TOP_K):
        _, hit = _first_max(cur, e_idx, N_EXPERTS)
        w = jnp.where(hit, scores, w)
        cur = jnp.where(hit, -jnp.inf, cur)
    tot = jnp.sum(jnp.sum(w, axis=1, keepdims=True), axis=0, keepdims=True)
    gates = (ROUTED_SCALE * w / tot).reshape(N_EXPERTS, n)
    gates_ref[...] = gates.T


def moe_router(h2, router_wt, router_bias):
    n = h2.shape[0]
    return pl.pallas_call(
        _router_kernel,
        grid=(n // TM,),
        in_specs=[pl.BlockSpec((TM, D_MODEL), lambda i: (i, 0)),
                  pl.BlockSpec((N_EXPERTS, D_MODEL), lambda i: (0, 0)),
                  pl.BlockSpec((N_GROUPS, N_EXPERTS // N_GROUPS, 1), lambda i: (0, 0, 0))],
        out_specs=pl.BlockSpec((TM, N_EXPERTS), lambda i: (i, 0)),
        out_shape=jax.ShapeDtypeStruct((n, N_EXPERTS), F32),
        compiler_params=_cparams(("arbitrary",)),
        name="moe_router",
    )(h2, router_wt, router_bias)


def _moe_kernel(ec, h_ref, gates_ref, x1_ref, mod_ref, wg_ref, wu_ref, wd_ref, ex_ref,
                sg_ref, su_ref, sd_ref, lng_ref, lnb_ref, out_ref, acc_scr):
    j = pl.program_id(1)
    h = h_ref[...]

    @pl.when(j == 0)
    def _():
        sh = _dot(h, sg_ref[...])
        sh = sh * _sigmoid(sh) * _dot(h, su_ref[...])
        acc_scr[...] = _dot(sh.astype(BF16), sd_ref[...])

    gates = gates_ref[...]
    g_hi = gates.astype(BF16)
    g_lo = (gates - g_hi.astype(F32)).astype(BF16)
    gexp = _dot(g_hi, ex_ref[0]) + _dot(g_lo, ex_ref[0])
    hg = _dot(h, wg_ref[...])
    hu = _dot(h, wu_ref[...])
    act = hg * _sigmoid(hg) * hu * gexp
    acc_scr[...] += _dot(act.astype(BF16), wd_ref[...])

    @pl.when(j == pl.num_programs(1) - 1)
    def _():
        m = mod_ref[0]
        out_ref[...] = _layer_norm(DN_ALPHA * x1_ref[...] + m[5:6] * acc_scr[...], lng_ref[...], lnb_ref[...])


def moe_ffn(h2, gates, x1, mods, mod_of_tile, p, tm=512, ec=8):
    n = h2.shape[0]
    wcols = ec * D_EXPERT
    n_j = N_EXPERTS // ec
    full = lambda shape: pl.BlockSpec(shape, lambda i, j: (0,) * len(shape))
    return pl.pallas_call(
        functools.partial(_moe_kernel, ec),
        grid=(n // tm, n_j),
        in_specs=[pl.BlockSpec((tm, D_MODEL), lambda i, j: (i, 0)),
                  pl.BlockSpec((tm, N_EXPERTS), lambda i, j: (i, 0)),
                  pl.BlockSpec((tm, D_MODEL), lambda i, j: (i, 0)),
                  pl.BlockSpec((1, SUBLANE, D_MODEL), lambda i, j: (mod_of_tile(i, tm), 0, 0)),
                  pl.BlockSpec((D_MODEL, wcols), lambda i, j: (0, j)),
                  pl.BlockSpec((D_MODEL, wcols), lambda i, j: (0, j)),
                  pl.BlockSpec((wcols, D_MODEL), lambda i, j: (j, 0)),
                  pl.BlockSpec((1, N_EXPERTS, wcols), lambda i, j: (j, 0, 0)),
                  full((D_MODEL, D_SHARED)), full((D_MODEL, D_SHARED)), full((D_SHARED, D_MODEL)),
                  full((1, D_MODEL)), full((1, D_MODEL))],
        out_specs=pl.BlockSpec((tm, D_MODEL), lambda i, j: (i, 0)),
        out_shape=jax.ShapeDtypeStruct((n, D_MODEL), F32),
        scratch_shapes=[pltpu.VMEM((tm, D_MODEL), F32)],
        compiler_params=_cparams(("arbitrary", "arbitrary"), VMEM_LIMIT),
        name="moe_ffn",
    )(h2, gates, x1, mods, p['wg'], p['wu'], p['wd'], p['expand'],
      p['sh_g'], p['sh_u'], p['sh_d'], p['ln2_g'], p['ln2_b'])


def _block_diag2(m):
    z = jnp.zeros_like(m[0])
    return jnp.concatenate([jnp.concatenate([m[0], z], axis=1), jnp.concatenate([z, m[1]], axis=1)], axis=0)


def _layer_params(P, l):
    g = lambda name: P[name][l]
    w_in = g('w_in')
    pad = jnp.zeros((D_MODEL, A_PAD - A_COLS), F32)
    p = {}
    p['w_in'] = jnp.concatenate([w_in[:, :A_COLS], pad, w_in[:, A_COLS:]], axis=1).astype(BF16)
    p['mu'] = jnp.concatenate([g('rwkv_mu'), jnp.zeros((A_PAD - A_COLS,), F32)])[None, :]
    p['w2bd'] = _block_diag2(g('rwkv_w2')).astype(BF16)
    p['a2bd'] = _block_diag2(g('rwkv_a2')).astype(BF16)
    p['g2'] = g('rwkv_g2').astype(BF16)
    p['w0'] = g('rwkv_w0').reshape(1, 2 * D_MIX)
    p['a0'] = g('rwkv_a0').reshape(1, 2 * D_MIX)
    p['kkp'] = g('rwkv_kk')[None, :]
    p['ka'] = g('rwkv_ka')[None, :]
    p['rk'] = g('rwkv_rk')[None, :]
    hid = np.arange(D_MIX) // HD
    p['seg_ones'] = jnp.asarray((hid[:, None] == hid[None, :]).astype(np.float32))
    p['gng'] = g('rwkv_gn_g').reshape(HEADS, 1, HD)
    p['gnb'] = g('rwkv_gn_b').reshape(HEADS, 1, HD)
    a = lax.complex(g('s5_a_re'), g('s5_a_im'))
    dt = jnp.exp(g('s5_log_dt'))[..., None]
    a_bar = jnp.exp(dt * a)
    b_bar = ((a_bar - 1.0) / a)[..., None] * lax.complex(g('s5_b_re'), g('s5_b_im'))
    eye_g = jnp.eye(C_GROUPS, dtype=F32)
    bd = lambda m: jnp.einsum('dgph,gk->dghkp', m, eye_g).reshape(2, D_MIX, S5_N)
    p['s5_bmat'] = jnp.concatenate([bd(b_bar.real), bd(b_bar.imag)], axis=-1).astype(BF16)
    cd = lambda m: jnp.einsum('ghp,gk->kpgh', m, eye_g).reshape(S5_N, D_MIX)
    p['s5_cmat'] = jnp.concatenate([cd(g('s5_c_re')), -cd(g('s5_c_im'))], axis=0).astype(BF16)
    p['s5_ar'] = a_bar.real.reshape(2, 1, S5_N)
    p['s5_ai'] = a_bar.imag.reshape(2, 1, S5_N)
    p['s5_d'] = g('s5_d')[None, :]
    p['w_glu'] = g('s5_w_glu').astype(BF16)
    p['b_glu'] = g('s5_b_glu')[None, :]
    p['w_branch'] = g('w_branch').astype(BF16)
    p['w_out'] = g('w_out').astype(BF16)
    p['ln1_g'] = g('ln1_g')[None, :]
    p['ln1_b'] = g('ln1_b')[None, :]
    p['router_wt'] = g('router_w').T.astype(BF16)
    p['router_bias'] = g('router_bias').reshape(N_GROUPS, N_EXPERTS // N_GROUPS, 1)
    ecols = N_EXPERTS * D_EXPERT
    p['wg'] = g('exp_w_gate').transpose(1, 0, 2).reshape(D_MODEL, ecols).astype(BF16)
    p['wu'] = g('exp_w_up').transpose(1, 0, 2).reshape(D_MODEL, ecols).astype(BF16)
    p['wd'] = g('exp_w_down').reshape(ecols, D_MODEL).astype(BF16)
    p['sh_g'] = g('sh_w_gate').astype(BF16)
    p['sh_u'] = g('sh_w_up').astype(BF16)
    p['sh_d'] = g('sh_w_down').astype(BF16)
    p['ln2_g'] = g('ln2_g')[None, :]
    p['ln2_b'] = g('ln2_b')[None, :]
    return p


def _expand_table(ec):
    n_j = N_EXPERTS // ec
    t = np.zeros((n_j, N_EXPERTS, ec * D_EXPERT), np.float32)
    for e in range(N_EXPERTS):
        j, q = divmod(e, ec)
        t[j, e, q * D_EXPERT:(q + 1) * D_EXPERT] = 1.0
    return jnp.asarray(t, dtype=BF16)


def kernel(x_prompt, x_sample, c, cache_na_k, cache_na_v, state_rwkv, state_s5_re, state_s5_im, c_ctx, w_ada, b_ada, w_in, rwkv_mu, rwkv_w0, rwkv_w2, rwkv_a0, rwkv_a2, rwkv_g2, rwkv_kk, rwkv_ka, rwkv_rk, rwkv_gn_g, rwkv_gn_b, na_rpb, s5_a_re, s5_a_im, s5_log_dt, s5_b_re, s5_b_im, s5_c_re, s5_c_im, s5_d, s5_w_glu, s5_b_glu, w_branch, w_out, ln1_g, ln1_b, router_w, router_bias, exp_w_gate, exp_w_up, exp_w_down, sh_w_gate, sh_w_up, sh_w_down, ln2_g, ln2_b):
    P = dict(w_in=w_in, rwkv_mu=rwkv_mu, rwkv_w0=rwkv_w0, rwkv_w2=rwkv_w2, rwkv_a0=rwkv_a0, rwkv_a2=rwkv_a2,
             rwkv_g2=rwkv_g2, rwkv_kk=rwkv_kk, rwkv_ka=rwkv_ka, rwkv_rk=rwkv_rk, rwkv_gn_g=rwkv_gn_g,
             rwkv_gn_b=rwkv_gn_b, s5_a_re=s5_a_re, s5_a_im=s5_a_im, s5_log_dt=s5_log_dt, s5_b_re=s5_b_re,
             s5_b_im=s5_b_im, s5_c_re=s5_c_re, s5_c_im=s5_c_im, s5_d=s5_d, s5_w_glu=s5_w_glu,
             s5_b_glu=s5_b_glu, w_branch=w_branch, w_out=w_out, ln1_g=ln1_g, ln1_b=ln1_b, router_w=router_w,
             router_bias=router_bias, exp_w_gate=exp_w_gate, exp_w_up=exp_w_up, exp_w_down=exp_w_down,
             sh_w_gate=sh_w_gate, sh_w_up=sh_w_up, sh_w_down=sh_w_down, ln2_g=ln2_g, ln2_b=ln2_b)
    bc, tc, _ = x_prompt.shape
    bl, tl, _ = x_sample.shape
    n_ctx = bc * tc
    n_lat = bl * tl
    n = n_ctx + n_lat
    depth = w_in.shape[0]

    def mod_of_tile(i, tm):
        tok = i * tm
        return jnp.where(tok < n_ctx, 0, 1 + (tok - n_ctx) // tl)

    tile_tok = np.arange(n // TM) * TM
    is_ctx = tile_tok < n_ctx
    seq_tiles = np.concatenate([
        np.where(is_ctx, tc // TM, tl // TM),
        np.where(is_ctx, (tile_tok % tc) // TM, ((tile_tok - n_ctx) % tl) // TM)]).astype(np.int32)
    seq_tiles = jnp.asarray(seq_tiles)

    cond = jnp.concatenate([c_ctx[None, :], c, jnp.zeros((SUBLANE - 1 - bl, D_MODEL), F32)], axis=0)
    ada = ada_modulation(cond, w_ada, b_ada)
    masks = jnp.asarray(_chunk_masks())
    lat_rows = t_rows = tl // GRID_W
    wr = min(WIN_R, lat_rows)
    expand = _expand_table(8)

    x = jnp.concatenate([x_prompt.reshape(n_ctx, D_MODEL), x_sample.reshape(n_lat, D_MODEL)], axis=0)
    new_k, new_v, new_rwkv, new_s5 = [], [], [], []
    r_lat = SUBLANE
    for l in range(depth):
        p = _layer_params(P, l)
        p['expand'] = expand
        mods = ada[l, :1 + bl].reshape(1 + bl, 6, D_MODEL)
        mods = jnp.concatenate([mods, jnp.zeros((1 + bl, SUBLANE - 6, D_MODEL), F32)], axis=1)

        z = in_projection(x, mods, p['w_in'], mod_of_tile)

        r, v, kk, ld, kd, bd, g, bonus = rwkv_prep(z, seq_tiles, p)
        prep = (r, v, kk, ld, kd, bd)
        gn_c, s_c = rwkv_scan(prep, masks, p['gng'], p['gnb'], None, 0, bc, tc)
        s0t = jnp.swapaxes(state_rwkv[:, l], -1, -2)
        gn_l, _ = rwkv_scan(prep, masks, p['gng'], p['gnb'], s0t, n_ctx, bl, tl)
        gn = jnp.concatenate([gn_c, gn_l], axis=1)
        new_rwkv.append(jnp.swapaxes(s_c, -1, -2))

        yb_c, k_h, v_h = ctx_attention(z, bc, tc)
        new_k.append(k_h)
        new_v.append(v_h)
        bias_tab = na_bias_table(na_rpb[l], wr)
        yb_l = na_attention(z, cache_na_k, cache_na_v, l, bias_tab, n_ctx, bl, tl)
        yb = jnp.concatenate([yb_c, yb_l], axis=0)

        u = z[:, A_PAD + 3 * D_MIX:A_PAD + 4 * D_MIX]
        u_c = jnp.swapaxes(u[:n_ctx].reshape(bc, tc, D_MIX), 0, 1)
        u_c = jnp.stack([u_c, u_c[::-1]])
        zero_c = jnp.zeros((2, bc, 2 * S5_N), F32)
        y_c, xf_c = s5_scan(u_c, p['s5_bmat'], p['s5_cmat'], p['s5_ar'], p['s5_ai'], zero_c, tb=512 // bc)
        yc_c = jnp.swapaxes(y_c[0] + y_c[1][::-1], 0, 1).reshape(n_ctx, D_MIX)
        u_l = jnp.swapaxes(u[n_ctx:].reshape(bl, tl, D_MIX), 0, 1)
        u_l = jnp.pad(u_l, ((0, 0), (0, r_lat - bl), (0, 0)))
        u_l = jnp.stack([u_l, u_l[::-1]])
        x0_l = jnp.concatenate([state_s5_re[:, l].reshape(bl, 2, S5_N), state_s5_im[:, l].reshape(bl, 2, S5_N)],
                               axis=-1)
        x0_l = jnp.pad(jnp.swapaxes(x0_l, 0, 1), ((0, 0), (0, r_lat - bl), (0, 0)))
        y_l, _ = s5_scan(u_l, p['s5_bmat'], p['s5_cmat'], p['s5_ar'], p['s5_ai'], x0_l, tb=512 // r_lat)
        yc_l = jnp.swapaxes((y_l[0] + y_l[1][::-1])[:, :bl], 0, 1).reshape(n_lat, D_MIX)
        yc = jnp.concatenate([yc_c, yc_l], axis=0)
        new_s5.append(jnp.swapaxes(xf_c, 0, 1))

        x1, h2 = merge_branches(gn, bonus, g, yb, yc, z, x, mods, mod_of_tile, p)
        gates = moe_router(h2, p['router_wt'], p['router_bias'])
        x = moe_ffn(h2, gates, x1, mods, mod_of_tile, p)

    s5_all = jnp.stack(new_s5, axis=1)
    s5_re = s5_all[..., :S5_N].reshape(bc, depth, 2, C_GROUPS, C_STATE)
    s5_im = s5_all[..., S5_N:].reshape(bc, depth, 2, C_GROUPS, C_STATE)
    return (x[:n_ctx].reshape(bc, tc, D_MODEL), x[n_ctx:].reshape(bl, tl, D_MODEL),
            jnp.stack(new_k, axis=1), jnp.stack(new_v, axis=1), jnp.stack(new_rwkv, axis=1), s5_re, s5_im)
```

```python
import functools
import math

import numpy as np
import jax
import jax.numpy as jnp
from jax import lax
from jax.experimental import pallas as pl
from jax.experimental.pallas import tpu as pltpu

F32 = jnp.float32
BF16 = jnp.bfloat16
HIGHEST = lax.Precision.HIGHEST

D_MODEL = 1024
DEPTH = 2
GRID_W = 64
D_MIX = 512
HEADS = 8
HD = 64
LORA = 128
WIN_R = 8
WIN_C = 16
C_GROUP = 16
C_GROUPS = D_MIX // C_GROUP
C_STATE = 64
S5_N = C_GROUPS * C_STATE
S5_CARRY_VREGS = 32
N_EXPERTS = 64
TOP_K = 8
N_GROUPS = 8
TOPK_GROUPS = 4
D_EXPERT = 128
D_SHARED = 128
ROUTED_SCALE = 2.5
LN_EPS = 1e-5
GN_EPS = 64e-5
NEG = -1e30
DN_ALPHA = (2 * DEPTH) ** 0.25
A_COLS = 3 * D_MIX + 3 * LORA
A_PAD = 2048
IN_COLS_P = A_PAD + 4 * D_MIX + 3 * D_MODEL

LANE = 128
SUBLANE = 8
TM = 256
CH = 64
HEAD_GROUP = 8
VMEM_LIMIT = 56 * 1024 * 1024


def _cparams(sem, vmem=None):
    return pltpu.CompilerParams(dimension_semantics=sem, vmem_limit_bytes=vmem)


def _sigmoid(x):
    return 1.0 / (1.0 + jnp.exp(-x))


def _dot(a, b, precision=None):
    return jnp.dot(a, b, preferred_element_type=F32, precision=precision)


def _dot_nt(a, b, precision=None):
    return lax.dot_general(a, b, (((1,), (1,)), ((), ())), preferred_element_type=F32, precision=precision)


def _dot_tn(a, b, precision=None):
    return lax.dot_general(a, b, (((0,), (0,)), ((), ())), preferred_element_type=F32, precision=precision)


def _split2(x):
    hi = x.astype(BF16)
    return hi, (x - hi.astype(F32)).astype(BF16)


def _split3(x):
    hi = x.astype(BF16)
    r1 = x - hi.astype(F32)
    mid = r1.astype(BF16)
    return hi, mid, (r1 - mid.astype(F32)).astype(BF16)


def _mm3(a, b, dot=_dot):
    return dot(a[0], b[0]) + (dot(a[1], b[0]) + dot(a[0], b[1]))


def _mm3_nt(a, b):
    return _mm3(a, b, _dot_nt)


def _layer_norm(x, g, b):
    mu = jnp.mean(x, axis=-1, keepdims=True)
    xc = x - mu
    var = jnp.mean(xc * xc, axis=-1, keepdims=True)
    return xc * lax.rsqrt(var + LN_EPS) * g + b


def _ada_kernel(c_ref, w_ref, b_ref, o_ref):
    c = c_ref[...]
    s = c * _sigmoid(c)
    o_ref[0] = _dot(s.astype(BF16), w_ref[0]) + b_ref[0]


def ada_modulation(cond, w_ada, b_ada):
    n_l, d, n6 = w_ada.shape
    tn = 1536
    return pl.pallas_call(
        _ada_kernel,
        grid=(n_l, n6 // tn),
        in_specs=[pl.BlockSpec((SUBLANE, d), lambda l, j: (0, 0)),
                  pl.BlockSpec((1, d, tn), lambda l, j: (l, 0, j)),
                  pl.BlockSpec((1, 1, tn), lambda l, j: (l, 0, j))],
        out_specs=pl.BlockSpec((1, SUBLANE, tn), lambda l, j: (l, 0, j)),
        out_shape=jax.ShapeDtypeStruct((n_l, SUBLANE, n6), F32),
        compiler_params=_cparams(("arbitrary", "arbitrary")),
        name="ada_modulation",
    )(cond, w_ada.astype(BF16), b_ada.reshape(n_l, 1, n6))


def _inproj_kernel(x_ref, mod_ref, w_ref, z_ref, h_scr):
    @pl.when(pl.program_id(1) == 0)
    def _():
        m = mod_ref[0]
        h_scr[...] = (x_ref[...] * (1.0 + m[1:2]) + m[0:1]).astype(BF16)
    z_ref[...] = _dot(h_scr[...], w_ref[...])


def in_projection(x, mods, w_in_p, mod_of_tile, tm=512, tn=1024):
    n, d = x.shape
    cols = w_in_p.shape[1]
    return pl.pallas_call(
        _inproj_kernel,
        grid=(n // tm, cols // tn),
        in_specs=[pl.BlockSpec((tm, d), lambda i, j: (i, 0)),
                  pl.BlockSpec((1, SUBLANE, d), lambda i, j: (mod_of_tile(i, tm), 0, 0)),
                  pl.BlockSpec((d, tn), lambda i, j: (0, j))],
        out_specs=pl.BlockSpec((tm, tn), lambda i, j: (i, j)),
        out_shape=jax.ShapeDtypeStruct((n, cols), F32),
        scratch_shapes=[pltpu.VMEM((tm, d), BF16)],
        compiler_params=_cparams(("arbitrary", "arbitrary"), VMEM_LIMIT),
        name="in_projection",
    )(x, mods, w_in_p)


def _rwkv_prep_kernel(seq_tiles_ref, z_ref, zp_ref, zn_ref, mu_ref, w2_ref, a2_ref, g2_ref, w0_ref, a0_ref,
                      kkp_ref, ka_ref, rk_ref, e_ref,
                      r_ref, v_ref, kk_ref, ld_ref, kd_ref, bd_ref, g_ref, bonus_ref):
    i = pl.program_id(0)
    tiles = seq_tiles_ref[i]
    pos = seq_tiles_ref[i + pl.num_programs(0)]
    x = z_ref[...]
    tm = x.shape[0]
    rows = lax.broadcasted_iota(jnp.int32, x.shape, 0)
    prev_row = jnp.where(pos == 0, 0.0, zp_ref[SUBLANE - 1:SUBLANE, :])
    next_row = jnp.where(pos == tiles - 1, 0.0, zn_ref[0:1, :])
    xm1 = jnp.where(rows == 0, prev_row, pltpu.roll(x, 1, axis=0))
    xp1 = jnp.where(rows == tm - 1, next_row, pltpu.roll(x, tm - 1, axis=0))
    za = x + mu_ref[...] * (0.5 * (xm1 + xp1) - x)

    r = za[:, 0:D_MIX]
    k = za[:, D_MIX:2 * D_MIX]
    v = za[:, 2 * D_MIX:3 * D_MIX]
    lw = za[:, 3 * D_MIX:3 * D_MIX + LORA]
    la = za[:, 3 * D_MIX + LORA:3 * D_MIX + 2 * LORA]
    lg = za[:, 3 * D_MIX + 2 * LORA:3 * D_MIX + 3 * LORA]

    w_both = w0_ref[...] + _dot(jnp.tanh(lw).astype(BF16), w2_ref[...])
    a_both = _sigmoid(a0_ref[...] + _dot(la.astype(BF16), a2_ref[...]))
    g_ref[...] = _dot(_sigmoid(lg).astype(BF16), g2_ref[...])

    e = e_ref[...]
    kks = k * kkp_ref[...]
    nrm = jnp.sqrt(_dot(kks * kks, e, HIGHEST))
    kk = kks / jnp.maximum(nrm, 1e-12)
    bonus = jnp.zeros_like(v)
    for h in range(HEADS):
        sl = slice(h * HD, (h + 1) * HD)
        r_ref[h] = r[:, sl]
        v_ref[h] = v[:, sl]
        kk_ref[h] = kk[:, sl]
    for d in range(2):
        w = w_both[:, d * D_MIX:(d + 1) * D_MIX]
        a = a_both[:, d * D_MIX:(d + 1) * D_MIX]
        ld = -math.exp(-0.5) * _sigmoid(w)
        kd = k * (1.0 + (a - 1.0) * ka_ref[...])
        bd = kk * a
        bonus = bonus + _dot(r * kd * rk_ref[...], e, HIGHEST) * v
        for h in range(HEADS):
            sl = slice(h * HD, (h + 1) * HD)
            ld_ref[d, h] = ld[:, sl]
            kd_ref[d, h] = kd[:, sl]
            bd_ref[d, h] = bd[:, sl]
    bonus_ref[...] = bonus


def rwkv_prep(z, seq_tiles, p):
    n = z.shape[0]
    nt = n // TM
    halo = TM // SUBLANE
    nb8 = n // SUBLANE
    hm = jax.ShapeDtypeStruct((HEADS, n, HD), F32)
    hm2 = jax.ShapeDtypeStruct((2, HEADS, n, HD), F32)
    tok = jax.ShapeDtypeStruct((n, D_MIX), F32)
    full = lambda shape: pl.BlockSpec(shape, lambda i, s: (0,) * len(shape))
    hm_spec = pl.BlockSpec((HEADS, TM, HD), lambda i, s: (0, i, 0))
    hm2_spec = pl.BlockSpec((2, HEADS, TM, HD), lambda i, s: (0, 0, i, 0))
    tok_spec = pl.BlockSpec((TM, D_MIX), lambda i, s: (i, 0))
    grid_spec = pltpu.PrefetchScalarGridSpec(
        num_scalar_prefetch=1,
        grid=(nt,),
        in_specs=[pl.BlockSpec((TM, A_PAD), lambda i, s: (i, 0)),
                  pl.BlockSpec((SUBLANE, A_PAD), lambda i, s: (jnp.maximum(i * halo - 1, 0), 0)),
                  pl.BlockSpec((SUBLANE, A_PAD), lambda i, s: (jnp.minimum((i + 1) * halo, nb8 - 1), 0)),
                  full((1, A_PAD)), full((LORA, 2 * D_MIX)), full((LORA, 2 * D_MIX)), full((LORA, D_MIX)),
                  full((1, 2 * D_MIX)), full((1, 2 * D_MIX)), full((1, D_MIX)), full((1, D_MIX)),
                  full((1, D_MIX)), full((D_MIX, D_MIX))],
        out_specs=[hm_spec, hm_spec, hm_spec, hm2_spec, hm2_spec, hm2_spec, tok_spec, tok_spec],
    )
    return pl.pallas_call(
        _rwkv_prep_kernel,
        grid_spec=grid_spec,
        out_shape=[hm, hm, hm, hm2, hm2, hm2, tok, tok],
        compiler_params=_cparams(("arbitrary",), VMEM_LIMIT),
        name="rwkv_prep",
    )(seq_tiles, z, z, z, p['mu'], p['w2bd'], p['a2bd'], p['g2'], p['w0'], p['a0'],
      p['kkp'], p['ka'], p['rk'], p['seg_ones'])


def _chunk_masks():
    t = np.arange(CH)
    fwd_incl = (t[:, None] >= t[None, :])
    out = []
    for incl in (fwd_incl, fwd_incl.T):
        strict = incl & (t[:, None] != t[None, :])
        ms = [incl, strict, strict & ((t[:, None] // 8) == (t[None, :] // 8))]
        for m in (8, 16, 32):
            ms.append(strict & ((t[:, None] // (2 * m)) == (t[None, :] // (2 * m)))
                      & ((t[:, None] // m) != (t[None, :] // m)))
        out.append(np.stack(ms))
    return np.stack(out).astype(np.float32)


def _rwkv_chunk_kernel(has_s0, *refs):
    if has_s0:
        (r_ref, v_ref, kk_ref, ld_ref, kd_ref, bd_ref, m_ref, gng_ref, gnb_ref, s0_ref,
         y_ref, sfin_ref, s_scr) = refs
    else:
        (r_ref, v_ref, kk_ref, ld_ref, kd_ref, bd_ref, m_ref, gng_ref, gnb_ref,
         y_ref, sfin_ref, s_scr) = refs
    c = pl.program_id(2)
    n_c = pl.num_programs(2)

    @pl.when(c == 0)
    def _():
        if has_s0:
            s_scr[...] = s0_ref[0, 0]
        else:
            s_scr[...] = jnp.zeros_like(s_scr)

    incl = m_ref[0, 0]
    strict = m_ref[0, 1]
    incl_b = incl.astype(BF16)
    ri = lax.broadcasted_iota(jnp.int32, (CH, CH), 0)
    ci = lax.broadcasted_iota(jnp.int32, (CH, CH), 1)
    eye = (ri == ci).astype(F32)

    for h0 in range(0, HEADS, HEAD_GROUP):
        hs = list(range(h0, h0 + HEAD_GROUP))
        each = lambda f, *cols: [f(*args) for args in zip(*cols)]
        R = [r_ref[h] for h in hs]
        V = [v_ref[h] for h in hs]
        KK = [kk_ref[h] for h in hs]
        LD = [ld_ref[0, h] for h in hs]
        Kd = [kd_ref[0, h] for h in hs]
        Bd = [bd_ref[0, h] for h in hs]
        L = each(lambda x: sum(_dot(incl_b, part) for part in _split3(x)), LD)
        ltot = each(lambda x: jnp.sum(x, axis=0, keepdims=True), LD)
        e_nl = each(lambda l: jnp.exp(-l), L)
        e_rest = each(lambda l, lt: jnp.exp(lt - l), L, ltot)
        Qh = each(lambda kk, l, ld: kk * jnp.exp(l - ld), KK, L, LD)
        Rh = each(lambda r, l: r * jnp.exp(l), R, L)
        Qs = each(_split2, Qh)
        Rs = each(_split2, Rh)
        Bts = each(lambda b, e: _split2(b * e), Bd, e_nl)
        Kts = each(lambda k, e: _split2(k * e), Kd, e_nl)
        BcT = each(lambda b, e: _split2((b * e).T), Bd, e_rest)
        KcT = each(lambda k, e: _split2((k * e).T), Kd, e_rest)
        Vs = each(_split2, V)
        Nl = each(lambda q, b: strict * _mm3_nt(q, b), Qs, Bts)
        Mks = each(lambda q, k: _split2(strict * _mm3_nt(q, k)), Qs, Kts)
        Mrbs = each(lambda r, b: _split2(incl * _mm3_nt(r, b)), Rs, Bts)
        Mrks = each(lambda r, k: _split2(incl * _mm3_nt(r, k)), Rs, Kts)
        N8 = each(lambda n: m_ref[0, 2] * n, Nl)
        N8s = each(_split2, N8)
        N2 = each(lambda a: _mm3(a, a), N8s)
        N2s = each(_split2, N2)
        N4 = each(lambda a: _mm3(a, a), N2s)
        W = each(lambda a, b: _mm3(_split2(eye - a), _split2(eye + b)), N8, N2)
        W = each(lambda w, n4: _mm3(_split2(w), _split2(eye + n4)), W, N4)
        for lvl in range(3):
            Ws = each(_split2, W)
            T = each(lambda n, w: _mm3(_split2(m_ref[0, 3 + lvl] * n), w), Nl, Ws)
            W = each(lambda w, ws, t: w - _mm3(ws, _split2(t)), W, Ws, T)
        Ws = each(_split2, W)
        Whs = each(lambda w, q: _split2(_mm3(w, q)), Ws, Qs)
        MkV = each(lambda m, v: _split2(_mm3(m, v)), Mks, Vs)
        U0s = each(lambda w, t: _split2(-_mm3(w, t)), Ws, MkV)
        Rbar = each(lambda rh, m, wh: rh - _mm3(m, wh), Rh, Mrbs, Whs)
        Y0 = each(lambda mb, u, mk, v: _mm3(mb, u) + _mm3(mk, v), Mrbs, U0s, Mrks, Vs)
        STs = [_split2(s_scr[h]) for h in hs]
        Y = each(lambda rb, st, y0: _mm3(_split2(rb), st) + y0, Rbar, STs, Y0)
        G = each(lambda lt, b, wh: eye * jnp.exp(lt) - _mm3(b, wh), ltot, BcT, Whs)
        Hm = each(lambda b, u, k, v: _mm3(b, u) + _mm3(k, v), BcT, U0s, KcT, Vs)
        Sn = each(lambda g, st, hm: _mm3(_split2(g), st) + hm, G, STs, Hm)
        for h, sn, y in zip(hs, Sn, Y):
            s_scr[h] = sn
            mu = jnp.mean(y, axis=-1, keepdims=True)
            yc = y - mu
            var = jnp.mean(yc * yc, axis=-1, keepdims=True)
            y_ref[0, :, h * HD:(h + 1) * HD] = yc * lax.rsqrt(var + GN_EPS) * gng_ref[h] + gnb_ref[h]

    @pl.when(c == n_c - 1)
    def _():
        sfin_ref[0, 0] = s_scr[...]


def rwkv_scan(prep, masks, gng, gnb, s0t, tok_base, bsz, t_len):
    r, v, kk, ld, kd, bd = prep
    n_c = t_len // CH
    base_blk = tok_base // CH

    def blk(d, b, c):
        cc = jnp.where(d == 0, c, n_c - 1 - c)
        return b * n_c + cc

    hm_spec = pl.BlockSpec((HEADS, CH, HD), lambda d, b, c: (0, base_blk + blk(d, b, c), 0))
    hm2_spec = pl.BlockSpec((1, HEADS, CH, HD), lambda d, b, c: (d, 0, base_blk + blk(d, b, c), 0))
    in_specs = [hm_spec, hm_spec, hm_spec, hm2_spec, hm2_spec, hm2_spec,
                pl.BlockSpec((1, 6, CH, CH), lambda d, b, c: (d, 0, 0, 0)),
                pl.BlockSpec((HEADS, 1, HD), lambda d, b, c: (0, 0, 0)),
                pl.BlockSpec((HEADS, 1, HD), lambda d, b, c: (0, 0, 0))]
    args = [r, v, kk, ld, kd, bd, masks, gng, gnb]
    if s0t is not None:
        in_specs.append(pl.BlockSpec((1, 1, HEADS, HD, HD), lambda d, b, c: (b, d, 0, 0, 0)))
        args.append(s0t)
    return pl.pallas_call(
        functools.partial(_rwkv_chunk_kernel, s0t is not None),
        grid=(2, bsz, n_c),
        in_specs=in_specs,
        out_specs=[pl.BlockSpec((1, CH, D_MIX), lambda d, b, c: (d, blk(d, b, c), 0)),
                   pl.BlockSpec((1, 1, HEADS, HD, HD), lambda d, b, c: (b, d, 0, 0, 0))],
        out_shape=[jax.ShapeDtypeStruct((2, bsz * t_len, D_MIX), F32),
                   jax.ShapeDtypeStruct((bsz, 2, HEADS, HD, HD), F32)],
        scratch_shapes=[pltpu.VMEM((HEADS, HD, HD), F32)],
        compiler_params=_cparams(("arbitrary", "arbitrary", "arbitrary")),
        name="rwkv_scan",
    )(*args)


def _ctx_attn_kernel(q_ref, k_ref, v_ref, y_ref, ko_ref, vo_ref):
    scale = HD ** -0.5
    for h in range(HEADS):
        sl = slice(h * HD, (h + 1) * HD)
        q = q_ref[:, sl]
        k = k_ref[:, sl]
        v = v_ref[:, sl]
        ko_ref[0, h] = k
        vo_ref[0, h] = v
        s = _dot_nt(q.astype(BF16), k.astype(BF16)) * scale
        m = jnp.max(s, axis=-1, keepdims=True)
        e = jnp.exp(s - m)
        p = e / jnp.sum(e, axis=-1, keepdims=True)
        y_ref[:, sl] = _dot(p.astype(BF16), v.astype(BF16))


def ctx_attention(z, bsz, t_len):
    qb = A_PAD // D_MIX
    return pl.pallas_call(
        _ctx_attn_kernel,
        grid=(bsz,),
        in_specs=[pl.BlockSpec((t_len, D_MIX), lambda b: (b, qb)),
                  pl.BlockSpec((t_len, D_MIX), lambda b: (b, qb + 1)),
                  pl.BlockSpec((t_len, D_MIX), lambda b: (b, qb + 2))],
        out_specs=[pl.BlockSpec((t_len, D_MIX), lambda b: (b, 0)),
                   pl.BlockSpec((1, HEADS, t_len, HD), lambda b: (b, 0, 0, 0)),
                   pl.BlockSpec((1, HEADS, t_len, HD), lambda b: (b, 0, 0, 0))],
        out_shape=[jax.ShapeDtypeStruct((bsz * t_len, D_MIX), F32),
                   jax.ShapeDtypeStruct((bsz, HEADS, t_len, HD), F32),
                   jax.ShapeDtypeStruct((bsz, HEADS, t_len, HD), F32)],
        compiler_params=_cparams(("arbitrary",)),
        name="ctx_attention",
    )(z, z, z)


def _na_kernel(wr, rows, *refs):
    q_ref = refs[0]
    k_refs = refs[1:1 + wr]
    v_refs = refs[1 + wr:1 + 2 * wr]
    kc_ref, vc_ref, bias_ref, y_ref = refs[1 + 2 * wr:]
    r = pl.program_id(1)
    rs = jnp.clip(r - wr // 2, 0, rows - wr)
    base = rs - r + WIN_R - 1
    scale = HD ** -0.5
    kwin = jnp.concatenate([kr[...] for kr in k_refs], axis=0).astype(BF16)
    vwin = jnp.concatenate([vr[...] for vr in v_refs], axis=0).astype(BF16)
    for h in range(HEADS):
        sl = slice(h * HD, (h + 1) * HD)
        q = q_ref[:, sl].astype(BF16)
        s_loc = _dot_nt(q, kwin[:, sl]) * scale + bias_ref[h, base]
        s_ctx = _dot_nt(q, kc_ref[0, 0, h].astype(BF16)) * scale
        m = jnp.maximum(jnp.max(s_loc, axis=-1, keepdims=True), jnp.max(s_ctx, axis=-1, keepdims=True))
        e_loc = jnp.exp(s_loc - m)
        e_ctx = jnp.exp(s_ctx - m)
        den = jnp.sum(e_loc, axis=-1, keepdims=True) + jnp.sum(e_ctx, axis=-1, keepdims=True)
        p_loc = (e_loc / den).astype(BF16)
        p_ctx = (e_ctx / den).astype(BF16)
        y_ref[:, sl] = _dot(p_loc, vwin[:, sl]) + _dot(p_ctx, vc_ref[0, 0, h].astype(BF16))


def na_bias_table(rpb, wr):
    cq = np.arange(GRID_W)[:, None]
    ck = np.arange(GRID_W)[None, :]
    cs = np.clip(cq - WIN_C // 2, 0, GRID_W - WIN_C)
    col_bias = np.where((ck >= cs) & (ck < cs + WIN_C), 0.0, NEG).astype(np.float32)
    col_idx = np.clip(ck - cq + WIN_C - 1, 0, 2 * WIN_C - 2)
    rpb_col = rpb.astype(F32)[:, :, col_idx] + col_bias
    n_base = 2 * WIN_R - wr
    tabs = [jnp.concatenate([rpb_col[:, b + j] for j in range(wr)], axis=-1) for b in range(n_base)]
    return jnp.stack(tabs, axis=1)


def na_attention(z, k_ctx, v_ctx, layer, bias_tab, tok_base, bsz, t_len):
    rows = t_len // GRID_W
    wr = min(WIN_R, rows)
    qb = A_PAD // D_MIX
    base_blk = tok_base // GRID_W
    past = k_ctx.shape[3]

    def win_spec(j, col):
        return pl.BlockSpec(
            (GRID_W, D_MIX),
            lambda b, r: (base_blk + b * rows + jnp.clip(r - wr // 2, 0, rows - wr) + j, col))

    in_specs = ([pl.BlockSpec((GRID_W, D_MIX), lambda b, r: (base_blk + b * rows + r, qb))]
                + [win_spec(j, qb + 1) for j in range(wr)]
                + [win_spec(j, qb + 2) for j in range(wr)]
                + [pl.BlockSpec((1, 1, HEADS, past, HD), lambda b, r: (b, layer, 0, 0, 0)),
                   pl.BlockSpec((1, 1, HEADS, past, HD), lambda b, r: (b, layer, 0, 0, 0)),
                   pl.BlockSpec(bias_tab.shape, lambda b, r: (0, 0, 0, 0))])
    return pl.pallas_call(
        functools.partial(_na_kernel, wr, rows),
        grid=(bsz, rows),
        in_specs=in_specs,
        out_specs=pl.BlockSpec((GRID_W, D_MIX), lambda b, r: (b * rows + r, 0)),
        out_shape=jax.ShapeDtypeStruct((bsz * t_len, D_MIX), F32),
        compiler_params=_cparams(("arbitrary", "arbitrary"), VMEM_LIMIT),
        name="na_attention",
    )(*([z] * (1 + 2 * wr)), k_ctx, v_ctx, bias_tab)


def _s5_kernel(tb, u_ref, bm_ref, cm_ref, ar_ref, ai_ref, x0_ref, y_ref, xf_ref, x_scr, carry_scr):
    t_blk = pl.program_id(1)
    rows = u_ref.shape[2]

    @pl.when(t_blk == 0)
    def _():
        carry_scr[...] = x0_ref[0]

    u = u_ref[0].reshape(tb * rows, D_MIX).astype(BF16)
    x_scr[...] = _dot(u, bm_ref[0]).reshape(tb, rows, 2 * S5_N)
    lanes = min(S5_N, S5_CARRY_VREGS * SUBLANE * LANE // (2 * rows))
    for lc in range(S5_N // lanes):
        re = slice(lc * lanes, (lc + 1) * lanes)
        im = slice(S5_N + lc * lanes, S5_N + (lc + 1) * lanes)
        ar = ar_ref[0, :, re]
        ai = ai_ref[0, :, re]

        def step(t, x, re=re, im=im, ar=ar, ai=ai):
            xr, xi = x
            nr = ar * xr - ai * xi + x_scr[t, :, re]
            ni = ar * xi + ai * xr + x_scr[t, :, im]
            x_scr[t, :, re] = nr
            x_scr[t, :, im] = ni
            return nr, ni

        xr, xi = lax.fori_loop(0, tb, step, (carry_scr[:, re], carry_scr[:, im]), unroll=4)
        carry_scr[:, re] = xr
        carry_scr[:, im] = xi
    xs = x_scr[...].reshape(tb * rows, 2 * S5_N).astype(BF16)
    y_ref[0] = _dot(xs, cm_ref[...]).reshape(tb, rows, D_MIX)

    @pl.when(t_blk == pl.num_programs(1) - 1)
    def _():
        xf_ref[0] = carry_scr[...]


def s5_scan(u_tm, bmat, cmat, a_re, a_im, x0, tb):
    _, t_len, rows, _ = u_tm.shape
    return pl.pallas_call(
        functools.partial(_s5_kernel, tb),
        grid=(2, t_len // tb),
        in_specs=[pl.BlockSpec((1, tb, rows, D_MIX), lambda d, t: (d, t, 0, 0)),
                  pl.BlockSpec((1, D_MIX, 2 * S5_N), lambda d, t: (d, 0, 0)),
                  pl.BlockSpec((2 * S5_N, D_MIX), lambda d, t: (0, 0)),
                  pl.BlockSpec((1, 1, S5_N), lambda d, t: (d, 0, 0)),
                  pl.BlockSpec((1, 1, S5_N), lambda d, t: (d, 0, 0)),
                  pl.BlockSpec((1, rows, 2 * S5_N), lambda d, t: (d, 0, 0))],
        out_specs=[pl.BlockSpec((1, tb, rows, D_MIX), lambda d, t: (d, t, 0, 0)),
                   pl.BlockSpec((1, rows, 2 * S5_N), lambda d, t: (d, 0, 0))],
        out_shape=[jax.ShapeDtypeStruct(u_tm.shape, F32),
                   jax.ShapeDtypeStruct((2, rows, 2 * S5_N), F32)],
        scratch_shapes=[pltpu.VMEM((tb, rows, 2 * S5_N), F32), pltpu.VMEM((rows, 2 * S5_N), F32)],
        compiler_params=_cparams(("arbitrary", "arbitrary"), VMEM_LIMIT),
        name="s5_scan",
    )(u_tm, bmat, cmat, a_re, a_im, x0)


def _merge_kernel(gn_ref, bonus_ref, g_ref, yb_ref, yc_ref, u_ref, zg0_ref, zg1_ref, zg2_ref, x_ref, mod_ref,
                  s5d_ref, wglu_ref, bglu_ref, wb_ref, wout_ref, lng_ref, lnb_ref, x1_ref, h2_ref):
    ya = (gn_ref[0] + gn_ref[1] + bonus_ref[...]) * g_ref[...]
    yc = yc_ref[...] + s5d_ref[...] * u_ref[...]
    yc = 0.5 * yc * (1.0 + jnp.tanh(math.sqrt(2.0 / math.pi) * (yc + 0.044715 * (yc * yc * yc))))
    yc = yc * _sigmoid(_dot(yc.astype(BF16), wglu_ref[...]) + bglu_ref[...])
    merged = (_dot(ya.astype(BF16), wb_ref[0]) * _sigmoid(zg0_ref[...])
              + _dot(yb_ref[...].astype(BF16), wb_ref[1]) * _sigmoid(zg1_ref[...])
              + _dot(yc.astype(BF16), wb_ref[2]) * _sigmoid(zg2_ref[...]))
    mo = _dot(merged.astype(BF16), wout_ref[...])
    m = mod_ref[0]
    x1 = _layer_norm(DN_ALPHA * x_ref[...] + m[2:3] * mo, lng_ref[...], lnb_ref[...])
    x1_ref[...] = x1
    h2_ref[...] = (x1 * (1.0 + m[4:5]) + m[3:4]).astype(BF16)


def merge_branches(gn, bonus, g, yb, yc, z, x, mods, mod_of_tile, p):
    n = x.shape[0]
    row = lambda w, col=0: pl.BlockSpec((TM, w), lambda i, col=col: (i, col))
    full = lambda shape: pl.BlockSpec(shape, lambda i: (0,) * len(shape))
    gb = (A_PAD + 4 * D_MIX) // D_MODEL
    return pl.pallas_call(
        _merge_kernel,
        grid=(n // TM,),
        in_specs=[pl.BlockSpec((2, TM, D_MIX), lambda i: (0, i, 0)),
                  row(D_MIX), row(D_MIX), row(D_MIX), row(D_MIX),
                  row(D_MIX, A_PAD // D_MIX + 3),
                  row(D_MODEL, gb), row(D_MODEL, gb + 1), row(D_MODEL, gb + 2),
                  row(D_MODEL),
                  pl.BlockSpec((1, SUBLANE, D_MODEL), lambda i: (mod_of_tile(i, TM), 0, 0)),
                  full((1, D_MIX)), full((D_MIX, D_MIX)), full((1, D_MIX)),
                  full((3, D_MIX, D_MODEL)), full((D_MODEL, D_MODEL)),
                  full((1, D_MODEL)), full((1, D_MODEL))],
        out_specs=[row(D_MODEL), row(D_MODEL)],
        out_shape=[jax.ShapeDtypeStruct((n, D_MODEL), F32), jax.ShapeDtypeStruct((n, D_MODEL), BF16)],
        compiler_params=_cparams(("arbitrary",), VMEM_LIMIT),
        name="merge_branches",
    )(gn, bonus, g, yb, yc, z, z, z, z, x, mods,
      p['s5_d'], p['w_glu'], p['b_glu'], p['w_branch'], p['w_out'], p['ln1_g'], p['ln1_b'])


def _first_max(val, idx, big):
    m = jnp.max(jnp.max(val, axis=1, keepdims=True), axis=0, keepdims=True)
    cand = jnp.where(val == m, idx, big)
    first = jnp.min(jnp.min(cand, axis=1, keepdims=True), axis=0, keepdims=True)
    return m, idx == first


def _router_kernel(h_ref, wt_ref, bias_ref, gates_ref):
    per = N_EXPERTS // N_GROUPS
    logits = _dot_nt(wt_ref[...], h_ref[...])
    n = logits.shape[1]
    scores = _sigmoid(logits).reshape(N_GROUPS, per, n)
    sel = scores + bias_ref[...]
    e_idx = (lax.broadcasted_iota(jnp.int32, (N_GROUPS, per, n), 0) * per
             + lax.broadcasted_iota(jnp.int32, (N_GROUPS, per, n), 1))
    in_grp = lax.broadcasted_iota(jnp.int32, (N_GROUPS, per, n), 1)
    m1 = jnp.max(sel, axis=1, keepdims=True)
    first = jnp.min(jnp.where(sel == m1, in_grp, per), axis=1, keepdims=True)
    m2 = jnp.max(jnp.where(in_grp == first, -jnp.inf, sel), axis=1, keepdims=True)
    grp = m1 + m2
    g_idx = lax.broadcasted_iota(jnp.int32, (N_GROUPS, 1, n), 0)
    gmask = jnp.zeros((N_GROUPS, 1, n), F32)
    for _ in range(TOPK_GROUPS):
        _, hit = _first_max(grp, g_idx, N_GROUPS)
        gmask = jnp.where(hit, 1.0, gmask)
        grp = jnp.where(hit, -jnp.inf, grp)
    cur = jnp.where(jnp.broadcast_to(gmask, sel.shape) > 0.0, sel, NEG)
    w = jnp.zeros((N_GROUPS, per, n), F32)
    for _ in range(TOP_K):
        _, hit = _first_max(cur, e_idx, N_EXPERTS)
        w = jnp.where(hit, scores, w)
        cur = jnp.where(hit, -jnp.inf, cur)
    tot = jnp.sum(jnp.sum(w, axis=1, keepdims=True), axis=0, keepdims=True)
    gates = (ROUTED_SCALE * w / tot).reshape(N_EXPERTS, n)
    gates_ref[...] = gates.T


def moe_router(h2, router_wt, router_bias):
    n = h2.shape[0]
    return pl.pallas_call(
        _router_kernel,
        grid=(n // TM,),
        in_specs=[pl.BlockSpec((TM, D_MODEL), lambda i: (i, 0)),
                  pl.BlockSpec((N_EXPERTS, D_MODEL), lambda i: (0, 0)),
                  pl.BlockSpec((N_GROUPS, N_EXPERTS // N_GROUPS, 1), lambda i: (0, 0, 0))],
        out_specs=pl.BlockSpec((TM, N_EXPERTS), lambda i: (i, 0)),
        out_shape=jax.ShapeDtypeStruct((n, N_EXPERTS), F32),
        compiler_params=_cparams(("arbitrary",)),
        name="moe_router",
    )(h2, router_wt, router_bias)


def _moe_kernel(ec, h_ref, gates_ref, x1_ref, mod_ref, wg_ref, wu_ref, wd_ref, ex_ref,
                sg_ref, su_ref, sd_ref, lng_ref, lnb_ref, out_ref, acc_scr):
    j = pl.program_id(1)
    h = h_ref[...]

    @pl.when(j == 0)
    def _():
        sh = _dot(h, sg_ref[...])
        sh = sh * _sigmoid(sh) * _dot(h, su_ref[...])
        acc_scr[...] = _dot(sh.astype(BF16), sd_ref[...])

    gates = gates_ref[...]
    g_hi = gates.astype(BF16)
    g_lo = (gates - g_hi.astype(F32)).astype(BF16)
    gexp = _dot(g_hi, ex_ref[0]) + _dot(g_lo, ex_ref[0])
    hg = _dot(h, wg_ref[...])
    hu = _dot(h, wu_ref[...])
    act = hg * _sigmoid(hg) * hu * gexp
    acc_scr[...] += _dot(act.astype(BF16), wd_ref[...])

    @pl.when(j == pl.num_programs(1) - 1)
    def _():
        m = mod_ref[0]
        out_ref[...] = _layer_norm(DN_ALPHA * x1_ref[...] + m[5:6] * acc_scr[...], lng_ref[...], lnb_ref[...])


def moe_ffn(h2, gates, x1, mods, mod_of_tile, p, tm=512, ec=8):
    n = h2.shape[0]
    wcols = ec * D_EXPERT
    n_j = N_EXPERTS // ec
    full = lambda shape: pl.BlockSpec(shape, lambda i, j: (0,) * len(shape))
    return pl.pallas_call(
        functools.partial(_moe_kernel, ec),
        grid=(n // tm, n_j),
        in_specs=[pl.BlockSpec((tm, D_MODEL), lambda i, j: (i, 0)),
                  pl.BlockSpec((tm, N_EXPERTS), lambda i, j: (i, 0)),
                  pl.BlockSpec((tm, D_MODEL), lambda i, j: (i, 0)),
                  pl.BlockSpec((1, SUBLANE, D_MODEL), lambda i, j: (mod_of_tile(i, tm), 0, 0)),
                  pl.BlockSpec((D_MODEL, wcols), lambda i, j: (0, j)),
                  pl.BlockSpec((D_MODEL, wcols), lambda i, j: (0, j)),
                  pl.BlockSpec((wcols, D_MODEL), lambda i, j: (j, 0)),
                  pl.BlockSpec((1, N_EXPERTS, wcols), lambda i, j: (j, 0, 0)),
                  full((D_MODEL, D_SHARED)), full((D_MODEL, D_SHARED)), full((D_SHARED, D_MODEL)),
                  full((1, D_MODEL)), full((1, D_MODEL))],
        out_specs=pl.BlockSpec((tm, D_MODEL), lambda i, j: (i, 0)),
        out_shape=jax.ShapeDtypeStruct((n, D_MODEL), F32),
        scratch_shapes=[pltpu.VMEM((tm, D_MODEL), F32)],
        compiler_params=_cparams(("arbitrary", "arbitrary"), VMEM_LIMIT),
        name="moe_ffn",
    )(h2, gates, x1, mods, p['wg'], p['wu'], p['wd'], p['expand'],
      p['sh_g'], p['sh_u'], p['sh_d'], p['ln2_g'], p['ln2_b'])


def _block_diag2(m):
    z = jnp.zeros_like(m[0])
    return jnp.concatenate([jnp.concatenate([m[0], z], axis=1), jnp.concatenate([z, m[1]], axis=1)], axis=0)


def _layer_params(P, l):
    g = lambda name: P[name][l]
    w_in = g('w_in')
    pad = jnp.zeros((D_MODEL, A_PAD - A_COLS), F32)
    p = {}
    p['w_in'] = jnp.concatenate([w_in[:, :A_COLS], pad, w_in[:, A_COLS:]], axis=1).astype(BF16)
    p['mu'] = jnp.concatenate([g('rwkv_mu'), jnp.zeros((A_PAD - A_COLS,), F32)])[None, :]
    p['w2bd'] = _block_diag2(g('rwkv_w2')).astype(BF16)
    p['a2bd'] = _block_diag2(g('rwkv_a2')).astype(BF16)
    p['g2'] = g('rwkv_g2').astype(BF16)
    p['w0'] = g('rwkv_w0').reshape(1, 2 * D_MIX)
    p['a0'] = g('rwkv_a0').reshape(1, 2 * D_MIX)
    p['kkp'] = g('rwkv_kk')[None, :]
    p['ka'] = g('rwkv_ka')[None, :]
    p['rk'] = g('rwkv_rk')[None, :]
    hid = np.arange(D_MIX) // HD
    p['seg_ones'] = jnp.asarray((hid[:, None] == hid[None, :]).astype(np.float32))
    p['gng'] = g('rwkv_gn_g').reshape(HEADS, 1, HD)
    p['gnb'] = g('rwkv_gn_b').reshape(HEADS, 1, HD)
    a = lax.complex(g('s5_a_re'), g('s5_a_im'))
    dt = jnp.exp(g('s5_log_dt'))[..., None]
    a_bar = jnp.exp(dt * a)
    b_bar = ((a_bar - 1.0) / a)[..., None] * lax.complex(g('s5_b_re'), g('s5_b_im'))
    eye_g = jnp.eye(C_GROUPS, dtype=F32)
    bd = lambda m: jnp.einsum('dgph,gk->dghkp', m, eye_g).reshape(2, D_MIX, S5_N)
    p['s5_bmat'] = jnp.concatenate([bd(b_bar.real), bd(b_bar.imag)], axis=-1).astype(BF16)
    cd = lambda m: jnp.einsum('ghp,gk->kpgh', m, eye_g).reshape(S5_N, D_MIX)
    p['s5_cmat'] = jnp.concatenate([cd(g('s5_c_re')), -cd(g('s5_c_im'))], axis=0).astype(BF16)
    p['s5_ar'] = a_bar.real.reshape(2, 1, S5_N)
    p['s5_ai'] = a_bar.imag.reshape(2, 1, S5_N)
    p['s5_d'] = g('s5_d')[None, :]
    p['w_glu'] = g('s5_w_glu').astype(BF16)
    p['b_glu'] = g('s5_b_glu')[None, :]
    p['w_branch'] = g('w_branch').astype(BF16)
    p['w_out'] = g('w_out').astype(BF16)
    p['ln1_g'] = g('ln1_g')[None, :]
    p['ln1_b'] = g('ln1_b')[None, :]
    p['router_wt'] = g('router_w').T.astype(BF16)
    p['router_bias'] = g('router_bias').reshape(N_GROUPS, N_EXPERTS // N_GROUPS, 1)
    ecols = N_EXPERTS * D_EXPERT
    p['wg'] = g('exp_w_gate').transpose(1, 0, 2).reshape(D_MODEL, ecols).astype(BF16)
    p['wu'] = g('exp_w_up').transpose(1, 0, 2).reshape(D_MODEL, ecols).astype(BF16)
    p['wd'] = g('exp_w_down').reshape(ecols, D_MODEL).astype(BF16)
    p['sh_g'] = g('sh_w_gate').astype(BF16)
    p['sh_u'] = g('sh_w_up').astype(BF16)
    p['sh_d'] = g('sh_w_down').astype(BF16)
    p['ln2_g'] = g('ln2_g')[None, :]
    p['ln2_b'] = g('ln2_b')[None, :]
    return p


def _expand_table(ec):
    n_j = N_EXPERTS // ec
    t = np.zeros((n_j, N_EXPERTS, ec * D_EXPERT), np.float32)
    for e in range(N_EXPERTS):
        j, q = divmod(e, ec)
        t[j, e, q * D_EXPERT:(q + 1) * D_EXPERT] = 1.0
    return jnp.asarray(t, dtype=BF16)


def kernel(x_prompt, x_sample, c, cache_na_k, cache_na_v, state_rwkv, state_s5_re, state_s5_im, c_ctx, w_ada, b_ada, w_in, rwkv_mu, rwkv_w0, rwkv_w2, rwkv_a0, rwkv_a2, rwkv_g2, rwkv_kk, rwkv_ka, rwkv_rk, rwkv_gn_g, rwkv_gn_b, na_rpb, s5_a_re, s5_a_im, s5_log_dt, s5_b_re, s5_b_im, s5_c_re, s5_c_im, s5_d, s5_w_glu, s5_b_glu, w_branch, w_out, ln1_g, ln1_b, router_w, router_bias, exp_w_gate, exp_w_up, exp_w_down, sh_w_gate, sh_w_up, sh_w_down, ln2_g, ln2_b):
    P = dict(w_in=w_in, rwkv_mu=rwkv_mu, rwkv_w0=rwkv_w0, rwkv_w2=rwkv_w2, rwkv_a0=rwkv_a0, rwkv_a2=rwkv_a2,
             rwkv_g2=rwkv_g2, rwkv_kk=rwkv_kk, rwkv_ka=rwkv_ka, rwkv_rk=rwkv_rk, rwkv_gn_g=rwkv_gn_g,
             rwkv_gn_b=rwkv_gn_b, s5_a_re=s5_a_re, s5_a_im=s5_a_im, s5_log_dt=s5_log_dt, s5_b_re=s5_b_re,
             s5_b_im=s5_b_im, s5_c_re=s5_c_re, s5_c_im=s5_c_im, s5_d=s5_d, s5_w_glu=s5_w_glu,
             s5_b_glu=s5_b_glu, w_branch=w_branch, w_out=w_out, ln1_g=ln1_g, ln1_b=ln1_b, router_w=router_w,
             router_bias=router_bias, exp_w_gate=exp_w_gate, exp_w_up=exp_w_up, exp_w_down=exp_w_down,
             sh_w_gate=sh_w_gate, sh_w_up=sh_w_up, sh_w_down=sh_w_down, ln2_g=ln2_g, ln2_b=ln2_b)
    bc, tc, _ = x_prompt.shape
    bl, tl, _ = x_sample.shape
    n_ctx = bc * tc
    n_lat = bl * tl
    n = n_ctx + n_lat
    depth = w_in.shape[0]

    def mod_of_tile(i, tm):
        tok = i * tm
        return jnp.where(tok < n_ctx, 0, 1 + (tok - n_ctx) // tl)

    tile_tok = np.arange(n // TM) * TM
    is_ctx = tile_tok < n_ctx
    seq_tiles = np.concatenate([
        np.where(is_ctx, tc // TM, tl // TM),
        np.where(is_ctx, (tile_tok % tc) // TM, ((tile_tok - n_ctx) % tl) // TM)]).astype(np.int32)
    seq_tiles = jnp.asarray(seq_tiles)

    cond = jnp.concatenate([c_ctx[None, :], c, jnp.zeros((SUBLANE - 1 - bl, D_MODEL), F32)], axis=0)
    ada = ada_modulation(cond, w_ada, b_ada)
    masks = jnp.asarray(_chunk_masks())
    lat_rows = t_rows = tl // GRID_W
    wr = min(WIN_R, lat_rows)
    expand = _expand_table(8)

    x = jnp.concatenate([x_prompt.reshape(n_ctx, D_MODEL), x_sample.reshape(n_lat, D_MODEL)], axis=0)
    new_k, new_v, new_rwkv, new_s5 = [], [], [], []
    r_lat = SUBLANE
    for l in range(depth):
        p = _layer_params(P, l)
        p['expand'] = expand
        mods = ada[l, :1 + bl].reshape(1 + bl, 6, D_MODEL)
        mods = jnp.concatenate([mods, jnp.zeros((1 + bl, SUBLANE - 6, D_MODEL), F32)], axis=1)

        z = in_projection(x, mods, p['w_in'], mod_of_tile)

        r, v, kk, ld, kd, bd, g, bonus = rwkv_prep(z, seq_tiles, p)
        prep = (r, v, kk, ld, kd, bd)
        gn_c, s_c = rwkv_scan(prep, masks, p['gng'], p['gnb'], None, 0, bc, tc)
        s0t = jnp.swapaxes(state_rwkv[:, l], -1, -2)
        gn_l, _ = rwkv_scan(prep, masks, p['gng'], p['gnb'], s0t, n_ctx, bl, tl)
        gn = jnp.concatenate([gn_c, gn_l], axis=1)
        new_rwkv.append(jnp.swapaxes(s_c, -1, -2))

        yb_c, k_h, v_h = ctx_attention(z, bc, tc)
        new_k.append(k_h)
        new_v.append(v_h)
        bias_tab = na_bias_table(na_rpb[l], wr)
        yb_l = na_attention(z, cache_na_k, cache_na_v, l, bias_tab, n_ctx, bl, tl)
        yb = jnp.concatenate([yb_c, yb_l], axis=0)

        u = z[:, A_PAD + 3 * D_MIX:A_PAD + 4 * D_MIX]
        u_c = jnp.swapaxes(u[:n_ctx].reshape(bc, tc, D_MIX), 0, 1)
        u_c = jnp.stack([u_c, u_c[::-1]])
        zero_c = jnp.zeros((2, bc, 2 * S5_N), F32)
        y_c, xf_c = s5_scan(u_c, p['s5_bmat'], p['s5_cmat'], p['s5_ar'], p['s5_ai'], zero_c, tb=512 // bc)
        yc_c = jnp.swapaxes(y_c[0] + y_c[1][::-1], 0, 1).reshape(n_ctx, D_MIX)
        u_l = jnp.swapaxes(u[n_ctx:].reshape(bl, tl, D_MIX), 0, 1)
        u_l = jnp.pad(u_l, ((0, 0), (0, r_lat - bl), (0, 0)))
        u_l = jnp.stack([u_l, u_l[::-1]])
        x0_l = jnp.concatenate([state_s5_re[:, l].reshape(bl, 2, S5_N), state_s5_im[:, l].reshape(bl, 2, S5_N)],
                               axis=-1)
        x0_l = jnp.pad(jnp.swapaxes(x0_l, 0, 1), ((0, 0), (0, r_lat - bl), (0, 0)))
        y_l, _ = s5_scan(u_l, p['s5_bmat'], p['s5_cmat'], p['s5_ar'], p['s5_ai'], x0_l, tb=512 // r_lat)
        yc_l = jnp.swapaxes((y_l[0] + y_l[1][::-1])[:, :bl], 0, 1).reshape(n_lat, D_MIX)
        yc = jnp.concatenate([yc_c, yc_l], axis=0)
        new_s5.append(jnp.swapaxes(xf_c, 0, 1))

        x1, h2 = merge_branches(gn, bonus, g, yb, yc, z, x, mods, mod_of_tile, p)
        gates = moe_router(h2, p['router_wt'], p['router_bias'])
        x = moe_ffn(h2, gates, x1, mods, mod_of_tile, p)

    s5_all = jnp.stack(new_s5, axis=1)
    s5_re = s5_all[..., :S5_N].reshape(bc, depth, 2, C_GROUPS, C_STATE)
    s5_im = s5_all[..., S5_N:].reshape(bc, depth, 2, C_GROUPS, C_STATE)
    return (x[:n_ctx].reshape(bc, tc, D_MODEL), x[n_ctx:].reshape(bl, tl, D_MODEL),
            jnp.stack(new_k, axis=1), jnp.stack(new_v, axis=1), jnp.stack(new_rwkv, axis=1), s5_re, s5_im)
```

```python
import functools
import math

import numpy as np
import jax
import jax.numpy as jnp
from jax import lax
from jax.experimental import pallas as pl
from jax.experimental.pallas import tpu as pltpu

F32 = jnp.float32
BF16 = jnp.bfloat16
HIGHEST = lax.Precision.HIGHEST

D_MODEL = 1024
DEPTH = 2
GRID_W = 64
D_MIX = 512
HEADS = 8
HD = 64
LORA = 128
WIN_R = 8
WIN_C = 16
C_GROUP = 16
C_GROUPS = D_MIX // C_GROUP
C_STATE = 64
S5_N = C_GROUPS * C_STATE
S5_ROWS = 512
S5_CARRY_VREGS = 32
N_EXPERTS = 64
TOP_K = 8
N_GROUPS = 8
TOPK_GROUPS = 4
D_EXPERT = 128
D_SHARED = 128
ROUTED_SCALE = 2.5
LN_EPS = 1e-5
GN_EPS = 64e-5
NEG = -1e30
DN_ALPHA = (2 * DEPTH) ** 0.25
A_COLS = 3 * D_MIX + 3 * LORA
A_PAD = 2048
IN_COLS_P = A_PAD + 4 * D_MIX + 3 * D_MODEL

LANE = 128
SUBLANE = 8
TM = 256
CH = 64
RWKV_CHUNKS = 2
VMEM_LIMIT = 56 * 1024 * 1024


def _cparams(sem, vmem=None):
    return pltpu.CompilerParams(dimension_semantics=sem, vmem_limit_bytes=vmem)


def _sigmoid(x):
    return 1.0 / (1.0 + jnp.exp(-x))


def _dot(a, b, precision=None):
    return jnp.dot(a, b, preferred_element_type=F32, precision=precision)


def _dot_nt(a, b, precision=None):
    return lax.dot_general(a, b, (((1,), (1,)), ((), ())), preferred_element_type=F32, precision=precision)


def _dot_tn(a, b, precision=None):
    return lax.dot_general(a, b, (((0,), (0,)), ((), ())), preferred_element_type=F32, precision=precision)


def _split2(x):
    hi = x.astype(BF16)
    return hi, (x - hi.astype(F32)).astype(BF16)


def _split3(x):
    hi = x.astype(BF16)
    r1 = x - hi.astype(F32)
    mid = r1.astype(BF16)
    return hi, mid, (r1 - mid.astype(F32)).astype(BF16)


def _mm3(a, b, dot=_dot):
    return dot(a[0], b[0]) + (dot(a[1], b[0]) + dot(a[0], b[1]))


def _mm3_nt(a, b):
    return _mm3(a, b, _dot_nt)


def _layer_norm(x, g, b):
    mu = jnp.mean(x, axis=-1, keepdims=True)
    xc = x - mu
    var = jnp.mean(xc * xc, axis=-1, keepdims=True)
    return xc * lax.rsqrt(var + LN_EPS) * g + b


def _ada_kernel(c_ref, w_ref, b_ref, o_ref):
    c = c_ref[...]
    s = c * _sigmoid(c)
    o_ref[0] = _dot(s.astype(BF16), w_ref[0]) + b_ref[0]


def ada_modulation(cond, w_ada, b_ada):
    n_l, d, n6 = w_ada.shape
    tn = 1536
    return pl.pallas_call(
        _ada_kernel,
        grid=(n_l, n6 // tn),
        in_specs=[pl.BlockSpec((SUBLANE, d), lambda l, j: (0, 0)),
                  pl.BlockSpec((1, d, tn), lambda l, j: (l, 0, j)),
                  pl.BlockSpec((1, 1, tn), lambda l, j: (l, 0, j))],
        out_specs=pl.BlockSpec((1, SUBLANE, tn), lambda l, j: (l, 0, j)),
        out_shape=jax.ShapeDtypeStruct((n_l, SUBLANE, n6), F32),
        compiler_params=_cparams(("arbitrary", "arbitrary")),
        name="ada_modulation",
    )(cond, w_ada.astype(BF16), b_ada.reshape(n_l, 1, n6))


def _inproj_kernel(x_ref, mod_ref, w_ref, z_ref, h_scr):
    @pl.when(pl.program_id(1) == 0)
    def _():
        m = mod_ref[0]
        h_scr[...] = (x_ref[...] * (1.0 + m[1:2]) + m[0:1]).astype(BF16)
    z_ref[...] = _dot(h_scr[...], w_ref[...])


def in_projection(x, mods, w_in_p, mod_of_tile, tm=512, tn=1024):
    n, d = x.shape
    cols = w_in_p.shape[1]
    return pl.pallas_call(
        _inproj_kernel,
        grid=(n // tm, cols // tn),
        in_specs=[pl.BlockSpec((tm, d), lambda i, j: (i, 0)),
                  pl.BlockSpec((1, SUBLANE, d), lambda i, j: (mod_of_tile(i, tm), 0, 0)),
                  pl.BlockSpec((d, tn), lambda i, j: (0, j))],
        out_specs=pl.BlockSpec((tm, tn), lambda i, j: (i, j)),
        out_shape=jax.ShapeDtypeStruct((n, cols), F32),
        scratch_shapes=[pltpu.VMEM((tm, d), BF16)],
        compiler_params=_cparams(("arbitrary", "arbitrary"), VMEM_LIMIT),
        name="in_projection",
    )(x, mods, w_in_p)


def _rwkv_prep_kernel(seq_tiles_ref, z_ref, zp_ref, zn_ref, mu_ref, w2_ref, a2_ref, g2_ref, w0_ref, a0_ref,
                      kkp_ref, ka_ref, rk_ref, e_ref,
                      r_ref, v_ref, kk_ref, ld_ref, kd_ref, bd_ref, g_ref, bonus_ref):
    i = pl.program_id(0)
    tiles = seq_tiles_ref[i]
    pos = seq_tiles_ref[i + pl.num_programs(0)]
    x = z_ref[...]
    tm = x.shape[0]
    rows = lax.broadcasted_iota(jnp.int32, x.shape, 0)
    prev_row = jnp.where(pos == 0, 0.0, zp_ref[SUBLANE - 1:SUBLANE, :])
    next_row = jnp.where(pos == tiles - 1, 0.0, zn_ref[0:1, :])
    xm1 = jnp.where(rows == 0, prev_row, pltpu.roll(x, 1, axis=0))
    xp1 = jnp.where(rows == tm - 1, next_row, pltpu.roll(x, tm - 1, axis=0))
    za = x + mu_ref[...] * (0.5 * (xm1 + xp1) - x)

    r = za[:, 0:D_MIX]
    k = za[:, D_MIX:2 * D_MIX]
    v = za[:, 2 * D_MIX:3 * D_MIX]
    lw = za[:, 3 * D_MIX:3 * D_MIX + LORA]
    la = za[:, 3 * D_MIX + LORA:3 * D_MIX + 2 * LORA]
    lg = za[:, 3 * D_MIX + 2 * LORA:3 * D_MIX + 3 * LORA]

    w_both = w0_ref[...] + _dot(jnp.tanh(lw).astype(BF16), w2_ref[...])
    a_both = _sigmoid(a0_ref[...] + _dot(la.astype(BF16), a2_ref[...]))
    g_ref[...] = _dot(_sigmoid(lg).astype(BF16), g2_ref[...])

    e = e_ref[...]
    kks = k * kkp_ref[...]
    nrm = jnp.sqrt(_dot(kks * kks, e, HIGHEST))
    kk = kks / jnp.maximum(nrm, 1e-12)
    bonus = jnp.zeros_like(v)
    for h in range(HEADS):
        sl = slice(h * HD, (h + 1) * HD)
        r_ref[h] = r[:, sl]
        v_ref[h] = v[:, sl]
        kk_ref[h] = kk[:, sl]
    for d in range(2):
        w = w_both[:, d * D_MIX:(d + 1) * D_MIX]
        a = a_both[:, d * D_MIX:(d + 1) * D_MIX]
        ld = -math.exp(-0.5) * _sigmoid(w)
        kd = k * (1.0 + (a - 1.0) * ka_ref[...])
        bd = kk * a
        bonus = bonus + _dot(r * kd * rk_ref[...], e, HIGHEST) * v
        for h in range(HEADS):
            sl = slice(h * HD, (h + 1) * HD)
            ld_ref[d, h] = ld[:, sl]
            kd_ref[d, h] = kd[:, sl]
            bd_ref[d, h] = bd[:, sl]
    bonus_ref[...] = bonus


def rwkv_prep(z, seq_tiles, p):
    n = z.shape[0]
    nt = n // TM
    halo = TM // SUBLANE
    nb8 = n // SUBLANE
    hm = jax.ShapeDtypeStruct((HEADS, n, HD), F32)
    hm2 = jax.ShapeDtypeStruct((2, HEADS, n, HD), F32)
    tok = jax.ShapeDtypeStruct((n, D_MIX), F32)
    full = lambda shape: pl.BlockSpec(shape, lambda i, s: (0,) * len(shape))
    hm_spec = pl.BlockSpec((HEADS, TM, HD), lambda i, s: (0, i, 0))
    hm2_spec = pl.BlockSpec((2, HEADS, TM, HD), lambda i, s: (0, 0, i, 0))
    tok_spec = pl.BlockSpec((TM, D_MIX), lambda i, s: (i, 0))
    grid_spec = pltpu.PrefetchScalarGridSpec(
        num_scalar_prefetch=1,
        grid=(nt,),
        in_specs=[pl.BlockSpec((TM, A_PAD), lambda i, s: (i, 0)),
                  pl.BlockSpec((SUBLANE, A_PAD), lambda i, s: (jnp.maximum(i * halo - 1, 0), 0)),
                  pl.BlockSpec((SUBLANE, A_PAD), lambda i, s: (jnp.minimum((i + 1) * halo, nb8 - 1), 0)),
                  full((1, A_PAD)), full((LORA, 2 * D_MIX)), full((LORA, 2 * D_MIX)), full((LORA, D_MIX)),
                  full((1, 2 * D_MIX)), full((1, 2 * D_MIX)), full((1, D_MIX)), full((1, D_MIX)),
                  full((1, D_MIX)), full((D_MIX, D_MIX))],
        out_specs=[hm_spec, hm_spec, hm_spec, hm2_spec, hm2_spec, hm2_spec, tok_spec, tok_spec],
    )
    return pl.pallas_call(
        _rwkv_prep_kernel,
        grid_spec=grid_spec,
        out_shape=[hm, hm, hm, hm2, hm2, hm2, tok, tok],
        compiler_params=_cparams(("arbitrary",), VMEM_LIMIT),
        name="rwkv_prep",
    )(seq_tiles, z, z, z, p['mu'], p['w2bd'], p['a2bd'], p['g2'], p['w0'], p['a0'],
      p['kkp'], p['ka'], p['rk'], p['seg_ones'])


def _chunk_masks():
    t = np.arange(CH)
    fwd_incl = (t[:, None] >= t[None, :])
    out = []
    for incl in (fwd_incl, fwd_incl.T):
        strict = incl & (t[:, None] != t[None, :])
        ms = [incl, strict, strict & ((t[:, None] // 8) == (t[None, :] // 8))]
        for m in (8, 16, 32):
            ms.append(strict & ((t[:, None] // (2 * m)) == (t[None, :] // (2 * m)))
                      & ((t[:, None] // m) != (t[None, :] // m)))
        out.append(np.stack(ms))
    return np.stack(out).astype(np.float32)


def _rwkv_chunk_kernel(cb, sched_ref, *refs):
    dir_refs = (refs[0:6], refs[6:12])
    m_ref, gng_ref, gnb_ref, s0_ref = refs[12:16]
    y_refs = refs[16:18]
    sfin_ref, s_scr = refs[18:20]
    step_id = pl.program_id(0)
    first = sched_ref[_SCHED_FIRST, step_id]

    @pl.when(first == 1)
    def _():
        s_scr[...] = s0_ref[0]

    ri = lax.broadcasted_iota(jnp.int32, (CH, CH), 0)
    ci = lax.broadcasted_iota(jnp.int32, (CH, CH), 1)
    eye = (ri == ci).astype(F32)

    bf = lambda x: x.astype(BF16)
    each = lambda f, *cols: [f(*args) for args in zip(*cols)]
    rows = lambda j: slice(j * CH, (j + 1) * CH)

    chains = [(d, j, h) for d in range(2) for j in range(cb) for h in range(HEADS)]
    msk = lambda k: [m_ref[d, k] for d, _, _ in chains]
    incl, strict, m8 = msk(0), msk(1), msk(2)
    incl_b = each(bf, incl)
    get = lambda k: [dir_refs[d][k][h, rows(j), :] if k < 3 else dir_refs[d][k][0, h, rows(j), :]
                     for d, j, h in chains]
    R, V, KK, LD, Kd, Bd = (get(k) for k in range(6))
    L = each(lambda m, x: sum(_dot(m, part) for part in _split3(x)), incl_b, LD)
    ltot = each(lambda x: jnp.sum(x, axis=0, keepdims=True), LD)
    e_nl = each(lambda l: jnp.exp(-l), L)
    e_rest = each(lambda l, lt: jnp.exp(lt - l), L, ltot)
    Qb = each(lambda kk, l, ld: bf(kk * jnp.exp(l - ld)), KK, L, LD)
    Rh = each(lambda r, l: r * jnp.exp(l), R, L)
    Rb = each(bf, Rh)
    Btb = each(lambda b, e: bf(b * e), Bd, e_nl)
    Ktb = each(lambda k, e: bf(k * e), Kd, e_nl)
    BcTb = each(lambda b, e: bf((b * e).T), Bd, e_rest)
    KcTb = each(lambda k, e: bf((k * e).T), Kd, e_rest)
    Vb = each(bf, V)
    Nl = each(lambda m, q, b: m * _dot_nt(q, b), strict, Qb, Btb)
    Mkb = each(lambda m, q, k: bf(m * _dot_nt(q, k)), strict, Qb, Ktb)
    Mrbb = each(lambda m, r, b: bf(m * _dot_nt(r, b)), incl, Rb, Btb)
    Mrkb = each(lambda m, r, k: bf(m * _dot_nt(r, k)), incl, Rb, Ktb)
    N8 = each(lambda m, n: m * n, m8, Nl)
    N8b = each(bf, N8)
    N2 = each(lambda a: _dot(a, a), N8b)
    N2b = each(bf, N2)
    N4 = each(lambda a: _dot(a, a), N2b)
    W = each(lambda a, b: _dot(bf(eye - a), bf(eye + b)), N8, N2)
    W = each(lambda w, n4: _dot(bf(w), bf(eye + n4)), W, N4)
    for lvl in range(3):
        Wb = each(bf, W)
        T = each(lambda m, n, w: _dot(bf(m * n), w), msk(3 + lvl), Nl, Wb)
        W = each(lambda w, wb, t: w - _dot(wb, bf(t)), W, Wb, T)
    Wb = each(bf, W)
    Whb = each(lambda w, q: bf(_dot(w, q)), Wb, Qb)
    MkVb = each(lambda m, v: bf(_dot(m, v)), Mkb, Vb)
    U0b = each(lambda w, t: bf(-_dot(w, t)), Wb, MkVb)
    Rbar = each(lambda rh, m, wh: _split2(rh - _dot(m, wh)), Rh, Mrbb, Whb)
    Y0 = each(lambda mb, u, mk, v: _dot(mb, u) + _dot(mk, v), Mrbb, U0b, Mrkb, Vb)
    G = each(lambda lt, b, wh: _split2(eye * jnp.exp(lt) - _dot(b, wh)), ltot, BcTb, Whb)
    Hm = each(lambda b, u, k, v: _dot(b, u) + _dot(k, v), BcTb, U0b, KcTb, Vb)

    lanes = [(d, h) for d in range(2) for h in range(HEADS)]
    ST = [s_scr[d, h] for d, h in lanes]
    for step in range(cb):
        idx = [chains.index((d, step if d == 0 else cb - 1 - step, h)) for d, h in lanes]
        STs = each(_split2, ST)
        Y = [_mm3(Rbar[i], st) + Y0[i] for i, st in zip(idx, STs)]
        ST = [_mm3(G[i], st) + Hm[i] for i, st in zip(idx, STs)]
        for i, y in zip(idx, Y):
            d, j, h = chains[i]
            mu = jnp.mean(y, axis=-1, keepdims=True)
            yc = y - mu
            var = jnp.mean(yc * yc, axis=-1, keepdims=True)
            y_refs[d][rows(j), h * HD:(h + 1) * HD] = yc * lax.rsqrt(var + GN_EPS) * gng_ref[h] + gnb_ref[h]
    for (d, h), st in zip(lanes, ST):
        s_scr[d, h] = st

    @pl.when(sched_ref[_SCHED_LAST, step_id] == 1)
    def _():
        sfin_ref[0] = s_scr[...]


_SCHED_FWD, _SCHED_BWD, _SCHED_SEQ, _SCHED_FIRST, _SCHED_LAST = range(5)


def rwkv_schedule(seq_lens):
    blk_rows = RWKV_CHUNKS * CH
    cols, base = [], 0
    for s, t_len in enumerate(seq_lens):
        n_b = t_len // blk_rows
        for i in range(n_b):
            cols.append((base + i, base + n_b - 1 - i, s, int(i == 0), int(i == n_b - 1)))
        base += n_b
    return np.asarray(cols, np.int32).T


def rwkv_scan(prep, masks, gng, gnb, s0t, sched):
    r, v, kk, ld, kd, bd = prep
    n = r.shape[1]
    n_seq = s0t.shape[0]
    blk_rows = RWKV_CHUNKS * CH
    in_specs, args = [], []
    for d, row in ((0, _SCHED_FWD), (1, _SCHED_BWD)):
        for a in (r, v, kk):
            in_specs.append(pl.BlockSpec((HEADS, blk_rows, HD), lambda i, s, row=row: (0, s[row, i], 0)))
            args.append(a)
        for a in (ld, kd, bd):
            in_specs.append(pl.BlockSpec((1, HEADS, blk_rows, HD), lambda i, s, row=row, d=d: (d, 0, s[row, i], 0)))
            args.append(a)
    in_specs += [pl.BlockSpec((2, 6, CH, CH), lambda i, s: (0, 0, 0, 0)),
                 pl.BlockSpec((HEADS, 1, HD), lambda i, s: (0, 0, 0)),
                 pl.BlockSpec((HEADS, 1, HD), lambda i, s: (0, 0, 0)),
                 pl.BlockSpec((1, 2, HEADS, HD, HD), lambda i, s: (s[_SCHED_SEQ, i], 0, 0, 0, 0))]
    args += [masks, gng, gnb, s0t]
    grid_spec = pltpu.PrefetchScalarGridSpec(
        num_scalar_prefetch=1,
        grid=(sched.shape[1],),
        in_specs=in_specs,
        out_specs=[pl.BlockSpec((blk_rows, D_MIX), lambda i, s: (s[_SCHED_FWD, i], 0)),
                   pl.BlockSpec((blk_rows, D_MIX), lambda i, s: (s[_SCHED_BWD, i], 0)),
                   pl.BlockSpec((1, 2, HEADS, HD, HD), lambda i, s: (s[_SCHED_SEQ, i], 0, 0, 0, 0))],
        scratch_shapes=[pltpu.VMEM((2, HEADS, HD, HD), F32)],
    )
    return pl.pallas_call(
        functools.partial(_rwkv_chunk_kernel, RWKV_CHUNKS),
        grid_spec=grid_spec,
        out_shape=[jax.ShapeDtypeStruct((n, D_MIX), F32), jax.ShapeDtypeStruct((n, D_MIX), F32),
                   jax.ShapeDtypeStruct((n_seq, 2, HEADS, HD, HD), F32)],
        compiler_params=_cparams(("arbitrary",), VMEM_LIMIT),
        name="rwkv_scan",
    )(jnp.asarray(sched), *args)


def _ctx_attn_kernel(q_ref, k_ref, v_ref, y_ref, ko_ref, vo_ref):
    scale = HD ** -0.5
    for h in range(HEADS):
        sl = slice(h * HD, (h + 1) * HD)
        q = q_ref[:, sl]
        k = k_ref[:, sl]
        v = v_ref[:, sl]
        ko_ref[0, h] = k
        vo_ref[0, h] = v
        s = _dot_nt(q.astype(BF16), k.astype(BF16)) * scale
        m = jnp.max(s, axis=-1, keepdims=True)
        e = jnp.exp(s - m)
        p = e / jnp.sum(e, axis=-1, keepdims=True)
        y_ref[:, sl] = _dot(p.astype(BF16), v.astype(BF16))


def ctx_attention(z, bsz, t_len):
    qb = A_PAD // D_MIX
    return pl.pallas_call(
        _ctx_attn_kernel,
        grid=(bsz,),
        in_specs=[pl.BlockSpec((t_len, D_MIX), lambda b: (b, qb)),
                  pl.BlockSpec((t_len, D_MIX), lambda b: (b, qb + 1)),
                  pl.BlockSpec((t_len, D_MIX), lambda b: (b, qb + 2))],
        out_specs=[pl.BlockSpec((t_len, D_MIX), lambda b: (b, 0)),
                   pl.BlockSpec((1, HEADS, t_len, HD), lambda b: (b, 0, 0, 0)),
                   pl.BlockSpec((1, HEADS, t_len, HD), lambda b: (b, 0, 0, 0))],
        out_shape=[jax.ShapeDtypeStruct((bsz * t_len, D_MIX), F32),
                   jax.ShapeDtypeStruct((bsz, HEADS, t_len, HD), F32),
                   jax.ShapeDtypeStruct((bsz, HEADS, t_len, HD), F32)],
        compiler_params=_cparams(("arbitrary",)),
        name="ctx_attention",
    )(z, z, z)


def _na_kernel(wr, rows, *refs):
    q_ref = refs[0]
    k_refs = refs[1:1 + wr]
    v_refs = refs[1 + wr:1 + 2 * wr]
    kc_ref, vc_ref, bias_ref, y_ref = refs[1 + 2 * wr:]
    r = pl.program_id(1)
    rs = jnp.clip(r - wr // 2, 0, rows - wr)
    base = rs - r + WIN_R - 1
    scale = HD ** -0.5
    kwin = jnp.concatenate([kr[...] for kr in k_refs], axis=0).astype(BF16)
    vwin = jnp.concatenate([vr[...] for vr in v_refs], axis=0).astype(BF16)
    for h in range(HEADS):
        sl = slice(h * HD, (h + 1) * HD)
        q = q_ref[:, sl].astype(BF16)
        s_loc = _dot_nt(q, kwin[:, sl]) * scale + bias_ref[h, base]
        s_ctx = _dot_nt(q, kc_ref[0, 0, h].astype(BF16)) * scale
        m = jnp.maximum(jnp.max(s_loc, axis=-1, keepdims=True), jnp.max(s_ctx, axis=-1, keepdims=True))
        e_loc = jnp.exp(s_loc - m)
        e_ctx = jnp.exp(s_ctx - m)
        den = jnp.sum(e_loc, axis=-1, keepdims=True) + jnp.sum(e_ctx, axis=-1, keepdims=True)
        p_loc = (e_loc / den).astype(BF16)
        p_ctx = (e_ctx / den).astype(BF16)
        y_ref[:, sl] = _dot(p_loc, vwin[:, sl]) + _dot(p_ctx, vc_ref[0, 0, h].astype(BF16))


def na_bias_table(rpb, wr):
    cq = np.arange(GRID_W)[:, None]
    ck = np.arange(GRID_W)[None, :]
    cs = np.clip(cq - WIN_C // 2, 0, GRID_W - WIN_C)
    col_bias = np.where((ck >= cs) & (ck < cs + WIN_C), 0.0, NEG).astype(np.float32)
    col_idx = np.clip(ck - cq + WIN_C - 1, 0, 2 * WIN_C - 2)
    rpb_col = rpb.astype(F32)[:, :, col_idx] + col_bias
    n_base = 2 * WIN_R - wr
    tabs = [jnp.concatenate([rpb_col[:, b + j] for j in range(wr)], axis=-1) for b in range(n_base)]
    return jnp.stack(tabs, axis=1)


def na_attention(z, k_ctx, v_ctx, layer, bias_tab, tok_base, bsz, t_len):
    rows = t_len // GRID_W
    wr = min(WIN_R, rows)
    qb = A_PAD // D_MIX
    base_blk = tok_base // GRID_W
    past = k_ctx.shape[3]

    def win_spec(j, col):
        return pl.BlockSpec(
            (GRID_W, D_MIX),
            lambda b, r: (base_blk + b * rows + jnp.clip(r - wr // 2, 0, rows - wr) + j, col))

    in_specs = ([pl.BlockSpec((GRID_W, D_MIX), lambda b, r: (base_blk + b * rows + r, qb))]
                + [win_spec(j, qb + 1) for j in range(wr)]
                + [win_spec(j, qb + 2) for j in range(wr)]
                + [pl.BlockSpec((1, 1, HEADS, past, HD), lambda b, r: (b, layer, 0, 0, 0)),
                   pl.BlockSpec((1, 1, HEADS, past, HD), lambda b, r: (b, layer, 0, 0, 0)),
                   pl.BlockSpec(bias_tab.shape, lambda b, r: (0, 0, 0, 0))])
    return pl.pallas_call(
        functools.partial(_na_kernel, wr, rows),
        grid=(bsz, rows),
        in_specs=in_specs,
        out_specs=pl.BlockSpec((GRID_W, D_MIX), lambda b, r: (b * rows + r, 0)),
        out_shape=jax.ShapeDtypeStruct((bsz * t_len, D_MIX), F32),
        compiler_params=_cparams(("arbitrary", "arbitrary"), VMEM_LIMIT),
        name="na_attention",
    )(*([z] * (1 + 2 * wr)), k_ctx, v_ctx, bias_tab)


def _s5_kernel(bsz, tb, uf_ref, ub_ref, bm_ref, cm_ref, ar_ref, ai_ref, x0_ref, yf_ref, yb_ref, xf_ref,
               x_scr, carry_scr):
    i = pl.program_id(0)

    @pl.when(i == 0)
    def _():
        carry_scr[...] = x0_ref[...]

    n_re = S5_N // LANE
    group = min(n_re, S5_CARRY_VREGS * SUBLANE // (2 * max(bsz, SUBLANE)))
    lane = lambda c: slice(c * LANE, (c + 1) * LANE)
    for d, (u_ref, y_ref) in enumerate(((uf_ref, yf_ref), (ub_ref, yb_ref))):
        u = u_ref[...].reshape(bsz * tb, D_MIX).astype(BF16)
        bu = _dot(u, bm_ref[d])
        for c in range(2 * n_re):
            x_scr[c] = bu[:, lane(c)]
        for c0 in range(0, n_re, group):
            tiles = list(range(c0, c0 + group))
            ar = [ar_ref[d, :, lane(c)] for c in tiles]
            ai = [ai_ref[d, :, lane(c)] for c in tiles]

            def step(k, x, d=d, tiles=tiles, ar=ar, ai=ai):
                t = k if d == 0 else tb - 1 - k
                rows = pl.ds(t, bsz, stride=tb)
                out = []
                for c, a_r, a_i, (xr, xi) in zip(tiles, ar, ai, x):
                    nr = a_r * xr - a_i * xi + x_scr[c, rows, :]
                    ni = a_r * xi + a_i * xr + x_scr[n_re + c, rows, :]
                    x_scr[c, rows, :] = nr
                    x_scr[n_re + c, rows, :] = ni
                    out.append((nr, ni))
                return tuple(out)

            init = tuple((carry_scr[d, :, lane(c)], carry_scr[d, :, lane(n_re + c)]) for c in tiles)
            fin = lax.fori_loop(0, tb, step, init, unroll=4)
            for c, (xr, xi) in zip(tiles, fin):
                carry_scr[d, :, lane(c)] = xr
                carry_scr[d, :, lane(n_re + c)] = xi
        xs = jnp.concatenate([x_scr[c] for c in range(2 * n_re)], axis=-1).astype(BF16)
        y_ref[...] = _dot(xs, cm_ref[...]).reshape(bsz, tb, D_MIX)

    @pl.when(i == pl.num_programs(0) - 1)
    def _():
        xf_ref[...] = carry_scr[...]


def s5_scan(z3, bmat, cmat, a_re, a_im, x0, tb):
    bsz, t_len, _ = z3.shape
    n_t = t_len // tb
    ucol = (A_PAD + 3 * D_MIX) // D_MIX
    full = lambda shape: pl.BlockSpec(shape, lambda i: (0,) * len(shape))
    y_shape = jax.ShapeDtypeStruct((bsz, t_len, D_MIX), F32)
    return pl.pallas_call(
        functools.partial(_s5_kernel, bsz, tb),
        grid=(n_t,),
        in_specs=[pl.BlockSpec((bsz, tb, D_MIX), lambda i: (0, i, ucol)),
                  pl.BlockSpec((bsz, tb, D_MIX), lambda i: (0, n_t - 1 - i, ucol)),
                  full((2, D_MIX, 2 * S5_N)), full((2 * S5_N, D_MIX)),
                  full((2, 1, S5_N)), full((2, 1, S5_N)), full((2, bsz, 2 * S5_N))],
        out_specs=[pl.BlockSpec((bsz, tb, D_MIX), lambda i: (0, i, 0)),
                   pl.BlockSpec((bsz, tb, D_MIX), lambda i: (0, n_t - 1 - i, 0)),
                   full((2, bsz, 2 * S5_N))],
        out_shape=[y_shape, y_shape, jax.ShapeDtypeStruct((2, bsz, 2 * S5_N), F32)],
        scratch_shapes=[pltpu.VMEM((2 * S5_N // LANE, bsz * tb, LANE), F32),
                        pltpu.VMEM((2, bsz, 2 * S5_N), F32)],
        compiler_params=_cparams(("arbitrary",), VMEM_LIMIT),
        name="s5_scan",
    )(z3, z3, bmat, cmat, a_re, a_im, x0)


def _merge_kernel(gnf_ref, gnb_ref, bonus_ref, g_ref, yb_ref, ycf_ref, ycb_ref, u_ref, zg0_ref, zg1_ref, zg2_ref,
                  x_ref, mod_ref, s5d_ref, wglu_ref, bglu_ref, wb_ref, wout_ref, lng_ref, lnb_ref, x1_ref, h2_ref):
    ya = (gnf_ref[...] + gnb_ref[...] + bonus_ref[...]) * g_ref[...]
    yc = ycf_ref[...] + ycb_ref[...] + s5d_ref[...] * u_ref[...]
    yc = 0.5 * yc * (1.0 + jnp.tanh(math.sqrt(2.0 / math.pi) * (yc + 0.044715 * (yc * yc * yc))))
    yc = yc * _sigmoid(_dot(yc.astype(BF16), wglu_ref[...]) + bglu_ref[...])
    merged = (_dot(ya.astype(BF16), wb_ref[0]) * _sigmoid(zg0_ref[...])
              + _dot(yb_ref[...].astype(BF16), wb_ref[1]) * _sigmoid(zg1_ref[...])
              + _dot(yc.astype(BF16), wb_ref[2]) * _sigmoid(zg2_ref[...]))
    mo = _dot(merged.astype(BF16), wout_ref[...])
    m = mod_ref[0]
    x1 = _layer_norm(DN_ALPHA * x_ref[...] + m[2:3] * mo, lng_ref[...], lnb_ref[...])
    x1_ref[...] = x1
    h2_ref[...] = (x1 * (1.0 + m[4:5]) + m[3:4]).astype(BF16)


def merge_branches(gn, bonus, g, yb, yc, z, x, mods, mod_of_tile, p):
    n = x.shape[0]
    row = lambda w, col=0: pl.BlockSpec((TM, w), lambda i, col=col: (i, col))
    full = lambda shape: pl.BlockSpec(shape, lambda i: (0,) * len(shape))
    gb = (A_PAD + 4 * D_MIX) // D_MODEL
    return pl.pallas_call(
        _merge_kernel,
        grid=(n // TM,),
        in_specs=[row(D_MIX), row(D_MIX), row(D_MIX), row(D_MIX), row(D_MIX), row(D_MIX), row(D_MIX),
                  row(D_MIX, A_PAD // D_MIX + 3),
                  row(D_MODEL, gb), row(D_MODEL, gb + 1), row(D_MODEL, gb + 2),
                  row(D_MODEL),
                  pl.BlockSpec((1, SUBLANE, D_MODEL), lambda i: (mod_of_tile(i, TM), 0, 0)),
                  full((1, D_MIX)), full((D_MIX, D_MIX)), full((1, D_MIX)),
                  full((3, D_MIX, D_MODEL)), full((D_MODEL, D_MODEL)),
                  full((1, D_MODEL)), full((1, D_MODEL))],
        out_specs=[row(D_MODEL), row(D_MODEL)],
        out_shape=[jax.ShapeDtypeStruct((n, D_MODEL), F32), jax.ShapeDtypeStruct((n, D_MODEL), BF16)],
        compiler_params=_cparams(("arbitrary",), VMEM_LIMIT),
        name="merge_branches",
    )(gn[0], gn[1], bonus, g, yb, yc[0], yc[1], z, z, z, z, x, mods,
      p['s5_d'], p['w_glu'], p['b_glu'], p['w_branch'], p['w_out'], p['ln1_g'], p['ln1_b'])


def _first_max(val, idx, big):
    m = jnp.max(jnp.max(val, axis=1, keepdims=True), axis=0, keepdims=True)
    cand = jnp.where(val == m, idx, big)
    first = jnp.min(jnp.min(cand, axis=1, keepdims=True), axis=0, keepdims=True)
    return m, idx == first


def _router_kernel(h_ref, wt_ref, bias_ref, gates_ref):
    per = N_EXPERTS // N_GROUPS
    logits = _dot_nt(wt_ref[...], h_ref[...])
    n = logits.shape[1]
    scores = _sigmoid(logits).reshape(N_GROUPS, per, n)
    sel = scores + bias_ref[...]
    e_idx = (lax.broadcasted_iota(jnp.int32, (N_GROUPS, per, n), 0) * per
             + lax.broadcasted_iota(jnp.int32, (N_GROUPS, per, n), 1))
    in_grp = lax.broadcasted_iota(jnp.int32, (N_GROUPS, per, n), 1)
    m1 = jnp.max(sel, axis=1, keepdims=True)
    first = jnp.min(jnp.where(sel == m1, in_grp, per), axis=1, keepdims=True)
    m2 = jnp.max(jnp.where(in_grp == first, -jnp.inf, sel), axis=1, keepdims=True)
    grp = m1 + m2
    g_idx = lax.broadcasted_iota(jnp.int32, (N_GROUPS, 1, n), 0)
    gmask = jnp.zeros((N_GROUPS, 1, n), F32)
    for _ in range(TOPK_GROUPS):
        _, hit = _first_max(grp, g_idx, N_GROUPS)
        gmask = jnp.where(hit, 1.0, gmask)
        grp = jnp.where(hit, -jnp.inf, grp)
    cur = jnp.where(jnp.broadcast_to(gmask, sel.shape) > 0.0, sel, NEG)
    w = jnp.zeros((N_GROUPS, per, n), F32)
    for _ in range(TOP_K):
        _, hit = _first_max(cur, e_idx, N_EXPERTS)
        w = jnp.where(hit, scores, w)
        cur = jnp.where(hit, -jnp.inf, cur)
    tot = jnp.sum(jnp.sum(w, axis=1, keepdims=True), axis=0, keepdims=True)
    gates = (ROUTED_SCALE * w / tot).reshape(N_EXPERTS, n)
    gates_ref[...] = gates.T


def moe_router(h2, router_wt, router_bias):
    n = h2.shape[0]
    return pl.pallas_call(
        _router_kernel,
        grid=(n // TM,),
        in_specs=[pl.BlockSpec((TM, D_MODEL), lambda i: (i, 0)),
                  pl.BlockSpec((N_EXPERTS, D_MODEL), lambda i: (0, 0)),
                  pl.BlockSpec((N_GROUPS, N_EXPERTS // N_GROUPS, 1), lambda i: (0, 0, 0))],
        out_specs=pl.BlockSpec((TM, N_EXPERTS), lambda i: (i, 0)),
        out_shape=jax.ShapeDtypeStruct((n, N_EXPERTS), F32),
        compiler_params=_cparams(("arbitrary",)),
        name="moe_router",
    )(h2, router_wt, router_bias)


def _moe_kernel(ec, h_ref, gates_ref, x1_ref, mod_ref, wg_ref, wu_ref, wd_ref, ex_ref,
                sg_ref, su_ref, sd_ref, lng_ref, lnb_ref, out_ref, acc_scr):
    j = pl.program_id(1)
    h = h_ref[...]

    @pl.when(j == 0)
    def _():
        sh = _dot(h, sg_ref[...])
        sh = sh * _sigmoid(sh) * _dot(h, su_ref[...])
        acc_scr[...] = _dot(sh.astype(BF16), sd_ref[...])

    gates = gates_ref[...]
    g_hi = gates.astype(BF16)
    g_lo = (gates - g_hi.astype(F32)).astype(BF16)
    gexp = _dot(g_hi, ex_ref[0]) + _dot(g_lo, ex_ref[0])
    hg = _dot(h, wg_ref[...])
    hu = _dot(h, wu_ref[...])
    act = hg * _sigmoid(hg) * hu * gexp
    acc_scr[...] += _dot(act.astype(BF16), wd_ref[...])

    @pl.when(j == pl.num_programs(1) - 1)
    def _():
        m = mod_ref[0]
        out_ref[...] = _layer_norm(DN_ALPHA * x1_ref[...] + m[5:6] * acc_scr[...], lng_ref[...], lnb_ref[...])


def moe_ffn(h2, gates, x1, mods, mod_of_tile, p, tm=512, ec=8):
    n = h2.shape[0]
    wcols = ec * D_EXPERT
    n_j = N_EXPERTS // ec
    full = lambda shape: pl.BlockSpec(shape, lambda i, j: (0,) * len(shape))
    return pl.pallas_call(
        functools.partial(_moe_kernel, ec),
        grid=(n // tm, n_j),
        in_specs=[pl.BlockSpec((tm, D_MODEL), lambda i, j: (i, 0)),
                  pl.BlockSpec((tm, N_EXPERTS), lambda i, j: (i, 0)),
                  pl.BlockSpec((tm, D_MODEL), lambda i, j: (i, 0)),
                  pl.BlockSpec((1, SUBLANE, D_MODEL), lambda i, j: (mod_of_tile(i, tm), 0, 0)),
                  pl.BlockSpec((D_MODEL, wcols), lambda i, j: (0, j)),
                  pl.BlockSpec((D_MODEL, wcols), lambda i, j: (0, j)),
                  pl.BlockSpec((wcols, D_MODEL), lambda i, j: (j, 0)),
                  pl.BlockSpec((1, N_EXPERTS, wcols), lambda i, j: (j, 0, 0)),
                  full((D_MODEL, D_SHARED)), full((D_MODEL, D_SHARED)), full((D_SHARED, D_MODEL)),
                  full((1, D_MODEL)), full((1, D_MODEL))],
        out_specs=pl.BlockSpec((tm, D_MODEL), lambda i, j: (i, 0)),
        out_shape=jax.ShapeDtypeStruct((n, D_MODEL), F32),
        scratch_shapes=[pltpu.VMEM((tm, D_MODEL), F32)],
        compiler_params=_cparams(("arbitrary", "arbitrary"), VMEM_LIMIT),
        name="moe_ffn",
    )(h2, gates, x1, mods, p['wg'], p['wu'], p['wd'], p['expand'],
      p['sh_g'], p['sh_u'], p['sh_d'], p['ln2_g'], p['ln2_b'])


def _block_diag2(m):
    z = jnp.zeros_like(m[0])
    return jnp.concatenate([jnp.concatenate([m[0], z], axis=1), jnp.concatenate([z, m[1]], axis=1)], axis=0)


def _layer_params(P, l):
    g = lambda name: P[name][l]
    w_in = g('w_in')
    pad = jnp.zeros((D_MODEL, A_PAD - A_COLS), F32)
    p = {}
    p['w_in'] = jnp.concatenate([w_in[:, :A_COLS], pad, w_in[:, A_COLS:]], axis=1).astype(BF16)
    p['mu'] = jnp.concatenate([g('rwkv_mu'), jnp.zeros((A_PAD - A_COLS,), F32)])[None, :]
    p['w2bd'] = _block_diag2(g('rwkv_w2')).astype(BF16)
    p['a2bd'] = _block_diag2(g('rwkv_a2')).astype(BF16)
    p['g2'] = g('rwkv_g2').astype(BF16)
    p['w0'] = g('rwkv_w0').reshape(1, 2 * D_MIX)
    p['a0'] = g('rwkv_a0').reshape(1, 2 * D_MIX)
    p['kkp'] = g('rwkv_kk')[None, :]
    p['ka'] = g('rwkv_ka')[None, :]
    p['rk'] = g('rwkv_rk')[None, :]
    hid = np.arange(D_MIX) // HD
    p['seg_ones'] = jnp.asarray((hid[:, None] == hid[None, :]).astype(np.float32))
    p['gng'] = g('rwkv_gn_g').reshape(HEADS, 1, HD)
    p['gnb'] = g('rwkv_gn_b').reshape(HEADS, 1, HD)
    a = lax.complex(g('s5_a_re'), g('s5_a_im'))
    dt = jnp.exp(g('s5_log_dt'))[..., None]
    a_bar = jnp.exp(dt * a)
    b_bar = ((a_bar - 1.0) / a)[..., None] * lax.complex(g('s5_b_re'), g('s5_b_im'))
    eye_g = jnp.eye(C_GROUPS, dtype=F32)
    bd = lambda m: jnp.einsum('dgph,gk->dghkp', m, eye_g).reshape(2, D_MIX, S5_N)
    p['s5_bmat'] = jnp.concatenate([bd(b_bar.real), bd(b_bar.imag)], axis=-1).astype(BF16)
    cd = lambda m: jnp.einsum('ghp,gk->kpgh', m, eye_g).reshape(S5_N, D_MIX)
    p['s5_cmat'] = jnp.concatenate([cd(g('s5_c_re')), -cd(g('s5_c_im'))], axis=0).astype(BF16)
    p['s5_ar'] = a_bar.real.reshape(2, 1, S5_N)
    p['s5_ai'] = a_bar.imag.reshape(2, 1, S5_N)
    p['s5_d'] = g('s5_d')[None, :]
    p['w_glu'] = g('s5_w_glu').astype(BF16)
    p['b_glu'] = g('s5_b_glu')[None, :]
    p['w_branch'] = g('w_branch').astype(BF16)
    p['w_out'] = g('w_out').astype(BF16)
    p['ln1_g'] = g('ln1_g')[None, :]
    p['ln1_b'] = g('ln1_b')[None, :]
    p['router_wt'] = g('router_w').T.astype(BF16)
    p['router_bias'] = g('router_bias').reshape(N_GROUPS, N_EXPERTS // N_GROUPS, 1)
    ecols = N_EXPERTS * D_EXPERT
    p['wg'] = g('exp_w_gate').transpose(1, 0, 2).reshape(D_MODEL, ecols).astype(BF16)
    p['wu'] = g('exp_w_up').transpose(1, 0, 2).reshape(D_MODEL, ecols).astype(BF16)
    p['wd'] = g('exp_w_down').reshape(ecols, D_MODEL).astype(BF16)
    p['sh_g'] = g('sh_w_gate').astype(BF16)
    p['sh_u'] = g('sh_w_up').astype(BF16)
    p['sh_d'] = g('sh_w_down').astype(BF16)
    p['ln2_g'] = g('ln2_g')[None, :]
    p['ln2_b'] = g('ln2_b')[None, :]
    return p


def _expand_table(ec):
    n_j = N_EXPERTS // ec
    t = np.zeros((n_j, N_EXPERTS, ec * D_EXPERT), np.float32)
    for e in range(N_EXPERTS):
        j, q = divmod(e, ec)
        t[j, e, q * D_EXPERT:(q + 1) * D_EXPERT] = 1.0
    return jnp.asarray(t, dtype=BF16)


def kernel(x_prompt, x_sample, c, cache_na_k, cache_na_v, state_rwkv, state_s5_re, state_s5_im, c_ctx, w_ada, b_ada, w_in, rwkv_mu, rwkv_w0, rwkv_w2, rwkv_a0, rwkv_a2, rwkv_g2, rwkv_kk, rwkv_ka, rwkv_rk, rwkv_gn_g, rwkv_gn_b, na_rpb, s5_a_re, s5_a_im, s5_log_dt, s5_b_re, s5_b_im, s5_c_re, s5_c_im, s5_d, s5_w_glu, s5_b_glu, w_branch, w_out, ln1_g, ln1_b, router_w, router_bias, exp_w_gate, exp_w_up, exp_w_down, sh_w_gate, sh_w_up, sh_w_down, ln2_g, ln2_b):
    P = dict(w_in=w_in, rwkv_mu=rwkv_mu, rwkv_w0=rwkv_w0, rwkv_w2=rwkv_w2, rwkv_a0=rwkv_a0, rwkv_a2=rwkv_a2,
             rwkv_g2=rwkv_g2, rwkv_kk=rwkv_kk, rwkv_ka=rwkv_ka, rwkv_rk=rwkv_rk, rwkv_gn_g=rwkv_gn_g,
             rwkv_gn_b=rwkv_gn_b, s5_a_re=s5_a_re, s5_a_im=s5_a_im, s5_log_dt=s5_log_dt, s5_b_re=s5_b_re,
             s5_b_im=s5_b_im, s5_c_re=s5_c_re, s5_c_im=s5_c_im, s5_d=s5_d, s5_w_glu=s5_w_glu,
             s5_b_glu=s5_b_glu, w_branch=w_branch, w_out=w_out, ln1_g=ln1_g, ln1_b=ln1_b, router_w=router_w,
             router_bias=router_bias, exp_w_gate=exp_w_gate, exp_w_up=exp_w_up, exp_w_down=exp_w_down,
             sh_w_gate=sh_w_gate, sh_w_up=sh_w_up, sh_w_down=sh_w_down, ln2_g=ln2_g, ln2_b=ln2_b)
    bc, tc, _ = x_prompt.shape
    bl, tl, _ = x_sample.shape
    depth = w_in.shape[0]

    cond = jnp.concatenate([c_ctx[None, :], c, jnp.zeros((SUBLANE - 1 - bl, D_MODEL), F32)], axis=0)
    ada = ada_modulation(cond, w_ada, b_ada)
    masks = jnp.asarray(_chunk_masks())
    wr = min(WIN_R, tl // GRID_W)
    expand = _expand_table(8)

    paths = {
        'ctx': dict(bsz=bc, t=tc, x=x_prompt.reshape(bc * tc, D_MODEL), mod_of_tile=lambda i, tm: 0),
        'lat': dict(bsz=bl, t=tl, x=x_sample.reshape(bl * tl, D_MODEL),
                    mod_of_tile=lambda i, tm: 1 + (i * tm) // tl),
    }
    for q in paths.values():
        tiles = q['t'] // TM
        pos = np.arange(q['bsz'] * tiles) % tiles
        q['seq_tiles'] = jnp.asarray(np.concatenate([np.full_like(pos, tiles), pos]).astype(np.int32))
        q['sched'] = rwkv_schedule([q['t']] * q['bsz'])
        q['s5_tb'] = S5_ROWS // q['bsz']

    new_k, new_v, new_rwkv, new_s5 = [], [], [], []
    for l in range(depth):
        p = _layer_params(P, l)
        p['expand'] = expand
        mods = ada[l, :1 + bl].reshape(1 + bl, 6, D_MODEL)
        mods = jnp.concatenate([mods, jnp.zeros((1 + bl, SUBLANE - 6, D_MODEL), F32)], axis=1)
        bias_tab = na_bias_table(na_rpb[l], wr)
        for name, q in paths.items():
            bsz, t_len, x, mod_of_tile = q['bsz'], q['t'], q['x'], q['mod_of_tile']
            z = in_projection(x, mods, p['w_in'], mod_of_tile)

            r, v, kk, ld, kd, bd, g, bonus = rwkv_prep(z, q['seq_tiles'], p)
            if name == 'ctx':
                s0t = jnp.zeros((bsz, 2, HEADS, HD, HD), F32)
            else:
                s0t = jnp.swapaxes(state_rwkv[:, l], -1, -2)
            gn_f, gn_b, s_fin = rwkv_scan((r, v, kk, ld, kd, bd), masks, p['gng'], p['gnb'], s0t, q['sched'])

            if name == 'ctx':
                yb, k_h, v_h = ctx_attention(z, bsz, t_len)
            else:
                yb = na_attention(z, cache_na_k, cache_na_v, l, bias_tab, 0, bsz, t_len)

            if name == 'ctx':
                x0 = jnp.zeros((2, bsz, 2 * S5_N), F32)
            else:
                x0 = jnp.concatenate([state_s5_re[:, l].reshape(bsz, 2, S5_N),
                                      state_s5_im[:, l].reshape(bsz, 2, S5_N)], axis=-1)
                x0 = jnp.swapaxes(x0, 0, 1)
            yc_f, yc_b, x_fin = s5_scan(z.reshape(bsz, t_len, IN_COLS_P), p['s5_bmat'], p['s5_cmat'],
                                        p['s5_ar'], p['s5_ai'], x0, q['s5_tb'])
            yc = (yc_f.reshape(bsz * t_len, D_MIX), yc_b.reshape(bsz * t_len, D_MIX))

            x1, h2 = merge_branches((gn_f, gn_b), bonus, g, yb, yc, z, x, mods, mod_of_tile, p)
            gates = moe_router(h2, p['router_wt'], p['router_bias'])
            q['x'] = moe_ffn(h2, gates, x1, mods, mod_of_tile, p)
            if name == 'ctx':
                new_k.append(k_h)
                new_v.append(v_h)
                new_rwkv.append(jnp.swapaxes(s_fin, -1, -2))
                new_s5.append(jnp.swapaxes(x_fin, 0, 1))

    s5_all = jnp.stack(new_s5, axis=1)
    s5_re = s5_all[..., :S5_N].reshape(bc, depth, 2, C_GROUPS, C_STATE)
    s5_im = s5_all[..., S5_N:].reshape(bc, depth, 2, C_GROUPS, C_STATE)
    return (paths['ctx']['x'].reshape(bc, tc, D_MODEL), paths['lat']['x'].reshape(bl, tl, D_MODEL),
            jnp.stack(new_k, axis=1), jnp.stack(new_v, axis=1), jnp.stack(new_rwkv, axis=1), s5_re, s5_im)
```

```python
import functools
import math

import numpy as np
import jax
import jax.numpy as jnp
from jax import lax
from jax.experimental import pallas as pl
from jax.experimental.pallas import tpu as pltpu

F32 = jnp.float32
BF16 = jnp.bfloat16

D_MODEL = 1024
DEPTH = 2
GRID_W = 64
D_MIX = 512
HEADS = 8
HD = 64
LORA = 128
WIN_R = 8
WIN_C = 16
C_GROUP = 16
C_GROUPS = D_MIX // C_GROUP
C_STATE = 64
S5_N = C_GROUPS * C_STATE
NA_QROWS = 8
NA_KROWS = NA_QROWS + WIN_R
S5_ROWS = 512
S5_CARRY_VREGS = 32
N_EXPERTS = 64
TOP_K = 8
N_GROUPS = 8
TOPK_GROUPS = 4
D_EXPERT = 128
D_SHARED = 128
ROUTED_SCALE = 2.5
LN_EPS = 1e-5
GN_EPS = 64e-5
NEG = -1e30
DN_ALPHA = (2 * DEPTH) ** 0.25
A_COLS = 3 * D_MIX + 3 * LORA
A_PAD = 2048
IN_COLS_P = A_PAD + 4 * D_MIX + 3 * D_MODEL

LANE = 128
SUBLANE = 8
TM = 256
CH = 64
RWKV_CHUNKS = 2
VMEM_LIMIT = 56 * 1024 * 1024


def _cparams(sem, vmem=None):
    return pltpu.CompilerParams(dimension_semantics=sem, vmem_limit_bytes=vmem)


def _sigmoid(x):
    return 1.0 / (1.0 + jnp.exp(-x))


def _dot(a, b, precision=None):
    return jnp.dot(a, b, preferred_element_type=F32, precision=precision)


def _dot_nt(a, b, precision=None):
    return lax.dot_general(a, b, (((1,), (1,)), ((), ())), preferred_element_type=F32, precision=precision)


def _dot_tn(a, b, precision=None):
    return lax.dot_general(a, b, (((0,), (0,)), ((), ())), preferred_element_type=F32, precision=precision)


def _split2(x):
    hi = x.astype(BF16)
    return hi, (x - hi.astype(F32)).astype(BF16)


def _split3(x):
    hi = x.astype(BF16)
    r1 = x - hi.astype(F32)
    mid = r1.astype(BF16)
    return hi, mid, (r1 - mid.astype(F32)).astype(BF16)


def _seg_sum(x, ones_bf16):
    return sum(_dot(part, ones_bf16) for part in _split3(x))


def _mm3(a, b, dot=_dot):
    return dot(a[0], b[0]) + (dot(a[1], b[0]) + dot(a[0], b[1]))


def _mm3_nt(a, b):
    return _mm3(a, b, _dot_nt)


def _layer_norm(x, g, b):
    mu = jnp.mean(x, axis=-1, keepdims=True)
    xc = x - mu
    var = jnp.mean(xc * xc, axis=-1, keepdims=True)
    return xc * lax.rsqrt(var + LN_EPS) * g + b


def _ada_kernel(c_ref, w_ref, b_ref, o_ref):
    c = c_ref[...]
    s = c * _sigmoid(c)
    o_ref[0] = _dot(s.astype(BF16), w_ref[0]) + b_ref[0]


def ada_modulation(cond, w_ada, b_ada):
    n_l, d, n6 = w_ada.shape
    tn = 1536
    return pl.pallas_call(
        _ada_kernel,
        grid=(n_l, n6 // tn),
        in_specs=[pl.BlockSpec((SUBLANE, d), lambda l, j: (0, 0)),
                  pl.BlockSpec((1, d, tn), lambda l, j: (l, 0, j)),
                  pl.BlockSpec((1, 1, tn), lambda l, j: (l, 0, j))],
        out_specs=pl.BlockSpec((1, SUBLANE, tn), lambda l, j: (l, 0, j)),
        out_shape=jax.ShapeDtypeStruct((n_l, SUBLANE, n6), F32),
        compiler_params=_cparams(("arbitrary", "arbitrary")),
        name="ada_modulation",
    )(cond, w_ada.astype(BF16), b_ada.reshape(n_l, 1, n6))


def _inproj_kernel(x_ref, mod_ref, w_ref, z_ref, h_scr):
    @pl.when(pl.program_id(1) == 0)
    def _():
        m = mod_ref[0]
        h_scr[...] = (x_ref[...] * (1.0 + m[1:2]) + m[0:1]).astype(BF16)
    z_ref[...] = _dot(h_scr[...], w_ref[...])


def in_projection(x, mods, w_in_p, mod_of_tile, tm=2048, tn=1024):
    n, d = x.shape
    cols = w_in_p.shape[1]
    return pl.pallas_call(
        _inproj_kernel,
        grid=(n // tm, cols // tn),
        in_specs=[pl.BlockSpec((tm, d), lambda i, j: (i, 0)),
                  pl.BlockSpec((1, SUBLANE, d), lambda i, j: (mod_of_tile(i, tm), 0, 0)),
                  pl.BlockSpec((d, tn), lambda i, j: (0, j))],
        out_specs=pl.BlockSpec((tm, tn), lambda i, j: (i, j)),
        out_shape=jax.ShapeDtypeStruct((n, cols), F32),
        scratch_shapes=[pltpu.VMEM((tm, d), BF16)],
        compiler_params=_cparams(("arbitrary", "arbitrary"), VMEM_LIMIT),
        name="in_projection",
    )(x, mods, w_in_p)


def _rwkv_prep_kernel(seq_tiles_ref, z_ref, zp_ref, zn_ref, mu_ref, w2_ref, a2_ref, g2_ref, w0_ref, a0_ref,
                      kkp_ref, ka_ref, rk_ref, e_ref,
                      r_ref, v_ref, kk_ref, ld_ref, kd_ref, bd_ref, g_ref, bonus_ref):
    i = pl.program_id(0)
    tiles = seq_tiles_ref[i]
    pos = seq_tiles_ref[i + pl.num_programs(0)]
    x = z_ref[...]
    tm = x.shape[0]
    rows = lax.broadcasted_iota(jnp.int32, x.shape, 0)
    prev_row = jnp.where(pos == 0, 0.0, zp_ref[SUBLANE - 1:SUBLANE, :])
    next_row = jnp.where(pos == tiles - 1, 0.0, zn_ref[0:1, :])
    xm1 = jnp.where(rows == 0, prev_row, pltpu.roll(x, 1, axis=0))
    xp1 = jnp.where(rows == tm - 1, next_row, pltpu.roll(x, tm - 1, axis=0))
    za = x + mu_ref[...] * (0.5 * (xm1 + xp1) - x)

    r = za[:, 0:D_MIX]
    k = za[:, D_MIX:2 * D_MIX]
    v = za[:, 2 * D_MIX:3 * D_MIX]
    lw = za[:, 3 * D_MIX:3 * D_MIX + LORA]
    la = za[:, 3 * D_MIX + LORA:3 * D_MIX + 2 * LORA]
    lg = za[:, 3 * D_MIX + 2 * LORA:3 * D_MIX + 3 * LORA]

    w_both = w0_ref[...] + _dot(jnp.tanh(lw).astype(BF16), w2_ref[...])
    a_both = _sigmoid(a0_ref[...] + _dot(la.astype(BF16), a2_ref[...]))
    g_ref[...] = _dot(_sigmoid(lg).astype(BF16), g2_ref[...])

    e = e_ref[...]
    kks = k * kkp_ref[...]
    nrm = jnp.sqrt(_seg_sum(kks * kks, e))
    kk = kks / jnp.maximum(nrm, 1e-12)
    bonus = jnp.zeros_like(v)
    for h in range(HEADS):
        sl = slice(h * HD, (h + 1) * HD)
        r_ref[h] = r[:, sl]
        v_ref[h] = v[:, sl]
        kk_ref[h] = kk[:, sl]
    for d in range(2):
        w = w_both[:, d * D_MIX:(d + 1) * D_MIX]
        a = a_both[:, d * D_MIX:(d + 1) * D_MIX]
        ld = -math.exp(-0.5) * _sigmoid(w)
        kd = k * (1.0 + (a - 1.0) * ka_ref[...])
        bd = kk * a
        bonus = bonus + _seg_sum(r * kd * rk_ref[...], e) * v
        for h in range(HEADS):
            sl = slice(h * HD, (h + 1) * HD)
            ld_ref[d, h] = ld[:, sl]
            kd_ref[d, h] = kd[:, sl]
            bd_ref[d, h] = bd[:, sl]
    bonus_ref[...] = bonus


def rwkv_prep(z, seq_tiles, p):
    n = z.shape[0]
    nt = n // TM
    halo = TM // SUBLANE
    nb8 = n // SUBLANE
    hm = jax.ShapeDtypeStruct((HEADS, n, HD), F32)
    hm2 = jax.ShapeDtypeStruct((2, HEADS, n, HD), F32)
    tok = jax.ShapeDtypeStruct((n, D_MIX), F32)
    full = lambda shape: pl.BlockSpec(shape, lambda i, s: (0,) * len(shape))
    hm_spec = pl.BlockSpec((HEADS, TM, HD), lambda i, s: (0, i, 0))
    hm2_spec = pl.BlockSpec((2, HEADS, TM, HD), lambda i, s: (0, 0, i, 0))
    tok_spec = pl.BlockSpec((TM, D_MIX), lambda i, s: (i, 0))
    grid_spec = pltpu.PrefetchScalarGridSpec(
        num_scalar_prefetch=1,
        grid=(nt,),
        in_specs=[pl.BlockSpec((TM, A_PAD), lambda i, s: (i, 0)),
                  pl.BlockSpec((SUBLANE, A_PAD), lambda i, s: (jnp.maximum(i * halo - 1, 0), 0)),
                  pl.BlockSpec((SUBLANE, A_PAD), lambda i, s: (jnp.minimum((i + 1) * halo, nb8 - 1), 0)),
                  full((1, A_PAD)), full((LORA, 2 * D_MIX)), full((LORA, 2 * D_MIX)), full((LORA, D_MIX)),
                  full((1, 2 * D_MIX)), full((1, 2 * D_MIX)), full((1, D_MIX)), full((1, D_MIX)),
                  full((1, D_MIX)), full((D_MIX, D_MIX))],
        out_specs=[hm_spec, hm_spec, hm_spec, hm2_spec, hm2_spec, hm2_spec, tok_spec, tok_spec],
    )
    return pl.pallas_call(
        _rwkv_prep_kernel,
        grid_spec=grid_spec,
        out_shape=[hm, hm, hm, hm2, hm2, hm2, tok, tok],
        compiler_params=_cparams(("arbitrary",), VMEM_LIMIT),
        name="rwkv_prep",
    )(seq_tiles, z, z, z, p['mu'], p['w2bd'], p['a2bd'], p['g2'], p['w0'], p['a0'],
      p['kkp'], p['ka'], p['rk'], p['seg_ones'])


def _chunk_masks():
    t = np.arange(CH)
    fwd_incl = (t[:, None] >= t[None, :])
    out = []
    for incl in (fwd_incl, fwd_incl.T):
        strict = incl & (t[:, None] != t[None, :])
        ms = [incl, strict, strict & ((t[:, None] // 8) == (t[None, :] // 8))]
        for m in (8, 16, 32):
            ms.append(strict & ((t[:, None] // (2 * m)) == (t[None, :] // (2 * m)))
                      & ((t[:, None] // m) != (t[None, :] // m)))
        out.append(np.stack(ms))
    return np.stack(out).astype(np.float32)


def _rwkv_chunk_kernel(cb, sched_ref, *refs):
    dir_refs = (refs[0:6], refs[6:12])
    m_ref, gng_ref, gnb_ref, s0_ref = refs[12:16]
    y_refs = refs[16:18]
    sfin_ref, s_scr = refs[18:20]
    step_id = pl.program_id(0)
    first = sched_ref[_SCHED_FIRST, step_id]

    @pl.when(first == 1)
    def _():
        s_scr[...] = s0_ref[0]

    ri = lax.broadcasted_iota(jnp.int32, (CH, CH), 0)
    ci = lax.broadcasted_iota(jnp.int32, (CH, CH), 1)
    eye = (ri == ci).astype(F32)

    bf = lambda x: x.astype(BF16)
    each = lambda f, *cols: [f(*args) for args in zip(*cols)]
    rows = lambda j: slice(j * CH, (j + 1) * CH)

    chains = [(d, j, h) for d in range(2) for j in range(cb) for h in range(HEADS)]
    msk = lambda k: [m_ref[d, k] for d, _, _ in chains]
    incl, strict, m8 = msk(0), msk(1), msk(2)
    incl_b = each(bf, incl)
    get = lambda k: [dir_refs[d][k][h, rows(j), :] if k < 3 else dir_refs[d][k][0, h, rows(j), :]
                     for d, j, h in chains]
    R, V, KK, LD, Kd, Bd = (get(k) for k in range(6))
    L = each(lambda m, x: sum(_dot(m, part) for part in _split3(x)), incl_b, LD)
    ltot = each(lambda x: jnp.sum(x, axis=0, keepdims=True), LD)
    e_nl = each(lambda l: jnp.exp(-l), L)
    e_rest = each(lambda l, lt: jnp.exp(lt - l), L, ltot)
    Qb = each(lambda kk, l, ld: bf(kk * jnp.exp(l - ld)), KK, L, LD)
    Rh = each(lambda r, l: r * jnp.exp(l), R, L)
    Rb = each(bf, Rh)
    Btb = each(lambda b, e: bf(b * e), Bd, e_nl)
    Ktb = each(lambda k, e: bf(k * e), Kd, e_nl)
    BcTb = each(lambda b, e: bf((b * e).T), Bd, e_rest)
    KcTb = each(lambda k, e: bf((k * e).T), Kd, e_rest)
    Vb = each(bf, V)
    Nl = each(lambda m, q, b: m * _dot_nt(q, b), strict, Qb, Btb)
    Mkb = each(lambda m, q, k: bf(m * _dot_nt(q, k)), strict, Qb, Ktb)
    Mrbb = each(lambda m, r, b: bf(m * _dot_nt(r, b)), incl, Rb, Btb)
    Mrkb = each(lambda m, r, k: bf(m * _dot_nt(r, k)), incl, Rb, Ktb)
    N8 = each(lambda m, n: m * n, m8, Nl)
    N8b = each(bf, N8)
    N2 = each(lambda a: _dot(a, a), N8b)
    N2b = each(bf, N2)
    N4 = each(lambda a: _dot(a, a), N2b)
    W = each(lambda a, b: _dot(bf(eye - a), bf(eye + b)), N8, N2)
    W = each(lambda w, n4: _dot(bf(w), bf(eye + n4)), W, N4)
    for lvl in range(3):
        Wb = each(bf, W)
        T = each(lambda m, n, w: _dot(bf(m * n), w), msk(3 + lvl), Nl, Wb)
        W = each(lambda w, wb, t: w - _dot(wb, bf(t)), W, Wb, T)
    Wb = each(bf, W)
    Whb = each(lambda w, q: bf(_dot(w, q)), Wb, Qb)
    MkVb = each(lambda m, v: bf(_dot(m, v)), Mkb, Vb)
    U0b = each(lambda w, t: bf(-_dot(w, t)), Wb, MkVb)
    Rbar = each(lambda rh, m, wh: _split2(rh - _dot(m, wh)), Rh, Mrbb, Whb)
    Y0 = each(lambda mb, u, mk, v: _dot(mb, u) + _dot(mk, v), Mrbb, U0b, Mrkb, Vb)
    G = each(lambda lt, b, wh: _split2(eye * jnp.exp(lt) - _dot(b, wh)), ltot, BcTb, Whb)
    Hm = each(lambda b, u, k, v: _dot(b, u) + _dot(k, v), BcTb, U0b, KcTb, Vb)

    lanes = [(d, h) for d in range(2) for h in range(HEADS)]
    ST = [s_scr[d, h] for d, h in lanes]
    for step in range(cb):
        idx = [chains.index((d, step if d == 0 else cb - 1 - step, h)) for d, h in lanes]
        STs = each(_split2, ST)
        Y = [_mm3(Rbar[i], st) + Y0[i] for i, st in zip(idx, STs)]
        ST = [_mm3(G[i], st) + Hm[i] for i, st in zip(idx, STs)]
        for i, y in zip(idx, Y):
            d, j, h = chains[i]
            mu = jnp.mean(y, axis=-1, keepdims=True)
            yc = y - mu
            var = jnp.mean(yc * yc, axis=-1, keepdims=True)
            y_refs[d][rows(j), h * HD:(h + 1) * HD] = yc * lax.rsqrt(var + GN_EPS) * gng_ref[h] + gnb_ref[h]
    for (d, h), st in zip(lanes, ST):
        s_scr[d, h] = st

    @pl.when(sched_ref[_SCHED_LAST, step_id] == 1)
    def _():
        sfin_ref[0] = s_scr[...]


_SCHED_FWD, _SCHED_BWD, _SCHED_SEQ, _SCHED_FIRST, _SCHED_LAST = range(5)


def rwkv_schedule(seq_lens):
    blk_rows = RWKV_CHUNKS * CH
    cols, base = [], 0
    for s, t_len in enumerate(seq_lens):
        n_b = t_len // blk_rows
        for i in range(n_b):
            cols.append((base + i, base + n_b - 1 - i, s, int(i == 0), int(i == n_b - 1)))
        base += n_b
    return np.asarray(cols, np.int32).T


def rwkv_scan(prep, masks, gng, gnb, s0t, sched):
    r, v, kk, ld, kd, bd = prep
    n = r.shape[1]
    n_seq = s0t.shape[0]
    blk_rows = RWKV_CHUNKS * CH
    in_specs, args = [], []
    for d, row in ((0, _SCHED_FWD), (1, _SCHED_BWD)):
        for a in (r, v, kk):
            in_specs.append(pl.BlockSpec((HEADS, blk_rows, HD), lambda i, s, row=row: (0, s[row, i], 0)))
            args.append(a)
        for a in (ld, kd, bd):
            in_specs.append(pl.BlockSpec((1, HEADS, blk_rows, HD), lambda i, s, row=row, d=d: (d, 0, s[row, i], 0)))
            args.append(a)
    in_specs += [pl.BlockSpec((2, 6, CH, CH), lambda i, s: (0, 0, 0, 0)),
                 pl.BlockSpec((HEADS, 1, HD), lambda i, s: (0, 0, 0)),
                 pl.BlockSpec((HEADS, 1, HD), lambda i, s: (0, 0, 0)),
                 pl.BlockSpec((1, 2, HEADS, HD, HD), lambda i, s: (s[_SCHED_SEQ, i], 0, 0, 0, 0))]
    args += [masks, gng, gnb, s0t]
    grid_spec = pltpu.PrefetchScalarGridSpec(
        num_scalar_prefetch=1,
        grid=(sched.shape[1],),
        in_specs=in_specs,
        out_specs=[pl.BlockSpec((blk_rows, D_MIX), lambda i, s: (s[_SCHED_FWD, i], 0)),
                   pl.BlockSpec((blk_rows, D_MIX), lambda i, s: (s[_SCHED_BWD, i], 0)),
                   pl.BlockSpec((1, 2, HEADS, HD, HD), lambda i, s: (s[_SCHED_SEQ, i], 0, 0, 0, 0))],
        scratch_shapes=[pltpu.VMEM((2, HEADS, HD, HD), F32)],
    )
    return pl.pallas_call(
        functools.partial(_rwkv_chunk_kernel, RWKV_CHUNKS),
        grid_spec=grid_spec,
        out_shape=[jax.ShapeDtypeStruct((n, D_MIX), F32), jax.ShapeDtypeStruct((n, D_MIX), F32),
                   jax.ShapeDtypeStruct((n_seq, 2, HEADS, HD, HD), F32)],
        compiler_params=_cparams(("arbitrary",), VMEM_LIMIT),
        name="rwkv_scan",
    )(jnp.asarray(sched), *args)


def _ctx_attn_kernel(q_ref, k_ref, v_ref, y_ref, ko_ref, vo_ref):
    scale = HD ** -0.5
    for h in range(HEADS):
        sl = slice(h * HD, (h + 1) * HD)
        q = q_ref[:, sl]
        k = k_ref[:, sl]
        v = v_ref[:, sl]
        ko_ref[0, h] = k
        vo_ref[0, h] = v
        s = _dot_nt(q.astype(BF16), k.astype(BF16)) * scale
        m = jnp.max(s, axis=-1, keepdims=True)
        e = jnp.exp(s - m)
        p = e / jnp.sum(e, axis=-1, keepdims=True)
        y_ref[:, sl] = _dot(p.astype(BF16), v.astype(BF16))


def ctx_attention(z, bsz, t_len):
    qb = A_PAD // D_MIX
    return pl.pallas_call(
        _ctx_attn_kernel,
        grid=(bsz,),
        in_specs=[pl.BlockSpec((t_len, D_MIX), lambda b: (b, qb)),
                  pl.BlockSpec((t_len, D_MIX), lambda b: (b, qb + 1)),
                  pl.BlockSpec((t_len, D_MIX), lambda b: (b, qb + 2))],
        out_specs=[pl.BlockSpec((t_len, D_MIX), lambda b: (b, 0)),
                   pl.BlockSpec((1, HEADS, t_len, HD), lambda b: (b, 0, 0, 0)),
                   pl.BlockSpec((1, HEADS, t_len, HD), lambda b: (b, 0, 0, 0))],
        out_shape=[jax.ShapeDtypeStruct((bsz * t_len, D_MIX), F32),
                   jax.ShapeDtypeStruct((bsz, HEADS, t_len, HD), F32),
                   jax.ShapeDtypeStruct((bsz, HEADS, t_len, HD), F32)],
        compiler_params=_cparams(("arbitrary",)),
        name="ctx_attention",
    )(z, z, z)


def _na_kernel(rows, nblk, *refs):
    q_ref = refs[0]
    k_refs = refs[1:1 + nblk]
    v_refs = refs[1 + nblk:1 + 2 * nblk]
    kc_ref, vc_ref, tab_ref, y_ref = refs[1 + 2 * nblk:]
    r0 = pl.program_id(1) * NA_QROWS
    u0 = jnp.clip(r0 - WIN_R // 2, 0, rows - NA_KROWS)
    scale = HD ** -0.5
    kwin = jnp.concatenate([kr[...] for kr in k_refs], axis=0).astype(BF16)
    vwin = jnp.concatenate([vr[...] for vr in v_refs], axis=0).astype(BF16)
    left = lax.broadcasted_iota(jnp.int32, (1, 2 * GRID_W), 1) < GRID_W
    bias = []
    for i in range(NA_QROWS):
        r = r0 + i
        rs = jnp.clip(r - WIN_R // 2, 0, rows - WIN_R)
        per_pair = []
        for jp in range(NA_KROWS // 2):
            kr = u0 + 2 * jp
            off = [jnp.where(jnp.logical_and(kr + e >= rs, kr + e < rs + WIN_R), 0.0, NEG) for e in range(2)]
            per_pair.append((jnp.clip(kr - r + WIN_R, 0, 2 * WIN_R - 1), jnp.where(left, off[0], off[1])))
        bias.append(per_pair)
    for h in range(HEADS):
        sl = slice(h * HD, (h + 1) * HD)
        q = q_ref[:, sl].astype(BF16)
        s_raw = _dot_nt(q, kwin[:, sl]) * scale
        s_loc = jnp.concatenate([
            jnp.concatenate([s_raw[i * GRID_W:(i + 1) * GRID_W, jp * 2 * GRID_W:(jp + 1) * 2 * GRID_W]
                             + tab_ref[h, bias[i][jp][0]] + bias[i][jp][1]
                             for jp in range(NA_KROWS // 2)], axis=1)
            for i in range(NA_QROWS)], axis=0)
        s_ctx = _dot_nt(q, kc_ref[0, 0, h].astype(BF16)) * scale
        m = jnp.maximum(jnp.max(s_loc, axis=-1, keepdims=True), jnp.max(s_ctx, axis=-1, keepdims=True))
        e_loc = jnp.exp(s_loc - m)
        e_ctx = jnp.exp(s_ctx - m)
        den = jnp.sum(e_loc, axis=-1, keepdims=True) + jnp.sum(e_ctx, axis=-1, keepdims=True)
        p_loc = (e_loc / den).astype(BF16)
        p_ctx = (e_ctx / den).astype(BF16)
        y_ref[:, sl] = _dot(p_loc, vwin[:, sl]) + _dot(p_ctx, vc_ref[0, 0, h].astype(BF16))


def na_bias_table(rpb):
    cq = np.arange(GRID_W)[:, None]
    ck = np.arange(GRID_W)[None, :]
    cs = np.clip(cq - WIN_C // 2, 0, GRID_W - WIN_C)
    col_bias = np.where((ck >= cs) & (ck < cs + WIN_C), 0.0, NEG).astype(np.float32)
    col_idx = np.clip(ck - cq + WIN_C - 1, 0, 2 * WIN_C - 2)
    rpb_col = rpb.astype(F32)[:, :, col_idx] + col_bias
    padded = jnp.pad(rpb_col, ((0, 0), (1, 1), (0, 0), (0, 0)), constant_values=NEG)
    return jnp.concatenate([padded[:, :-1], padded[:, 1:]], axis=-1)


def na_attention(z, k_ctx, v_ctx, layer, bias_tab, bsz, t_len):
    rows = t_len // GRID_W
    assert rows >= NA_KROWS and rows % NA_QROWS == 0, "latent grid too small for the row-group tiling"
    qb = A_PAD // D_MIX
    past = k_ctx.shape[3]
    blk_rows = WIN_R // 2
    nblk = NA_KROWS // blk_rows
    blk_tok = blk_rows * GRID_W
    q_tok = NA_QROWS * GRID_W

    def win_spec(j, col):
        def index(b, g):
            u0 = jnp.clip(g * NA_QROWS - WIN_R // 2, 0, rows - NA_KROWS)
            return (b * (rows // blk_rows) + u0 // blk_rows + j, col)
        return pl.BlockSpec((blk_tok, D_MIX), index)

    in_specs = ([pl.BlockSpec((q_tok, D_MIX), lambda b, g: (b * (rows // NA_QROWS) + g, qb))]
                + [win_spec(j, qb + 1) for j in range(nblk)]
                + [win_spec(j, qb + 2) for j in range(nblk)]
                + [pl.BlockSpec((1, 1, HEADS, past, HD), lambda b, g: (b, layer, 0, 0, 0)),
                   pl.BlockSpec((1, 1, HEADS, past, HD), lambda b, g: (b, layer, 0, 0, 0)),
                   pl.BlockSpec(bias_tab.shape, lambda b, g: (0, 0, 0, 0))])
    return pl.pallas_call(
        functools.partial(_na_kernel, rows, nblk),
        grid=(bsz, rows // NA_QROWS),
        in_specs=in_specs,
        out_specs=pl.BlockSpec((q_tok, D_MIX), lambda b, g: (b * (rows // NA_QROWS) + g, 0)),
        out_shape=jax.ShapeDtypeStruct((bsz * t_len, D_MIX), F32),
        compiler_params=_cparams(("arbitrary", "arbitrary"), VMEM_LIMIT),
        name="na_attention",
    )(*([z] * (1 + 2 * nblk)), k_ctx, v_ctx, bias_tab)


def _s5_kernel(bsz, tb, uf_ref, ub_ref, bm_ref, cm_ref, ar_ref, ai_ref, x0_ref, yf_ref, yb_ref, xf_ref,
               x_scr, carry_scr):
    i = pl.program_id(0)

    @pl.when(i == 0)
    def _():
        carry_scr[...] = x0_ref[...]

    n_re = S5_N // LANE
    group = min(n_re, S5_CARRY_VREGS * SUBLANE // (2 * max(bsz, SUBLANE)))
    lane = lambda c: slice(c * LANE, (c + 1) * LANE)
    for d, (u_ref, y_ref) in enumerate(((uf_ref, yf_ref), (ub_ref, yb_ref))):
        u = u_ref[...].reshape(bsz * tb, D_MIX).astype(BF16)
        bu = _dot(u, bm_ref[d])
        for c in range(2 * n_re):
            x_scr[c] = bu[:, lane(c)]
        for c0 in range(0, n_re, group):
            tiles = list(range(c0, c0 + group))
            ar = [ar_ref[d, :, lane(c)] for c in tiles]
            ai = [ai_ref[d, :, lane(c)] for c in tiles]

            def step(k, x, d=d, tiles=tiles, ar=ar, ai=ai):
                t = k if d == 0 else tb - 1 - k
                rows = pl.ds(t, bsz, stride=tb)
                out = []
                for c, a_r, a_i, (xr, xi) in zip(tiles, ar, ai, x):
                    nr = a_r * xr - a_i * xi + x_scr[c, rows, :]
                    ni = a_r * xi + a_i * xr + x_scr[n_re + c, rows, :]
                    x_scr[c, rows, :] = nr
                    x_scr[n_re + c, rows, :] = ni
                    out.append((nr, ni))
                return tuple(out)

            init = tuple((carry_scr[d, :, lane(c)], carry_scr[d, :, lane(n_re + c)]) for c in tiles)
            fin = lax.fori_loop(0, tb, step, init, unroll=4)
            for c, (xr, xi) in zip(tiles, fin):
                carry_scr[d, :, lane(c)] = xr
                carry_scr[d, :, lane(n_re + c)] = xi
        xs = jnp.concatenate([x_scr[c] for c in range(2 * n_re)], axis=-1).astype(BF16)
        y_ref[...] = _dot(xs, cm_ref[...]).reshape(bsz, tb, D_MIX)

    @pl.when(i == pl.num_programs(0) - 1)
    def _():
        xf_ref[...] = carry_scr[...]


def s5_scan(z3, bmat, cmat, a_re, a_im, x0, tb):
    bsz, t_len, _ = z3.shape
    n_t = t_len // tb
    ucol = (A_PAD + 3 * D_MIX) // D_MIX
    full = lambda shape: pl.BlockSpec(shape, lambda i: (0,) * len(shape))
    y_shape = jax.ShapeDtypeStruct((bsz, t_len, D_MIX), F32)
    return pl.pallas_call(
        functools.partial(_s5_kernel, bsz, tb),
        grid=(n_t,),
        in_specs=[pl.BlockSpec((bsz, tb, D_MIX), lambda i: (0, i, ucol)),
                  pl.BlockSpec((bsz, tb, D_MIX), lambda i: (0, n_t - 1 - i, ucol)),
                  full((2, D_MIX, 2 * S5_N)), full((2 * S5_N, D_MIX)),
                  full((2, 1, S5_N)), full((2, 1, S5_N)), full((2, bsz, 2 * S5_N))],
        out_specs=[pl.BlockSpec((bsz, tb, D_MIX), lambda i: (0, i, 0)),
                   pl.BlockSpec((bsz, tb, D_MIX), lambda i: (0, n_t - 1 - i, 0)),
                   full((2, bsz, 2 * S5_N))],
        out_shape=[y_shape, y_shape, jax.ShapeDtypeStruct((2, bsz, 2 * S5_N), F32)],
        scratch_shapes=[pltpu.VMEM((2 * S5_N // LANE, bsz * tb, LANE), F32),
                        pltpu.VMEM((2, bsz, 2 * S5_N), F32)],
        compiler_params=_cparams(("arbitrary",), VMEM_LIMIT),
        name="s5_scan",
    )(z3, z3, bmat, cmat, a_re, a_im, x0)


def _merge_kernel(gnf_ref, gnb_ref, bonus_ref, g_ref, yb_ref, ycf_ref, ycb_ref, u_ref, zg0_ref, zg1_ref, zg2_ref,
                  x_ref, mod_ref, s5d_ref, wglu_ref, bglu_ref, wb_ref, wout_ref, lng_ref, lnb_ref, x1_ref, h2_ref):
    ya = (gnf_ref[...] + gnb_ref[...] + bonus_ref[...]) * g_ref[...]
    yc = ycf_ref[...] + ycb_ref[...] + s5d_ref[...] * u_ref[...]
    yc = 0.5 * yc * (1.0 + jnp.tanh(math.sqrt(2.0 / math.pi) * (yc + 0.044715 * (yc * yc * yc))))
    yc = yc * _sigmoid(_dot(yc.astype(BF16), wglu_ref[...]) + bglu_ref[...])
    merged = (_dot(ya.astype(BF16), wb_ref[0]) * _sigmoid(zg0_ref[...])
              + _dot(yb_ref[...].astype(BF16), wb_ref[1]) * _sigmoid(zg1_ref[...])
              + _dot(yc.astype(BF16), wb_ref[2]) * _sigmoid(zg2_ref[...]))
    mo = _dot(merged.astype(BF16), wout_ref[...])
    m = mod_ref[0]
    x1 = _layer_norm(DN_ALPHA * x_ref[...] + m[2:3] * mo, lng_ref[...], lnb_ref[...])
    x1_ref[...] = x1
    h2_ref[...] = (x1 * (1.0 + m[4:5]) + m[3:4]).astype(BF16)


def merge_branches(gn, bonus, g, yb, yc, z, x, mods, mod_of_tile, p):
    n = x.shape[0]
    row = lambda w, col=0: pl.BlockSpec((TM, w), lambda i, col=col: (i, col))
    full = lambda shape: pl.BlockSpec(shape, lambda i: (0,) * len(shape))
    gb = (A_PAD + 4 * D_MIX) // D_MODEL
    return pl.pallas_call(
        _merge_kernel,
        grid=(n // TM,),
        in_specs=[row(D_MIX), row(D_MIX), row(D_MIX), row(D_MIX), row(D_MIX), row(D_MIX), row(D_MIX),
                  row(D_MIX, A_PAD // D_MIX + 3),
                  row(D_MODEL, gb), row(D_MODEL, gb + 1), row(D_MODEL, gb + 2),
                  row(D_MODEL),
                  pl.BlockSpec((1, SUBLANE, D_MODEL), lambda i: (mod_of_tile(i, TM), 0, 0)),
                  full((1, D_MIX)), full((D_MIX, D_MIX)), full((1, D_MIX)),
                  full((3, D_MIX, D_MODEL)), full((D_MODEL, D_MODEL)),
                  full((1, D_MODEL)), full((1, D_MODEL))],
        out_specs=[row(D_MODEL), row(D_MODEL)],
        out_shape=[jax.ShapeDtypeStruct((n, D_MODEL), F32), jax.ShapeDtypeStruct((n, D_MODEL), BF16)],
        compiler_params=_cparams(("arbitrary",), VMEM_LIMIT),
        name="merge_branches",
    )(gn[0], gn[1], bonus, g, yb, yc[0], yc[1], z, z, z, z, x, mods,
      p['s5_d'], p['w_glu'], p['b_glu'], p['w_branch'], p['w_out'], p['ln1_g'], p['ln1_b'])


def _first_max(val, idx, big):
    m = jnp.max(jnp.max(val, axis=1, keepdims=True), axis=0, keepdims=True)
    cand = jnp.where(val == m, idx, big)
    first = jnp.min(jnp.min(cand, axis=1, keepdims=True), axis=0, keepdims=True)
    return m, idx == first


def _router_kernel(h_ref, wt_ref, bias_ref, gates_ref):
    per = N_EXPERTS // N_GROUPS
    logits = _dot_nt(wt_ref[...], h_ref[...])
    n = logits.shape[1]
    scores = _sigmoid(logits).reshape(N_GROUPS, per, n)
    sel = scores + bias_ref[...]
    e_idx = (lax.broadcasted_iota(jnp.int32, (N_GROUPS, per, n), 0) * per
             + lax.broadcasted_iota(jnp.int32, (N_GROUPS, per, n), 1))
    in_grp = lax.broadcasted_iota(jnp.int32, (N_GROUPS, per, n), 1)
    m1 = jnp.max(sel, axis=1, keepdims=True)
    first = jnp.min(jnp.where(sel == m1, in_grp, per), axis=1, keepdims=True)
    m2 = jnp.max(jnp.where(in_grp == first, -jnp.inf, sel), axis=1, keepdims=True)
    grp = m1 + m2
    g_idx = lax.broadcasted_iota(jnp.int32, (N_GROUPS, 1, n), 0)
    gmask = jnp.zeros((N_GROUPS, 1, n), F32)
    for _ in range(TOPK_GROUPS):
        _, hit = _first_max(grp, g_idx, N_GROUPS)
        gmask = jnp.where(hit, 1.0, gmask)
        grp = jnp.where(hit, -jnp.inf, grp)
    cur = jnp.where(jnp.broadcast_to(gmask, sel.shape) > 0.0, sel, NEG)
    w = jnp.zeros((N_GROUPS, per, n), F32)
    for _ in range(TOP_K):
        _, hit = _first_max(cur, e_idx, N_EXPERTS)
        w = jnp.where(hit, scores, w)
        cur = jnp.where(hit, -jnp.inf, cur)
    tot = jnp.sum(jnp.sum(w, axis=1, keepdims=True), axis=0, keepdims=True)
    gates = (ROUTED_SCALE * w / tot).reshape(N_EXPERTS, n)
    gates_ref[...] = gates.T


def moe_router(h2, router_wt, router_bias):
    n = h2.shape[0]
    return pl.pallas_call(
        _router_kernel,
        grid=(n // TM,),
        in_specs=[pl.BlockSpec((TM, D_MODEL), lambda i: (i, 0)),
                  pl.BlockSpec((N_EXPERTS, D_MODEL), lambda i: (0, 0)),
                  pl.BlockSpec((N_GROUPS, N_EXPERTS // N_GROUPS, 1), lambda i: (0, 0, 0))],
        out_specs=pl.BlockSpec((TM, N_EXPERTS), lambda i: (i, 0)),
        out_shape=jax.ShapeDtypeStruct((n, N_EXPERTS), F32),
        compiler_params=_cparams(("arbitrary",)),
        name="moe_router",
    )(h2, router_wt, router_bias)


def _moe_kernel(ec, h_ref, gates_ref, x1_ref, mod_ref, wg_ref, wu_ref, wd_ref, ex_ref,
                sg_ref, su_ref, sd_ref, lng_ref, lnb_ref, out_ref, acc_scr):
    j = pl.program_id(1)
    h = h_ref[...]

    @pl.when(j == 0)
    def _():
        sh = _dot(h, sg_ref[...])
        sh = sh * _sigmoid(sh) * _dot(h, su_ref[...])
        acc_scr[...] = _dot(sh.astype(BF16), sd_ref[...])

    gates = gates_ref[...]
    g_hi = gates.astype(BF16)
    g_lo = (gates - g_hi.astype(F32)).astype(BF16)
    gexp = _dot(g_hi, ex_ref[0]) + _dot(g_lo, ex_ref[0])
    hg = _dot(h, wg_ref[...])
    hu = _dot(h, wu_ref[...])
    act = hg * _sigmoid(hg) * hu * gexp
    acc_scr[...] += _dot(act.astype(BF16), wd_ref[...])

    @pl.when(j == pl.num_programs(1) - 1)
    def _():
        m = mod_ref[0]
        out_ref[...] = _layer_norm(DN_ALPHA * x1_ref[...] + m[5:6] * acc_scr[...], lng_ref[...], lnb_ref[...])


def moe_ffn(h2, gates, x1, mods, mod_of_tile, p, tm=512, ec=8):
    n = h2.shape[0]
    wcols = ec * D_EXPERT
    n_j = N_EXPERTS // ec
    full = lambda shape: pl.BlockSpec(shape, lambda i, j: (0,) * len(shape))
    return pl.pallas_call(
        functools.partial(_moe_kernel, ec),
        grid=(n // tm, n_j),
        in_specs=[pl.BlockSpec((tm, D_MODEL), lambda i, j: (i, 0)),
                  pl.BlockSpec((tm, N_EXPERTS), lambda i, j: (i, 0)),
                  pl.BlockSpec((tm, D_MODEL), lambda i, j: (i, 0)),
                  pl.BlockSpec((1, SUBLANE, D_MODEL), lambda i, j: (mod_of_tile(i, tm), 0, 0)),
                  pl.BlockSpec((D_MODEL, wcols), lambda i, j: (0, j)),
                  pl.BlockSpec((D_MODEL, wcols), lambda i, j: (0, j)),
                  pl.BlockSpec((wcols, D_MODEL), lambda i, j: (j, 0)),
                  pl.BlockSpec((1, N_EXPERTS, wcols), lambda i, j: (j, 0, 0)),
                  full((D_MODEL, D_SHARED)), full((D_MODEL, D_SHARED)), full((D_SHARED, D_MODEL)),
                  full((1, D_MODEL)), full((1, D_MODEL))],
        out_specs=pl.BlockSpec((tm, D_MODEL), lambda i, j: (i, 0)),
        out_shape=jax.ShapeDtypeStruct((n, D_MODEL), F32),
        scratch_shapes=[pltpu.VMEM((tm, D_MODEL), F32)],
        compiler_params=_cparams(("arbitrary", "arbitrary"), VMEM_LIMIT),
        name="moe_ffn",
    )(h2, gates, x1, mods, p['wg'], p['wu'], p['wd'], p['expand'],
      p['sh_g'], p['sh_u'], p['sh_d'], p['ln2_g'], p['ln2_b'])


def _block_diag2(m):
    z = jnp.zeros_like(m[0])
    return jnp.concatenate([jnp.concatenate([m[0], z], axis=1), jnp.concatenate([z, m[1]], axis=1)], axis=0)


def _layer_params(P, l):
    g = lambda name: P[name][l]
    w_in = g('w_in')
    pad = jnp.zeros((D_MODEL, A_PAD - A_COLS), F32)
    p = {}
    p['w_in'] = jnp.concatenate([w_in[:, :A_COLS], pad, w_in[:, A_COLS:]], axis=1).astype(BF16)
    p['mu'] = jnp.concatenate([g('rwkv_mu'), jnp.zeros((A_PAD - A_COLS,), F32)])[None, :]
    p['w2bd'] = _block_diag2(g('rwkv_w2')).astype(BF16)
    p['a2bd'] = _block_diag2(g('rwkv_a2')).astype(BF16)
    p['g2'] = g('rwkv_g2').astype(BF16)
    p['w0'] = g('rwkv_w0').reshape(1, 2 * D_MIX)
    p['a0'] = g('rwkv_a0').reshape(1, 2 * D_MIX)
    p['kkp'] = g('rwkv_kk')[None, :]
    p['ka'] = g('rwkv_ka')[None, :]
    p['rk'] = g('rwkv_rk')[None, :]
    hid = np.arange(D_MIX) // HD
    p['seg_ones'] = jnp.asarray((hid[:, None] == hid[None, :]).astype(np.float32), dtype=BF16)
    p['gng'] = g('rwkv_gn_g').reshape(HEADS, 1, HD)
    p['gnb'] = g('rwkv_gn_b').reshape(HEADS, 1, HD)
    a = lax.complex(g('s5_a_re'), g('s5_a_im'))
    dt = jnp.exp(g('s5_log_dt'))[..., None]
    a_bar = jnp.exp(dt * a)
    b_bar = ((a_bar - 1.0) / a)[..., None] * lax.complex(g('s5_b_re'), g('s5_b_im'))
    eye_g = jnp.eye(C_GROUPS, dtype=F32)
    bd = lambda m: jnp.einsum('dgph,gk->dghkp', m, eye_g).reshape(2, D_MIX, S5_N)
    p['s5_bmat'] = jnp.concatenate([bd(b_bar.real), bd(b_bar.imag)], axis=-1).astype(BF16)
    cd = lambda m: jnp.einsum('ghp,gk->kpgh', m, eye_g).reshape(S5_N, D_MIX)
    p['s5_cmat'] = jnp.concatenate([cd(g('s5_c_re')), -cd(g('s5_c_im'))], axis=0).astype(BF16)
    p['s5_ar'] = a_bar.real.reshape(2, 1, S5_N)
    p['s5_ai'] = a_bar.imag.reshape(2, 1, S5_N)
    p['s5_d'] = g('s5_d')[None, :]
    p['w_glu'] = g('s5_w_glu').astype(BF16)
    p['b_glu'] = g('s5_b_glu')[None, :]
    p['w_branch'] = g('w_branch').astype(BF16)
    p['w_out'] = g('w_out').astype(BF16)
    p['ln1_g'] = g('ln1_g')[None, :]
    p['ln1_b'] = g('ln1_b')[None, :]
    p['router_wt'] = g('router_w').T.astype(BF16)
    p['router_bias'] = g('router_bias').reshape(N_GROUPS, N_EXPERTS // N_GROUPS, 1)
    ecols = N_EXPERTS * D_EXPERT
    p['wg'] = g('exp_w_gate').transpose(1, 0, 2).reshape(D_MODEL, ecols).astype(BF16)
    p['wu'] = g('exp_w_up').transpose(1, 0, 2).reshape(D_MODEL, ecols).astype(BF16)
    p['wd'] = g('exp_w_down').reshape(ecols, D_MODEL).astype(BF16)
    p['sh_g'] = g('sh_w_gate').astype(BF16)
    p['sh_u'] = g('sh_w_up').astype(BF16)
    p['sh_d'] = g('sh_w_down').astype(BF16)
    p['ln2_g'] = g('ln2_g')[None, :]
    p['ln2_b'] = g('ln2_b')[None, :]
    return p


def _expand_table(ec):
    n_j = N_EXPERTS // ec
    t = np.zeros((n_j, N_EXPERTS, ec * D_EXPERT), np.float32)
    for e in range(N_EXPERTS):
        j, q = divmod(e, ec)
        t[j, e, q * D_EXPERT:(q + 1) * D_EXPERT] = 1.0
    return jnp.asarray(t, dtype=BF16)


def kernel(x_prompt, x_sample, c, cache_na_k, cache_na_v, state_rwkv, state_s5_re, state_s5_im, c_ctx, w_ada, b_ada, w_in, rwkv_mu, rwkv_w0, rwkv_w2, rwkv_a0, rwkv_a2, rwkv_g2, rwkv_kk, rwkv_ka, rwkv_rk, rwkv_gn_g, rwkv_gn_b, na_rpb, s5_a_re, s5_a_im, s5_log_dt, s5_b_re, s5_b_im, s5_c_re, s5_c_im, s5_d, s5_w_glu, s5_b_glu, w_branch, w_out, ln1_g, ln1_b, router_w, router_bias, exp_w_gate, exp_w_up, exp_w_down, sh_w_gate, sh_w_up, sh_w_down, ln2_g, ln2_b):
    P = dict(w_in=w_in, rwkv_mu=rwkv_mu, rwkv_w0=rwkv_w0, rwkv_w2=rwkv_w2, rwkv_a0=rwkv_a0, rwkv_a2=rwkv_a2,
             rwkv_g2=rwkv_g2, rwkv_kk=rwkv_kk, rwkv_ka=rwkv_ka, rwkv_rk=rwkv_rk, rwkv_gn_g=rwkv_gn_g,
             rwkv_gn_b=rwkv_gn_b, s5_a_re=s5_a_re, s5_a_im=s5_a_im, s5_log_dt=s5_log_dt, s5_b_re=s5_b_re,
             s5_b_im=s5_b_im, s5_c_re=s5_c_re, s5_c_im=s5_c_im, s5_d=s5_d, s5_w_glu=s5_w_glu,
             s5_b_glu=s5_b_glu, w_branch=w_branch, w_out=w_out, ln1_g=ln1_g, ln1_b=ln1_b, router_w=router_w,
             router_bias=router_bias, exp_w_gate=exp_w_gate, exp_w_up=exp_w_up, exp_w_down=exp_w_down,
             sh_w_gate=sh_w_gate, sh_w_up=sh_w_up, sh_w_down=sh_w_down, ln2_g=ln2_g, ln2_b=ln2_b)
    bc, tc, _ = x_prompt.shape
    bl, tl, _ = x_sample.shape
    depth = w_in.shape[0]

    cond = jnp.concatenate([c_ctx[None, :], c, jnp.zeros((SUBLANE - 1 - bl, D_MODEL), F32)], axis=0)
    ada = ada_modulation(cond, w_ada, b_ada)
    masks = jnp.asarray(_chunk_masks())
    expand = _expand_table(8)

    paths = {
        'ctx': dict(bsz=bc, t=tc, x=x_prompt.reshape(bc * tc, D_MODEL), mod_of_tile=lambda i, tm: 0),
        'lat': dict(bsz=bl, t=tl, x=x_sample.reshape(bl * tl, D_MODEL),
                    mod_of_tile=lambda i, tm: 1 + (i * tm) // tl),
    }
    for q in paths.values():
        tiles = q['t'] // TM
        pos = np.arange(q['bsz'] * tiles) % tiles
        q['seq_tiles'] = jnp.asarray(np.concatenate([np.full_like(pos, tiles), pos]).astype(np.int32))
        q['sched'] = rwkv_schedule([q['t']] * q['bsz'])
        q['s5_tb'] = S5_ROWS // q['bsz']

    new_k, new_v, new_rwkv, new_s5 = [], [], [], []
    for l in range(depth):
        p = _layer_params(P, l)
        p['expand'] = expand
        mods = ada[l, :1 + bl].reshape(1 + bl, 6, D_MODEL)
        mods = jnp.concatenate([mods, jnp.zeros((1 + bl, SUBLANE - 6, D_MODEL), F32)], axis=1)
        bias_tab = na_bias_table(na_rpb[l])
        for name, q in paths.items():
            bsz, t_len, x, mod_of_tile = q['bsz'], q['t'], q['x'], q['mod_of_tile']
            z = in_projection(x, mods, p['w_in'], mod_of_tile)

            r, v, kk, ld, kd, bd, g, bonus = rwkv_prep(z, q['seq_tiles'], p)
            if name == 'ctx':
                s0t = jnp.zeros((bsz, 2, HEADS, HD, HD), F32)
            else:
                s0t = jnp.swapaxes(state_rwkv[:, l], -1, -2)
            gn_f, gn_b, s_fin = rwkv_scan((r, v, kk, ld, kd, bd), masks, p['gng'], p['gnb'], s0t, q['sched'])

            if name == 'ctx':
                yb, k_h, v_h = ctx_attention(z, bsz, t_len)
            else:
                yb = na_attention(z, cache_na_k, cache_na_v, l, bias_tab, bsz, t_len)

            if name == 'ctx':
                x0 = jnp.zeros((2, bsz, 2 * S5_N), F32)
            else:
                x0 = jnp.concatenate([state_s5_re[:, l].reshape(bsz, 2, S5_N),
                                      state_s5_im[:, l].reshape(bsz, 2, S5_N)], axis=-1)
                x0 = jnp.swapaxes(x0, 0, 1)
            yc_f, yc_b, x_fin = s5_scan(z.reshape(bsz, t_len, IN_COLS_P), p['s5_bmat'], p['s5_cmat'],
                                        p['s5_ar'], p['s5_ai'], x0, q['s5_tb'])
            yc = (yc_f.reshape(bsz * t_len, D_MIX), yc_b.reshape(bsz * t_len, D_MIX))

            x1, h2 = merge_branches((gn_f, gn_b), bonus, g, yb, yc, z, x, mods, mod_of_tile, p)
            gates = moe_router(h2, p['router_wt'], p['router_bias'])
            q['x'] = moe_ffn(h2, gates, x1, mods, mod_of_tile, p)
            if name == 'ctx':
                new_k.append(k_h)
                new_v.append(v_h)
                new_rwkv.append(jnp.swapaxes(s_fin, -1, -2))
                new_s5.append(jnp.swapaxes(x_fin, 0, 1))

    s5_all = jnp.stack(new_s5, axis=1)
    s5_re = s5_all[..., :S5_N].reshape(bc, depth, 2, C_GROUPS, C_STATE)
    s5_im = s5_all[..., S5_N:].reshape(bc, depth, 2, C_GROUPS, C_STATE)
    return (paths['ctx']['x'].reshape(bc, tc, D_MODEL), paths['lat']['x'].reshape(bl, tl, D_MODEL),
            jnp.stack(new_k, axis=1), jnp.stack(new_v, axis=1), jnp.stack(new_rwkv, axis=1), s5_re, s5_im)
```

```python
import functools
import math

import numpy as np
import jax
import jax.numpy as jnp
from jax import lax
from jax.experimental import pallas as pl
from jax.experimental.pallas import tpu as pltpu

F32 = jnp.float32
BF16 = jnp.bfloat16

D_MODEL = 1024
DEPTH = 2
GRID_W = 64
D_MIX = 512
HEADS = 8
HD = 64
LORA = 128
WIN_R = 8
WIN_C = 16
C_GROUP = 16
C_GROUPS = D_MIX // C_GROUP
C_STATE = 64
S5_N = C_GROUPS * C_STATE
NA_QROWS = 8
NA_KROWS = NA_QROWS + WIN_R
S5_GB = 4
S5_GW = S5_N // S5_GB
S5_ROWS = 512
S5_CARRY_VREGS = 32
N_EXPERTS = 64
TOP_K = 8
N_GROUPS = 8
TOPK_GROUPS = 4
D_EXPERT = 128
D_SHARED = 128
ROUTED_SCALE = 2.5
LN_EPS = 1e-5
GN_EPS = 64e-5
NEG = -1e30
DN_ALPHA = (2 * DEPTH) ** 0.25
A_COLS = 3 * D_MIX + 3 * LORA
A_PAD = 2048
IN_COLS_P = A_PAD + 4 * D_MIX + 3 * D_MODEL

LANE = 128
SUBLANE = 8
TM = 256
CH = 64
RWKV_LANES = 16
RWKV_CHUNKS = 2
VMEM_LIMIT = 56 * 1024 * 1024


def _cparams(sem, vmem=None):
    return pltpu.CompilerParams(dimension_semantics=sem, vmem_limit_bytes=vmem)


def _sigmoid(x):
    return 1.0 / (1.0 + jnp.exp(-x))


def _dot(a, b, precision=None):
    return jnp.dot(a, b, preferred_element_type=F32, precision=precision)


def _dot_nt(a, b, precision=None):
    return lax.dot_general(a, b, (((1,), (1,)), ((), ())), preferred_element_type=F32, precision=precision)


def _dot_tn(a, b, precision=None):
    return lax.dot_general(a, b, (((0,), (0,)), ((), ())), preferred_element_type=F32, precision=precision)


def _split2(x):
    hi = x.astype(BF16)
    return hi, (x - hi.astype(F32)).astype(BF16)


def _split3(x):
    hi = x.astype(BF16)
    r1 = x - hi.astype(F32)
    mid = r1.astype(BF16)
    return hi, mid, (r1 - mid.astype(F32)).astype(BF16)


def _seg_sum(x, ones_bf16):
    return sum(_dot(part, ones_bf16) for part in _split3(x))


def _mm3(a, b, dot=_dot):
    return dot(a[0], b[0]) + (dot(a[1], b[0]) + dot(a[0], b[1]))


def _mm3_nt(a, b):
    return _mm3(a, b, _dot_nt)


def _layer_norm(x, g, b):
    mu = jnp.mean(x, axis=-1, keepdims=True)
    xc = x - mu
    var = jnp.mean(xc * xc, axis=-1, keepdims=True)
    return xc * lax.rsqrt(var + LN_EPS) * g + b


def _ada_kernel(c_ref, w_ref, b_ref, o_ref):
    c = c_ref[...]
    s = c * _sigmoid(c)
    o_ref[0] = _dot(s.astype(BF16), w_ref[0]) + b_ref[0]


def ada_modulation(cond, w_ada, b_ada):
    n_l, d, n6 = w_ada.shape
    tn = 1536
    return pl.pallas_call(
        _ada_kernel,
        grid=(n_l, n6 // tn),
        in_specs=[pl.BlockSpec((SUBLANE, d), lambda l, j: (0, 0)),
                  pl.BlockSpec((1, d, tn), lambda l, j: (l, 0, j)),
                  pl.BlockSpec((1, 1, tn), lambda l, j: (l, 0, j))],
        out_specs=pl.BlockSpec((1, SUBLANE, tn), lambda l, j: (l, 0, j)),
        out_shape=jax.ShapeDtypeStruct((n_l, SUBLANE, n6), F32),
        compiler_params=_cparams(("arbitrary", "arbitrary")),
        name="ada_modulation",
    )(cond, w_ada.astype(BF16), b_ada.reshape(n_l, 1, n6))


def _inproj_kernel(x_ref, mod_ref, w_ref, z_ref, h_scr):
    @pl.when(pl.program_id(1) == 0)
    def _():
        m = mod_ref[0]
        h_scr[...] = (x_ref[...] * (1.0 + m[1:2]) + m[0:1]).astype(BF16)
    z_ref[...] = _dot(h_scr[...], w_ref[...])


def in_projection(x, mods, w_in_p, mod_of_tile, tm=2048, tn=1024):
    n, d = x.shape
    cols = w_in_p.shape[1]
    return pl.pallas_call(
        _inproj_kernel,
        grid=(n // tm, cols // tn),
        in_specs=[pl.BlockSpec((tm, d), lambda i, j: (i, 0)),
                  pl.BlockSpec((1, SUBLANE, d), lambda i, j: (mod_of_tile(i, tm), 0, 0)),
                  pl.BlockSpec((d, tn), lambda i, j: (0, j))],
        out_specs=pl.BlockSpec((tm, tn), lambda i, j: (i, j)),
        out_shape=jax.ShapeDtypeStruct((n, cols), F32),
        scratch_shapes=[pltpu.VMEM((tm, d), BF16)],
        compiler_params=_cparams(("arbitrary", "arbitrary"), VMEM_LIMIT),
        name="in_projection",
    )(x, mods, w_in_p)


def _rwkv_prep_kernel(seq_tiles_ref, z_ref, zp_ref, zn_ref, mu_ref, w2_ref, a2_ref, g2_ref, w0_ref, a0_ref,
                      kkp_ref, ka_ref, rk_ref, e_ref,
                      r_ref, v_ref, kk_ref, ld_ref, kd_ref, bd_ref, g_ref, bonus_ref):
    i = pl.program_id(0)
    tiles = seq_tiles_ref[i]
    pos = seq_tiles_ref[i + pl.num_programs(0)]
    x = z_ref[...]
    tm = x.shape[0]
    rows = lax.broadcasted_iota(jnp.int32, x.shape, 0)
    prev_row = jnp.where(pos == 0, 0.0, zp_ref[SUBLANE - 1:SUBLANE, :])
    next_row = jnp.where(pos == tiles - 1, 0.0, zn_ref[0:1, :])
    xm1 = jnp.where(rows == 0, prev_row, pltpu.roll(x, 1, axis=0))
    xp1 = jnp.where(rows == tm - 1, next_row, pltpu.roll(x, tm - 1, axis=0))
    za = x + mu_ref[...] * (0.5 * (xm1 + xp1) - x)

    r = za[:, 0:D_MIX]
    k = za[:, D_MIX:2 * D_MIX]
    v = za[:, 2 * D_MIX:3 * D_MIX]
    lw = za[:, 3 * D_MIX:3 * D_MIX + LORA]
    la = za[:, 3 * D_MIX + LORA:3 * D_MIX + 2 * LORA]
    lg = za[:, 3 * D_MIX + 2 * LORA:3 * D_MIX + 3 * LORA]

    w_both = w0_ref[...] + _dot(jnp.tanh(lw).astype(BF16), w2_ref[...])
    a_both = _sigmoid(a0_ref[...] + _dot(la.astype(BF16), a2_ref[...]))
    g_ref[...] = _dot(_sigmoid(lg).astype(BF16), g2_ref[...])

    e = e_ref[...]
    kks = k * kkp_ref[...]
    nrm = jnp.sqrt(_seg_sum(kks * kks, e))
    kk = kks / jnp.maximum(nrm, 1e-12)
    bonus = jnp.zeros_like(v)
    for h in range(HEADS):
        sl = slice(h * HD, (h + 1) * HD)
        r_ref[h] = r[:, sl]
        v_ref[h] = v[:, sl]
        kk_ref[h] = kk[:, sl]
    for d in range(2):
        w = w_both[:, d * D_MIX:(d + 1) * D_MIX]
        a = a_both[:, d * D_MIX:(d + 1) * D_MIX]
        ld = -math.exp(-0.5) * _sigmoid(w)
        kd = k * (1.0 + (a - 1.0) * ka_ref[...])
        bd = kk * a
        bonus = bonus + _seg_sum(r * kd * rk_ref[...], e) * v
        for h in range(HEADS):
            sl = slice(h * HD, (h + 1) * HD)
            ld_ref[d, h] = ld[:, sl]
            kd_ref[d, h] = kd[:, sl]
            bd_ref[d, h] = bd[:, sl]
    bonus_ref[...] = bonus


def rwkv_prep(z, seq_tiles, p):
    n = z.shape[0]
    nt = n // TM
    halo = TM // SUBLANE
    nb8 = n // SUBLANE
    hm = jax.ShapeDtypeStruct((HEADS, n, HD), F32)
    hm2 = jax.ShapeDtypeStruct((2, HEADS, n, HD), F32)
    tok = jax.ShapeDtypeStruct((n, D_MIX), F32)
    full = lambda shape: pl.BlockSpec(shape, lambda i, s: (0,) * len(shape))
    hm_spec = pl.BlockSpec((HEADS, TM, HD), lambda i, s: (0, i, 0))
    hm2_spec = pl.BlockSpec((2, HEADS, TM, HD), lambda i, s: (0, 0, i, 0))
    tok_spec = pl.BlockSpec((TM, D_MIX), lambda i, s: (i, 0))
    grid_spec = pltpu.PrefetchScalarGridSpec(
        num_scalar_prefetch=1,
        grid=(nt,),
        in_specs=[pl.BlockSpec((TM, A_PAD), lambda i, s: (i, 0)),
                  pl.BlockSpec((SUBLANE, A_PAD), lambda i, s: (jnp.maximum(i * halo - 1, 0), 0)),
                  pl.BlockSpec((SUBLANE, A_PAD), lambda i, s: (jnp.minimum((i + 1) * halo, nb8 - 1), 0)),
                  full((1, A_PAD)), full((LORA, 2 * D_MIX)), full((LORA, 2 * D_MIX)), full((LORA, D_MIX)),
                  full((1, 2 * D_MIX)), full((1, 2 * D_MIX)), full((1, D_MIX)), full((1, D_MIX)),
                  full((1, D_MIX)), full((D_MIX, D_MIX))],
        out_specs=[hm_spec, hm_spec, hm_spec, hm2_spec, hm2_spec, hm2_spec, tok_spec, tok_spec],
    )
    return pl.pallas_call(
        _rwkv_prep_kernel,
        grid_spec=grid_spec,
        out_shape=[hm, hm, hm, hm2, hm2, hm2, tok, tok],
        compiler_params=_cparams(("arbitrary",), VMEM_LIMIT),
        name="rwkv_prep",
    )(seq_tiles, z, z, z, p['mu'], p['w2bd'], p['a2bd'], p['g2'], p['w0'], p['a0'],
      p['kkp'], p['ka'], p['rk'], p['seg_ones'])


def _chunk_masks():
    t = np.arange(CH)
    fwd_incl = (t[:, None] >= t[None, :])
    out = []
    for incl in (fwd_incl, fwd_incl.T):
        strict = incl & (t[:, None] != t[None, :])
        ms = [incl, strict, strict & ((t[:, None] // 8) == (t[None, :] // 8))]
        for m in (8, 16, 32):
            ms.append(strict & ((t[:, None] // (2 * m)) == (t[None, :] // (2 * m)))
                      & ((t[:, None] // m) != (t[None, :] // m)))
        out.append(np.stack(ms))
    return np.stack(out).astype(np.float32)


def _rwkv_chunk_kernel(cb, sched_ref, *refs):
    dir_refs = (refs[0:6], refs[6:12])
    m_ref, gng_ref, gnb_ref, s0_ref = refs[12:16]
    y_refs = refs[16:18]
    sfin_ref, s_scr = refs[18:20]
    step_id = pl.program_id(0)
    first = sched_ref[_SCHED_FIRST, step_id]

    @pl.when(first == 1)
    def _():
        s_scr[...] = s0_ref[0]

    ri = lax.broadcasted_iota(jnp.int32, (CH, CH), 0)
    ci = lax.broadcasted_iota(jnp.int32, (CH, CH), 1)
    eye = (ri == ci).astype(F32)

    bf = lambda x: x.astype(BF16)
    each = lambda f, *cols: [f(*args) for args in zip(*cols)]
    rows = lambda j: slice(j * CH, (j + 1) * CH)

    all_lanes = [(d, h) for d in range(2) for h in range(HEADS)]
    for g0 in range(0, len(all_lanes), RWKV_LANES):
        lanes = all_lanes[g0:g0 + RWKV_LANES]
        chains = [(d, j, h) for d, h in lanes for j in range(cb)]
        msk = lambda k: [m_ref[d, k] for d, _, _ in chains]
        incl, strict, m8 = msk(0), msk(1), msk(2)
        incl_b = each(bf, incl)
        get = lambda k: [dir_refs[d][k][h, rows(j), :] if k < 3 else dir_refs[d][k][0, h, rows(j), :]
                         for d, j, h in chains]
        R, V, KK, LD, Kd, Bd = (get(k) for k in range(6))
        L = each(lambda m, x: sum(_dot(m, part) for part in _split3(x)), incl_b, LD)
        ltot = each(lambda x: jnp.sum(x, axis=0, keepdims=True), LD)
        e_nl = each(lambda l: jnp.exp(-l), L)
        e_rest = each(lambda l, lt: jnp.exp(lt - l), L, ltot)
        Qb = each(lambda kk, l, ld: bf(kk * jnp.exp(l - ld)), KK, L, LD)
        Rh = each(lambda r, l: r * jnp.exp(l), R, L)
        Rb = each(bf, Rh)
        Btb = each(lambda b, e: bf(b * e), Bd, e_nl)
        Ktb = each(lambda k, e: bf(k * e), Kd, e_nl)
        BcTb = each(lambda b, e: bf((b * e).T), Bd, e_rest)
        KcTb = each(lambda k, e: bf((k * e).T), Kd, e_rest)
        Vb = each(bf, V)
        stack = lambda *xs: jnp.concatenate(xs, axis=0)
        top, mid, low = slice(0, CH), slice(CH, 2 * CH), slice(2 * CH, 3 * CH)
        QRb = each(stack, Qb, Rb)
        QRB = each(_dot_nt, QRb, Btb)
        QRK = each(_dot_nt, QRb, Ktb)
        Nl = each(lambda m, x: m * x[top], strict, QRB)
        Mrbb = each(lambda m, x: bf(m * x[mid]), incl, QRB)
        Mkb = each(lambda m, x: bf(m * x[top]), strict, QRK)
        Mrkb = each(lambda m, x: bf(m * x[mid]), incl, QRK)
        N8 = each(lambda m, n: m * n, m8, Nl)
        N8b = each(bf, N8)
        N2 = each(lambda a: _dot(a, a), N8b)
        N2b = each(bf, N2)
        N4 = each(lambda a: _dot(a, a), N2b)
        W = each(lambda a, b: _dot(bf(eye - a), bf(eye + b)), N8, N2)
        W = each(lambda w, n4: _dot(bf(w), bf(eye + n4)), W, N4)
        for lvl in range(3):
            Wb = each(bf, W)
            T = each(lambda m, n, w: _dot(bf(m * n), w), msk(3 + lvl), Nl, Wb)
            W = each(lambda w, wb, t: w - _dot(wb, bf(t)), W, Wb, T)
        Wb = each(bf, W)
        Whb = each(lambda w, q: bf(_dot(w, q)), Wb, Qb)
        XV = each(lambda mk, mrk, kc, v: _dot(stack(mk, mrk, kc), v), Mkb, Mrkb, KcTb, Vb)
        U0b = each(lambda w, x: bf(-_dot(w, bf(x[top]))), Wb, XV)
        MBb = each(stack, Mrbb, BcTb)
        XW = each(_dot, MBb, Whb)
        XU = each(_dot, MBb, U0b)
        Y0 = each(lambda xu, xv: xu[top] + xv[mid], XU, XV)
        Hm = each(lambda xu, xv: xu[mid] + xv[low], XU, XV)
        RG = each(lambda rh, lt, xw: _split2(stack(rh - xw[top], eye * jnp.exp(lt) - xw[mid])), Rh, ltot, XW)

        ST = [s_scr[d, h] for d, h in lanes]
        for step in range(cb):
            idx = [chains.index((d, step if d == 0 else cb - 1 - step, h)) for d, h in lanes]
            XS = [_mm3(RG[i], _split2(st)) for i, st in zip(idx, ST)]
            Y = [x[top] + Y0[i] for i, x in zip(idx, XS)]
            ST = [x[mid] + Hm[i] for i, x in zip(idx, XS)]
            for i, y in zip(idx, Y):
                d, j, h = chains[i]
                mu = jnp.mean(y, axis=-1, keepdims=True)
                yc = y - mu
                var = jnp.mean(yc * yc, axis=-1, keepdims=True)
                y_refs[d][rows(j), h * HD:(h + 1) * HD] = yc * lax.rsqrt(var + GN_EPS) * gng_ref[h] + gnb_ref[h]
        for (d, h), st in zip(lanes, ST):
            s_scr[d, h] = st

    @pl.when(sched_ref[_SCHED_LAST, step_id] == 1)
    def _():
        sfin_ref[0] = s_scr[...]


_SCHED_FWD, _SCHED_BWD, _SCHED_SEQ, _SCHED_FIRST, _SCHED_LAST = range(5)


def rwkv_schedule(seq_lens):
    blk_rows = RWKV_CHUNKS * CH
    cols, base = [], 0
    for s, t_len in enumerate(seq_lens):
        n_b = t_len // blk_rows
        for i in range(n_b):
            cols.append((base + i, base + n_b - 1 - i, s, int(i == 0), int(i == n_b - 1)))
        base += n_b
    return np.asarray(cols, np.int32).T


def rwkv_scan(prep, masks, gng, gnb, s0t, sched):
    r, v, kk, ld, kd, bd = prep
    n = r.shape[1]
    n_seq = s0t.shape[0]
    blk_rows = RWKV_CHUNKS * CH
    in_specs, args = [], []
    for d, row in ((0, _SCHED_FWD), (1, _SCHED_BWD)):
        for a in (r, v, kk):
            in_specs.append(pl.BlockSpec((HEADS, blk_rows, HD), lambda i, s, row=row: (0, s[row, i], 0)))
            args.append(a)
        for a in (ld, kd, bd):
            in_specs.append(pl.BlockSpec((1, HEADS, blk_rows, HD), lambda i, s, row=row, d=d: (d, 0, s[row, i], 0)))
            args.append(a)
    in_specs += [pl.BlockSpec((2, 6, CH, CH), lambda i, s: (0, 0, 0, 0)),
                 pl.BlockSpec((HEADS, 1, HD), lambda i, s: (0, 0, 0)),
                 pl.BlockSpec((HEADS, 1, HD), lambda i, s: (0, 0, 0)),
                 pl.BlockSpec((1, 2, HEADS, HD, HD), lambda i, s: (s[_SCHED_SEQ, i], 0, 0, 0, 0))]
    args += [masks, gng, gnb, s0t]
    grid_spec = pltpu.PrefetchScalarGridSpec(
        num_scalar_prefetch=1,
        grid=(sched.shape[1],),
        in_specs=in_specs,
        out_specs=[pl.BlockSpec((blk_rows, D_MIX), lambda i, s: (s[_SCHED_FWD, i], 0)),
                   pl.BlockSpec((blk_rows, D_MIX), lambda i, s: (s[_SCHED_BWD, i], 0)),
                   pl.BlockSpec((1, 2, HEADS, HD, HD), lambda i, s: (s[_SCHED_SEQ, i], 0, 0, 0, 0))],
        scratch_shapes=[pltpu.VMEM((2, HEADS, HD, HD), F32)],
    )
    return pl.pallas_call(
        functools.partial(_rwkv_chunk_kernel, RWKV_CHUNKS),
        grid_spec=grid_spec,
        out_shape=[jax.ShapeDtypeStruct((n, D_MIX), F32), jax.ShapeDtypeStruct((n, D_MIX), F32),
                   jax.ShapeDtypeStruct((n_seq, 2, HEADS, HD, HD), F32)],
        compiler_params=_cparams(("arbitrary",), VMEM_LIMIT),
        name="rwkv_scan",
    )(jnp.asarray(sched), *args)


def _ctx_attn_kernel(q_ref, k_ref, v_ref, y_ref, ko_ref, vo_ref):
    scale = HD ** -0.5
    for h in range(HEADS):
        sl = slice(h * HD, (h + 1) * HD)
        q = q_ref[:, sl]
        k = k_ref[:, sl]
        v = v_ref[:, sl]
        ko_ref[0, h] = k
        vo_ref[0, h] = v
        s = _dot_nt(q.astype(BF16), k.astype(BF16)) * scale
        m = jnp.max(s, axis=-1, keepdims=True)
        e = jnp.exp(s - m)
        p = e / jnp.sum(e, axis=-1, keepdims=True)
        y_ref[:, sl] = _dot(p.astype(BF16), v.astype(BF16))


def ctx_attention(z, bsz, t_len):
    qb = A_PAD // D_MIX
    return pl.pallas_call(
        _ctx_attn_kernel,
        grid=(bsz,),
        in_specs=[pl.BlockSpec((t_len, D_MIX), lambda b: (b, qb)),
                  pl.BlockSpec((t_len, D_MIX), lambda b: (b, qb + 1)),
                  pl.BlockSpec((t_len, D_MIX), lambda b: (b, qb + 2))],
        out_specs=[pl.BlockSpec((t_len, D_MIX), lambda b: (b, 0)),
                   pl.BlockSpec((1, HEADS, t_len, HD), lambda b: (b, 0, 0, 0)),
                   pl.BlockSpec((1, HEADS, t_len, HD), lambda b: (b, 0, 0, 0))],
        out_shape=[jax.ShapeDtypeStruct((bsz * t_len, D_MIX), F32),
                   jax.ShapeDtypeStruct((bsz, HEADS, t_len, HD), F32),
                   jax.ShapeDtypeStruct((bsz, HEADS, t_len, HD), F32)],
        compiler_params=_cparams(("arbitrary",)),
        name="ctx_attention",
    )(z, z, z)


def _na_kernel(rows, nblk, *refs):
    q_ref = refs[0]
    k_refs = refs[1:1 + nblk]
    v_refs = refs[1 + nblk:1 + 2 * nblk]
    kc_ref, vc_ref, tab_ref, y_ref = refs[1 + 2 * nblk:]
    r0 = pl.program_id(1) * NA_QROWS
    u0 = jnp.clip(r0 - WIN_R // 2, 0, rows - NA_KROWS)
    scale = HD ** -0.5
    kwin = jnp.concatenate([kr[...] for kr in k_refs], axis=0).astype(BF16)
    vwin = jnp.concatenate([vr[...] for vr in v_refs], axis=0).astype(BF16)
    left = lax.broadcasted_iota(jnp.int32, (1, 2 * GRID_W), 1) < GRID_W
    bias = []
    for i in range(NA_QROWS):
        r = r0 + i
        rs = jnp.clip(r - WIN_R // 2, 0, rows - WIN_R)
        per_pair = []
        for jp in range(NA_KROWS // 2):
            kr = u0 + 2 * jp
            off = [jnp.where(jnp.logical_and(kr + e >= rs, kr + e < rs + WIN_R), 0.0, NEG) for e in range(2)]
            per_pair.append((jnp.clip(kr - r + WIN_R, 0, 2 * WIN_R - 1), jnp.where(left, off[0], off[1])))
        bias.append(per_pair)
    for h in range(HEADS):
        sl = slice(h * HD, (h + 1) * HD)
        q = q_ref[:, sl].astype(BF16)
        s_raw = _dot_nt(q, kwin[:, sl]) * scale
        s_loc = jnp.concatenate([
            jnp.concatenate([s_raw[i * GRID_W:(i + 1) * GRID_W, jp * 2 * GRID_W:(jp + 1) * 2 * GRID_W]
                             + tab_ref[h, bias[i][jp][0]] + bias[i][jp][1]
                             for jp in range(NA_KROWS // 2)], axis=1)
            for i in range(NA_QROWS)], axis=0)
        s_ctx = _dot_nt(q, kc_ref[0, 0, h].astype(BF16)) * scale
        m = jnp.maximum(jnp.max(s_loc, axis=-1, keepdims=True), jnp.max(s_ctx, axis=-1, keepdims=True))
        e_loc = jnp.exp(s_loc - m)
        e_ctx = jnp.exp(s_ctx - m)
        den = jnp.sum(e_loc, axis=-1, keepdims=True) + jnp.sum(e_ctx, axis=-1, keepdims=True)
        p_loc = (e_loc / den).astype(BF16)
        p_ctx = (e_ctx / den).astype(BF16)
        y_ref[:, sl] = _dot(p_loc, vwin[:, sl]) + _dot(p_ctx, vc_ref[0, 0, h].astype(BF16))


def na_bias_table(rpb):
    cq = np.arange(GRID_W)[:, None]
    ck = np.arange(GRID_W)[None, :]
    cs = np.clip(cq - WIN_C // 2, 0, GRID_W - WIN_C)
    col_bias = np.where((ck >= cs) & (ck < cs + WIN_C), 0.0, NEG).astype(np.float32)
    col_idx = np.clip(ck - cq + WIN_C - 1, 0, 2 * WIN_C - 2)
    rpb_col = rpb.astype(F32)[:, :, col_idx] + col_bias
    padded = jnp.pad(rpb_col, ((0, 0), (1, 1), (0, 0), (0, 0)), constant_values=NEG)
    return jnp.concatenate([padded[:, :-1], padded[:, 1:]], axis=-1)


def na_attention(z, k_ctx, v_ctx, layer, bias_tab, bsz, t_len):
    rows = t_len // GRID_W
    assert rows >= NA_KROWS and rows % NA_QROWS == 0, "latent grid too small for the row-group tiling"
    qb = A_PAD // D_MIX
    past = k_ctx.shape[3]
    blk_rows = WIN_R // 2
    nblk = NA_KROWS // blk_rows
    blk_tok = blk_rows * GRID_W
    q_tok = NA_QROWS * GRID_W

    def win_spec(j, col):
        def index(b, g):
            u0 = jnp.clip(g * NA_QROWS - WIN_R // 2, 0, rows - NA_KROWS)
            return (b * (rows // blk_rows) + u0 // blk_rows + j, col)
        return pl.BlockSpec((blk_tok, D_MIX), index)

    in_specs = ([pl.BlockSpec((q_tok, D_MIX), lambda b, g: (b * (rows // NA_QROWS) + g, qb))]
                + [win_spec(j, qb + 1) for j in range(nblk)]
                + [win_spec(j, qb + 2) for j in range(nblk)]
                + [pl.BlockSpec((1, 1, HEADS, past, HD), lambda b, g: (b, layer, 0, 0, 0)),
                   pl.BlockSpec((1, 1, HEADS, past, HD), lambda b, g: (b, layer, 0, 0, 0)),
                   pl.BlockSpec(bias_tab.shape, lambda b, g: (0, 0, 0, 0))])
    return pl.pallas_call(
        functools.partial(_na_kernel, rows, nblk),
        grid=(bsz, rows // NA_QROWS),
        in_specs=in_specs,
        out_specs=pl.BlockSpec((q_tok, D_MIX), lambda b, g: (b * (rows // NA_QROWS) + g, 0)),
        out_shape=jax.ShapeDtypeStruct((bsz * t_len, D_MIX), F32),
        compiler_params=_cparams(("arbitrary", "arbitrary"), VMEM_LIMIT),
        name="na_attention",
    )(*([z] * (1 + 2 * nblk)), k_ctx, v_ctx, bias_tab)


def _s5_kernel(bsz, tb, uf_ref, ub_ref, bm_ref, cm_ref, ar_ref, ai_ref, x0_ref, yf_ref, yb_ref, xf_ref,
               x_scr, carry_scr):
    i = pl.program_id(0)

    @pl.when(i == 0)
    def _():
        carry_scr[...] = x0_ref[...]

    n_re = S5_GW // LANE
    n_rows = S5_GB * bsz
    blk = bsz * tb
    group = max(1, min(n_re, S5_CARRY_VREGS * SUBLANE // (2 * n_rows)))
    ch = D_MIX // S5_GB
    lane = lambda c: slice(c * LANE, (c + 1) * LANE)
    for d, (u_ref, y_ref) in enumerate(((uf_ref, yf_ref), (ub_ref, yb_ref))):
        u = u_ref[...].reshape(blk, D_MIX).astype(BF16)
        for k in range(S5_GB):
            bu = _dot(u[:, k * ch:(k + 1) * ch], bm_ref[d, k])
            for c in range(2 * n_re):
                x_scr[c, k * blk:(k + 1) * blk, :] = bu[:, lane(c)]
        for c0 in range(0, n_re, group):
            tiles = list(range(c0, c0 + group))
            ar = [ar_ref[d, :, lane(c)] for c in tiles]
            ai = [ai_ref[d, :, lane(c)] for c in tiles]

            def step(s, x, d=d, tiles=tiles, ar=ar, ai=ai):
                t = s if d == 0 else tb - 1 - s
                rows = pl.ds(t, n_rows, stride=tb)
                out = []
                for c, a_r, a_i, (xr, xi) in zip(tiles, ar, ai, x):
                    nr = a_r * xr - a_i * xi + x_scr[c, rows, :]
                    ni = a_r * xi + a_i * xr + x_scr[n_re + c, rows, :]
                    x_scr[c, rows, :] = nr
                    x_scr[n_re + c, rows, :] = ni
                    out.append((nr, ni))
                return tuple(out)

            init = tuple((carry_scr[d, :, lane(c)], carry_scr[d, :, lane(n_re + c)]) for c in tiles)
            fin = lax.fori_loop(0, tb, step, init, unroll=4)
            for c, (xr, xi) in zip(tiles, fin):
                carry_scr[d, :, lane(c)] = xr
                carry_scr[d, :, lane(n_re + c)] = xi
        ys = []
        for k in range(S5_GB):
            xs = jnp.concatenate([x_scr[c, k * blk:(k + 1) * blk, :] for c in range(2 * n_re)], axis=-1)
            ys.append(_dot(xs.astype(BF16), cm_ref[k]))
        y_ref[...] = jnp.concatenate(ys, axis=-1).reshape(bsz, tb, D_MIX)

    @pl.when(i == pl.num_programs(0) - 1)
    def _():
        xf_ref[...] = carry_scr[...]


def s5_scan(z3, bmat, cmat, a_re, a_im, x0, tb):
    bsz, t_len, _ = z3.shape
    n_t = t_len // tb
    n_rows = S5_GB * bsz
    ucol = (A_PAD + 3 * D_MIX) // D_MIX
    full = lambda shape: pl.BlockSpec(shape, lambda i: (0,) * len(shape))
    y_shape = jax.ShapeDtypeStruct((bsz, t_len, D_MIX), F32)
    return pl.pallas_call(
        functools.partial(_s5_kernel, bsz, tb),
        grid=(n_t,),
        in_specs=[pl.BlockSpec((bsz, tb, D_MIX), lambda i: (0, i, ucol)),
                  pl.BlockSpec((bsz, tb, D_MIX), lambda i: (0, n_t - 1 - i, ucol)),
                  full((2, S5_GB, D_MIX // S5_GB, 2 * S5_GW)), full((S5_GB, 2 * S5_GW, D_MIX // S5_GB)),
                  full((2, n_rows, S5_GW)), full((2, n_rows, S5_GW)), full((2, n_rows, 2 * S5_GW))],
        out_specs=[pl.BlockSpec((bsz, tb, D_MIX), lambda i: (0, i, 0)),
                   pl.BlockSpec((bsz, tb, D_MIX), lambda i: (0, n_t - 1 - i, 0)),
                   full((2, n_rows, 2 * S5_GW))],
        out_shape=[y_shape, y_shape, jax.ShapeDtypeStruct((2, n_rows, 2 * S5_GW), F32)],
        scratch_shapes=[pltpu.VMEM((2 * S5_GW // LANE, n_rows * tb, LANE), F32),
                        pltpu.VMEM((2, n_rows, 2 * S5_GW), F32)],
        compiler_params=_cparams(("arbitrary",), VMEM_LIMIT),
        name="s5_scan",
    )(z3, z3, bmat, cmat, a_re, a_im, x0)


def s5_rows(x, bsz):
    return x.reshape(2, bsz, S5_GB, S5_GW).transpose(0, 2, 1, 3).reshape(2, S5_GB * bsz, S5_GW)


def s5_unrows(x, bsz):
    return x.reshape(2, S5_GB, bsz, S5_GW).transpose(0, 2, 1, 3).reshape(2, bsz, S5_N)


def _merge_kernel(gnf_ref, gnb_ref, bonus_ref, g_ref, yb_ref, ycf_ref, ycb_ref, u_ref, zg0_ref, zg1_ref, zg2_ref,
                  x_ref, mod_ref, s5d_ref, wglu_ref, bglu_ref, wb_ref, wout_ref, lng_ref, lnb_ref, x1_ref, h2_ref):
    ya = (gnf_ref[...] + gnb_ref[...] + bonus_ref[...]) * g_ref[...]
    yc = ycf_ref[...] + ycb_ref[...] + s5d_ref[...] * u_ref[...]
    yc = 0.5 * yc * (1.0 + jnp.tanh(math.sqrt(2.0 / math.pi) * (yc + 0.044715 * (yc * yc * yc))))
    yc = yc * _sigmoid(_dot(yc.astype(BF16), wglu_ref[...]) + bglu_ref[...])
    merged = (_dot(ya.astype(BF16), wb_ref[0]) * _sigmoid(zg0_ref[...])
              + _dot(yb_ref[...].astype(BF16), wb_ref[1]) * _sigmoid(zg1_ref[...])
              + _dot(yc.astype(BF16), wb_ref[2]) * _sigmoid(zg2_ref[...]))
    mo = _dot(merged.astype(BF16), wout_ref[...])
    m = mod_ref[0]
    x1 = _layer_norm(DN_ALPHA * x_ref[...] + m[2:3] * mo, lng_ref[...], lnb_ref[...])
    x1_ref[...] = x1
    h2_ref[...] = (x1 * (1.0 + m[4:5]) + m[3:4]).astype(BF16)


def merge_branches(gn, bonus, g, yb, yc, z, x, mods, mod_of_tile, p):
    n = x.shape[0]
    row = lambda w, col=0: pl.BlockSpec((TM, w), lambda i, col=col: (i, col))
    full = lambda shape: pl.BlockSpec(shape, lambda i: (0,) * len(shape))
    gb = (A_PAD + 4 * D_MIX) // D_MODEL
    return pl.pallas_call(
        _merge_kernel,
        grid=(n // TM,),
        in_specs=[row(D_MIX), row(D_MIX), row(D_MIX), row(D_MIX), row(D_MIX), row(D_MIX), row(D_MIX),
                  row(D_MIX, A_PAD // D_MIX + 3),
                  row(D_MODEL, gb), row(D_MODEL, gb + 1), row(D_MODEL, gb + 2),
                  row(D_MODEL),
                  pl.BlockSpec((1, SUBLANE, D_MODEL), lambda i: (mod_of_tile(i, TM), 0, 0)),
                  full((1, D_MIX)), full((D_MIX, D_MIX)), full((1, D_MIX)),
                  full((3, D_MIX, D_MODEL)), full((D_MODEL, D_MODEL)),
                  full((1, D_MODEL)), full((1, D_MODEL))],
        out_specs=[row(D_MODEL), row(D_MODEL)],
        out_shape=[jax.ShapeDtypeStruct((n, D_MODEL), F32), jax.ShapeDtypeStruct((n, D_MODEL), BF16)],
        compiler_params=_cparams(("arbitrary",), VMEM_LIMIT),
        name="merge_branches",
    )(gn[0], gn[1], bonus, g, yb, yc[0], yc[1], z, z, z, z, x, mods,
      p['s5_d'], p['w_glu'], p['b_glu'], p['w_branch'], p['w_out'], p['ln1_g'], p['ln1_b'])


def _first_max(val, idx, big):
    m = jnp.max(jnp.max(val, axis=1, keepdims=True), axis=0, keepdims=True)
    cand = jnp.where(val == m, idx, big)
    first = jnp.min(jnp.min(cand, axis=1, keepdims=True), axis=0, keepdims=True)
    return m, idx == first


def _router_kernel(h_ref, wt_ref, bias_ref, gates_ref):
    per = N_EXPERTS // N_GROUPS
    logits = _dot_nt(wt_ref[...], h_ref[...])
    n = logits.shape[1]
    scores = _sigmoid(logits).reshape(N_GROUPS, per, n)
    sel = scores + bias_ref[...]
    e_idx = (lax.broadcasted_iota(jnp.int32, (N_GROUPS, per, n), 0) * per
             + lax.broadcasted_iota(jnp.int32, (N_GROUPS, per, n), 1))
    in_grp = lax.broadcasted_iota(jnp.int32, (N_GROUPS, per, n), 1)
    m1 = jnp.max(sel, axis=1, keepdims=True)
    first = jnp.min(jnp.where(sel == m1, in_grp, per), axis=1, keepdims=True)
    m2 = jnp.max(jnp.where(in_grp == first, -jnp.inf, sel), axis=1, keepdims=True)
    grp = m1 + m2
    g_idx = lax.broadcasted_iota(jnp.int32, (N_GROUPS, 1, n), 0)
    gmask = jnp.zeros((N_GROUPS, 1, n), F32)
    for _ in range(TOPK_GROUPS):
        _, hit = _first_max(grp, g_idx, N_GROUPS)
        gmask = jnp.where(hit, 1.0, gmask)
        grp = jnp.where(hit, -jnp.inf, grp)
    cur = jnp.where(jnp.broadcast_to(gmask, sel.shape) > 0.0, sel, NEG)
    w = jnp.zeros((N_GROUPS, per, n), F32)
    for _ in range(TOP_K):
        _, hit = _first_max(cur, e_idx, N_EXPERTS)
        w = jnp.where(hit, scores, w)
        cur = jnp.where(hit, -jnp.inf, cur)
    tot = jnp.sum(jnp.sum(w, axis=1, keepdims=True), axis=0, keepdims=True)
    gates = (ROUTED_SCALE * w / tot).reshape(N_EXPERTS, n)
    hi = gates.astype(BF16).astype(F32)
    lo = (gates - hi).astype(BF16).astype(F32)
    gates_ref[...] = jnp.concatenate([hi, lo], axis=0).T.astype(BF16)


def moe_router(h2, router_wt, router_bias):
    n = h2.shape[0]
    return pl.pallas_call(
        _router_kernel,
        grid=(n // TM,),
        in_specs=[pl.BlockSpec((TM, D_MODEL), lambda i: (i, 0)),
                  pl.BlockSpec((N_EXPERTS, D_MODEL), lambda i: (0, 0)),
                  pl.BlockSpec((N_GROUPS, N_EXPERTS // N_GROUPS, 1), lambda i: (0, 0, 0))],
        out_specs=pl.BlockSpec((TM, 2 * N_EXPERTS), lambda i: (i, 0)),
        out_shape=jax.ShapeDtypeStruct((n, 2 * N_EXPERTS), BF16),
        compiler_params=_cparams(("arbitrary",)),
        name="moe_router",
    )(h2, router_wt, router_bias)


def _moe_kernel(ec, h_ref, gates_ref, x1_ref, mod_ref, wg_ref, wu_ref, wd_ref, ex_ref,
                sg_ref, su_ref, sd_ref, lng_ref, lnb_ref, out_ref, acc_scr):
    j = pl.program_id(1)
    h = h_ref[...]

    @pl.when(j == 0)
    def _():
        sh = _dot(h, sg_ref[...])
        sh = sh * _sigmoid(sh) * _dot(h, su_ref[...])
        acc_scr[...] = _dot(sh.astype(BF16), sd_ref[...])

    gexp = _dot(gates_ref[...], ex_ref[0])
    hg = _dot(h, wg_ref[...])
    hu = _dot(h, wu_ref[...])
    act = hg * _sigmoid(hg) * hu * gexp
    acc_scr[...] += _dot(act.astype(BF16), wd_ref[...])

    @pl.when(j == pl.num_programs(1) - 1)
    def _():
        m = mod_ref[0]
        out_ref[...] = _layer_norm(DN_ALPHA * x1_ref[...] + m[5:6] * acc_scr[...], lng_ref[...], lnb_ref[...])


def moe_ffn(h2, gates, x1, mods, mod_of_tile, p, tm=512, ec=8):
    n = h2.shape[0]
    wcols = ec * D_EXPERT
    n_j = N_EXPERTS // ec
    full = lambda shape: pl.BlockSpec(shape, lambda i, j: (0,) * len(shape))
    return pl.pallas_call(
        functools.partial(_moe_kernel, ec),
        grid=(n // tm, n_j),
        in_specs=[pl.BlockSpec((tm, D_MODEL), lambda i, j: (i, 0)),
                  pl.BlockSpec((tm, 2 * N_EXPERTS), lambda i, j: (i, 0)),
                  pl.BlockSpec((tm, D_MODEL), lambda i, j: (i, 0)),
                  pl.BlockSpec((1, SUBLANE, D_MODEL), lambda i, j: (mod_of_tile(i, tm), 0, 0)),
                  pl.BlockSpec((D_MODEL, wcols), lambda i, j: (0, j)),
                  pl.BlockSpec((D_MODEL, wcols), lambda i, j: (0, j)),
                  pl.BlockSpec((wcols, D_MODEL), lambda i, j: (j, 0)),
                  pl.BlockSpec((1, 2 * N_EXPERTS, wcols), lambda i, j: (j, 0, 0)),
                  full((D_MODEL, D_SHARED)), full((D_MODEL, D_SHARED)), full((D_SHARED, D_MODEL)),
                  full((1, D_MODEL)), full((1, D_MODEL))],
        out_specs=pl.BlockSpec((tm, D_MODEL), lambda i, j: (i, 0)),
        out_shape=jax.ShapeDtypeStruct((n, D_MODEL), F32),
        scratch_shapes=[pltpu.VMEM((tm, D_MODEL), F32)],
        compiler_params=_cparams(("arbitrary", "arbitrary"), VMEM_LIMIT),
        name="moe_ffn",
    )(h2, gates, x1, mods, p['wg'], p['wu'], p['wd'], p['expand'],
      p['sh_g'], p['sh_u'], p['sh_d'], p['ln2_g'], p['ln2_b'])


def _block_diag2(m):
    z = jnp.zeros_like(m[0])
    return jnp.concatenate([jnp.concatenate([m[0], z], axis=1), jnp.concatenate([z, m[1]], axis=1)], axis=0)


def _layer_params(P, l):
    g = lambda name: P[name][l]
    w_in = g('w_in')
    pad = jnp.zeros((D_MODEL, A_PAD - A_COLS), F32)
    p = {}
    p['w_in'] = jnp.concatenate([w_in[:, :A_COLS], pad, w_in[:, A_COLS:]], axis=1).astype(BF16)
    p['mu'] = jnp.concatenate([g('rwkv_mu'), jnp.zeros((A_PAD - A_COLS,), F32)])[None, :]
    p['w2bd'] = _block_diag2(g('rwkv_w2')).astype(BF16)
    p['a2bd'] = _block_diag2(g('rwkv_a2')).astype(BF16)
    p['g2'] = g('rwkv_g2').astype(BF16)
    p['w0'] = g('rwkv_w0').reshape(1, 2 * D_MIX)
    p['a0'] = g('rwkv_a0').reshape(1, 2 * D_MIX)
    p['kkp'] = g('rwkv_kk')[None, :]
    p['ka'] = g('rwkv_ka')[None, :]
    p['rk'] = g('rwkv_rk')[None, :]
    hid = np.arange(D_MIX) // HD
    p['seg_ones'] = jnp.asarray((hid[:, None] == hid[None, :]).astype(np.float32), dtype=BF16)
    p['gng'] = g('rwkv_gn_g').reshape(HEADS, 1, HD)
    p['gnb'] = g('rwkv_gn_b').reshape(HEADS, 1, HD)
    a = lax.complex(g('s5_a_re'), g('s5_a_im'))
    dt = jnp.exp(g('s5_log_dt'))[..., None]
    a_bar = jnp.exp(dt * a)
    b_bar = ((a_bar - 1.0) / a)[..., None] * lax.complex(g('s5_b_re'), g('s5_b_im'))
    eye_g = jnp.eye(C_GROUPS // S5_GB, dtype=F32)
    gpb = C_GROUPS // S5_GB
    bd = lambda m: jnp.einsum('dkgph,gj->dkghjp', m.reshape(2, S5_GB, gpb, C_STATE, C_GROUP),
                              eye_g).reshape(2, S5_GB, D_MIX // S5_GB, S5_GW)
    p['s5_bmat'] = jnp.concatenate([bd(b_bar.real), bd(b_bar.imag)], axis=-1).astype(BF16)
    cd = lambda m: jnp.einsum('kghp,gj->kjpgh', m.reshape(S5_GB, gpb, C_GROUP, C_STATE),
                              eye_g).reshape(S5_GB, S5_GW, D_MIX // S5_GB)
    p['s5_cmat'] = jnp.concatenate([cd(g('s5_c_re')), -cd(g('s5_c_im'))], axis=1).astype(BF16)
    p['s5_ar'] = a_bar.real.reshape(2, 1, S5_N)
    p['s5_ai'] = a_bar.imag.reshape(2, 1, S5_N)
    p['s5_d'] = g('s5_d')[None, :]
    p['w_glu'] = g('s5_w_glu').astype(BF16)
    p['b_glu'] = g('s5_b_glu')[None, :]
    p['w_branch'] = g('w_branch').astype(BF16)
    p['w_out'] = g('w_out').astype(BF16)
    p['ln1_g'] = g('ln1_g')[None, :]
    p['ln1_b'] = g('ln1_b')[None, :]
    p['router_wt'] = g('router_w').T.astype(BF16)
    p['router_bias'] = g('router_bias').reshape(N_GROUPS, N_EXPERTS // N_GROUPS, 1)
    ecols = N_EXPERTS * D_EXPERT
    p['wg'] = g('exp_w_gate').transpose(1, 0, 2).reshape(D_MODEL, ecols).astype(BF16)
    p['wu'] = g('exp_w_up').transpose(1, 0, 2).reshape(D_MODEL, ecols).astype(BF16)
    p['wd'] = g('exp_w_down').reshape(ecols, D_MODEL).astype(BF16)
    p['sh_g'] = g('sh_w_gate').astype(BF16)
    p['sh_u'] = g('sh_w_up').astype(BF16)
    p['sh_d'] = g('sh_w_down').astype(BF16)
    p['ln2_g'] = g('ln2_g')[None, :]
    p['ln2_b'] = g('ln2_b')[None, :]
    return p


def _expand_table(ec):
    n_j = N_EXPERTS // ec
    t = np.zeros((n_j, 2 * N_EXPERTS, ec * D_EXPERT), np.float32)
    for e in range(N_EXPERTS):
        j, q = divmod(e, ec)
        t[j, e, q * D_EXPERT:(q + 1) * D_EXPERT] = 1.0
        t[j, N_EXPERTS + e, q * D_EXPERT:(q + 1) * D_EXPERT] = 1.0
    return jnp.asarray(t, dtype=BF16)


def kernel(x_prompt, x_sample, c, cache_na_k, cache_na_v, state_rwkv, state_s5_re, state_s5_im, c_ctx, w_ada, b_ada, w_in, rwkv_mu, rwkv_w0, rwkv_w2, rwkv_a0, rwkv_a2, rwkv_g2, rwkv_kk, rwkv_ka, rwkv_rk, rwkv_gn_g, rwkv_gn_b, na_rpb, s5_a_re, s5_a_im, s5_log_dt, s5_b_re, s5_b_im, s5_c_re, s5_c_im, s5_d, s5_w_glu, s5_b_glu, w_branch, w_out, ln1_g, ln1_b, router_w, router_bias, exp_w_gate, exp_w_up, exp_w_down, sh_w_gate, sh_w_up, sh_w_down, ln2_g, ln2_b):
    P = dict(w_in=w_in, rwkv_mu=rwkv_mu, rwkv_w0=rwkv_w0, rwkv_w2=rwkv_w2, rwkv_a0=rwkv_a0, rwkv_a2=rwkv_a2,
             rwkv_g2=rwkv_g2, rwkv_kk=rwkv_kk, rwkv_ka=rwkv_ka, rwkv_rk=rwkv_rk, rwkv_gn_g=rwkv_gn_g,
             rwkv_gn_b=rwkv_gn_b, s5_a_re=s5_a_re, s5_a_im=s5_a_im, s5_log_dt=s5_log_dt, s5_b_re=s5_b_re,
             s5_b_im=s5_b_im, s5_c_re=s5_c_re, s5_c_im=s5_c_im, s5_d=s5_d, s5_w_glu=s5_w_glu,
             s5_b_glu=s5_b_glu, w_branch=w_branch, w_out=w_out, ln1_g=ln1_g, ln1_b=ln1_b, router_w=router_w,
             router_bias=router_bias, exp_w_gate=exp_w_gate, exp_w_up=exp_w_up, exp_w_down=exp_w_down,
             sh_w_gate=sh_w_gate, sh_w_up=sh_w_up, sh_w_down=sh_w_down, ln2_g=ln2_g, ln2_b=ln2_b)
    bc, tc, _ = x_prompt.shape
    bl, tl, _ = x_sample.shape
    depth = w_in.shape[0]

    cond = jnp.concatenate([c_ctx[None, :], c, jnp.zeros((SUBLANE - 1 - bl, D_MODEL), F32)], axis=0)
    ada = ada_modulation(cond, w_ada, b_ada)
    masks = jnp.asarray(_chunk_masks())
    expand = _expand_table(8)

    paths = {
        'ctx': dict(bsz=bc, t=tc, x=x_prompt.reshape(bc * tc, D_MODEL), mod_of_tile=lambda i, tm: 0),
        'lat': dict(bsz=bl, t=tl, x=x_sample.reshape(bl * tl, D_MODEL),
                    mod_of_tile=lambda i, tm: 1 + (i * tm) // tl),
    }
    for q in paths.values():
        tiles = q['t'] // TM
        pos = np.arange(q['bsz'] * tiles) % tiles
        q['seq_tiles'] = jnp.asarray(np.concatenate([np.full_like(pos, tiles), pos]).astype(np.int32))
        q['sched'] = rwkv_schedule([q['t']] * q['bsz'])
        q['s5_tb'] = S5_ROWS // q['bsz']
        assert q['s5_tb'] % SUBLANE == 0 and q['t'] % q['s5_tb'] == 0

    new_k, new_v, new_rwkv, new_s5 = [], [], [], []
    for l in range(depth):
        p = _layer_params(P, l)
        p['expand'] = expand
        mods = ada[l, :1 + bl].reshape(1 + bl, 6, D_MODEL)
        mods = jnp.concatenate([mods, jnp.zeros((1 + bl, SUBLANE - 6, D_MODEL), F32)], axis=1)
        bias_tab = na_bias_table(na_rpb[l])
        for name, q in paths.items():
            bsz, t_len, x, mod_of_tile = q['bsz'], q['t'], q['x'], q['mod_of_tile']
            z = in_projection(x, mods, p['w_in'], mod_of_tile)

            r, v, kk, ld, kd, bd, g, bonus = rwkv_prep(z, q['seq_tiles'], p)
            if name == 'ctx':
                s0t = jnp.zeros((bsz, 2, HEADS, HD, HD), F32)
            else:
                s0t = jnp.swapaxes(state_rwkv[:, l], -1, -2)
            gn_f, gn_b, s_fin = rwkv_scan((r, v, kk, ld, kd, bd), masks, p['gng'], p['gnb'], s0t, q['sched'])

            if name == 'ctx':
                yb, k_h, v_h = ctx_attention(z, bsz, t_len)
            else:
                yb = na_attention(z, cache_na_k, cache_na_v, l, bias_tab, bsz, t_len)

            if name == 'ctx':
                x0 = jnp.zeros((2, S5_GB * bsz, 2 * S5_GW), F32)
            else:
                x0 = jnp.concatenate(
                    [s5_rows(jnp.swapaxes(s[:, l].reshape(bsz, 2, S5_N), 0, 1), bsz)
                     for s in (state_s5_re, state_s5_im)], axis=-1)
            a_re, a_im = (s5_rows(jnp.broadcast_to(a, (2, bsz, S5_N)), bsz) for a in (p['s5_ar'], p['s5_ai']))
            yc_f, yc_b, x_fin = s5_scan(z.reshape(bsz, t_len, IN_COLS_P), p['s5_bmat'], p['s5_cmat'],
                                        a_re, a_im, x0, q['s5_tb'])
            yc = (yc_f.reshape(bsz * t_len, D_MIX), yc_b.reshape(bsz * t_len, D_MIX))

            x1, h2 = merge_branches((gn_f, gn_b), bonus, g, yb, yc, z, x, mods, mod_of_tile, p)
            gates = moe_router(h2, p['router_wt'], p['router_bias'])
            q['x'] = moe_ffn(h2, gates, x1, mods, mod_of_tile, p)
            if name == 'ctx':
                new_k.append(k_h)
                new_v.append(v_h)
                new_rwkv.append(jnp.swapaxes(s_fin, -1, -2))
                new_s5.append([jnp.swapaxes(s5_unrows(part, bsz), 0, 1)
                               for part in (x_fin[..., :S5_GW], x_fin[..., S5_GW:])])

    s5_re, s5_im = (jnp.stack([layer[part] for layer in new_s5], axis=1).reshape(bc, depth, 2, C_GROUPS, C_STATE)
                    for part in range(2))
    return (paths['ctx']['x'].reshape(bc, tc, D_MODEL), paths['lat']['x'].reshape(bl, tl, D_MODEL),
            jnp.stack(new_k, axis=1), jnp.stack(new_v, axis=1), jnp.stack(new_rwkv, axis=1), s5_re, s5_im)
```

```python
import functools
import math

import numpy as np
import jax
import jax.numpy as jnp
from jax import lax
from jax.experimental import pallas as pl
from jax.experimental.pallas import tpu as pltpu

F32 = jnp.float32
BF16 = jnp.bfloat16

D_MODEL = 1024
DEPTH = 2
GRID_W = 64
D_MIX = 512
HEADS = 8
HD = 64
LORA = 128
WIN_R = 8
WIN_C = 16
C_GROUP = 16
C_GROUPS = D_MIX // C_GROUP
C_STATE = 64
S5_N = C_GROUPS * C_STATE
NA_QROWS = 8
NA_KROWS = NA_QROWS + WIN_R
S5_GB = 4
S5_GW = S5_N // S5_GB
S5_ROWS = 512
S5_CARRY_VREGS = 32
N_EXPERTS = 64
TOP_K = 8
N_GROUPS = 8
TOPK_GROUPS = 4
D_EXPERT = 128
D_SHARED = 128
ROUTED_SCALE = 2.5
LN_EPS = 1e-5
GN_EPS = 64e-5
NEG = -1e30
DN_ALPHA = (2 * DEPTH) ** 0.25
A_COLS = 3 * D_MIX + 3 * LORA
A_PAD = 2048
IN_COLS_P = A_PAD + 4 * D_MIX + 3 * D_MODEL

LANE = 128
SUBLANE = 8
TM = 256
CH = 64
RWKV_LANES = 16
RWKV_CHUNKS = 2
VMEM_LIMIT = 56 * 1024 * 1024


def _cparams(sem, vmem=None):
    return pltpu.CompilerParams(dimension_semantics=sem, vmem_limit_bytes=vmem)


def _sigmoid(x):
    return 1.0 / (1.0 + jnp.exp(-x))


def _dot(a, b, precision=None):
    return jnp.dot(a, b, preferred_element_type=F32, precision=precision)


def _dot_nt(a, b, precision=None):
    return lax.dot_general(a, b, (((1,), (1,)), ((), ())), preferred_element_type=F32, precision=precision)


def _dot_tn(a, b, precision=None):
    return lax.dot_general(a, b, (((0,), (0,)), ((), ())), preferred_element_type=F32, precision=precision)


def _split2(x):
    hi = x.astype(BF16)
    return hi, (x - hi.astype(F32)).astype(BF16)


def _split3(x):
    hi = x.astype(BF16)
    r1 = x - hi.astype(F32)
    mid = r1.astype(BF16)
    return hi, mid, (r1 - mid.astype(F32)).astype(BF16)


def _seg_sum(x, ones_bf16):
    return sum(_dot(part, ones_bf16) for part in _split3(x))


def _mm3(a, b, dot=_dot):
    return dot(a[0], b[0]) + (dot(a[1], b[0]) + dot(a[0], b[1]))


def _mm3_nt(a, b):
    return _mm3(a, b, _dot_nt)


def _layer_norm(x, g, b):
    mu = jnp.mean(x, axis=-1, keepdims=True)
    xc = x - mu
    var = jnp.mean(xc * xc, axis=-1, keepdims=True)
    return xc * lax.rsqrt(var + LN_EPS) * g + b


def _ada_kernel(c_ref, w_ref, b_ref, o_ref):
    c = c_ref[...]
    s = c * _sigmoid(c)
    o_ref[0] = _dot(s.astype(BF16), w_ref[0]) + b_ref[0]


def ada_modulation(cond, w_ada, b_ada):
    n_l, d, n6 = w_ada.shape
    tn = 1536
    return pl.pallas_call(
        _ada_kernel,
        grid=(n_l, n6 // tn),
        in_specs=[pl.BlockSpec((SUBLANE, d), lambda l, j: (0, 0)),
                  pl.BlockSpec((1, d, tn), lambda l, j: (l, 0, j)),
                  pl.BlockSpec((1, 1, tn), lambda l, j: (l, 0, j))],
        out_specs=pl.BlockSpec((1, SUBLANE, tn), lambda l, j: (l, 0, j)),
        out_shape=jax.ShapeDtypeStruct((n_l, SUBLANE, n6), F32),
        compiler_params=_cparams(("arbitrary", "arbitrary")),
        name="ada_modulation",
    )(cond, w_ada.astype(BF16), b_ada.reshape(n_l, 1, n6))


def _inproj_kernel(x_ref, mod_ref, w_ref, z_ref, h_scr):
    @pl.when(pl.program_id(1) == 0)
    def _():
        m = mod_ref[0]
        h_scr[...] = (x_ref[...] * (1.0 + m[1:2]) + m[0:1]).astype(BF16)
    z_ref[...] = _dot(h_scr[...], w_ref[...])


def in_projection(x, mods, w_in_p, mod_of_tile, tm=2048, tn=1024):
    n, d = x.shape
    cols = w_in_p.shape[1]
    return pl.pallas_call(
        _inproj_kernel,
        grid=(n // tm, cols // tn),
        in_specs=[pl.BlockSpec((tm, d), lambda i, j: (i, 0)),
                  pl.BlockSpec((1, SUBLANE, d), lambda i, j: (mod_of_tile(i, tm), 0, 0)),
                  pl.BlockSpec((d, tn), lambda i, j: (0, j))],
        out_specs=pl.BlockSpec((tm, tn), lambda i, j: (i, j)),
        out_shape=jax.ShapeDtypeStruct((n, cols), F32),
        scratch_shapes=[pltpu.VMEM((tm, d), BF16)],
        compiler_params=_cparams(("arbitrary", "arbitrary"), VMEM_LIMIT),
        name="in_projection",
    )(x, mods, w_in_p)


def _rwkv_prep_kernel(seq_tiles_ref, z_ref, zp_ref, zn_ref, mu_ref, w2_ref, a2_ref, g2_ref, w0_ref, a0_ref,
                      kkp_ref, ka_ref, rk_ref, e_ref,
                      r_ref, v_ref, kk_ref, ld_ref, kd_ref, bd_ref, g_ref, bonus_ref):
    i = pl.program_id(0)
    tiles = seq_tiles_ref[i]
    pos = seq_tiles_ref[i + pl.num_programs(0)]
    x = z_ref[...]
    tm = x.shape[0]
    rows = lax.broadcasted_iota(jnp.int32, x.shape, 0)
    prev_row = jnp.where(pos == 0, 0.0, zp_ref[SUBLANE - 1:SUBLANE, :])
    next_row = jnp.where(pos == tiles - 1, 0.0, zn_ref[0:1, :])
    xm1 = jnp.where(rows == 0, prev_row, pltpu.roll(x, 1, axis=0))
    xp1 = jnp.where(rows == tm - 1, next_row, pltpu.roll(x, tm - 1, axis=0))
    za = x + mu_ref[...] * (0.5 * (xm1 + xp1) - x)

    r = za[:, 0:D_MIX]
    k = za[:, D_MIX:2 * D_MIX]
    v = za[:, 2 * D_MIX:3 * D_MIX]
    lw = za[:, 3 * D_MIX:3 * D_MIX + LORA]
    la = za[:, 3 * D_MIX + LORA:3 * D_MIX + 2 * LORA]
    lg = za[:, 3 * D_MIX + 2 * LORA:3 * D_MIX + 3 * LORA]

    w_both = w0_ref[...] + _dot(jnp.tanh(lw).astype(BF16), w2_ref[...])
    a_both = _sigmoid(a0_ref[...] + _dot(la.astype(BF16), a2_ref[...]))
    g_ref[...] = _dot(_sigmoid(lg).astype(BF16), g2_ref[...])

    e = e_ref[...]
    kks = k * kkp_ref[...]
    nrm = jnp.sqrt(_seg_sum(kks * kks, e))
    kk = kks / jnp.maximum(nrm, 1e-12)
    bonus = jnp.zeros_like(v)
    for h in range(HEADS):
        sl = slice(h * HD, (h + 1) * HD)
        r_ref[h] = r[:, sl]
        v_ref[h] = v[:, sl]
        kk_ref[h] = kk[:, sl]
    for d in range(2):
        w = w_both[:, d * D_MIX:(d + 1) * D_MIX]
        a = a_both[:, d * D_MIX:(d + 1) * D_MIX]
        ld = -math.exp(-0.5) * _sigmoid(w)
        kd = k * (1.0 + (a - 1.0) * ka_ref[...])
        bd = kk * a
        bonus = bonus + _seg_sum(r * kd * rk_ref[...], e) * v
        for h in range(HEADS):
            sl = slice(h * HD, (h + 1) * HD)
            ld_ref[d, h] = ld[:, sl]
            kd_ref[d, h] = kd[:, sl]
            bd_ref[d, h] = bd[:, sl]
    bonus_ref[...] = bonus


def rwkv_prep(z, seq_tiles, p):
    n = z.shape[0]
    nt = n // TM
    halo = TM // SUBLANE
    nb8 = n // SUBLANE
    hm = jax.ShapeDtypeStruct((HEADS, n, HD), F32)
    hm2 = jax.ShapeDtypeStruct((2, HEADS, n, HD), F32)
    tok = jax.ShapeDtypeStruct((n, D_MIX), F32)
    full = lambda shape: pl.BlockSpec(shape, lambda i, s: (0,) * len(shape))
    hm_spec = pl.BlockSpec((HEADS, TM, HD), lambda i, s: (0, i, 0))
    hm2_spec = pl.BlockSpec((2, HEADS, TM, HD), lambda i, s: (0, 0, i, 0))
    tok_spec = pl.BlockSpec((TM, D_MIX), lambda i, s: (i, 0))
    grid_spec = pltpu.PrefetchScalarGridSpec(
        num_scalar_prefetch=1,
        grid=(nt,),
        in_specs=[pl.BlockSpec((TM, A_PAD), lambda i, s: (i, 0)),
                  pl.BlockSpec((SUBLANE, A_PAD), lambda i, s: (jnp.maximum(i * halo - 1, 0), 0)),
                  pl.BlockSpec((SUBLANE, A_PAD), lambda i, s: (jnp.minimum((i + 1) * halo, nb8 - 1), 0)),
                  full((1, A_PAD)), full((LORA, 2 * D_MIX)), full((LORA, 2 * D_MIX)), full((LORA, D_MIX)),
                  full((1, 2 * D_MIX)), full((1, 2 * D_MIX)), full((1, D_MIX)), full((1, D_MIX)),
                  full((1, D_MIX)), full((D_MIX, D_MIX))],
        out_specs=[hm_spec, hm_spec, hm_spec, hm2_spec, hm2_spec, hm2_spec, tok_spec, tok_spec],
    )
    return pl.pallas_call(
        _rwkv_prep_kernel,
        grid_spec=grid_spec,
        out_shape=[hm, hm, hm, hm2, hm2, hm2, tok, tok],
        compiler_params=_cparams(("arbitrary",), VMEM_LIMIT),
        name="rwkv_prep",
    )(seq_tiles, z, z, z, p['mu'], p['w2bd'], p['a2bd'], p['g2'], p['w0'], p['a0'],
      p['kkp'], p['ka'], p['rk'], p['seg_ones'])


def _chunk_masks():
    t = np.arange(CH)
    fwd_incl = (t[:, None] >= t[None, :])
    out = []
    for incl in (fwd_incl, fwd_incl.T):
        strict = incl & (t[:, None] != t[None, :])
        ms = [incl, strict, strict & ((t[:, None] // 8) == (t[None, :] // 8))]
        for m in (8, 16, 32):
            ms.append(strict & ((t[:, None] // (2 * m)) == (t[None, :] // (2 * m)))
                      & ((t[:, None] // m) != (t[None, :] // m)))
        out.append(np.stack(ms))
    return np.stack(out).astype(np.float32)


def _rwkv_chunk_kernel(cb, sched_ref, *refs):
    dir_refs = (refs[0:6], refs[6:12])
    m_ref, gng_ref, gnb_ref, s0_ref = refs[12:16]
    y_refs = refs[16:18]
    sfin_ref, s_scr = refs[18:20]
    step_id = pl.program_id(0)
    first = sched_ref[_SCHED_FIRST, step_id]

    @pl.when(first == 1)
    def _():
        s_scr[...] = s0_ref[0]

    ri = lax.broadcasted_iota(jnp.int32, (CH, CH), 0)
    ci = lax.broadcasted_iota(jnp.int32, (CH, CH), 1)
    eye = (ri == ci).astype(F32)

    bf = lambda x: x.astype(BF16)
    each = lambda f, *cols: [f(*args) for args in zip(*cols)]
    rows = lambda j: slice(j * CH, (j + 1) * CH)

    all_lanes = [(d, h) for d in range(2) for h in range(HEADS)]
    for g0 in range(0, len(all_lanes), RWKV_LANES):
        lanes = all_lanes[g0:g0 + RWKV_LANES]
        chains = [(d, j, h) for d, h in lanes for j in range(cb)]
        msk = lambda k: [m_ref[d, k] for d, _, _ in chains]
        incl, strict, m8 = msk(0), msk(1), msk(2)
        incl_b = each(bf, incl)
        get = lambda k: [dir_refs[d][k][h, rows(j), :] if k < 3 else dir_refs[d][k][0, h, rows(j), :]
                         for d, j, h in chains]
        R, V, KK, LD, Kd, Bd = (get(k) for k in range(6))
        L = each(lambda m, x: sum(_dot(m, part) for part in _split3(x)), incl_b, LD)
        ltot = each(lambda x: jnp.sum(x, axis=0, keepdims=True), LD)
        e_nl = each(lambda l: jnp.exp(-l), L)
        e_rest = each(lambda l, lt: jnp.exp(lt - l), L, ltot)
        Qb = each(lambda kk, l, ld: bf(kk * jnp.exp(l - ld)), KK, L, LD)
        Rh = each(lambda r, l: r * jnp.exp(l), R, L)
        Rb = each(bf, Rh)
        Btb = each(lambda b, e: bf(b * e), Bd, e_nl)
        Ktb = each(lambda k, e: bf(k * e), Kd, e_nl)
        BcTb = each(lambda b, e: bf((b * e).T), Bd, e_rest)
        KcTb = each(lambda k, e: bf((k * e).T), Kd, e_rest)
        Vb = each(bf, V)
        stack = lambda *xs: jnp.concatenate(xs, axis=0)
        top, mid, low = slice(0, CH), slice(CH, 2 * CH), slice(2 * CH, 3 * CH)
        QRb = each(stack, Qb, Rb)
        QRB = each(_dot_nt, QRb, Btb)
        QRK = each(_dot_nt, QRb, Ktb)
        Nl = each(lambda m, x: m * x[top], strict, QRB)
        Mrbb = each(lambda m, x: bf(m * x[mid]), incl, QRB)
        Mkb = each(lambda m, x: bf(m * x[top]), strict, QRK)
        Mrkb = each(lambda m, x: bf(m * x[mid]), incl, QRK)
        N8 = each(lambda m, n: m * n, m8, Nl)
        N8b = each(bf, N8)
        N2 = each(lambda a: _dot(a, a), N8b)
        N2b = each(bf, N2)
        N4 = each(lambda a: _dot(a, a), N2b)
        W = each(lambda a, b: _dot(bf(eye - a), bf(eye + b)), N8, N2)
        W = each(lambda w, n4: _dot(bf(w), bf(eye + n4)), W, N4)
        for lvl in range(3):
            Wb = each(bf, W)
            T = each(lambda m, n, w: _dot(bf(m * n), w), msk(3 + lvl), Nl, Wb)
            W = each(lambda w, wb, t: w - _dot(wb, bf(t)), W, Wb, T)
        Wb = each(bf, W)
        Whb = each(lambda w, q: bf(_dot(w, q)), Wb, Qb)
        XV = each(lambda mk, mrk, kc, v: _dot(stack(mk, mrk, kc), v), Mkb, Mrkb, KcTb, Vb)
        U0b = each(lambda w, x: bf(-_dot(w, bf(x[top]))), Wb, XV)
        MBb = each(stack, Mrbb, BcTb)
        XW = each(_dot, MBb, Whb)
        XU = each(_dot, MBb, U0b)
        Y0 = each(lambda xu, xv: xu[top] + xv[mid], XU, XV)
        Hm = each(lambda xu, xv: xu[mid] + xv[low], XU, XV)
        RG = each(lambda rh, lt, xw: _split2(stack(rh - xw[top], eye * jnp.exp(lt) - xw[mid])), Rh, ltot, XW)

        ST = [s_scr[d, h] for d, h in lanes]
        for step in range(cb):
            idx = [chains.index((d, step if d == 0 else cb - 1 - step, h)) for d, h in lanes]
            XS = [_mm3(RG[i], _split2(st)) for i, st in zip(idx, ST)]
            Y = [x[top] + Y0[i] for i, x in zip(idx, XS)]
            ST = [x[mid] + Hm[i] for i, x in zip(idx, XS)]
            for i, y in zip(idx, Y):
                d, j, h = chains[i]
                mu = jnp.mean(y, axis=-1, keepdims=True)
                yc = y - mu
                var = jnp.mean(yc * yc, axis=-1, keepdims=True)
                y_refs[d][rows(j), h * HD:(h + 1) * HD] = yc * lax.rsqrt(var + GN_EPS) * gng_ref[h] + gnb_ref[h]
        for (d, h), st in zip(lanes, ST):
            s_scr[d, h] = st

    @pl.when(sched_ref[_SCHED_LAST, step_id] == 1)
    def _():
        sfin_ref[0] = s_scr[...]


_SCHED_FWD, _SCHED_BWD, _SCHED_SEQ, _SCHED_FIRST, _SCHED_LAST = range(5)


def rwkv_schedule(seq_lens):
    blk_rows = RWKV_CHUNKS * CH
    cols, base = [], 0
    for s, t_len in enumerate(seq_lens):
        n_b = t_len // blk_rows
        for i in range(n_b):
            cols.append((base + i, base + n_b - 1 - i, s, int(i == 0), int(i == n_b - 1)))
        base += n_b
    return np.asarray(cols, np.int32).T


def rwkv_scan(prep, masks, gng, gnb, s0t, sched):
    r, v, kk, ld, kd, bd = prep
    n = r.shape[1]
    n_seq = s0t.shape[0]
    blk_rows = RWKV_CHUNKS * CH
    in_specs, args = [], []
    for d, row in ((0, _SCHED_FWD), (1, _SCHED_BWD)):
        for a in (r, v, kk):
            in_specs.append(pl.BlockSpec((HEADS, blk_rows, HD), lambda i, s, row=row: (0, s[row, i], 0)))
            args.append(a)
        for a in (ld, kd, bd):
            in_specs.append(pl.BlockSpec((1, HEADS, blk_rows, HD), lambda i, s, row=row, d=d: (d, 0, s[row, i], 0)))
            args.append(a)
    in_specs += [pl.BlockSpec((2, 6, CH, CH), lambda i, s: (0, 0, 0, 0)),
                 pl.BlockSpec((HEADS, 1, HD), lambda i, s: (0, 0, 0)),
                 pl.BlockSpec((HEADS, 1, HD), lambda i, s: (0, 0, 0)),
                 pl.BlockSpec((1, 2, HEADS, HD, HD), lambda i, s: (s[_SCHED_SEQ, i], 0, 0, 0, 0))]
    args += [masks, gng, gnb, s0t]
    grid_spec = pltpu.PrefetchScalarGridSpec(
        num_scalar_prefetch=1,
        grid=(sched.shape[1],),
        in_specs=in_specs,
        out_specs=[pl.BlockSpec((blk_rows, D_MIX), lambda i, s: (s[_SCHED_FWD, i], 0)),
                   pl.BlockSpec((blk_rows, D_MIX), lambda i, s: (s[_SCHED_BWD, i], 0)),
                   pl.BlockSpec((1, 2, HEADS, HD, HD), lambda i, s: (s[_SCHED_SEQ, i], 0, 0, 0, 0))],
        scratch_shapes=[pltpu.VMEM((2, HEADS, HD, HD), F32)],
    )
    return pl.pallas_call(
        functools.partial(_rwkv_chunk_kernel, RWKV_CHUNKS),
        grid_spec=grid_spec,
        out_shape=[jax.ShapeDtypeStruct((n, D_MIX), F32), jax.ShapeDtypeStruct((n, D_MIX), F32),
                   jax.ShapeDtypeStruct((n_seq, 2, HEADS, HD, HD), F32)],
        compiler_params=_cparams(("arbitrary",), VMEM_LIMIT),
        name="rwkv_scan",
    )(jnp.asarray(sched), *args)


def _ctx_attn_kernel(q_ref, k_ref, v_ref, y_ref, ko_ref, vo_ref):
    scale = HD ** -0.5
    for h in range(HEADS):
        sl = slice(h * HD, (h + 1) * HD)
        q = q_ref[:, sl]
        k = k_ref[:, sl]
        v = v_ref[:, sl]
        ko_ref[0, h] = k
        vo_ref[0, h] = v
        s = _dot_nt(q.astype(BF16), k.astype(BF16)) * scale
        m = jnp.max(s, axis=-1, keepdims=True)
        e = jnp.exp(s - m)
        p = e / jnp.sum(e, axis=-1, keepdims=True)
        y_ref[:, sl] = _dot(p.astype(BF16), v.astype(BF16))


def ctx_attention(z, bsz, t_len):
    qb = A_PAD // D_MIX
    return pl.pallas_call(
        _ctx_attn_kernel,
        grid=(bsz,),
        in_specs=[pl.BlockSpec((t_len, D_MIX), lambda b: (b, qb)),
                  pl.BlockSpec((t_len, D_MIX), lambda b: (b, qb + 1)),
                  pl.BlockSpec((t_len, D_MIX), lambda b: (b, qb + 2))],
        out_specs=[pl.BlockSpec((t_len, D_MIX), lambda b: (b, 0)),
                   pl.BlockSpec((1, HEADS, t_len, HD), lambda b: (b, 0, 0, 0)),
                   pl.BlockSpec((1, HEADS, t_len, HD), lambda b: (b, 0, 0, 0))],
        out_shape=[jax.ShapeDtypeStruct((bsz * t_len, D_MIX), F32),
                   jax.ShapeDtypeStruct((bsz, HEADS, t_len, HD), F32),
                   jax.ShapeDtypeStruct((bsz, HEADS, t_len, HD), F32)],
        compiler_params=_cparams(("arbitrary",)),
        name="ctx_attention",
    )(z, z, z)


def _na_kernel(rows, nblk, *refs):
    q_ref = refs[0]
    k_refs = refs[1:1 + nblk]
    v_refs = refs[1 + nblk:1 + 2 * nblk]
    kc_ref, vc_ref, tab_ref, y_ref = refs[1 + 2 * nblk:]
    r0 = pl.program_id(1) * NA_QROWS
    u0 = jnp.clip(r0 - WIN_R // 2, 0, rows - NA_KROWS)
    scale = HD ** -0.5
    kwin = jnp.concatenate([kr[...] for kr in k_refs], axis=0).astype(BF16)
    vwin = jnp.concatenate([vr[...] for vr in v_refs], axis=0).astype(BF16)
    left = lax.broadcasted_iota(jnp.int32, (1, 2 * GRID_W), 1) < GRID_W
    bias = []
    for i in range(NA_QROWS):
        r = r0 + i
        rs = jnp.clip(r - WIN_R // 2, 0, rows - WIN_R)
        per_pair = []
        for jp in range(NA_KROWS // 2):
            kr = u0 + 2 * jp
            off = [jnp.where(jnp.logical_and(kr + e >= rs, kr + e < rs + WIN_R), 0.0, NEG) for e in range(2)]
            per_pair.append((jnp.clip(kr - r + WIN_R, 0, 2 * WIN_R - 1), jnp.where(left, off[0], off[1])))
        bias.append(per_pair)
    for h in range(HEADS):
        sl = slice(h * HD, (h + 1) * HD)
        q = q_ref[:, sl].astype(BF16)
        s_raw = _dot_nt(q, kwin[:, sl]) * scale
        s_loc = jnp.concatenate([
            jnp.concatenate([s_raw[i * GRID_W:(i + 1) * GRID_W, jp * 2 * GRID_W:(jp + 1) * 2 * GRID_W]
                             + tab_ref[h, bias[i][jp][0]] + bias[i][jp][1]
                             for jp in range(NA_KROWS // 2)], axis=1)
            for i in range(NA_QROWS)], axis=0)
        s_ctx = _dot_nt(q, kc_ref[0, 0, h].astype(BF16)) * scale
        m = jnp.maximum(jnp.max(s_loc, axis=-1, keepdims=True), jnp.max(s_ctx, axis=-1, keepdims=True))
        e_loc = jnp.exp(s_loc - m)
        e_ctx = jnp.exp(s_ctx - m)
        den = jnp.sum(e_loc, axis=-1, keepdims=True) + jnp.sum(e_ctx, axis=-1, keepdims=True)
        p_loc = (e_loc / den).astype(BF16)
        p_ctx = (e_ctx / den).astype(BF16)
        y_ref[:, sl] = _dot(p_loc, vwin[:, sl]) + _dot(p_ctx, vc_ref[0, 0, h].astype(BF16))


def na_bias_table(rpb):
    cq = np.arange(GRID_W)[:, None]
    ck = np.arange(GRID_W)[None, :]
    cs = np.clip(cq - WIN_C // 2, 0, GRID_W - WIN_C)
    col_bias = np.where((ck >= cs) & (ck < cs + WIN_C), 0.0, NEG).astype(np.float32)
    col_idx = np.clip(ck - cq + WIN_C - 1, 0, 2 * WIN_C - 2)
    rpb_col = rpb.astype(F32)[:, :, col_idx] + col_bias
    padded = jnp.pad(rpb_col, ((0, 0), (1, 1), (0, 0), (0, 0)), constant_values=NEG)
    return jnp.concatenate([padded[:, :-1], padded[:, 1:]], axis=-1)


def na_attention(z, k_ctx, v_ctx, layer, bias_tab, bsz, t_len):
    rows = t_len // GRID_W
    assert rows >= NA_KROWS and rows % NA_QROWS == 0, "latent grid too small for the row-group tiling"
    qb = A_PAD // D_MIX
    past = k_ctx.shape[3]
    blk_rows = WIN_R // 2
    nblk = NA_KROWS // blk_rows
    blk_tok = blk_rows * GRID_W
    q_tok = NA_QROWS * GRID_W

    def win_spec(j, col):
        def index(b, g):
            u0 = jnp.clip(g * NA_QROWS - WIN_R // 2, 0, rows - NA_KROWS)
            return (b * (rows // blk_rows) + u0 // blk_rows + j, col)
        return pl.BlockSpec((blk_tok, D_MIX), index)

    in_specs = ([pl.BlockSpec((q_tok, D_MIX), lambda b, g: (b * (rows // NA_QROWS) + g, qb))]
                + [win_spec(j, qb + 1) for j in range(nblk)]
                + [win_spec(j, qb + 2) for j in range(nblk)]
                + [pl.BlockSpec((1, 1, HEADS, past, HD), lambda b, g: (b, layer, 0, 0, 0)),
                   pl.BlockSpec((1, 1, HEADS, past, HD), lambda b, g: (b, layer, 0, 0, 0)),
                   pl.BlockSpec(bias_tab.shape, lambda b, g: (0, 0, 0, 0))])
    return pl.pallas_call(
        functools.partial(_na_kernel, rows, nblk),
        grid=(bsz, rows // NA_QROWS),
        in_specs=in_specs,
        out_specs=pl.BlockSpec((q_tok, D_MIX), lambda b, g: (b * (rows // NA_QROWS) + g, 0)),
        out_shape=jax.ShapeDtypeStruct((bsz * t_len, D_MIX), F32),
        compiler_params=_cparams(("arbitrary", "arbitrary"), VMEM_LIMIT),
        name="na_attention",
    )(*([z] * (1 + 2 * nblk)), k_ctx, v_ctx, bias_tab)


def _s5_kernel(bsz, tb, uf_ref, ub_ref, bm_ref, cm_ref, ar_ref, ai_ref, x0_ref, yf_ref, yb_ref, xf_ref,
               x_scr, xo_scr, carry_scr):
    i = pl.program_id(0)

    @pl.when(i == 0)
    def _():
        carry_scr[...] = x0_ref[...]

    n_re = S5_GW // LANE
    n_rows = S5_GB * bsz
    blk = bsz * tb
    group = max(1, min(n_re, S5_CARRY_VREGS * SUBLANE // (2 * n_rows)))
    ch = D_MIX // S5_GB
    lane = lambda c: slice(c * LANE, (c + 1) * LANE)
    step_major = n_rows == SUBLANE
    if step_major:
        steps_of = lambda row: pl.ds(row, tb, stride=n_rows)
        rows_of = lambda t: pl.ds(pl.multiple_of(t * n_rows, SUBLANE), n_rows)
    else:
        steps_of = lambda row: pl.ds(row * tb, tb)
        rows_of = lambda t: pl.ds(t, n_rows, stride=tb)
    for d, (u_ref, y_ref) in enumerate(((uf_ref, yf_ref), (ub_ref, yb_ref))):
        u = u_ref[...].reshape(blk, D_MIX).astype(BF16)
        for k in range(S5_GB):
            bu = _dot(u[:, k * ch:(k + 1) * ch], bm_ref[d, k])
            for c in range(2 * n_re):
                if step_major:
                    for b in range(bsz):
                        x_scr[c, steps_of(k * bsz + b), :] = bu[b * tb:(b + 1) * tb, lane(c)]
                else:
                    x_scr[c, k * blk:(k + 1) * blk, :] = bu[:, lane(c)]
        for c0 in range(0, n_re, group):
            tiles = list(range(c0, c0 + group))
            ar = [ar_ref[d, :, lane(c)] for c in tiles]
            ai = [ai_ref[d, :, lane(c)] for c in tiles]

            def step(s, x, d=d, tiles=tiles, ar=ar, ai=ai):
                t = s if d == 0 else tb - 1 - s
                rows = rows_of(t)
                out = []
                for c, a_r, a_i, (xr, xi) in zip(tiles, ar, ai, x):
                    nr = a_r * xr - a_i * xi + x_scr[c, rows, :]
                    ni = a_r * xi + a_i * xr + x_scr[n_re + c, rows, :]
                    xo_scr[c, rows, :] = nr
                    xo_scr[n_re + c, rows, :] = ni
                    out.append((nr, ni))
                return tuple(out)

            init = tuple((carry_scr[d, :, lane(c)], carry_scr[d, :, lane(n_re + c)]) for c in tiles)
            fin = lax.fori_loop(0, tb, step, init, unroll=4)
            for c, (xr, xi) in zip(tiles, fin):
                carry_scr[d, :, lane(c)] = xr
                carry_scr[d, :, lane(n_re + c)] = xi
        ys = []
        for k in range(S5_GB):
            if step_major:
                xs = jnp.concatenate(
                    [jnp.concatenate([xo_scr[c, steps_of(k * bsz + b), :] for b in range(bsz)], axis=0)
                     for c in range(2 * n_re)], axis=-1)
            else:
                xs = jnp.concatenate([xo_scr[c, k * blk:(k + 1) * blk, :] for c in range(2 * n_re)], axis=-1)
            ys.append(_dot(xs.astype(BF16), cm_ref[k]))
        y_ref[...] = jnp.concatenate(ys, axis=-1).reshape(bsz, tb, D_MIX)

    @pl.when(i == pl.num_programs(0) - 1)
    def _():
        xf_ref[...] = carry_scr[...]


def s5_scan(z3, bmat, cmat, a_re, a_im, x0, tb):
    bsz, t_len, _ = z3.shape
    n_t = t_len // tb
    n_rows = S5_GB * bsz
    ucol = (A_PAD + 3 * D_MIX) // D_MIX
    full = lambda shape: pl.BlockSpec(shape, lambda i: (0,) * len(shape))
    y_shape = jax.ShapeDtypeStruct((bsz, t_len, D_MIX), F32)
    return pl.pallas_call(
        functools.partial(_s5_kernel, bsz, tb),
        grid=(n_t,),
        in_specs=[pl.BlockSpec((bsz, tb, D_MIX), lambda i: (0, i, ucol)),
                  pl.BlockSpec((bsz, tb, D_MIX), lambda i: (0, n_t - 1 - i, ucol)),
                  full((2, S5_GB, D_MIX // S5_GB, 2 * S5_GW)), full((S5_GB, 2 * S5_GW, D_MIX // S5_GB)),
                  full((2, n_rows, S5_GW)), full((2, n_rows, S5_GW)), full((2, n_rows, 2 * S5_GW))],
        out_specs=[pl.BlockSpec((bsz, tb, D_MIX), lambda i: (0, i, 0)),
                   pl.BlockSpec((bsz, tb, D_MIX), lambda i: (0, n_t - 1 - i, 0)),
                   full((2, n_rows, 2 * S5_GW))],
        out_shape=[y_shape, y_shape, jax.ShapeDtypeStruct((2, n_rows, 2 * S5_GW), F32)],
        scratch_shapes=[pltpu.VMEM((2 * S5_GW // LANE, n_rows * tb, LANE), F32),
                        pltpu.VMEM((2 * S5_GW // LANE, n_rows * tb, LANE), F32),
                        pltpu.VMEM((2, n_rows, 2 * S5_GW), F32)],
        compiler_params=_cparams(("arbitrary",), VMEM_LIMIT),
        name="s5_scan",
    )(z3, z3, bmat, cmat, a_re, a_im, x0)


def s5_rows(x, bsz):
    return x.reshape(2, bsz, S5_GB, S5_GW).transpose(0, 2, 1, 3).reshape(2, S5_GB * bsz, S5_GW)


def s5_unrows(x, bsz):
    return x.reshape(2, S5_GB, bsz, S5_GW).transpose(0, 2, 1, 3).reshape(2, bsz, S5_N)


def _merge_kernel(gnf_ref, gnb_ref, bonus_ref, g_ref, yb_ref, ycf_ref, ycb_ref, u_ref, zg0_ref, zg1_ref, zg2_ref,
                  x_ref, mod_ref, s5d_ref, wglu_ref, bglu_ref, wb_ref, wout_ref, lng_ref, lnb_ref, x1_ref, h2_ref):
    ya = (gnf_ref[...] + gnb_ref[...] + bonus_ref[...]) * g_ref[...]
    yc = ycf_ref[...] + ycb_ref[...] + s5d_ref[...] * u_ref[...]
    yc = 0.5 * yc * (1.0 + jnp.tanh(math.sqrt(2.0 / math.pi) * (yc + 0.044715 * (yc * yc * yc))))
    yc = yc * _sigmoid(_dot(yc.astype(BF16), wglu_ref[...]) + bglu_ref[...])
    merged = (_dot(ya.astype(BF16), wb_ref[0]) * _sigmoid(zg0_ref[...])
              + _dot(yb_ref[...].astype(BF16), wb_ref[1]) * _sigmoid(zg1_ref[...])
              + _dot(yc.astype(BF16), wb_ref[2]) * _sigmoid(zg2_ref[...]))
    mo = _dot(merged.astype(BF16), wout_ref[...])
    m = mod_ref[0]
    x1 = _layer_norm(DN_ALPHA * x_ref[...] + m[2:3] * mo, lng_ref[...], lnb_ref[...])
    x1_ref[...] = x1
    h2_ref[...] = (x1 * (1.0 + m[4:5]) + m[3:4]).astype(BF16)


def merge_branches(gn, bonus, g, yb, yc, z, x, mods, mod_of_tile, p):
    n = x.shape[0]
    row = lambda w, col=0: pl.BlockSpec((TM, w), lambda i, col=col: (i, col))
    full = lambda shape: pl.BlockSpec(shape, lambda i: (0,) * len(shape))
    gb = (A_PAD + 4 * D_MIX) // D_MODEL
    return pl.pallas_call(
        _merge_kernel,
        grid=(n // TM,),
        in_specs=[row(D_MIX), row(D_MIX), row(D_MIX), row(D_MIX), row(D_MIX), row(D_MIX), row(D_MIX),
                  row(D_MIX, A_PAD // D_MIX + 3),
                  row(D_MODEL, gb), row(D_MODEL, gb + 1), row(D_MODEL, gb + 2),
                  row(D_MODEL),
                  pl.BlockSpec((1, SUBLANE, D_MODEL), lambda i: (mod_of_tile(i, TM), 0, 0)),
                  full((1, D_MIX)), full((D_MIX, D_MIX)), full((1, D_MIX)),
                  full((3, D_MIX, D_MODEL)), full((D_MODEL, D_MODEL)),
                  full((1, D_MODEL)), full((1, D_MODEL))],
        out_specs=[row(D_MODEL), row(D_MODEL)],
        out_shape=[jax.ShapeDtypeStruct((n, D_MODEL), F32), jax.ShapeDtypeStruct((n, D_MODEL), BF16)],
        compiler_params=_cparams(("arbitrary",), VMEM_LIMIT),
        name="merge_branches",
    )(gn[0], gn[1], bonus, g, yb, yc[0], yc[1], z, z, z, z, x, mods,
      p['s5_d'], p['w_glu'], p['b_glu'], p['w_branch'], p['w_out'], p['ln1_g'], p['ln1_b'])


def _first_max(val, idx, big):
    m = jnp.max(jnp.max(val, axis=1, keepdims=True), axis=0, keepdims=True)
    cand = jnp.where(val == m, idx, big)
    first = jnp.min(jnp.min(cand, axis=1, keepdims=True), axis=0, keepdims=True)
    return m, idx == first


def _router_kernel(h_ref, wt_ref, bias_ref, gates_ref):
    per = N_EXPERTS // N_GROUPS
    logits = _dot_nt(wt_ref[...], h_ref[...])
    n = logits.shape[1]
    scores = _sigmoid(logits).reshape(N_GROUPS, per, n)
    sel = scores + bias_ref[...]
    e_idx = (lax.broadcasted_iota(jnp.int32, (N_GROUPS, per, n), 0) * per
             + lax.broadcasted_iota(jnp.int32, (N_GROUPS, per, n), 1))
    in_grp = lax.broadcasted_iota(jnp.int32, (N_GROUPS, per, n), 1)
    m1 = jnp.max(sel, axis=1, keepdims=True)
    first = jnp.min(jnp.where(sel == m1, in_grp, per), axis=1, keepdims=True)
    m2 = jnp.max(jnp.where(in_grp == first, -jnp.inf, sel), axis=1, keepdims=True)
    grp = m1 + m2
    g_idx = lax.broadcasted_iota(jnp.int32, (N_GROUPS, 1, n), 0)
    gmask = jnp.zeros((N_GROUPS, 1, n), F32)
    for _ in range(TOPK_GROUPS):
        _, hit = _first_max(grp, g_idx, N_GROUPS)
        gmask = jnp.where(hit, 1.0, gmask)
        grp = jnp.where(hit, -jnp.inf, grp)
    cur = jnp.where(jnp.broadcast_to(gmask, sel.shape) > 0.0, sel, NEG)
    w = jnp.zeros((N_GROUPS, per, n), F32)
    for _ in range(TOP_K):
        _, hit = _first_max(cur, e_idx, N_EXPERTS)
        w = jnp.where(hit, scores, w)
        cur = jnp.where(hit, -jnp.inf, cur)
    tot = jnp.sum(jnp.sum(w, axis=1, keepdims=True), axis=0, keepdims=True)
    gates = (ROUTED_SCALE * w / tot).reshape(N_EXPERTS, n)
    hi = gates.astype(BF16).astype(F32)
    lo = (gates - hi).astype(BF16).astype(F32)
    gates_ref[...] = jnp.concatenate([hi, lo], axis=0).T.astype(BF16)


def moe_router(h2, router_wt, router_bias):
    n = h2.shape[0]
    return pl.pallas_call(
        _router_kernel,
        grid=(n // TM,),
        in_specs=[pl.BlockSpec((TM, D_MODEL), lambda i: (i, 0)),
                  pl.BlockSpec((N_EXPERTS, D_MODEL), lambda i: (0, 0)),
                  pl.BlockSpec((N_GROUPS, N_EXPERTS // N_GROUPS, 1), lambda i: (0, 0, 0))],
        out_specs=pl.BlockSpec((TM, 2 * N_EXPERTS), lambda i: (i, 0)),
        out_shape=jax.ShapeDtypeStruct((n, 2 * N_EXPERTS), BF16),
        compiler_params=_cparams(("arbitrary",)),
        name="moe_router",
    )(h2, router_wt, router_bias)


def _moe_kernel(ec, h_ref, gates_ref, x1_ref, mod_ref, wg_ref, wu_ref, wd_ref, ex_ref,
                sg_ref, su_ref, sd_ref, lng_ref, lnb_ref, out_ref, acc_scr):
    j = pl.program_id(1)
    h = h_ref[...]

    @pl.when(j == 0)
    def _():
        sh = _dot(h, sg_ref[...])
        sh = sh * _sigmoid(sh) * _dot(h, su_ref[...])
        acc_scr[...] = _dot(sh.astype(BF16), sd_ref[...])

    gexp = _dot(gates_ref[...], ex_ref[0])
    hg = _dot(h, wg_ref[...])
    hu = _dot(h, wu_ref[...])
    act = hg * _sigmoid(hg) * hu * gexp
    acc_scr[...] += _dot(act.astype(BF16), wd_ref[...])

    @pl.when(j == pl.num_programs(1) - 1)
    def _():
        m = mod_ref[0]
        out_ref[...] = _layer_norm(DN_ALPHA * x1_ref[...] + m[5:6] * acc_scr[...], lng_ref[...], lnb_ref[...])


def moe_ffn(h2, gates, x1, mods, mod_of_tile, p, tm=512, ec=8):
    n = h2.shape[0]
    wcols = ec * D_EXPERT
    n_j = N_EXPERTS // ec
    full = lambda shape: pl.BlockSpec(shape, lambda i, j: (0,) * len(shape))
    return pl.pallas_call(
        functools.partial(_moe_kernel, ec),
        grid=(n // tm, n_j),
        in_specs=[pl.BlockSpec((tm, D_MODEL), lambda i, j: (i, 0)),
                  pl.BlockSpec((tm, 2 * N_EXPERTS), lambda i, j: (i, 0)),
                  pl.BlockSpec((tm, D_MODEL), lambda i, j: (i, 0)),
                  pl.BlockSpec((1, SUBLANE, D_MODEL), lambda i, j: (mod_of_tile(i, tm), 0, 0)),
                  pl.BlockSpec((D_MODEL, wcols), lambda i, j: (0, j)),
                  pl.BlockSpec((D_MODEL, wcols), lambda i, j: (0, j)),
                  pl.BlockSpec((wcols, D_MODEL), lambda i, j: (j, 0)),
                  pl.BlockSpec((1, 2 * N_EXPERTS, wcols), lambda i, j: (j, 0, 0)),
                  full((D_MODEL, D_SHARED)), full((D_MODEL, D_SHARED)), full((D_SHARED, D_MODEL)),
                  full((1, D_MODEL)), full((1, D_MODEL))],
        out_specs=pl.BlockSpec((tm, D_MODEL), lambda i, j: (i, 0)),
        out_shape=jax.ShapeDtypeStruct((n, D_MODEL), F32),
        scratch_shapes=[pltpu.VMEM((tm, D_MODEL), F32)],
        compiler_params=_cparams(("arbitrary", "arbitrary"), VMEM_LIMIT),
        name="moe_ffn",
    )(h2, gates, x1, mods, p['wg'], p['wu'], p['wd'], p['expand'],
      p['sh_g'], p['sh_u'], p['sh_d'], p['ln2_g'], p['ln2_b'])


def _block_diag2(m):
    z = jnp.zeros_like(m[0])
    return jnp.concatenate([jnp.concatenate([m[0], z], axis=1), jnp.concatenate([z, m[1]], axis=1)], axis=0)


def _layer_params(P, l):
    g = lambda name: P[name][l]
    w_in = g('w_in')
    pad = jnp.zeros((D_MODEL, A_PAD - A_COLS), F32)
    p = {}
    p['w_in'] = jnp.concatenate([w_in[:, :A_COLS], pad, w_in[:, A_COLS:]], axis=1).astype(BF16)
    p['mu'] = jnp.concatenate([g('rwkv_mu'), jnp.zeros((A_PAD - A_COLS,), F32)])[None, :]
    p['w2bd'] = _block_diag2(g('rwkv_w2')).astype(BF16)
    p['a2bd'] = _block_diag2(g('rwkv_a2')).astype(BF16)
    p['g2'] = g('rwkv_g2').astype(BF16)
    p['w0'] = g('rwkv_w0').reshape(1, 2 * D_MIX)
    p['a0'] = g('rwkv_a0').reshape(1, 2 * D_MIX)
    p['kkp'] = g('rwkv_kk')[None, :]
    p['ka'] = g('rwkv_ka')[None, :]
    p['rk'] = g('rwkv_rk')[None, :]
    hid = np.arange(D_MIX) // HD
    p['seg_ones'] = jnp.asarray((hid[:, None] == hid[None, :]).astype(np.float32), dtype=BF16)
    p['gng'] = g('rwkv_gn_g').reshape(HEADS, 1, HD)
    p['gnb'] = g('rwkv_gn_b').reshape(HEADS, 1, HD)
    a = lax.complex(g('s5_a_re'), g('s5_a_im'))
    dt = jnp.exp(g('s5_log_dt'))[..., None]
    a_bar = jnp.exp(dt * a)
    b_bar = ((a_bar - 1.0) / a)[..., None] * lax.complex(g('s5_b_re'), g('s5_b_im'))
    eye_g = jnp.eye(C_GROUPS // S5_GB, dtype=F32)
    gpb = C_GROUPS // S5_GB
    bd = lambda m: jnp.einsum('dkgph,gj->dkghjp', m.reshape(2, S5_GB, gpb, C_STATE, C_GROUP),
                              eye_g).reshape(2, S5_GB, D_MIX // S5_GB, S5_GW)
    p['s5_bmat'] = jnp.concatenate([bd(b_bar.real), bd(b_bar.imag)], axis=-1).astype(BF16)
    cd = lambda m: jnp.einsum('kghp,gj->kjpgh', m.reshape(S5_GB, gpb, C_GROUP, C_STATE),
                              eye_g).reshape(S5_GB, S5_GW, D_MIX // S5_GB)
    p['s5_cmat'] = jnp.concatenate([cd(g('s5_c_re')), -cd(g('s5_c_im'))], axis=1).astype(BF16)
    p['s5_ar'] = a_bar.real.reshape(2, 1, S5_N)
    p['s5_ai'] = a_bar.imag.reshape(2, 1, S5_N)
    p['s5_d'] = g('s5_d')[None, :]
    p['w_glu'] = g('s5_w_glu').astype(BF16)
    p['b_glu'] = g('s5_b_glu')[None, :]
    p['w_branch'] = g('w_branch').astype(BF16)
    p['w_out'] = g('w_out').astype(BF16)
    p['ln1_g'] = g('ln1_g')[None, :]
    p['ln1_b'] = g('ln1_b')[None, :]
    p['router_wt'] = g('router_w').T.astype(BF16)
    p['router_bias'] = g('router_bias').reshape(N_GROUPS, N_EXPERTS // N_GROUPS, 1)
    ecols = N_EXPERTS * D_EXPERT
    p['wg'] = g('exp_w_gate').transpose(1, 0, 2).reshape(D_MODEL, ecols).astype(BF16)
    p['wu'] = g('exp_w_up').transpose(1, 0, 2).reshape(D_MODEL, ecols).astype(BF16)
    p['wd'] = g('exp_w_down').reshape(ecols, D_MODEL).astype(BF16)
    p['sh_g'] = g('sh_w_gate').astype(BF16)
    p['sh_u'] = g('sh_w_up').astype(BF16)
    p['sh_d'] = g('sh_w_down').astype(BF16)
    p['ln2_g'] = g('ln2_g')[None, :]
    p['ln2_b'] = g('ln2_b')[None, :]
    return p


def _expand_table(ec):
    n_j = N_EXPERTS // ec
    t = np.zeros((n_j, 2 * N_EXPERTS, ec * D_EXPERT), np.float32)
    for e in range(N_EXPERTS):
        j, q = divmod(e, ec)
        t[j, e, q * D_EXPERT:(q + 1) * D_EXPERT] = 1.0
        t[j, N_EXPERTS + e, q * D_EXPERT:(q + 1) * D_EXPERT] = 1.0
    return jnp.asarray(t, dtype=BF16)


def kernel(x_prompt, x_sample, c, cache_na_k, cache_na_v, state_rwkv, state_s5_re, state_s5_im, c_ctx, w_ada, b_ada, w_in, rwkv_mu, rwkv_w0, rwkv_w2, rwkv_a0, rwkv_a2, rwkv_g2, rwkv_kk, rwkv_ka, rwkv_rk, rwkv_gn_g, rwkv_gn_b, na_rpb, s5_a_re, s5_a_im, s5_log_dt, s5_b_re, s5_b_im, s5_c_re, s5_c_im, s5_d, s5_w_glu, s5_b_glu, w_branch, w_out, ln1_g, ln1_b, router_w, router_bias, exp_w_gate, exp_w_up, exp_w_down, sh_w_gate, sh_w_up, sh_w_down, ln2_g, ln2_b):
    P = dict(w_in=w_in, rwkv_mu=rwkv_mu, rwkv_w0=rwkv_w0, rwkv_w2=rwkv_w2, rwkv_a0=rwkv_a0, rwkv_a2=rwkv_a2,
             rwkv_g2=rwkv_g2, rwkv_kk=rwkv_kk, rwkv_ka=rwkv_ka, rwkv_rk=rwkv_rk, rwkv_gn_g=rwkv_gn_g,
             rwkv_gn_b=rwkv_gn_b, s5_a_re=s5_a_re, s5_a_im=s5_a_im, s5_log_dt=s5_log_dt, s5_b_re=s5_b_re,
             s5_b_im=s5_b_im, s5_c_re=s5_c_re, s5_c_im=s5_c_im, s5_d=s5_d, s5_w_glu=s5_w_glu,
             s5_b_glu=s5_b_glu, w_branch=w_branch, w_out=w_out, ln1_g=ln1_g, ln1_b=ln1_b, router_w=router_w,
             router_bias=router_bias, exp_w_gate=exp_w_gate, exp_w_up=exp_w_up, exp_w_down=exp_w_down,
             sh_w_gate=sh_w_gate, sh_w_up=sh_w_up, sh_w_down=sh_w_down, ln2_g=ln2_g, ln2_b=ln2_b)
    bc, tc, _ = x_prompt.shape
    bl, tl, _ = x_sample.shape
    depth = w_in.shape[0]

    cond = jnp.concatenate([c_ctx[None, :], c, jnp.zeros((SUBLANE - 1 - bl, D_MODEL), F32)], axis=0)
    ada = ada_modulation(cond, w_ada, b_ada)
    masks = jnp.asarray(_chunk_masks())
    expand = _expand_table(8)

    paths = {
        'ctx': dict(bsz=bc, t=tc, x=x_prompt.reshape(bc * tc, D_MODEL), mod_of_tile=lambda i, tm: 0),
        'lat': dict(bsz=bl, t=tl, x=x_sample.reshape(bl * tl, D_MODEL),
                    mod_of_tile=lambda i, tm: 1 + (i * tm) // tl),
    }
    for q in paths.values():
        tiles = q['t'] // TM
        pos = np.arange(q['bsz'] * tiles) % tiles
        q['seq_tiles'] = jnp.asarray(np.concatenate([np.full_like(pos, tiles), pos]).astype(np.int32))
        q['sched'] = rwkv_schedule([q['t']] * q['bsz'])
        q['s5_tb'] = S5_ROWS // q['bsz'] if S5_GB * q['bsz'] == SUBLANE else SUBLANE
        assert q['s5_tb'] % SUBLANE == 0 and q['t'] % q['s5_tb'] == 0

    new_k, new_v, new_rwkv, new_s5 = [], [], [], []
    for l in range(depth):
        p = _layer_params(P, l)
        p['expand'] = expand
        mods = ada[l, :1 + bl].reshape(1 + bl, 6, D_MODEL)
        mods = jnp.concatenate([mods, jnp.zeros((1 + bl, SUBLANE - 6, D_MODEL), F32)], axis=1)
        bias_tab = na_bias_table(na_rpb[l])
        for name, q in paths.items():
            bsz, t_len, x, mod_of_tile = q['bsz'], q['t'], q['x'], q['mod_of_tile']
            z = in_projection(x, mods, p['w_in'], mod_of_tile)

            r, v, kk, ld, kd, bd, g, bonus = rwkv_prep(z, q['seq_tiles'], p)
            if name == 'ctx':
                s0t = jnp.zeros((bsz, 2, HEADS, HD, HD), F32)
            else:
                s0t = jnp.swapaxes(state_rwkv[:, l], -1, -2)
            gn_f, gn_b, s_fin = rwkv_scan((r, v, kk, ld, kd, bd), masks, p['gng'], p['gnb'], s0t, q['sched'])

            if name == 'ctx':
                yb, k_h, v_h = ctx_attention(z, bsz, t_len)
            else:
                yb = na_attention(z, cache_na_k, cache_na_v, l, bias_tab, bsz, t_len)

            if name == 'ctx':
                x0 = jnp.zeros((2, S5_GB * bsz, 2 * S5_GW), F32)
            else:
                x0 = jnp.concatenate(
                    [s5_rows(jnp.swapaxes(s[:, l].reshape(bsz, 2, S5_N), 0, 1), bsz)
                     for s in (state_s5_re, state_s5_im)], axis=-1)
            a_re, a_im = (s5_rows(jnp.broadcast_to(a, (2, bsz, S5_N)), bsz) for a in (p['s5_ar'], p['s5_ai']))
            yc_f, yc_b, x_fin = s5_scan(z.reshape(bsz, t_len, IN_COLS_P), p['s5_bmat'], p['s5_cmat'],
                                        a_re, a_im, x0, q['s5_tb'])
            yc = (yc_f.reshape(bsz * t_len, D_MIX), yc_b.reshape(bsz * t_len, D_MIX))

            x1, h2 = merge_branches((gn_f, gn_b), bonus, g, yb, yc, z, x, mods, mod_of_tile, p)
            gates = moe_router(h2, p['router_wt'], p['router_bias'])
            q['x'] = moe_ffn(h2, gates, x1, mods, mod_of_tile, p)
            if name == 'ctx':
                new_k.append(k_h)
                new_v.append(v_h)
                new_rwkv.append(jnp.swapaxes(s_fin, -1, -2))
                new_s5.append([jnp.swapaxes(s5_unrows(part, bsz), 0, 1)
                               for part in (x_fin[..., :S5_GW], x_fin[..., S5_GW:])])

    s5_re, s5_im = (jnp.stack([layer[part] for layer in new_s5], axis=1).reshape(bc, depth, 2, C_GROUPS, C_STATE)
                    for part in range(2))
    return (paths['ctx']['x'].reshape(bc, tc, D_MODEL), paths['lat']['x'].reshape(bl, tl, D_MODEL),
            jnp.stack(new_k, axis=1), jnp.stack(new_v, axis=1), jnp.stack(new_rwkv, axis=1), s5_re, s5_im)
```

```python
import functools
import math

import numpy as np
import jax
import jax.numpy as jnp
from jax import lax
from jax.experimental import pallas as pl
from jax.experimental.pallas import tpu as pltpu

F32 = jnp.float32
BF16 = jnp.bfloat16

D_MODEL = 1024
DEPTH = 2
GRID_W = 64
D_MIX = 512
HEADS = 8
HD = 64
LORA = 128
WIN_R = 8
WIN_C = 16
C_GROUP = 16
C_GROUPS = D_MIX // C_GROUP
C_STATE = 64
S5_N = C_GROUPS * C_STATE
NA_QROWS = 8
NA_KROWS = NA_QROWS + WIN_R
MOE_SUB = 256
S5_GB = 4
S5_GW = S5_N // S5_GB
S5_ROWS = 512
S5_CARRY_VREGS = 32
N_EXPERTS = 64
TOP_K = 8
N_GROUPS = 8
TOPK_GROUPS = 4
D_EXPERT = 128
D_SHARED = 128
ROUTED_SCALE = 2.5
LN_EPS = 1e-5
GN_EPS = 64e-5
NEG = -1e30
DN_ALPHA = (2 * DEPTH) ** 0.25
A_COLS = 3 * D_MIX + 3 * LORA
A_PAD = 2048
IN_COLS_P = A_PAD + 4 * D_MIX + 3 * D_MODEL

LANE = 128
SUBLANE = 8
TM = 256
CH = HD
RWKV_CHUNKS = 2
VMEM_LIMIT = 56 * 1024 * 1024


def _cparams(sem, vmem=None):
    return pltpu.CompilerParams(dimension_semantics=sem, vmem_limit_bytes=vmem)


def _sigmoid(x):
    return 1.0 / (1.0 + jnp.exp(-x))


def _dot(a, b, precision=None):
    return jnp.dot(a, b, preferred_element_type=F32, precision=precision)


def _dot_nt(a, b, precision=None):
    return lax.dot_general(a, b, (((1,), (1,)), ((), ())), preferred_element_type=F32, precision=precision)


def _split2(x):
    hi = x.astype(BF16)
    return hi, (x - hi.astype(F32)).astype(BF16)


def _split3(x):
    hi = x.astype(BF16)
    r1 = x - hi.astype(F32)
    mid = r1.astype(BF16)
    return hi, mid, (r1 - mid.astype(F32)).astype(BF16)


def _seg_sum(x, ones_bf16):
    return sum(_dot(part, ones_bf16) for part in _split3(x))


def _mm3(a, b, dot=_dot):
    return dot(a[0], b[0]) + (dot(a[1], b[0]) + dot(a[0], b[1]))


def _layer_norm(x, g, b):
    mu = jnp.mean(x, axis=-1, keepdims=True)
    xc = x - mu
    var = jnp.mean(xc * xc, axis=-1, keepdims=True)
    return xc * lax.rsqrt(var + LN_EPS) * g + b


def _ada_kernel(c_ref, w_ref, b_ref, o_ref):
    c = c_ref[...]
    s = c * _sigmoid(c)
    o_ref[0] = _dot(s.astype(BF16), w_ref[0]) + b_ref[0]


def ada_modulation(cond, w_ada, b_ada):
    n_l, d, n6 = w_ada.shape
    tn = 1536
    return pl.pallas_call(
        _ada_kernel,
        grid=(n_l, n6 // tn),
        in_specs=[pl.BlockSpec((SUBLANE, d), lambda l, j: (0, 0)),
                  pl.BlockSpec((1, d, tn), lambda l, j: (l, 0, j)),
                  pl.BlockSpec((1, 1, tn), lambda l, j: (l, 0, j))],
        out_specs=pl.BlockSpec((1, SUBLANE, tn), lambda l, j: (l, 0, j)),
        out_shape=jax.ShapeDtypeStruct((n_l, SUBLANE, n6), F32),
        compiler_params=_cparams(("arbitrary", "arbitrary")),
        name="ada_modulation",
    )(cond, w_ada.astype(BF16), b_ada.reshape(n_l, 1, n6))


def _inproj_kernel(x_ref, mod_ref, w_ref, z_ref, h_scr):
    @pl.when(pl.program_id(1) == 0)
    def _():
        m = mod_ref[0]
        h_scr[...] = (x_ref[...] * (1.0 + m[1:2]) + m[0:1]).astype(BF16)
    z_ref[...] = _dot(h_scr[...], w_ref[...])


def in_projection(x, mods, w_in_p, mod_of_tile, tm=2048, tn=1024):
    n, d = x.shape
    cols = w_in_p.shape[1]
    return pl.pallas_call(
        _inproj_kernel,
        grid=(n // tm, cols // tn),
        in_specs=[pl.BlockSpec((tm, d), lambda i, j: (i, 0)),
                  pl.BlockSpec((1, SUBLANE, d), lambda i, j: (mod_of_tile(i, tm), 0, 0)),
                  pl.BlockSpec((d, tn), lambda i, j: (0, j))],
        out_specs=pl.BlockSpec((tm, tn), lambda i, j: (i, j)),
        out_shape=jax.ShapeDtypeStruct((n, cols), F32),
        scratch_shapes=[pltpu.VMEM((tm, d), BF16)],
        compiler_params=_cparams(("arbitrary", "arbitrary"), VMEM_LIMIT),
        name="in_projection",
    )(x, mods, w_in_p)


def _rwkv_prep_kernel(seq_tiles_ref, z_ref, zp_ref, zn_ref, mu_ref, w2_ref, a2_ref, g2_ref, w0_ref, a0_ref,
                      kkp_ref, ka_ref, rk_ref, e_ref,
                      r_ref, v_ref, kk_ref, ld_ref, kd_ref, bd_ref, g_ref, bonus_ref):
    i = pl.program_id(0)
    tiles = seq_tiles_ref[i]
    pos = seq_tiles_ref[i + pl.num_programs(0)]
    x = z_ref[...]
    tm = x.shape[0]
    rows = lax.broadcasted_iota(jnp.int32, x.shape, 0)
    prev_row = jnp.where(pos == 0, 0.0, zp_ref[SUBLANE - 1:SUBLANE, :])
    next_row = jnp.where(pos == tiles - 1, 0.0, zn_ref[0:1, :])
    xm1 = jnp.where(rows == 0, prev_row, pltpu.roll(x, 1, axis=0))
    xp1 = jnp.where(rows == tm - 1, next_row, pltpu.roll(x, tm - 1, axis=0))
    za = x + mu_ref[...] * (0.5 * (xm1 + xp1) - x)

    r = za[:, 0:D_MIX]
    k = za[:, D_MIX:2 * D_MIX]
    v = za[:, 2 * D_MIX:3 * D_MIX]
    lw = za[:, 3 * D_MIX:3 * D_MIX + LORA]
    la = za[:, 3 * D_MIX + LORA:3 * D_MIX + 2 * LORA]
    lg = za[:, 3 * D_MIX + 2 * LORA:3 * D_MIX + 3 * LORA]

    w_both = w0_ref[...] + _dot(jnp.tanh(lw).astype(BF16), w2_ref[...])
    a_both = _sigmoid(a0_ref[...] + _dot(la.astype(BF16), a2_ref[...]))
    g_ref[...] = _dot(_sigmoid(lg).astype(BF16), g2_ref[...])

    e = e_ref[...]
    kks = k * kkp_ref[...]
    nrm = jnp.sqrt(_seg_sum(kks * kks, e))
    kk = kks / jnp.maximum(nrm, 1e-12)
    bonus = jnp.zeros_like(v)
    r_ref[...] = r
    v_ref[...] = v
    kk_ref[...] = kk
    for d in range(2):
        w = w_both[:, d * D_MIX:(d + 1) * D_MIX]
        a = a_both[:, d * D_MIX:(d + 1) * D_MIX]
        ld = -math.exp(-0.5) * _sigmoid(w)
        kd = k * (1.0 + (a - 1.0) * ka_ref[...])
        bd = kk * a
        bonus = bonus + _seg_sum(r * kd * rk_ref[...], e) * v
        ld_ref[d] = ld
        kd_ref[d] = kd
        bd_ref[d] = bd
    bonus_ref[...] = bonus


def rwkv_prep(z, seq_tiles, p):
    n = z.shape[0]
    nt = n // TM
    halo = TM // SUBLANE
    nb8 = n // SUBLANE
    tok = jax.ShapeDtypeStruct((n, D_MIX), F32)
    tok2 = jax.ShapeDtypeStruct((2, n, D_MIX), F32)
    full = lambda shape: pl.BlockSpec(shape, lambda i, s: (0,) * len(shape))
    tok_spec = pl.BlockSpec((TM, D_MIX), lambda i, s: (i, 0))
    tok2_spec = pl.BlockSpec((2, TM, D_MIX), lambda i, s: (0, i, 0))
    grid_spec = pltpu.PrefetchScalarGridSpec(
        num_scalar_prefetch=1,
        grid=(nt,),
        in_specs=[pl.BlockSpec((TM, A_PAD), lambda i, s: (i, 0)),
                  pl.BlockSpec((SUBLANE, A_PAD), lambda i, s: (jnp.maximum(i * halo - 1, 0), 0)),
                  pl.BlockSpec((SUBLANE, A_PAD), lambda i, s: (jnp.minimum((i + 1) * halo, nb8 - 1), 0)),
                  full((1, A_PAD)), full((LORA, 2 * D_MIX)), full((LORA, 2 * D_MIX)), full((LORA, D_MIX)),
                  full((1, 2 * D_MIX)), full((1, 2 * D_MIX)), full((1, D_MIX)), full((1, D_MIX)),
                  full((1, D_MIX)), full((D_MIX, D_MIX))],
        out_specs=[tok_spec, tok_spec, tok_spec, tok2_spec, tok2_spec, tok2_spec, tok_spec, tok_spec],
    )
    return pl.pallas_call(
        _rwkv_prep_kernel,
        grid_spec=grid_spec,
        out_shape=[tok, tok, tok, tok2, tok2, tok2, tok, tok],
        compiler_params=_cparams(("arbitrary",), VMEM_LIMIT),
        name="rwkv_prep",
    )(seq_tiles, z, z, z, p['mu'], p['w2bd'], p['a2bd'], p['g2'], p['w0'], p['a0'],
      p['kkp'], p['ka'], p['rk'], p['seg_ones'])


def _chunk_masks():
    t = np.arange(CH)
    fwd_incl = (t[:, None] >= t[None, :])
    out = []
    for incl in (fwd_incl, fwd_incl.T):
        strict = incl & (t[:, None] != t[None, :])
        ms = [incl, strict, strict & ((t[:, None] // 8) == (t[None, :] // 8))]
        for m in (8, 16, 32):
            ms.append(strict & ((t[:, None] // (2 * m)) == (t[None, :] // (2 * m)))
                      & ((t[:, None] // m) != (t[None, :] // m)))
        out.append(np.stack(ms))
    masks = np.stack(out).astype(np.float32)
    return np.concatenate([masks, masks], axis=-1)


def _rwkv_pair_kernel(cb, sched_ref, *refs):
    dir_refs = (refs[0:6], refs[6:12])
    m_ref, gng_ref, gnb_ref, s0_ref = refs[12:16]
    y_refs = refs[16:18]
    sfin_ref, s_scr = refs[18:20]
    step_id = pl.program_id(0)

    @pl.when(sched_ref[_SCHED_FIRST, step_id] == 1)
    def _():
        s_scr[...] = s0_ref[0]

    pw = 2 * HD
    ri = lax.broadcasted_iota(jnp.int32, (pw, pw), 0)
    ci = lax.broadcasted_iota(jnp.int32, (pw, pw), 1)
    eye_bd = (ri == ci).astype(F32)
    mask_bd = ((ri // HD) == (ci // HD)).astype(F32)
    eye12 = (lax.broadcasted_iota(jnp.int32, (CH, pw), 0)
             == lax.broadcasted_iota(jnp.int32, (CH, pw), 1) % HD).astype(F32)
    left = lax.broadcasted_iota(jnp.int32, (1, pw), 1) < HD

    bf = lambda x: x.astype(BF16)
    each = lambda f, *cols: [f(*args) for args in zip(*cols)]
    rows = lambda j: slice(j * CH, (j + 1) * CH)
    lanes_of = lambda p: slice(p * pw, (p + 1) * pw)
    stack = lambda *xs: jnp.concatenate(xs, axis=0)
    side = lambda *xs: jnp.concatenate(xs, axis=1)
    top, mid = slice(0, CH), slice(CH, 2 * CH)
    zero = jnp.zeros((), BF16)
    bd = lambda x: stack(jnp.where(left, x, zero), jnp.where(left, zero, x))

    lanes = [(d, p) for d in range(2) for p in range(HEADS // 2)]
    chains = [(d, j, p) for d, p in lanes for j in range(cb)]
    msk = lambda k: [m_ref[d, k] for d, _, _ in chains]
    incl, strict, m8 = msk(0), msk(1), msk(2)
    incl_b = each(lambda m: bf(m[:, :CH]), incl)
    get = lambda k: [dir_refs[d][k][rows(j), lanes_of(p)] if k < 3 else dir_refs[d][k][0, rows(j), lanes_of(p)]
                     for d, j, p in chains]
    R, V, KK, LD, Kd, Bd = (get(k) for k in range(6))
    L = each(lambda m, x: sum(_dot(m, part) for part in _split3(x)), incl_b, LD)
    ltot = each(lambda x: jnp.sum(x, axis=0, keepdims=True), LD)
    e_nl = each(lambda l: jnp.exp(-l), L)
    e_rest = each(lambda l, lt: jnp.exp(lt - l), L, ltot)
    Qb = each(lambda kk, l, ld: bf(kk * jnp.exp(l - ld)), KK, L, LD)
    Rh = each(lambda r, l: r * jnp.exp(l), R, L)
    QRb = each(lambda q, rh: stack(q, bf(rh)), Qb, Rh)
    Btd = each(lambda b, e: bd(bf(b * e)), Bd, e_nl)
    Ktd = each(lambda k, e: bd(bf(k * e)), Kd, e_nl)
    BcTb = each(lambda b, e: bf((b * e).T), Bd, e_rest)
    KcTb = each(lambda k, e: bf((k * e).T), Kd, e_rest)
    Vb = each(bf, V)
    QRB = each(_dot_nt, QRb, Btd)
    QRK = each(_dot_nt, QRb, Ktd)
    Nl = each(lambda m, x: m * x[top], strict, QRB)
    Mrbb = each(lambda m, x: bf(m * x[mid]), incl, QRB)
    Mkb = each(lambda m, x: bf(m * x[top]), strict, QRK)
    Mrkb = each(lambda m, x: bf(m * x[mid]), incl, QRK)
    N8 = each(lambda m, n: m * n, m8, Nl)
    N8b = each(bf, N8)
    N2 = each(lambda a: _dot(a, bd(a)), N8b)
    N2b = each(bf, N2)
    N4 = each(lambda a: _dot(a, bd(a)), N2b)
    W = each(lambda a, b: _dot(bf(eye12 - a), bd(bf(eye12 + b))), N8, N2)
    W = each(lambda w, n4: _dot(bf(w), bd(bf(eye12 + n4))), W, N4)
    for lvl in range(3):
        Wb = each(bf, W)
        T = each(lambda m, n, w: _dot(bf(m * n), bd(w)), msk(3 + lvl), Nl, Wb)
        W = each(lambda w, wb, t: w - _dot(wb, bd(bf(t))), W, Wb, T)
    Wb = each(bf, W)
    Whb = each(lambda w, q: bf(_dot(w, bd(q))), Wb, Qb)
    XV = each(lambda mk, mrk, v: _dot(stack(mk, mrk), bd(v)), Mkb, Mrkb, Vb)
    U0b = each(lambda w, x: bf(-_dot(w, bd(bf(x[top])))), Wb, XV)
    XWU = each(lambda m, wh, u: _dot(m, side(bd(wh), bd(u))), Mrbb, Whb, U0b)
    BWU = each(lambda b, wh, u: _dot(b, side(wh, u)), BcTb, Whb, U0b)
    KV = each(_dot, KcTb, Vb)
    Y0 = each(lambda xwu, xv: xwu[:, pw:] + xv[mid], XWU, XV)
    Hd = each(lambda bwu, kv: mask_bd * (bwu[:, pw:] + kv), BWU, KV)
    RG = each(lambda rh, lt, xwu, bwu: _split2(stack(rh - xwu[:, :pw],
                                                     eye_bd * jnp.exp(lt) - mask_bd * bwu[:, :pw])),
              Rh, ltot, XWU, BWU)

    ST = [s_scr[d, p] for d, p in lanes]
    for step in range(cb):
        idx = [chains.index((d, step if d == 0 else cb - 1 - step, p)) for d, p in lanes]
        XS = [_mm3(RG[i], _split2(st)) for i, st in zip(idx, ST)]
        ST = [x[CH:] + Hd[i] for i, x in zip(idx, XS)]
        for i, x in zip(idx, XS):
            d, j, p = chains[i]
            y = x[top] + Y0[i]
            half_mean = lambda a: jnp.where(left, jnp.sum(jnp.where(left, a, 0.0), axis=-1, keepdims=True),
                                            jnp.sum(jnp.where(left, 0.0, a), axis=-1, keepdims=True)) * (1.0 / HD)
            yc = y - half_mean(y)
            var = half_mean(yc * yc)
            y_refs[d][rows(j), lanes_of(p)] = yc * lax.rsqrt(var + GN_EPS) * gng_ref[p] + gnb_ref[p]
    for (d, p), st in zip(lanes, ST):
        s_scr[d, p] = st

    @pl.when(sched_ref[_SCHED_LAST, step_id] == 1)
    def _():
        sfin_ref[0] = s_scr[...]


_SCHED_FWD, _SCHED_BWD, _SCHED_SEQ, _SCHED_FIRST, _SCHED_LAST = range(5)


def rwkv_schedule(seq_lens):
    blk_rows = RWKV_CHUNKS * CH
    cols, base = [], 0
    for s, t_len in enumerate(seq_lens):
        n_b = t_len // blk_rows
        for i in range(n_b):
            cols.append((base + i, base + n_b - 1 - i, s, int(i == 0), int(i == n_b - 1)))
        base += n_b
    return np.asarray(cols, np.int32).T


def rwkv_scan(prep, masks, gng, gnb, s0t, sched):
    r, v, kk, ld, kd, bd = prep
    n = r.shape[0]
    n_seq = s0t.shape[0]
    blk_rows = RWKV_CHUNKS * CH
    pairs, pw = HEADS // 2, 2 * HD
    in_specs, args = [], []
    for d, row in ((0, _SCHED_FWD), (1, _SCHED_BWD)):
        for a in (r, v, kk):
            in_specs.append(pl.BlockSpec((blk_rows, D_MIX), lambda i, s, row=row: (s[row, i], 0)))
            args.append(a)
        for a in (ld, kd, bd):
            in_specs.append(pl.BlockSpec((1, blk_rows, D_MIX), lambda i, s, row=row, d=d: (d, s[row, i], 0)))
            args.append(a)
    state_spec = pl.BlockSpec((1, 2, pairs, pw, pw), lambda i, s: (s[_SCHED_SEQ, i], 0, 0, 0, 0))
    in_specs += [pl.BlockSpec((2, 6, CH, pw), lambda i, s: (0, 0, 0, 0)),
                 pl.BlockSpec((pairs, 1, pw), lambda i, s: (0, 0, 0)),
                 pl.BlockSpec((pairs, 1, pw), lambda i, s: (0, 0, 0)),
                 state_spec]
    args += [masks, gng, gnb, s0t]
    grid_spec = pltpu.PrefetchScalarGridSpec(
        num_scalar_prefetch=1,
        grid=(sched.shape[1],),
        in_specs=in_specs,
        out_specs=[pl.BlockSpec((blk_rows, D_MIX), lambda i, s: (s[_SCHED_FWD, i], 0)),
                   pl.BlockSpec((blk_rows, D_MIX), lambda i, s: (s[_SCHED_BWD, i], 0)),
                   state_spec],
        scratch_shapes=[pltpu.VMEM((2, pairs, pw, pw), F32)],
    )
    return pl.pallas_call(
        functools.partial(_rwkv_pair_kernel, RWKV_CHUNKS),
        grid_spec=grid_spec,
        out_shape=[jax.ShapeDtypeStruct((n, D_MIX), F32), jax.ShapeDtypeStruct((n, D_MIX), F32),
                   jax.ShapeDtypeStruct((n_seq, 2, pairs, pw, pw), F32)],
        compiler_params=_cparams(("arbitrary",), VMEM_LIMIT),
        name="rwkv_scan",
    )(jnp.asarray(sched), *args)


def rwkv_pack_state(s):
    n = s.shape[0]
    st = jnp.swapaxes(s, -1, -2).reshape(n, 2, HEADS // 2, 2, HD, HD)
    eye2 = jnp.eye(2, dtype=s.dtype)
    return jnp.einsum('ndpakv,ab->ndpakbv', st, eye2).reshape(n, 2, HEADS // 2, 2 * HD, 2 * HD)


def rwkv_unpack_state(sp):
    n = sp.shape[0]
    blocks = sp.reshape(n, 2, HEADS // 2, 2, HD, 2, HD)
    diag = jnp.stack([blocks[:, :, :, a, :, a, :] for a in range(2)], axis=3)
    return jnp.swapaxes(diag.reshape(n, 2, HEADS, HD, HD), -1, -2)


def _ctx_attn_kernel(q_ref, k_ref, v_ref, y_ref, ko_ref, vo_ref):
    scale = HD ** -0.5
    for h in range(HEADS):
        sl = slice(h * HD, (h + 1) * HD)
        q = q_ref[:, sl]
        k = k_ref[:, sl]
        v = v_ref[:, sl]
        ko_ref[0, h] = k
        vo_ref[0, h] = v
        s = _dot_nt(q.astype(BF16), k.astype(BF16)) * scale
        m = jnp.max(s, axis=-1, keepdims=True)
        e = jnp.exp(s - m)
        p = e / jnp.sum(e, axis=-1, keepdims=True)
        y_ref[:, sl] = _dot(p.astype(BF16), v.astype(BF16))


def ctx_attention(z, bsz, t_len):
    qb = A_PAD // D_MIX
    return pl.pallas_call(
        _ctx_attn_kernel,
        grid=(bsz,),
        in_specs=[pl.BlockSpec((t_len, D_MIX), lambda b: (b, qb)),
                  pl.BlockSpec((t_len, D_MIX), lambda b: (b, qb + 1)),
                  pl.BlockSpec((t_len, D_MIX), lambda b: (b, qb + 2))],
        out_specs=[pl.BlockSpec((t_len, D_MIX), lambda b: (b, 0)),
                   pl.BlockSpec((1, HEADS, t_len, HD), lambda b: (b, 0, 0, 0)),
                   pl.BlockSpec((1, HEADS, t_len, HD), lambda b: (b, 0, 0, 0))],
        out_shape=[jax.ShapeDtypeStruct((bsz * t_len, D_MIX), F32),
                   jax.ShapeDtypeStruct((bsz, HEADS, t_len, HD), F32),
                   jax.ShapeDtypeStruct((bsz, HEADS, t_len, HD), F32)],
        compiler_params=_cparams(("arbitrary",)),
        name="ctx_attention",
    )(z, z, z)


def _na_kernel(rows, nblk, *refs):
    q_ref = refs[0]
    k_refs = refs[1:1 + nblk]
    v_refs = refs[1 + nblk:1 + 2 * nblk]
    kc_ref, vc_ref, tab_ref, y_ref = refs[1 + 2 * nblk:]
    r0 = pl.program_id(1) * NA_QROWS
    u0 = jnp.clip(r0 - WIN_R // 2, 0, rows - NA_KROWS)
    scale = HD ** -0.5
    kwin = jnp.concatenate([kr[...] for kr in k_refs], axis=0).astype(BF16)
    vwin = jnp.concatenate([vr[...] for vr in v_refs], axis=0).astype(BF16)
    left = lax.broadcasted_iota(jnp.int32, (1, 2 * GRID_W), 1) < GRID_W
    bias = []
    for i in range(NA_QROWS):
        r = r0 + i
        rs = jnp.clip(r - WIN_R // 2, 0, rows - WIN_R)
        per_pair = []
        for jp in range(NA_KROWS // 2):
            kr = u0 + 2 * jp
            off = [jnp.where(jnp.logical_and(kr + e >= rs, kr + e < rs + WIN_R), 0.0, NEG) for e in range(2)]
            per_pair.append((jnp.clip(kr - r + WIN_R, 0, 2 * WIN_R - 1), jnp.where(left, off[0], off[1])))
        bias.append(per_pair)
    for h in range(HEADS):
        sl = slice(h * HD, (h + 1) * HD)
        q = q_ref[:, sl].astype(BF16)
        s_raw = _dot_nt(q, kwin[:, sl]) * scale
        s_loc = jnp.concatenate([
            jnp.concatenate([s_raw[i * GRID_W:(i + 1) * GRID_W, jp * 2 * GRID_W:(jp + 1) * 2 * GRID_W]
                             + tab_ref[h, bias[i][jp][0]] + bias[i][jp][1]
                             for jp in range(NA_KROWS // 2)], axis=1)
            for i in range(NA_QROWS)], axis=0)
        s_ctx = _dot_nt(q, kc_ref[0, 0, h].astype(BF16)) * scale
        m = jnp.maximum(jnp.max(s_loc, axis=-1, keepdims=True), jnp.max(s_ctx, axis=-1, keepdims=True))
        e_loc = jnp.exp(s_loc - m)
        e_ctx = jnp.exp(s_ctx - m)
        den = jnp.sum(e_loc, axis=-1, keepdims=True) + jnp.sum(e_ctx, axis=-1, keepdims=True)
        p_loc = (e_loc / den).astype(BF16)
        p_ctx = (e_ctx / den).astype(BF16)
        y_ref[:, sl] = _dot(p_loc, vwin[:, sl]) + _dot(p_ctx, vc_ref[0, 0, h].astype(BF16))


def na_bias_table(rpb):
    cq = np.arange(GRID_W)[:, None]
    ck = np.arange(GRID_W)[None, :]
    cs = np.clip(cq - WIN_C // 2, 0, GRID_W - WIN_C)
    col_bias = np.where((ck >= cs) & (ck < cs + WIN_C), 0.0, NEG).astype(np.float32)
    col_idx = np.clip(ck - cq + WIN_C - 1, 0, 2 * WIN_C - 2)
    rpb_col = rpb.astype(F32)[:, :, col_idx] + col_bias
    padded = jnp.pad(rpb_col, ((0, 0), (1, 1), (0, 0), (0, 0)), constant_values=NEG)
    return jnp.concatenate([padded[:, :-1], padded[:, 1:]], axis=-1)


def na_attention(z, k_ctx, v_ctx, layer, bias_tab, bsz, t_len):
    rows = t_len // GRID_W
    assert rows >= NA_KROWS and rows % NA_QROWS == 0, "latent grid too small for the row-group tiling"
    qb = A_PAD // D_MIX
    past = k_ctx.shape[3]
    blk_rows = WIN_R // 2
    nblk = NA_KROWS // blk_rows
    blk_tok = blk_rows * GRID_W
    q_tok = NA_QROWS * GRID_W

    def win_spec(j, col):
        def index(b, g):
            u0 = jnp.clip(g * NA_QROWS - WIN_R // 2, 0, rows - NA_KROWS)
            return (b * (rows // blk_rows) + u0 // blk_rows + j, col)
        return pl.BlockSpec((blk_tok, D_MIX), index)

    in_specs = ([pl.BlockSpec((q_tok, D_MIX), lambda b, g: (b * (rows // NA_QROWS) + g, qb))]
                + [win_spec(j, qb + 1) for j in range(nblk)]
                + [win_spec(j, qb + 2) for j in range(nblk)]
                + [pl.BlockSpec((1, 1, HEADS, past, HD), lambda b, g: (b, layer, 0, 0, 0)),
                   pl.BlockSpec((1, 1, HEADS, past, HD), lambda b, g: (b, layer, 0, 0, 0)),
                   pl.BlockSpec(bias_tab.shape, lambda b, g: (0, 0, 0, 0))])
    return pl.pallas_call(
        functools.partial(_na_kernel, rows, nblk),
        grid=(bsz, rows // NA_QROWS),
        in_specs=in_specs,
        out_specs=pl.BlockSpec((q_tok, D_MIX), lambda b, g: (b * (rows // NA_QROWS) + g, 0)),
        out_shape=jax.ShapeDtypeStruct((bsz * t_len, D_MIX), F32),
        compiler_params=_cparams(("arbitrary", "arbitrary"), VMEM_LIMIT),
        name="na_attention",
    )(*([z] * (1 + 2 * nblk)), k_ctx, v_ctx, bias_tab)


def _s5_kernel(bsz, tb, uf_ref, ub_ref, bm_ref, cm_ref, ar_ref, ai_ref, x0_ref, yf_ref, yb_ref, xf_ref,
               x_scr, xo_scr, carry_scr):
    i = pl.program_id(0)

    @pl.when(i == 0)
    def _():
        carry_scr[...] = x0_ref[...]

    n_re = S5_GW // LANE
    n_rows = S5_GB * bsz
    blk = bsz * tb
    group = max(1, min(n_re, S5_CARRY_VREGS * SUBLANE // (2 * n_rows)))
    ch = D_MIX // S5_GB
    lane = lambda c: slice(c * LANE, (c + 1) * LANE)
    step_major = n_rows == SUBLANE
    if step_major:
        steps_of = lambda row: pl.ds(row, tb, stride=n_rows)
        rows_of = lambda t: pl.ds(pl.multiple_of(t * n_rows, SUBLANE), n_rows)
    else:
        steps_of = lambda row: pl.ds(row * tb, tb)
        rows_of = lambda t: pl.ds(t, n_rows, stride=tb)
    for d, (u_ref, y_ref) in enumerate(((uf_ref, yf_ref), (ub_ref, yb_ref))):
        u = u_ref[...].reshape(blk, D_MIX).astype(BF16)
        for k in range(S5_GB):
            bu = _dot(u[:, k * ch:(k + 1) * ch], bm_ref[d, k])
            for c in range(2 * n_re):
                if step_major:
                    for b in range(bsz):
                        x_scr[c, steps_of(k * bsz + b), :] = bu[b * tb:(b + 1) * tb, lane(c)]
                else:
                    x_scr[c, k * blk:(k + 1) * blk, :] = bu[:, lane(c)]
        for c0 in range(0, n_re, group):
            tiles = list(range(c0, c0 + group))
            ar = [ar_ref[d, :, lane(c)] for c in tiles]
            ai = [ai_ref[d, :, lane(c)] for c in tiles]

            def step(s, x, d=d, tiles=tiles, ar=ar, ai=ai):
                t = s if d == 0 else tb - 1 - s
                rows = rows_of(t)
                out = []
                for c, a_r, a_i, (xr, xi) in zip(tiles, ar, ai, x):
                    nr = a_r * xr - a_i * xi + x_scr[c, rows, :]
                    ni = a_r * xi + a_i * xr + x_scr[n_re + c, rows, :]
                    xo_scr[c, rows, :] = nr
                    xo_scr[n_re + c, rows, :] = ni
                    out.append((nr, ni))
                return tuple(out)

            init = tuple((carry_scr[d, :, lane(c)], carry_scr[d, :, lane(n_re + c)]) for c in tiles)
            fin = lax.fori_loop(0, tb, step, init, unroll=4)
            for c, (xr, xi) in zip(tiles, fin):
                carry_scr[d, :, lane(c)] = xr
                carry_scr[d, :, lane(n_re + c)] = xi
        ys = []
        for k in range(S5_GB):
            if step_major:
                xs = jnp.concatenate(
                    [jnp.concatenate([xo_scr[c, steps_of(k * bsz + b), :] for b in range(bsz)], axis=0)
                     for c in range(2 * n_re)], axis=-1)
            else:
                xs = jnp.concatenate([xo_scr[c, k * blk:(k + 1) * blk, :] for c in range(2 * n_re)], axis=-1)
            ys.append(_dot(xs.astype(BF16), cm_ref[k]))
        y_ref[...] = jnp.concatenate(ys, axis=-1).reshape(bsz, tb, D_MIX)

    @pl.when(i == pl.num_programs(0) - 1)
    def _():
        xf_ref[...] = carry_scr[...]


def s5_scan(z3, bmat, cmat, a_re, a_im, x0, tb):
    bsz, t_len, _ = z3.shape
    n_t = t_len // tb
    n_rows = S5_GB * bsz
    ucol = (A_PAD + 3 * D_MIX) // D_MIX
    full = lambda shape: pl.BlockSpec(shape, lambda i: (0,) * len(shape))
    y_shape = jax.ShapeDtypeStruct((bsz, t_len, D_MIX), F32)
    return pl.pallas_call(
        functools.partial(_s5_kernel, bsz, tb),
        grid=(n_t,),
        in_specs=[pl.BlockSpec((bsz, tb, D_MIX), lambda i: (0, i, ucol)),
                  pl.BlockSpec((bsz, tb, D_MIX), lambda i: (0, n_t - 1 - i, ucol)),
                  full((2, S5_GB, D_MIX // S5_GB, 2 * S5_GW)), full((S5_GB, 2 * S5_GW, D_MIX // S5_GB)),
                  full((2, n_rows, S5_GW)), full((2, n_rows, S5_GW)), full((2, n_rows, 2 * S5_GW))],
        out_specs=[pl.BlockSpec((bsz, tb, D_MIX), lambda i: (0, i, 0)),
                   pl.BlockSpec((bsz, tb, D_MIX), lambda i: (0, n_t - 1 - i, 0)),
                   full((2, n_rows, 2 * S5_GW))],
        out_shape=[y_shape, y_shape, jax.ShapeDtypeStruct((2, n_rows, 2 * S5_GW), F32)],
        scratch_shapes=[pltpu.VMEM((2 * S5_GW // LANE, n_rows * tb, LANE), F32),
                        pltpu.VMEM((2 * S5_GW // LANE, n_rows * tb, LANE), F32),
                        pltpu.VMEM((2, n_rows, 2 * S5_GW), F32)],
        compiler_params=_cparams(("arbitrary",), VMEM_LIMIT),
        name="s5_scan",
    )(z3, z3, bmat, cmat, a_re, a_im, x0)


def s5_rows(x, bsz):
    return x.reshape(2, bsz, S5_GB, S5_GW).transpose(0, 2, 1, 3).reshape(2, S5_GB * bsz, S5_GW)


def s5_unrows(x, bsz):
    return x.reshape(2, S5_GB, bsz, S5_GW).transpose(0, 2, 1, 3).reshape(2, bsz, S5_N)


def _merge_kernel(gnf_ref, gnb_ref, bonus_ref, g_ref, yb_ref, ycf_ref, ycb_ref, u_ref, zg0_ref, zg1_ref, zg2_ref,
                  x_ref, mod_ref, s5d_ref, wglu_ref, bglu_ref, wb_ref, wout_ref, lng_ref, lnb_ref, x1_ref, h2_ref):
    ya = (gnf_ref[...] + gnb_ref[...] + bonus_ref[...]) * g_ref[...]
    yc = ycf_ref[...] + ycb_ref[...] + s5d_ref[...] * u_ref[...]
    yc = 0.5 * yc * (1.0 + jnp.tanh(math.sqrt(2.0 / math.pi) * (yc + 0.044715 * (yc * yc * yc))))
    yc = yc * _sigmoid(_dot(yc.astype(BF16), wglu_ref[...]) + bglu_ref[...])
    merged = (_dot(ya.astype(BF16), wb_ref[0]) * _sigmoid(zg0_ref[...])
              + _dot(yb_ref[...].astype(BF16), wb_ref[1]) * _sigmoid(zg1_ref[...])
              + _dot(yc.astype(BF16), wb_ref[2]) * _sigmoid(zg2_ref[...]))
    mo = _dot(merged.astype(BF16), wout_ref[...])
    m = mod_ref[0]
    x1 = _layer_norm(DN_ALPHA * x_ref[...] + m[2:3] * mo, lng_ref[...], lnb_ref[...])
    x1_ref[...] = x1
    h2_ref[...] = (x1 * (1.0 + m[4:5]) + m[3:4]).astype(BF16)


def merge_branches(gn, bonus, g, yb, yc, z, x, mods, mod_of_tile, p):
    n = x.shape[0]
    row = lambda w, col=0: pl.BlockSpec((TM, w), lambda i, col=col: (i, col))
    full = lambda shape: pl.BlockSpec(shape, lambda i: (0,) * len(shape))
    gb = (A_PAD + 4 * D_MIX) // D_MODEL
    return pl.pallas_call(
        _merge_kernel,
        grid=(n // TM,),
        in_specs=[row(D_MIX), row(D_MIX), row(D_MIX), row(D_MIX), row(D_MIX), row(D_MIX), row(D_MIX),
                  row(D_MIX, A_PAD // D_MIX + 3),
                  row(D_MODEL, gb), row(D_MODEL, gb + 1), row(D_MODEL, gb + 2),
                  row(D_MODEL),
                  pl.BlockSpec((1, SUBLANE, D_MODEL), lambda i: (mod_of_tile(i, TM), 0, 0)),
                  full((1, D_MIX)), full((D_MIX, D_MIX)), full((1, D_MIX)),
                  full((3, D_MIX, D_MODEL)), full((D_MODEL, D_MODEL)),
                  full((1, D_MODEL)), full((1, D_MODEL))],
        out_specs=[row(D_MODEL), row(D_MODEL)],
        out_shape=[jax.ShapeDtypeStruct((n, D_MODEL), F32), jax.ShapeDtypeStruct((n, D_MODEL), BF16)],
        compiler_params=_cparams(("arbitrary",), VMEM_LIMIT),
        name="merge_branches",
    )(gn[0], gn[1], bonus, g, yb, yc[0], yc[1], z, z, z, z, x, mods,
      p['s5_d'], p['w_glu'], p['b_glu'], p['w_branch'], p['w_out'], p['ln1_g'], p['ln1_b'])


def _first_max(val, idx, big):
    m = jnp.max(jnp.max(val, axis=1, keepdims=True), axis=0, keepdims=True)
    cand = jnp.where(val == m, idx, big)
    first = jnp.min(jnp.min(cand, axis=1, keepdims=True), axis=0, keepdims=True)
    return m, idx == first


def _router_kernel(h_ref, wt_ref, bias_ref, gates_ref):
    per = N_EXPERTS // N_GROUPS
    logits = _dot_nt(wt_ref[...], h_ref[...])
    n = logits.shape[1]
    scores = _sigmoid(logits).reshape(N_GROUPS, per, n)
    sel = scores + bias_ref[...]
    e_idx = (lax.broadcasted_iota(jnp.int32, (N_GROUPS, per, n), 0) * per
             + lax.broadcasted_iota(jnp.int32, (N_GROUPS, per, n), 1))
    in_grp = lax.broadcasted_iota(jnp.int32, (N_GROUPS, per, n), 1)
    m1 = jnp.max(sel, axis=1, keepdims=True)
    first = jnp.min(jnp.where(sel == m1, in_grp, per), axis=1, keepdims=True)
    m2 = jnp.max(jnp.where(in_grp == first, -jnp.inf, sel), axis=1, keepdims=True)
    grp = m1 + m2
    g_idx = lax.broadcasted_iota(jnp.int32, (N_GROUPS, 1, n), 0)
    gmask = jnp.zeros((N_GROUPS, 1, n), F32)
    for _ in range(TOPK_GROUPS):
        _, hit = _first_max(grp, g_idx, N_GROUPS)
        gmask = jnp.where(hit, 1.0, gmask)
        grp = jnp.where(hit, -jnp.inf, grp)
    cur = jnp.where(jnp.broadcast_to(gmask, sel.shape) > 0.0, sel, NEG)
    w = jnp.zeros((N_GROUPS, per, n), F32)
    for _ in range(TOP_K):
        _, hit = _first_max(cur, e_idx, N_EXPERTS)
        w = jnp.where(hit, scores, w)
        cur = jnp.where(hit, -jnp.inf, cur)
    tot = jnp.sum(jnp.sum(w, axis=1, keepdims=True), axis=0, keepdims=True)
    gates = (ROUTED_SCALE * w / tot).reshape(N_EXPERTS, n)
    hi = gates.astype(BF16).astype(F32)
    lo = (gates - hi).astype(BF16).astype(F32)
    gates_ref[...] = jnp.concatenate([hi, lo], axis=0).T.astype(BF16)


def moe_router(h2, router_wt, router_bias):
    n = h2.shape[0]
    return pl.pallas_call(
        _router_kernel,
        grid=(n // TM,),
        in_specs=[pl.BlockSpec((TM, D_MODEL), lambda i: (i, 0)),
                  pl.BlockSpec((N_EXPERTS, D_MODEL), lambda i: (0, 0)),
                  pl.BlockSpec((N_GROUPS, N_EXPERTS // N_GROUPS, 1), lambda i: (0, 0, 0))],
        out_specs=pl.BlockSpec((TM, 2 * N_EXPERTS), lambda i: (i, 0)),
        out_shape=jax.ShapeDtypeStruct((n, 2 * N_EXPERTS), BF16),
        compiler_params=_cparams(("arbitrary",)),
        name="moe_router",
    )(h2, router_wt, router_bias)


def _moe_kernel(ec, h_ref, gates_ref, x1_ref, mod_ref, wg_ref, wu_ref, wd_ref, ex_ref,
                sg_ref, su_ref, sd_ref, lng_ref, lnb_ref, out_ref, acc_scr):
    j = pl.program_id(1)
    h = h_ref[...]

    @pl.when(j == 0)
    def _():
        sh = _dot(h, sg_ref[...])
        sh = sh * _sigmoid(sh) * _dot(h, su_ref[...])
        acc_scr[...] = _dot(sh.astype(BF16), sd_ref[...])

    for r0 in range(0, h.shape[0], MOE_SUB):
        rows = slice(r0, r0 + MOE_SUB)
        gexp = _dot(gates_ref[rows, :], ex_ref[0])
        hg = _dot(h[rows], wg_ref[...])
        hu = _dot(h[rows], wu_ref[...])
        act = hg * _sigmoid(hg) * hu * gexp
        acc_scr[rows, :] += _dot(act.astype(BF16), wd_ref[...])

    @pl.when(j == pl.num_programs(1) - 1)
    def _():
        m = mod_ref[0]
        out_ref[...] = _layer_norm(DN_ALPHA * x1_ref[...] + m[5:6] * acc_scr[...], lng_ref[...], lnb_ref[...])


def moe_ffn(h2, gates, x1, mods, mod_of_tile, p, tm=512, ec=8):
    n = h2.shape[0]
    wcols = ec * D_EXPERT
    n_j = N_EXPERTS // ec
    full = lambda shape: pl.BlockSpec(shape, lambda i, j: (0,) * len(shape))
    return pl.pallas_call(
        functools.partial(_moe_kernel, ec),
        grid=(n // tm, n_j),
        in_specs=[pl.BlockSpec((tm, D_MODEL), lambda i, j: (i, 0)),
                  pl.BlockSpec((tm, 2 * N_EXPERTS), lambda i, j: (i, 0)),
                  pl.BlockSpec((tm, D_MODEL), lambda i, j: (i, 0)),
                  pl.BlockSpec((1, SUBLANE, D_MODEL), lambda i, j: (mod_of_tile(i, tm), 0, 0)),
                  pl.BlockSpec((D_MODEL, wcols), lambda i, j: (0, j)),
                  pl.BlockSpec((D_MODEL, wcols), lambda i, j: (0, j)),
                  pl.BlockSpec((wcols, D_MODEL), lambda i, j: (j, 0)),
                  pl.BlockSpec((1, 2 * N_EXPERTS, wcols), lambda i, j: (j, 0, 0)),
                  full((D_MODEL, D_SHARED)), full((D_MODEL, D_SHARED)), full((D_SHARED, D_MODEL)),
                  full((1, D_MODEL)), full((1, D_MODEL))],
        out_specs=pl.BlockSpec((tm, D_MODEL), lambda i, j: (i, 0)),
        out_shape=jax.ShapeDtypeStruct((n, D_MODEL), F32),
        scratch_shapes=[pltpu.VMEM((tm, D_MODEL), F32)],
        compiler_params=_cparams(("arbitrary", "arbitrary"), VMEM_LIMIT),
        name="moe_ffn",
    )(h2, gates, x1, mods, p['wg'], p['wu'], p['wd'], p['expand'],
      p['sh_g'], p['sh_u'], p['sh_d'], p['ln2_g'], p['ln2_b'])


def _block_diag2(m):
    z = jnp.zeros_like(m[0])
    return jnp.concatenate([jnp.concatenate([m[0], z], axis=1), jnp.concatenate([z, m[1]], axis=1)], axis=0)


def _layer_params(P, l):
    g = lambda name: P[name][l]
    w_in = g('w_in')
    pad = jnp.zeros((D_MODEL, A_PAD - A_COLS), F32)
    p = {}
    p['w_in'] = jnp.concatenate([w_in[:, :A_COLS], pad, w_in[:, A_COLS:]], axis=1).astype(BF16)
    p['mu'] = jnp.concatenate([g('rwkv_mu'), jnp.zeros((A_PAD - A_COLS,), F32)])[None, :]
    p['w2bd'] = _block_diag2(g('rwkv_w2')).astype(BF16)
    p['a2bd'] = _block_diag2(g('rwkv_a2')).astype(BF16)
    p['g2'] = g('rwkv_g2').astype(BF16)
    p['w0'] = g('rwkv_w0').reshape(1, 2 * D_MIX)
    p['a0'] = g('rwkv_a0').reshape(1, 2 * D_MIX)
    p['kkp'] = g('rwkv_kk')[None, :]
    p['ka'] = g('rwkv_ka')[None, :]
    p['rk'] = g('rwkv_rk')[None, :]
    hid = np.arange(D_MIX) // HD
    p['seg_ones'] = jnp.asarray((hid[:, None] == hid[None, :]).astype(np.float32), dtype=BF16)
    p['gng'] = g('rwkv_gn_g').reshape(HEADS // 2, 1, 2 * HD)
    p['gnb'] = g('rwkv_gn_b').reshape(HEADS // 2, 1, 2 * HD)
    a = lax.complex(g('s5_a_re'), g('s5_a_im'))
    dt = jnp.exp(g('s5_log_dt'))[..., None]
    a_bar = jnp.exp(dt * a)
    b_bar = ((a_bar - 1.0) / a)[..., None] * lax.complex(g('s5_b_re'), g('s5_b_im'))
    eye_g = jnp.eye(C_GROUPS // S5_GB, dtype=F32)
    gpb = C_GROUPS // S5_GB
    bd = lambda m: jnp.einsum('dkgph,gj->dkghjp', m.reshape(2, S5_GB, gpb, C_STATE, C_GROUP),
                              eye_g).reshape(2, S5_GB, D_MIX // S5_GB, S5_GW)
    p['s5_bmat'] = jnp.concatenate([bd(b_bar.real), bd(b_bar.imag)], axis=-1).astype(BF16)
    cd = lambda m: jnp.einsum('kghp,gj->kjpgh', m.reshape(S5_GB, gpb, C_GROUP, C_STATE),
                              eye_g).reshape(S5_GB, S5_GW, D_MIX // S5_GB)
    p['s5_cmat'] = jnp.concatenate([cd(g('s5_c_re')), -cd(g('s5_c_im'))], axis=1).astype(BF16)
    p['s5_ar'] = a_bar.real.reshape(2, 1, S5_N)
    p['s5_ai'] = a_bar.imag.reshape(2, 1, S5_N)
    p['s5_d'] = g('s5_d')[None, :]
    p['w_glu'] = g('s5_w_glu').astype(BF16)
    p['b_glu'] = g('s5_b_glu')[None, :]
    p['w_branch'] = g('w_branch').astype(BF16)
    p['w_out'] = g('w_out').astype(BF16)
    p['ln1_g'] = g('ln1_g')[None, :]
    p['ln1_b'] = g('ln1_b')[None, :]
    p['router_wt'] = g('router_w').T.astype(BF16)
    p['router_bias'] = g('router_bias').reshape(N_GROUPS, N_EXPERTS // N_GROUPS, 1)
    ecols = N_EXPERTS * D_EXPERT
    p['wg'] = g('exp_w_gate').transpose(1, 0, 2).reshape(D_MODEL, ecols).astype(BF16)
    p['wu'] = g('exp_w_up').transpose(1, 0, 2).reshape(D_MODEL, ecols).astype(BF16)
    p['wd'] = g('exp_w_down').reshape(ecols, D_MODEL).astype(BF16)
    p['sh_g'] = g('sh_w_gate').astype(BF16)
    p['sh_u'] = g('sh_w_up').astype(BF16)
    p['sh_d'] = g('sh_w_down').astype(BF16)
    p['ln2_g'] = g('ln2_g')[None, :]
    p['ln2_b'] = g('ln2_b')[None, :]
    return p


def _expand_table(ec):
    n_j = N_EXPERTS // ec
    t = np.zeros((n_j, 2 * N_EXPERTS, ec * D_EXPERT), np.float32)
    for e in range(N_EXPERTS):
        j, q = divmod(e, ec)
        t[j, e, q * D_EXPERT:(q + 1) * D_EXPERT] = 1.0
        t[j, N_EXPERTS + e, q * D_EXPERT:(q + 1) * D_EXPERT] = 1.0
    return jnp.asarray(t, dtype=BF16)


def kernel(x_prompt, x_sample, c, cache_na_k, cache_na_v, state_rwkv, state_s5_re, state_s5_im, c_ctx, w_ada, b_ada, w_in, rwkv_mu, rwkv_w0, rwkv_w2, rwkv_a0, rwkv_a2, rwkv_g2, rwkv_kk, rwkv_ka, rwkv_rk, rwkv_gn_g, rwkv_gn_b, na_rpb, s5_a_re, s5_a_im, s5_log_dt, s5_b_re, s5_b_im, s5_c_re, s5_c_im, s5_d, s5_w_glu, s5_b_glu, w_branch, w_out, ln1_g, ln1_b, router_w, router_bias, exp_w_gate, exp_w_up, exp_w_down, sh_w_gate, sh_w_up, sh_w_down, ln2_g, ln2_b):
    P = dict(w_in=w_in, rwkv_mu=rwkv_mu, rwkv_w0=rwkv_w0, rwkv_w2=rwkv_w2, rwkv_a0=rwkv_a0, rwkv_a2=rwkv_a2,
             rwkv_g2=rwkv_g2, rwkv_kk=rwkv_kk, rwkv_ka=rwkv_ka, rwkv_rk=rwkv_rk, rwkv_gn_g=rwkv_gn_g,
             rwkv_gn_b=rwkv_gn_b, s5_a_re=s5_a_re, s5_a_im=s5_a_im, s5_log_dt=s5_log_dt, s5_b_re=s5_b_re,
             s5_b_im=s5_b_im, s5_c_re=s5_c_re, s5_c_im=s5_c_im, s5_d=s5_d, s5_w_glu=s5_w_glu,
             s5_b_glu=s5_b_glu, w_branch=w_branch, w_out=w_out, ln1_g=ln1_g, ln1_b=ln1_b, router_w=router_w,
             router_bias=router_bias, exp_w_gate=exp_w_gate, exp_w_up=exp_w_up, exp_w_down=exp_w_down,
             sh_w_gate=sh_w_gate, sh_w_up=sh_w_up, sh_w_down=sh_w_down, ln2_g=ln2_g, ln2_b=ln2_b)
    bc, tc, _ = x_prompt.shape
    bl, tl, _ = x_sample.shape
    depth = w_in.shape[0]

    cond = jnp.concatenate([c_ctx[None, :], c, jnp.zeros((SUBLANE - 1 - bl, D_MODEL), F32)], axis=0)
    ada = ada_modulation(cond, w_ada, b_ada)
    masks = jnp.asarray(_chunk_masks())
    expand = _expand_table(8)

    paths = {
        'ctx': dict(bsz=bc, t=tc, x=x_prompt.reshape(bc * tc, D_MODEL), mod_of_tile=lambda i, tm: 0),
        'lat': dict(bsz=bl, t=tl, x=x_sample.reshape(bl * tl, D_MODEL),
                    mod_of_tile=lambda i, tm: 1 + (i * tm) // tl),
    }
    for q in paths.values():
        tiles = q['t'] // TM
        pos = np.arange(q['bsz'] * tiles) % tiles
        q['seq_tiles'] = jnp.asarray(np.concatenate([np.full_like(pos, tiles), pos]).astype(np.int32))
        q['sched'] = rwkv_schedule([q['t']] * q['bsz'])
        q['s5_tb'] = S5_ROWS // q['bsz'] if S5_GB * q['bsz'] == SUBLANE else SUBLANE
        assert q['s5_tb'] % SUBLANE == 0 and q['t'] % q['s5_tb'] == 0

    new_k, new_v, new_rwkv, new_s5 = [], [], [], []
    for l in range(depth):
        p = _layer_params(P, l)
        p['expand'] = expand
        mods = ada[l, :1 + bl].reshape(1 + bl, 6, D_MODEL)
        mods = jnp.concatenate([mods, jnp.zeros((1 + bl, SUBLANE - 6, D_MODEL), F32)], axis=1)
        bias_tab = na_bias_table(na_rpb[l])
        for name, q in paths.items():
            bsz, t_len, x, mod_of_tile = q['bsz'], q['t'], q['x'], q['mod_of_tile']
            z = in_projection(x, mods, p['w_in'], mod_of_tile)

            r, v, kk, ld, kd, bd, g, bonus = rwkv_prep(z, q['seq_tiles'], p)
            if name == 'ctx':
                s0t = jnp.zeros((bsz, 2, HEADS // 2, 2 * HD, 2 * HD), F32)
            else:
                s0t = rwkv_pack_state(state_rwkv[:, l])
            gn_f, gn_b, s_fin = rwkv_scan((r, v, kk, ld, kd, bd), masks, p['gng'], p['gnb'], s0t, q['sched'])

            if name == 'ctx':
                yb, k_h, v_h = ctx_attention(z, bsz, t_len)
            else:
                yb = na_attention(z, cache_na_k, cache_na_v, l, bias_tab, bsz, t_len)

            if name == 'ctx':
                x0 = jnp.zeros((2, S5_GB * bsz, 2 * S5_GW), F32)
            else:
                x0 = jnp.concatenate(
                    [s5_rows(jnp.swapaxes(s[:, l].reshape(bsz, 2, S5_N), 0, 1), bsz)
                     for s in (state_s5_re, state_s5_im)], axis=-1)
            a_re, a_im = (s5_rows(jnp.broadcast_to(a, (2, bsz, S5_N)), bsz) for a in (p['s5_ar'], p['s5_ai']))
            yc_f, yc_b, x_fin = s5_scan(z.reshape(bsz, t_len, IN_COLS_P), p['s5_bmat'], p['s5_cmat'],
                                        a_re, a_im, x0, q['s5_tb'])
            yc = (yc_f.reshape(bsz * t_len, D_MIX), yc_b.reshape(bsz * t_len, D_MIX))

            x1, h2 = merge_branches((gn_f, gn_b), bonus, g, yb, yc, z, x, mods, mod_of_tile, p)
            gates = moe_router(h2, p['router_wt'], p['router_bias'])
            q['x'] = moe_ffn(h2, gates, x1, mods, mod_of_tile, p)
            if name == 'ctx':
                new_k.append(k_h)
                new_v.append(v_h)
                new_rwkv.append(rwkv_unpack_state(s_fin))
                new_s5.append([jnp.swapaxes(s5_unrows(part, bsz), 0, 1)
                               for part in (x_fin[..., :S5_GW], x_fin[..., S5_GW:])])

    s5_re, s5_im = (jnp.stack([layer[part] for layer in new_s5], axis=1).reshape(bc, depth, 2, C_GROUPS, C_STATE)
                    for part in range(2))
    return (paths['ctx']['x'].reshape(bc, tc, D_MODEL), paths['lat']['x'].reshape(bl, tl, D_MODEL),
            jnp.stack(new_k, axis=1), jnp.stack(new_v, axis=1), jnp.stack(new_rwkv, axis=1), s5_re, s5_im)
```

```python
import functools
import math

import numpy as np
import jax
import jax.numpy as jnp
from jax import lax
from jax.experimental import pallas as pl
from jax.experimental.pallas import tpu as pltpu

F32 = jnp.float32
BF16 = jnp.bfloat16

D_MODEL = 1024
DEPTH = 2
GRID_W = 64
D_MIX = 512
HEADS = 8
HD = 64
LORA = 128
WIN_R = 8
WIN_C = 16
C_GROUP = 16
C_GROUPS = D_MIX // C_GROUP
C_STATE = 64
S5_N = C_GROUPS * C_STATE
NA_QROWS = 8
NA_KROWS = NA_QROWS + WIN_R
S5_GB = 4
S5_GW = S5_N // S5_GB
S5_ROWS = 512
S5_CARRY_VREGS = 32
N_EXPERTS = 64
TOP_K = 8
N_GROUPS = 8
TOPK_GROUPS = 4
D_EXPERT = 128
D_SHARED = 128
ROUTED_SCALE = 2.5
LN_EPS = 1e-5
GN_EPS = 64e-5
NEG = -1e30
DN_ALPHA = (2 * DEPTH) ** 0.25
A_COLS = 3 * D_MIX + 3 * LORA
A_PAD = 2048
IN_COLS_P = A_PAD + 4 * D_MIX + 3 * D_MODEL

LANE = 128
SUBLANE = 8
TM = 256
CH = HD
RWKV_CHUNKS = 2
VMEM_LIMIT = 56 * 1024 * 1024


def _cparams(sem, vmem=None):
    return pltpu.CompilerParams(dimension_semantics=sem, vmem_limit_bytes=vmem)


def _sigmoid(x):
    return 1.0 / (1.0 + jnp.exp(-x))


def _dot(a, b, precision=None):
    return jnp.dot(a, b, preferred_element_type=F32, precision=precision)


def _dot_nt(a, b, precision=None):
    return lax.dot_general(a, b, (((1,), (1,)), ((), ())), preferred_element_type=F32, precision=precision)


def _split2(x):
    hi = x.astype(BF16)
    return hi, (x - hi.astype(F32)).astype(BF16)


def _split3(x):
    hi = x.astype(BF16)
    r1 = x - hi.astype(F32)
    mid = r1.astype(BF16)
    return hi, mid, (r1 - mid.astype(F32)).astype(BF16)


def _seg_sum(x, ones_bf16):
    return sum(_dot(part, ones_bf16) for part in _split3(x))


def _mm3(a, b, dot=_dot):
    return dot(a[0], b[0]) + (dot(a[1], b[0]) + dot(a[0], b[1]))


def _layer_norm(x, g, b):
    mu = jnp.mean(x, axis=-1, keepdims=True)
    xc = x - mu
    var = jnp.mean(xc * xc, axis=-1, keepdims=True)
    return xc * lax.rsqrt(var + LN_EPS) * g + b


def _ada_kernel(c_ref, w_ref, b_ref, o_ref):
    c = c_ref[...]
    s = c * _sigmoid(c)
    o_ref[0] = _dot(s.astype(BF16), w_ref[0]) + b_ref[0]


def ada_modulation(cond, w_ada, b_ada):
    n_l, d, n6 = w_ada.shape
    tn = 1536
    return pl.pallas_call(
        _ada_kernel,
        grid=(n_l, n6 // tn),
        in_specs=[pl.BlockSpec((SUBLANE, d), lambda l, j: (0, 0)),
                  pl.BlockSpec((1, d, tn), lambda l, j: (l, 0, j)),
                  pl.BlockSpec((1, 1, tn), lambda l, j: (l, 0, j))],
        out_specs=pl.BlockSpec((1, SUBLANE, tn), lambda l, j: (l, 0, j)),
        out_shape=jax.ShapeDtypeStruct((n_l, SUBLANE, n6), F32),
        compiler_params=_cparams(("arbitrary", "arbitrary")),
        name="ada_modulation",
    )(cond, w_ada.astype(BF16), b_ada.reshape(n_l, 1, n6))


def _inproj_kernel(x_ref, mod_ref, w_ref, z_ref, h_scr):
    @pl.when(pl.program_id(1) == 0)
    def _():
        m = mod_ref[0]
        h_scr[...] = (x_ref[...] * (1.0 + m[1:2]) + m[0:1]).astype(BF16)
    z_ref[...] = _dot(h_scr[...], w_ref[...])


def in_projection(x, mods, w_in_p, mod_of_tile, tm=2048, tn=1024):
    n, d = x.shape
    cols = w_in_p.shape[1]
    return pl.pallas_call(
        _inproj_kernel,
        grid=(n // tm, cols // tn),
        in_specs=[pl.BlockSpec((tm, d), lambda i, j: (i, 0)),
                  pl.BlockSpec((1, SUBLANE, d), lambda i, j: (mod_of_tile(i, tm), 0, 0)),
                  pl.BlockSpec((d, tn), lambda i, j: (0, j))],
        out_specs=pl.BlockSpec((tm, tn), lambda i, j: (i, j)),
        out_shape=jax.ShapeDtypeStruct((n, cols), F32),
        scratch_shapes=[pltpu.VMEM((tm, d), BF16)],
        compiler_params=_cparams(("arbitrary", "arbitrary"), VMEM_LIMIT),
        name="in_projection",
    )(x, mods, w_in_p)


def _rwkv_prep_kernel(seq_tiles_ref, z_ref, zp_ref, zn_ref, mu_ref, w2_ref, a2_ref, g2_ref, w0_ref, a0_ref,
                      kkp_ref, ka_ref, rk_ref, e_ref,
                      r_ref, v_ref, kk_ref, ld_ref, kd_ref, bd_ref, g_ref, bonus_ref):
    i = pl.program_id(0)
    tiles = seq_tiles_ref[i]
    pos = seq_tiles_ref[i + pl.num_programs(0)]
    x = z_ref[...]
    tm = x.shape[0]
    rows = lax.broadcasted_iota(jnp.int32, x.shape, 0)
    prev_row = jnp.where(pos == 0, 0.0, zp_ref[SUBLANE - 1:SUBLANE, :])
    next_row = jnp.where(pos == tiles - 1, 0.0, zn_ref[0:1, :])
    xm1 = jnp.where(rows == 0, prev_row, pltpu.roll(x, 1, axis=0))
    xp1 = jnp.where(rows == tm - 1, next_row, pltpu.roll(x, tm - 1, axis=0))
    za = x + mu_ref[...] * (0.5 * (xm1 + xp1) - x)

    r = za[:, 0:D_MIX]
    k = za[:, D_MIX:2 * D_MIX]
    v = za[:, 2 * D_MIX:3 * D_MIX]
    lw = za[:, 3 * D_MIX:3 * D_MIX + LORA]
    la = za[:, 3 * D_MIX + LORA:3 * D_MIX + 2 * LORA]
    lg = za[:, 3 * D_MIX + 2 * LORA:3 * D_MIX + 3 * LORA]

    w_both = w0_ref[...] + _dot(jnp.tanh(lw).astype(BF16), w2_ref[...])
    a_both = _sigmoid(a0_ref[...] + _dot(la.astype(BF16), a2_ref[...]))
    g_ref[...] = _dot(_sigmoid(lg).astype(BF16), g2_ref[...])

    e = e_ref[...]
    kks = k * kkp_ref[...]
    nrm = jnp.sqrt(_seg_sum(kks * kks, e))
    kk = kks / jnp.maximum(nrm, 1e-12)
    bonus = jnp.zeros_like(v)
    r_ref[...] = r
    v_ref[...] = v
    kk_ref[...] = kk
    for d in range(2):
        w = w_both[:, d * D_MIX:(d + 1) * D_MIX]
        a = a_both[:, d * D_MIX:(d + 1) * D_MIX]
        ld = -math.exp(-0.5) * _sigmoid(w)
        kd = k * (1.0 + (a - 1.0) * ka_ref[...])
        bd = kk * a
        bonus = bonus + _seg_sum(r * kd * rk_ref[...], e) * v
        ld_ref[d] = ld
        kd_ref[d] = kd
        bd_ref[d] = bd
    bonus_ref[...] = bonus


def rwkv_prep(z, seq_tiles, p):
    n = z.shape[0]
    nt = n // TM
    halo = TM // SUBLANE
    nb8 = n // SUBLANE
    tok = jax.ShapeDtypeStruct((n, D_MIX), F32)
    tok2 = jax.ShapeDtypeStruct((2, n, D_MIX), F32)
    full = lambda shape: pl.BlockSpec(shape, lambda i, s: (0,) * len(shape))
    tok_spec = pl.BlockSpec((TM, D_MIX), lambda i, s: (i, 0))
    tok2_spec = pl.BlockSpec((2, TM, D_MIX), lambda i, s: (0, i, 0))
    grid_spec = pltpu.PrefetchScalarGridSpec(
        num_scalar_prefetch=1,
        grid=(nt,),
        in_specs=[pl.BlockSpec((TM, A_PAD), lambda i, s: (i, 0)),
                  pl.BlockSpec((SUBLANE, A_PAD), lambda i, s: (jnp.maximum(i * halo - 1, 0), 0)),
                  pl.BlockSpec((SUBLANE, A_PAD), lambda i, s: (jnp.minimum((i + 1) * halo, nb8 - 1), 0)),
                  full((1, A_PAD)), full((LORA, 2 * D_MIX)), full((LORA, 2 * D_MIX)), full((LORA, D_MIX)),
                  full((1, 2 * D_MIX)), full((1, 2 * D_MIX)), full((1, D_MIX)), full((1, D_MIX)),
                  full((1, D_MIX)), full((D_MIX, D_MIX))],
        out_specs=[tok_spec, tok_spec, tok_spec, tok2_spec, tok2_spec, tok2_spec, tok_spec, tok_spec],
    )
    return pl.pallas_call(
        _rwkv_prep_kernel,
        grid_spec=grid_spec,
        out_shape=[tok, tok, tok, tok2, tok2, tok2, tok, tok],
        compiler_params=_cparams(("arbitrary",), VMEM_LIMIT),
        name="rwkv_prep",
    )(seq_tiles, z, z, z, p['mu'], p['w2bd'], p['a2bd'], p['g2'], p['w0'], p['a0'],
      p['kkp'], p['ka'], p['rk'], p['seg_ones'])


def _chunk_masks():
    t = np.arange(CH)
    fwd_incl = (t[:, None] >= t[None, :])
    out = []
    for incl in (fwd_incl, fwd_incl.T):
        strict = incl & (t[:, None] != t[None, :])
        ms = [incl, strict, strict & ((t[:, None] // 8) == (t[None, :] // 8))]
        for m in (8, 16, 32):
            ms.append(strict & ((t[:, None] // (2 * m)) == (t[None, :] // (2 * m)))
                      & ((t[:, None] // m) != (t[None, :] // m)))
        out.append(np.stack(ms))
    masks = np.stack(out).astype(np.float32)
    return np.concatenate([masks, masks], axis=-1)


def _rwkv_pair_kernel(cb, sched_ref, *refs):
    dir_refs = (refs[0:6], refs[6:12])
    m_ref, gng_ref, gnb_ref, s0_ref = refs[12:16]
    y_refs = refs[16:18]
    sfin_ref, s_scr = refs[18:20]
    step_id = pl.program_id(0)

    @pl.when(sched_ref[_SCHED_FIRST, step_id] == 1)
    def _():
        s_scr[...] = s0_ref[0]

    pw = 2 * HD
    ri = lax.broadcasted_iota(jnp.int32, (pw, pw), 0)
    ci = lax.broadcasted_iota(jnp.int32, (pw, pw), 1)
    eye_bd = (ri == ci).astype(F32)
    mask_bd = ((ri // HD) == (ci // HD)).astype(F32)
    eye12 = (lax.broadcasted_iota(jnp.int32, (CH, pw), 0)
             == lax.broadcasted_iota(jnp.int32, (CH, pw), 1) % HD).astype(F32)
    left = lax.broadcasted_iota(jnp.int32, (1, pw), 1) < HD

    bf = lambda x: x.astype(BF16)
    each = lambda f, *cols: [f(*args) for args in zip(*cols)]
    rows = lambda j: slice(j * CH, (j + 1) * CH)
    lanes_of = lambda p: slice(p * pw, (p + 1) * pw)
    stack = lambda *xs: jnp.concatenate(xs, axis=0)
    side = lambda *xs: jnp.concatenate(xs, axis=1)
    top, mid = slice(0, CH), slice(CH, 2 * CH)
    zero = jnp.zeros((), BF16)
    bd = lambda x: stack(jnp.where(left, x, zero), jnp.where(left, zero, x))

    lanes = [(d, p) for d in range(2) for p in range(HEADS // 2)]
    chains = [(d, j, p) for d, p in lanes for j in range(cb)]
    msk = lambda k: [m_ref[d, k] for d, _, _ in chains]
    incl, strict, m8 = msk(0), msk(1), msk(2)
    incl_b = each(lambda m: bf(m[:, :CH]), incl)
    get = lambda k: [dir_refs[d][k][rows(j), lanes_of(p)] if k < 3 else dir_refs[d][k][0, rows(j), lanes_of(p)]
                     for d, j, p in chains]
    R, V, KK, LD, Kd, Bd = (get(k) for k in range(6))
    L = each(lambda m, x: sum(_dot(m, part) for part in _split3(x)), incl_b, LD)
    ltot = each(lambda x: jnp.sum(x, axis=0, keepdims=True), LD)
    e_nl = each(lambda l: jnp.exp(-l), L)
    e_rest = each(lambda l, lt: jnp.exp(lt - l), L, ltot)
    Qb = each(lambda kk, l, ld: bf(kk * jnp.exp(l - ld)), KK, L, LD)
    Rh = each(lambda r, l: r * jnp.exp(l), R, L)
    QRb = each(lambda q, rh: stack(q, bf(rh)), Qb, Rh)
    Btd = each(lambda b, e: bd(bf(b * e)), Bd, e_nl)
    Ktd = each(lambda k, e: bd(bf(k * e)), Kd, e_nl)
    BcTb = each(lambda b, e: bf((b * e).T), Bd, e_rest)
    KcTb = each(lambda k, e: bf((k * e).T), Kd, e_rest)
    Vb = each(bf, V)
    QRB = each(_dot_nt, QRb, Btd)
    QRK = each(_dot_nt, QRb, Ktd)
    Nl = each(lambda m, x: m * x[top], strict, QRB)
    Mrbb = each(lambda m, x: bf(m * x[mid]), incl, QRB)
    Mkb = each(lambda m, x: bf(m * x[top]), strict, QRK)
    Mrkb = each(lambda m, x: bf(m * x[mid]), incl, QRK)
    N8 = each(lambda m, n: m * n, m8, Nl)
    N8b = each(bf, N8)
    N2 = each(lambda a: _dot(a, bd(a)), N8b)
    N2b = each(bf, N2)
    N4 = each(lambda a: _dot(a, bd(a)), N2b)
    W = each(lambda a, b: _dot(bf(eye12 - a), bd(bf(eye12 + b))), N8, N2)
    W = each(lambda w, n4: _dot(bf(w), bd(bf(eye12 + n4))), W, N4)
    for lvl in range(3):
        Wb = each(bf, W)
        T = each(lambda m, n, w: _dot(bf(m * n), bd(w)), msk(3 + lvl), Nl, Wb)
        W = each(lambda w, wb, t: w - _dot(wb, bd(bf(t))), W, Wb, T)
    Wb = each(bf, W)
    Whb = each(lambda w, q: bf(_dot(w, bd(q))), Wb, Qb)
    XV = each(lambda mk, mrk, v: _dot(stack(mk, mrk), bd(v)), Mkb, Mrkb, Vb)
    U0b = each(lambda w, x: bf(-_dot(w, bd(bf(x[top])))), Wb, XV)
    XWU = each(lambda m, wh, u: _dot(m, side(bd(wh), bd(u))), Mrbb, Whb, U0b)
    BWU = each(lambda b, wh, u: _dot(b, side(wh, u)), BcTb, Whb, U0b)
    KV = each(_dot, KcTb, Vb)
    Y0 = each(lambda xwu, xv: xwu[:, pw:] + xv[mid], XWU, XV)
    Hd = each(lambda bwu, kv: mask_bd * (bwu[:, pw:] + kv), BWU, KV)
    RG = each(lambda rh, lt, xwu, bwu: _split2(stack(rh - xwu[:, :pw],
                                                     eye_bd * jnp.exp(lt) - mask_bd * bwu[:, :pw])),
              Rh, ltot, XWU, BWU)

    ST = [s_scr[d, p] for d, p in lanes]
    for step in range(cb):
        idx = [chains.index((d, step if d == 0 else cb - 1 - step, p)) for d, p in lanes]
        XS = [_mm3(RG[i], _split2(st)) for i, st in zip(idx, ST)]
        ST = [x[CH:] + Hd[i] for i, x in zip(idx, XS)]
        for i, x in zip(idx, XS):
            d, j, p = chains[i]
            y = x[top] + Y0[i]
            half_mean = lambda a: jnp.where(left, jnp.sum(jnp.where(left, a, 0.0), axis=-1, keepdims=True),
                                            jnp.sum(jnp.where(left, 0.0, a), axis=-1, keepdims=True)) * (1.0 / HD)
            yc = y - half_mean(y)
            var = half_mean(yc * yc)
            y_refs[d][rows(j), lanes_of(p)] = yc * lax.rsqrt(var + GN_EPS) * gng_ref[p] + gnb_ref[p]
    for (d, p), st in zip(lanes, ST):
        s_scr[d, p] = st

    @pl.when(sched_ref[_SCHED_LAST, step_id] == 1)
    def _():
        for d, p in lanes:
            st = s_scr[d, p]
            sfin_ref[0, d, 2 * p] = st[:HD, :HD]
            sfin_ref[0, d, 2 * p + 1] = st[HD:, HD:]


_SCHED_FWD, _SCHED_BWD, _SCHED_SEQ, _SCHED_FIRST, _SCHED_LAST = range(5)


def rwkv_schedule(seq_lens):
    blk_rows = RWKV_CHUNKS * CH
    cols, base = [], 0
    for s, t_len in enumerate(seq_lens):
        n_b = t_len // blk_rows
        for i in range(n_b):
            cols.append((base + i, base + n_b - 1 - i, s, int(i == 0), int(i == n_b - 1)))
        base += n_b
    return np.asarray(cols, np.int32).T


def rwkv_scan(prep, masks, gng, gnb, s0t, sched):
    r, v, kk, ld, kd, bd = prep
    n = r.shape[0]
    n_seq = s0t.shape[0]
    blk_rows = RWKV_CHUNKS * CH
    pairs, pw = HEADS // 2, 2 * HD
    in_specs, args = [], []
    for d, row in ((0, _SCHED_FWD), (1, _SCHED_BWD)):
        for a in (r, v, kk):
            in_specs.append(pl.BlockSpec((blk_rows, D_MIX), lambda i, s, row=row: (s[row, i], 0)))
            args.append(a)
        for a in (ld, kd, bd):
            in_specs.append(pl.BlockSpec((1, blk_rows, D_MIX), lambda i, s, row=row, d=d: (d, s[row, i], 0)))
            args.append(a)
    state_spec = pl.BlockSpec((1, 2, pairs, pw, pw), lambda i, s: (s[_SCHED_SEQ, i], 0, 0, 0, 0))
    in_specs += [pl.BlockSpec((2, 6, CH, pw), lambda i, s: (0, 0, 0, 0)),
                 pl.BlockSpec((pairs, 1, pw), lambda i, s: (0, 0, 0)),
                 pl.BlockSpec((pairs, 1, pw), lambda i, s: (0, 0, 0)),
                 state_spec]
    args += [masks, gng, gnb, s0t]
    grid_spec = pltpu.PrefetchScalarGridSpec(
        num_scalar_prefetch=1,
        grid=(sched.shape[1],),
        in_specs=in_specs,
        out_specs=[pl.BlockSpec((blk_rows, D_MIX), lambda i, s: (s[_SCHED_FWD, i], 0)),
                   pl.BlockSpec((blk_rows, D_MIX), lambda i, s: (s[_SCHED_BWD, i], 0)),
                   pl.BlockSpec((1, 2, HEADS, HD, HD), lambda i, s: (s[_SCHED_SEQ, i], 0, 0, 0, 0))],
        scratch_shapes=[pltpu.VMEM((2, pairs, pw, pw), F32)],
    )
    return pl.pallas_call(
        functools.partial(_rwkv_pair_kernel, RWKV_CHUNKS),
        grid_spec=grid_spec,
        out_shape=[jax.ShapeDtypeStruct((n, D_MIX), F32), jax.ShapeDtypeStruct((n, D_MIX), F32),
                   jax.ShapeDtypeStruct((n_seq, 2, HEADS, HD, HD), F32)],
        compiler_params=_cparams(("arbitrary",), VMEM_LIMIT),
        name="rwkv_scan",
    )(jnp.asarray(sched), *args)


def rwkv_pack_state(s):
    n = s.shape[0]
    st = jnp.swapaxes(s, -1, -2).reshape(n, 2, HEADS // 2, 2, HD, HD)
    eye2 = jnp.eye(2, dtype=s.dtype)
    return jnp.einsum('ndpakv,ab->ndpakbv', st, eye2).reshape(n, 2, HEADS // 2, 2 * HD, 2 * HD)


def _ctx_attn_kernel(q_ref, k_ref, v_ref, y_ref, ko_ref, vo_ref):
    scale = HD ** -0.5
    for h in range(HEADS):
        sl = slice(h * HD, (h + 1) * HD)
        q = q_ref[:, sl]
        k = k_ref[:, sl]
        v = v_ref[:, sl]
        ko_ref[0, h] = k
        vo_ref[0, h] = v
        s = _dot_nt(q.astype(BF16), k.astype(BF16)) * scale
        m = jnp.max(s, axis=-1, keepdims=True)
        e = jnp.exp(s - m)
        p = e / jnp.sum(e, axis=-1, keepdims=True)
        y_ref[:, sl] = _dot(p.astype(BF16), v.astype(BF16))


def ctx_attention(z, bsz, t_len):
    qb = A_PAD // D_MIX
    return pl.pallas_call(
        _ctx_attn_kernel,
        grid=(bsz,),
        in_specs=[pl.BlockSpec((t_len, D_MIX), lambda b: (b, qb)),
                  pl.BlockSpec((t_len, D_MIX), lambda b: (b, qb + 1)),
                  pl.BlockSpec((t_len, D_MIX), lambda b: (b, qb + 2))],
        out_specs=[pl.BlockSpec((t_len, D_MIX), lambda b: (b, 0)),
                   pl.BlockSpec((1, HEADS, t_len, HD), lambda b: (b, 0, 0, 0)),
                   pl.BlockSpec((1, HEADS, t_len, HD), lambda b: (b, 0, 0, 0))],
        out_shape=[jax.ShapeDtypeStruct((bsz * t_len, D_MIX), F32),
                   jax.ShapeDtypeStruct((bsz, HEADS, t_len, HD), F32),
                   jax.ShapeDtypeStruct((bsz, HEADS, t_len, HD), F32)],
        compiler_params=_cparams(("arbitrary",)),
        name="ctx_attention",
    )(z, z, z)


def _na_kernel(rows, nblk, *refs):
    q_ref = refs[0]
    k_refs = refs[1:1 + nblk]
    v_refs = refs[1 + nblk:1 + 2 * nblk]
    kc_ref, vc_ref, tab_ref, y_ref = refs[1 + 2 * nblk:]
    r0 = pl.program_id(1) * NA_QROWS
    u0 = jnp.clip(r0 - WIN_R // 2, 0, rows - NA_KROWS)
    scale = HD ** -0.5
    kwin = jnp.concatenate([kr[...] for kr in k_refs], axis=0).astype(BF16)
    vwin = jnp.concatenate([vr[...] for vr in v_refs], axis=0).astype(BF16)
    left = lax.broadcasted_iota(jnp.int32, (1, 2 * GRID_W), 1) < GRID_W
    bias = []
    for i in range(NA_QROWS):
        r = r0 + i
        rs = jnp.clip(r - WIN_R // 2, 0, rows - WIN_R)
        per_pair = []
        for jp in range(NA_KROWS // 2):
            kr = u0 + 2 * jp
            off = [jnp.where(jnp.logical_and(kr + e >= rs, kr + e < rs + WIN_R), 0.0, NEG) for e in range(2)]
            per_pair.append((jnp.clip(kr - r + WIN_R, 0, 2 * WIN_R - 1), jnp.where(left, off[0], off[1])))
        bias.append(per_pair)
    for h in range(HEADS):
        sl = slice(h * HD, (h + 1) * HD)
        q = q_ref[:, sl].astype(BF16)
        s_raw = _dot_nt(q, kwin[:, sl]) * scale
        s_loc = jnp.concatenate([
            jnp.concatenate([s_raw[i * GRID_W:(i + 1) * GRID_W, jp * 2 * GRID_W:(jp + 1) * 2 * GRID_W]
                             + tab_ref[h, bias[i][jp][0]] + bias[i][jp][1]
                             for jp in range(NA_KROWS // 2)], axis=1)
            for i in range(NA_QROWS)], axis=0)
        s_ctx = _dot_nt(q, kc_ref[0, 0, h].astype(BF16)) * scale
        m = jnp.maximum(jnp.max(s_loc, axis=-1, keepdims=True), jnp.max(s_ctx, axis=-1, keepdims=True))
        e_loc = jnp.exp(s_loc - m)
        e_ctx = jnp.exp(s_ctx - m)
        den = jnp.sum(e_loc, axis=-1, keepdims=True) + jnp.sum(e_ctx, axis=-1, keepdims=True)
        p_loc = (e_loc / den).astype(BF16)
        p_ctx = (e_ctx / den).astype(BF16)
        y_ref[:, sl] = _dot(p_loc, vwin[:, sl]) + _dot(p_ctx, vc_ref[0, 0, h].astype(BF16))


def na_bias_table(rpb):
    cq = np.arange(GRID_W)[:, None]
    ck = np.arange(GRID_W)[None, :]
    cs = np.clip(cq - WIN_C // 2, 0, GRID_W - WIN_C)
    col_bias = np.where((ck >= cs) & (ck < cs + WIN_C), 0.0, NEG).astype(np.float32)
    col_idx = np.clip(ck - cq + WIN_C - 1, 0, 2 * WIN_C - 2)
    rpb_col = rpb.astype(F32)[:, :, col_idx] + col_bias
    padded = jnp.pad(rpb_col, ((0, 0), (1, 1), (0, 0), (0, 0)), constant_values=NEG)
    return jnp.concatenate([padded[:, :-1], padded[:, 1:]], axis=-1)


def na_attention(z, k_ctx, v_ctx, layer, bias_tab, bsz, t_len):
    rows = t_len // GRID_W
    assert rows >= NA_KROWS and rows % NA_QROWS == 0, "latent grid too small for the row-group tiling"
    qb = A_PAD // D_MIX
    past = k_ctx.shape[3]
    blk_rows = WIN_R // 2
    nblk = NA_KROWS // blk_rows
    blk_tok = blk_rows * GRID_W
    q_tok = NA_QROWS * GRID_W

    def win_spec(j, col):
        def index(b, g):
            u0 = jnp.clip(g * NA_QROWS - WIN_R // 2, 0, rows - NA_KROWS)
            return (b * (rows // blk_rows) + u0 // blk_rows + j, col)
        return pl.BlockSpec((blk_tok, D_MIX), index)

    in_specs = ([pl.BlockSpec((q_tok, D_MIX), lambda b, g: (b * (rows // NA_QROWS) + g, qb))]
                + [win_spec(j, qb + 1) for j in range(nblk)]
                + [win_spec(j, qb + 2) for j in range(nblk)]
                + [pl.BlockSpec((1, 1, HEADS, past, HD), lambda b, g: (b, layer, 0, 0, 0)),
                   pl.BlockSpec((1, 1, HEADS, past, HD), lambda b, g: (b, layer, 0, 0, 0)),
                   pl.BlockSpec(bias_tab.shape, lambda b, g: (0, 0, 0, 0))])
    return pl.pallas_call(
        functools.partial(_na_kernel, rows, nblk),
        grid=(bsz, rows // NA_QROWS),
        in_specs=in_specs,
        out_specs=pl.BlockSpec((q_tok, D_MIX), lambda b, g: (b * (rows // NA_QROWS) + g, 0)),
        out_shape=jax.ShapeDtypeStruct((bsz * t_len, D_MIX), F32),
        compiler_params=_cparams(("arbitrary", "arbitrary"), VMEM_LIMIT),
        name="na_attention",
    )(*([z] * (1 + 2 * nblk)), k_ctx, v_ctx, bias_tab)


def _s5_kernel(bsz, tb, uf_ref, ub_ref, bm_ref, cm_ref, ar_ref, ai_ref, x0_ref, yf_ref, yb_ref, xf_ref,
               x_scr, xo_scr, carry_scr):
    i = pl.program_id(0)

    @pl.when(i == 0)
    def _():
        carry_scr[...] = x0_ref[...]

    n_re = S5_GW // LANE
    n_rows = S5_GB * bsz
    blk = bsz * tb
    group = max(1, min(n_re, S5_CARRY_VREGS * SUBLANE // (2 * n_rows)))
    ch = D_MIX // S5_GB
    lane = lambda c: slice(c * LANE, (c + 1) * LANE)
    step_major = n_rows == SUBLANE
    if step_major:
        steps_of = lambda row: pl.ds(row, tb, stride=n_rows)
        rows_of = lambda t: pl.ds(pl.multiple_of(t * n_rows, SUBLANE), n_rows)
    else:
        steps_of = lambda row: pl.ds(row * tb, tb)
        rows_of = lambda t: pl.ds(t, n_rows, stride=tb)
    for d, (u_ref, y_ref) in enumerate(((uf_ref, yf_ref), (ub_ref, yb_ref))):
        u = u_ref[...].reshape(blk, D_MIX).astype(BF16)
        for k in range(S5_GB):
            bu = _dot(u[:, k * ch:(k + 1) * ch], bm_ref[d, k])
            for c in range(2 * n_re):
                if step_major:
                    for b in range(bsz):
                        x_scr[c, steps_of(k * bsz + b), :] = bu[b * tb:(b + 1) * tb, lane(c)]
                else:
                    x_scr[c, k * blk:(k + 1) * blk, :] = bu[:, lane(c)]
        for c0 in range(0, n_re, group):
            tiles = list(range(c0, c0 + group))
            ar = [ar_ref[d, :, lane(c)] for c in tiles]
            ai = [ai_ref[d, :, lane(c)] for c in tiles]

            def step(s, x, d=d, tiles=tiles, ar=ar, ai=ai):
                t = s if d == 0 else tb - 1 - s
                rows = rows_of(t)
                out = []
                for c, a_r, a_i, (xr, xi) in zip(tiles, ar, ai, x):
                    nr = a_r * xr - a_i * xi + x_scr[c, rows, :]
                    ni = a_r * xi + a_i * xr + x_scr[n_re + c, rows, :]
                    xo_scr[c, rows, :] = nr
                    xo_scr[n_re + c, rows, :] = ni
                    out.append((nr, ni))
                return tuple(out)

            init = tuple((carry_scr[d, :, lane(c)], carry_scr[d, :, lane(n_re + c)]) for c in tiles)
            fin = lax.fori_loop(0, tb, step, init, unroll=4)
            for c, (xr, xi) in zip(tiles, fin):
                carry_scr[d, :, lane(c)] = xr
                carry_scr[d, :, lane(n_re + c)] = xi
        ys = []
        for k in range(S5_GB):
            if step_major:
                xs = jnp.concatenate(
                    [jnp.concatenate([xo_scr[c, steps_of(k * bsz + b), :] for b in range(bsz)], axis=0)
                     for c in range(2 * n_re)], axis=-1)
            else:
                xs = jnp.concatenate([xo_scr[c, k * blk:(k + 1) * blk, :] for c in range(2 * n_re)], axis=-1)
            ys.append(_dot(xs.astype(BF16), cm_ref[k]))
        y_ref[...] = jnp.concatenate(ys, axis=-1).reshape(bsz, tb, D_MIX)

    @pl.when(i == pl.num_programs(0) - 1)
    def _():
        xf_ref[...] = carry_scr[...]


def s5_scan(z3, bmat, cmat, a_re, a_im, x0, tb):
    bsz, t_len, _ = z3.shape
    n_t = t_len // tb
    n_rows = S5_GB * bsz
    ucol = (A_PAD + 3 * D_MIX) // D_MIX
    full = lambda shape: pl.BlockSpec(shape, lambda i: (0,) * len(shape))
    y_shape = jax.ShapeDtypeStruct((bsz, t_len, D_MIX), F32)
    return pl.pallas_call(
        functools.partial(_s5_kernel, bsz, tb),
        grid=(n_t,),
        in_specs=[pl.BlockSpec((bsz, tb, D_MIX), lambda i: (0, i, ucol)),
                  pl.BlockSpec((bsz, tb, D_MIX), lambda i: (0, n_t - 1 - i, ucol)),
                  full((2, S5_GB, D_MIX // S5_GB, 2 * S5_GW)), full((S5_GB, 2 * S5_GW, D_MIX // S5_GB)),
                  full((2, n_rows, S5_GW)), full((2, n_rows, S5_GW)), full((2, n_rows, 2 * S5_GW))],
        out_specs=[pl.BlockSpec((bsz, tb, D_MIX), lambda i: (0, i, 0)),
                   pl.BlockSpec((bsz, tb, D_MIX), lambda i: (0, n_t - 1 - i, 0)),
                   full((2, n_rows, 2 * S5_GW))],
        out_shape=[y_shape, y_shape, jax.ShapeDtypeStruct((2, n_rows, 2 * S5_GW), F32)],
        scratch_shapes=[pltpu.VMEM((2 * S5_GW // LANE, n_rows * tb, LANE), F32),
                        pltpu.VMEM((2 * S5_GW // LANE, n_rows * tb, LANE), F32),
                        pltpu.VMEM((2, n_rows, 2 * S5_GW), F32)],
        compiler_params=_cparams(("arbitrary",), VMEM_LIMIT),
        name="s5_scan",
    )(z3, z3, bmat, cmat, a_re, a_im, x0)


def s5_rows(x, bsz):
    return x.reshape(2, bsz, S5_GB, S5_GW).transpose(0, 2, 1, 3).reshape(2, S5_GB * bsz, S5_GW)


def s5_unrows(x, bsz):
    return x.reshape(2, S5_GB, bsz, S5_GW).transpose(0, 2, 1, 3).reshape(2, bsz, S5_N)


def _merge_kernel(gnf_ref, gnb_ref, bonus_ref, g_ref, yb_ref, ycf_ref, ycb_ref, u_ref, zg0_ref, zg1_ref, zg2_ref,
                  x_ref, mod_ref, s5d_ref, wglu_ref, bglu_ref, wb_ref, wout_ref, lng_ref, lnb_ref, x1_ref, h2_ref):
    ya = (gnf_ref[...] + gnb_ref[...] + bonus_ref[...]) * g_ref[...]
    yc = ycf_ref[...] + ycb_ref[...] + s5d_ref[...] * u_ref[...]
    yc = 0.5 * yc * (1.0 + jnp.tanh(math.sqrt(2.0 / math.pi) * (yc + 0.044715 * (yc * yc * yc))))
    yc = yc * _sigmoid(_dot(yc.astype(BF16), wglu_ref[...]) + bglu_ref[...])
    merged = (_dot(ya.astype(BF16), wb_ref[0]) * _sigmoid(zg0_ref[...])
              + _dot(yb_ref[...].astype(BF16), wb_ref[1]) * _sigmoid(zg1_ref[...])
              + _dot(yc.astype(BF16), wb_ref[2]) * _sigmoid(zg2_ref[...]))
    mo = _dot(merged.astype(BF16), wout_ref[...])
    m = mod_ref[0]
    x1 = _layer_norm(DN_ALPHA * x_ref[...] + m[2:3] * mo, lng_ref[...], lnb_ref[...])
    x1_ref[...] = x1
    h2_ref[...] = (x1 * (1.0 + m[4:5]) + m[3:4]).astype(BF16)


def merge_branches(gn, bonus, g, yb, yc, z, x, mods, mod_of_tile, p):
    n = x.shape[0]
    row = lambda w, col=0: pl.BlockSpec((TM, w), lambda i, col=col: (i, col))
    full = lambda shape: pl.BlockSpec(shape, lambda i: (0,) * len(shape))
    gb = (A_PAD + 4 * D_MIX) // D_MODEL
    return pl.pallas_call(
        _merge_kernel,
        grid=(n // TM,),
        in_specs=[row(D_MIX), row(D_MIX), row(D_MIX), row(D_MIX), row(D_MIX), row(D_MIX), row(D_MIX),
                  row(D_MIX, A_PAD // D_MIX + 3),
                  row(D_MODEL, gb), row(D_MODEL, gb + 1), row(D_MODEL, gb + 2),
                  row(D_MODEL),
                  pl.BlockSpec((1, SUBLANE, D_MODEL), lambda i: (mod_of_tile(i, TM), 0, 0)),
                  full((1, D_MIX)), full((D_MIX, D_MIX)), full((1, D_MIX)),
                  full((3, D_MIX, D_MODEL)), full((D_MODEL, D_MODEL)),
                  full((1, D_MODEL)), full((1, D_MODEL))],
        out_specs=[row(D_MODEL), row(D_MODEL)],
        out_shape=[jax.ShapeDtypeStruct((n, D_MODEL), F32), jax.ShapeDtypeStruct((n, D_MODEL), BF16)],
        compiler_params=_cparams(("arbitrary",), VMEM_LIMIT),
        name="merge_branches",
    )(gn[0], gn[1], bonus, g, yb, yc[0], yc[1], z, z, z, z, x, mods,
      p['s5_d'], p['w_glu'], p['b_glu'], p['w_branch'], p['w_out'], p['ln1_g'], p['ln1_b'])


def _first_max(val, idx, big):
    m = jnp.max(jnp.max(val, axis=1, keepdims=True), axis=0, keepdims=True)
    cand = jnp.where(val == m, idx, big)
    first = jnp.min(jnp.min(cand, axis=1, keepdims=True), axis=0, keepdims=True)
    return m, idx == first


def _router_kernel(h_ref, wt_ref, bias_ref, gates_ref):
    per = N_EXPERTS // N_GROUPS
    logits = _dot_nt(wt_ref[...], h_ref[...])
    n = logits.shape[1]
    scores = _sigmoid(logits).reshape(N_GROUPS, per, n)
    sel = scores + bias_ref[...]
    e_idx = (lax.broadcasted_iota(jnp.int32, (N_GROUPS, per, n), 0) * per
             + lax.broadcasted_iota(jnp.int32, (N_GROUPS, per, n), 1))
    in_grp = lax.broadcasted_iota(jnp.int32, (N_GROUPS, per, n), 1)
    m1 = jnp.max(sel, axis=1, keepdims=True)
    first = jnp.min(jnp.where(sel == m1, in_grp, per), axis=1, keepdims=True)
    m2 = jnp.max(jnp.where(in_grp == first, -jnp.inf, sel), axis=1, keepdims=True)
    grp = m1 + m2
    g_idx = lax.broadcasted_iota(jnp.int32, (N_GROUPS, 1, n), 0)
    gmask = jnp.zeros((N_GROUPS, 1, n), F32)
    for _ in range(TOPK_GROUPS):
        _, hit = _first_max(grp, g_idx, N_GROUPS)
        gmask = jnp.where(hit, 1.0, gmask)
        grp = jnp.where(hit, -jnp.inf, grp)
    cur = jnp.where(jnp.broadcast_to(gmask, sel.shape) > 0.0, sel, NEG)
    w = jnp.zeros((N_GROUPS, per, n), F32)
    for _ in range(TOP_K):
        _, hit = _first_max(cur, e_idx, N_EXPERTS)
        w = jnp.where(hit, scores, w)
        cur = jnp.where(hit, -jnp.inf, cur)
    tot = jnp.sum(jnp.sum(w, axis=1, keepdims=True), axis=0, keepdims=True)
    gates = (ROUTED_SCALE * w / tot).reshape(N_EXPERTS, n)
    hi = gates.astype(BF16).astype(F32)
    lo = (gates - hi).astype(BF16).astype(F32)
    gates_ref[...] = jnp.concatenate([hi, lo], axis=0).T.astype(BF16)


def moe_router(h2, router_wt, router_bias):
    n = h2.shape[0]
    return pl.pallas_call(
        _router_kernel,
        grid=(n // TM,),
        in_specs=[pl.BlockSpec((TM, D_MODEL), lambda i: (i, 0)),
                  pl.BlockSpec((N_EXPERTS, D_MODEL), lambda i: (0, 0)),
                  pl.BlockSpec((N_GROUPS, N_EXPERTS // N_GROUPS, 1), lambda i: (0, 0, 0))],
        out_specs=pl.BlockSpec((TM, 2 * N_EXPERTS), lambda i: (i, 0)),
        out_shape=jax.ShapeDtypeStruct((n, 2 * N_EXPERTS), BF16),
        compiler_params=_cparams(("arbitrary",)),
        name="moe_router",
    )(h2, router_wt, router_bias)


def _moe_kernel(ec, h_ref, gates_ref, x1_ref, mod_ref, wg_ref, wu_ref, wd_ref, ex_ref,
                sg_ref, su_ref, sd_ref, lng_ref, lnb_ref, out_ref, acc_scr):
    j = pl.program_id(1)
    h = h_ref[...]

    @pl.when(j == 0)
    def _():
        sh = _dot(h, sg_ref[...])
        sh = sh * _sigmoid(sh) * _dot(h, su_ref[...])
        acc_scr[...] = _dot(sh.astype(BF16), sd_ref[...])

    gexp = _dot(gates_ref[...], ex_ref[0])
    wg = jnp.concatenate([wg_ref[e] for e in range(ec)], axis=1)
    wu = jnp.concatenate([wu_ref[e] for e in range(ec)], axis=1)
    hg = _dot(h, wg)
    hu = _dot(h, wu)
    act = hg * _sigmoid(hg) * hu * gexp
    acc_scr[...] += _dot(act.astype(BF16), wd_ref[...])

    @pl.when(j == pl.num_programs(1) - 1)
    def _():
        m = mod_ref[0]
        out_ref[...] = _layer_norm(DN_ALPHA * x1_ref[...] + m[5:6] * acc_scr[...], lng_ref[...], lnb_ref[...])


def moe_ffn(h2, gates, x1, mods, mod_of_tile, p, tm=512, ec=8):
    n = h2.shape[0]
    wcols = ec * D_EXPERT
    n_j = N_EXPERTS // ec
    full = lambda shape: pl.BlockSpec(shape, lambda i, j: (0,) * len(shape))
    return pl.pallas_call(
        functools.partial(_moe_kernel, ec),
        grid=(n // tm, n_j),
        in_specs=[pl.BlockSpec((tm, D_MODEL), lambda i, j: (i, 0)),
                  pl.BlockSpec((tm, 2 * N_EXPERTS), lambda i, j: (i, 0)),
                  pl.BlockSpec((tm, D_MODEL), lambda i, j: (i, 0)),
                  pl.BlockSpec((1, SUBLANE, D_MODEL), lambda i, j: (mod_of_tile(i, tm), 0, 0)),
                  pl.BlockSpec((ec, D_MODEL, D_EXPERT), lambda i, j: (j, 0, 0)),
                  pl.BlockSpec((ec, D_MODEL, D_EXPERT), lambda i, j: (j, 0, 0)),
                  pl.BlockSpec((wcols, D_MODEL), lambda i, j: (j, 0)),
                  pl.BlockSpec((1, 2 * N_EXPERTS, wcols), lambda i, j: (j, 0, 0)),
                  full((D_MODEL, D_SHARED)), full((D_MODEL, D_SHARED)), full((D_SHARED, D_MODEL)),
                  full((1, D_MODEL)), full((1, D_MODEL))],
        out_specs=pl.BlockSpec((tm, D_MODEL), lambda i, j: (i, 0)),
        out_shape=jax.ShapeDtypeStruct((n, D_MODEL), F32),
        scratch_shapes=[pltpu.VMEM((tm, D_MODEL), F32)],
        compiler_params=_cparams(("arbitrary", "arbitrary"), VMEM_LIMIT),
        name="moe_ffn",
    )(h2, gates, x1, mods, p['wg'], p['wu'], p['wd'], p['expand'],
      p['sh_g'], p['sh_u'], p['sh_d'], p['ln2_g'], p['ln2_b'])


def _block_diag2(m):
    z = jnp.zeros_like(m[0])
    return jnp.concatenate([jnp.concatenate([m[0], z], axis=1), jnp.concatenate([z, m[1]], axis=1)], axis=0)


def _layer_params(P, l):
    g = lambda name: P[name][l]
    w_in = g('w_in')
    pad = jnp.zeros((D_MODEL, A_PAD - A_COLS), F32)
    p = {}
    p['w_in'] = jnp.concatenate([w_in[:, :A_COLS], pad, w_in[:, A_COLS:]], axis=1).astype(BF16)
    p['mu'] = jnp.concatenate([g('rwkv_mu'), jnp.zeros((A_PAD - A_COLS,), F32)])[None, :]
    p['w2bd'] = _block_diag2(g('rwkv_w2')).astype(BF16)
    p['a2bd'] = _block_diag2(g('rwkv_a2')).astype(BF16)
    p['g2'] = g('rwkv_g2').astype(BF16)
    p['w0'] = g('rwkv_w0').reshape(1, 2 * D_MIX)
    p['a0'] = g('rwkv_a0').reshape(1, 2 * D_MIX)
    p['kkp'] = g('rwkv_kk')[None, :]
    p['ka'] = g('rwkv_ka')[None, :]
    p['rk'] = g('rwkv_rk')[None, :]
    hid = np.arange(D_MIX) // HD
    p['seg_ones'] = jnp.asarray((hid[:, None] == hid[None, :]).astype(np.float32), dtype=BF16)
    p['gng'] = g('rwkv_gn_g').reshape(HEADS // 2, 1, 2 * HD)
    p['gnb'] = g('rwkv_gn_b').reshape(HEADS // 2, 1, 2 * HD)
    a = lax.complex(g('s5_a_re'), g('s5_a_im'))
    dt = jnp.exp(g('s5_log_dt'))[..., None]
    a_bar = jnp.exp(dt * a)
    b_bar = ((a_bar - 1.0) / a)[..., None] * lax.complex(g('s5_b_re'), g('s5_b_im'))
    eye_g = jnp.eye(C_GROUPS // S5_GB, dtype=F32)
    gpb = C_GROUPS // S5_GB
    bd = lambda m: jnp.einsum('dkgph,gj->dkghjp', m.reshape(2, S5_GB, gpb, C_STATE, C_GROUP),
                              eye_g).reshape(2, S5_GB, D_MIX // S5_GB, S5_GW)
    p['s5_bmat'] = jnp.concatenate([bd(b_bar.real), bd(b_bar.imag)], axis=-1).astype(BF16)
    cd = lambda m: jnp.einsum('kghp,gj->kjpgh', m.reshape(S5_GB, gpb, C_GROUP, C_STATE),
                              eye_g).reshape(S5_GB, S5_GW, D_MIX // S5_GB)
    p['s5_cmat'] = jnp.concatenate([cd(g('s5_c_re')), -cd(g('s5_c_im'))], axis=1).astype(BF16)
    p['s5_ar'] = a_bar.real.reshape(2, 1, S5_N)
    p['s5_ai'] = a_bar.imag.reshape(2, 1, S5_N)
    p['s5_d'] = g('s5_d')[None, :]
    p['w_glu'] = g('s5_w_glu').astype(BF16)
    p['b_glu'] = g('s5_b_glu')[None, :]
    p['w_branch'] = g('w_branch').astype(BF16)
    p['w_out'] = g('w_out').astype(BF16)
    p['ln1_g'] = g('ln1_g')[None, :]
    p['ln1_b'] = g('ln1_b')[None, :]
    p['router_wt'] = g('router_w').T.astype(BF16)
    p['router_bias'] = g('router_bias').reshape(N_GROUPS, N_EXPERTS // N_GROUPS, 1)
    ecols = N_EXPERTS * D_EXPERT
    p['wg'] = g('exp_w_gate').astype(BF16)
    p['wu'] = g('exp_w_up').astype(BF16)
    p['wd'] = g('exp_w_down').reshape(ecols, D_MODEL).astype(BF16)
    p['sh_g'] = g('sh_w_gate').astype(BF16)
    p['sh_u'] = g('sh_w_up').astype(BF16)
    p['sh_d'] = g('sh_w_down').astype(BF16)
    p['ln2_g'] = g('ln2_g')[None, :]
    p['ln2_b'] = g('ln2_b')[None, :]
    return p


def _expand_table(ec):
    n_j = N_EXPERTS // ec
    t = np.zeros((n_j, 2 * N_EXPERTS, ec * D_EXPERT), np.float32)
    for e in range(N_EXPERTS):
        j, q = divmod(e, ec)
        t[j, e, q * D_EXPERT:(q + 1) * D_EXPERT] = 1.0
        t[j, N_EXPERTS + e, q * D_EXPERT:(q + 1) * D_EXPERT] = 1.0
    return jnp.asarray(t, dtype=BF16)


def kernel(x_prompt, x_sample, c, cache_na_k, cache_na_v, state_rwkv, state_s5_re, state_s5_im, c_ctx, w_ada, b_ada, w_in, rwkv_mu, rwkv_w0, rwkv_w2, rwkv_a0, rwkv_a2, rwkv_g2, rwkv_kk, rwkv_ka, rwkv_rk, rwkv_gn_g, rwkv_gn_b, na_rpb, s5_a_re, s5_a_im, s5_log_dt, s5_b_re, s5_b_im, s5_c_re, s5_c_im, s5_d, s5_w_glu, s5_b_glu, w_branch, w_out, ln1_g, ln1_b, router_w, router_bias, exp_w_gate, exp_w_up, exp_w_down, sh_w_gate, sh_w_up, sh_w_down, ln2_g, ln2_b):
    P = dict(w_in=w_in, rwkv_mu=rwkv_mu, rwkv_w0=rwkv_w0, rwkv_w2=rwkv_w2, rwkv_a0=rwkv_a0, rwkv_a2=rwkv_a2,
             rwkv_g2=rwkv_g2, rwkv_kk=rwkv_kk, rwkv_ka=rwkv_ka, rwkv_rk=rwkv_rk, rwkv_gn_g=rwkv_gn_g,
             rwkv_gn_b=rwkv_gn_b, s5_a_re=s5_a_re, s5_a_im=s5_a_im, s5_log_dt=s5_log_dt, s5_b_re=s5_b_re,
             s5_b_im=s5_b_im, s5_c_re=s5_c_re, s5_c_im=s5_c_im, s5_d=s5_d, s5_w_glu=s5_w_glu,
             s5_b_glu=s5_b_glu, w_branch=w_branch, w_out=w_out, ln1_g=ln1_g, ln1_b=ln1_b, router_w=router_w,
             router_bias=router_bias, exp_w_gate=exp_w_gate, exp_w_up=exp_w_up, exp_w_down=exp_w_down,
             sh_w_gate=sh_w_gate, sh_w_up=sh_w_up, sh_w_down=sh_w_down, ln2_g=ln2_g, ln2_b=ln2_b)
    bc, tc, _ = x_prompt.shape
    bl, tl, _ = x_sample.shape
    depth = w_in.shape[0]

    cond = jnp.concatenate([c_ctx[None, :], c, jnp.zeros((SUBLANE - 1 - bl, D_MODEL), F32)], axis=0)
    ada = ada_modulation(cond, w_ada, b_ada)
    masks = jnp.asarray(_chunk_masks())
    expand = _expand_table(8)

    paths = {
        'ctx': dict(bsz=bc, t=tc, x=x_prompt.reshape(bc * tc, D_MODEL), mod_of_tile=lambda i, tm: 0),
        'lat': dict(bsz=bl, t=tl, x=x_sample.reshape(bl * tl, D_MODEL),
                    mod_of_tile=lambda i, tm: 1 + (i * tm) // tl),
    }
    for q in paths.values():
        tiles = q['t'] // TM
        pos = np.arange(q['bsz'] * tiles) % tiles
        q['seq_tiles'] = jnp.asarray(np.concatenate([np.full_like(pos, tiles), pos]).astype(np.int32))
        q['sched'] = rwkv_schedule([q['t']] * q['bsz'])
        q['s5_tb'] = S5_ROWS // q['bsz'] if S5_GB * q['bsz'] == SUBLANE else SUBLANE
        assert q['s5_tb'] % SUBLANE == 0 and q['t'] % q['s5_tb'] == 0

    new_k, new_v, new_rwkv, new_s5 = [], [], [], []
    for l in range(depth):
        p = _layer_params(P, l)
        p['expand'] = expand
        mods = ada[l, :1 + bl].reshape(1 + bl, 6, D_MODEL)
        mods = jnp.concatenate([mods, jnp.zeros((1 + bl, SUBLANE - 6, D_MODEL), F32)], axis=1)
        bias_tab = na_bias_table(na_rpb[l])
        for name, q in paths.items():
            bsz, t_len, x, mod_of_tile = q['bsz'], q['t'], q['x'], q['mod_of_tile']
            z = in_projection(x, mods, p['w_in'], mod_of_tile)

            r, v, kk, ld, kd, bd, g, bonus = rwkv_prep(z, q['seq_tiles'], p)
            if name == 'ctx':
                s0t = jnp.zeros((bsz, 2, HEADS // 2, 2 * HD, 2 * HD), F32)
            else:
                s0t = rwkv_pack_state(state_rwkv[:, l])
            gn_f, gn_b, s_fin = rwkv_scan((r, v, kk, ld, kd, bd), masks, p['gng'], p['gnb'], s0t, q['sched'])

            if name == 'ctx':
                yb, k_h, v_h = ctx_attention(z, bsz, t_len)
            else:
                yb = na_attention(z, cache_na_k, cache_na_v, l, bias_tab, bsz, t_len)

            if name == 'ctx':
                x0 = jnp.zeros((2, S5_GB * bsz, 2 * S5_GW), F32)
            else:
                x0 = jnp.concatenate(
                    [s5_rows(jnp.swapaxes(s[:, l].reshape(bsz, 2, S5_N), 0, 1), bsz)
                     for s in (state_s5_re, state_s5_im)], axis=-1)
            a_re, a_im = (s5_rows(jnp.broadcast_to(a, (2, bsz, S5_N)), bsz) for a in (p['s5_ar'], p['s5_ai']))
            yc_f, yc_b, x_fin = s5_scan(z.reshape(bsz, t_len, IN_COLS_P), p['s5_bmat'], p['s5_cmat'],
                                        a_re, a_im, x0, q['s5_tb'])
            yc = (yc_f.reshape(bsz * t_len, D_MIX), yc_b.reshape(bsz * t_len, D_MIX))

            x1, h2 = merge_branches((gn_f, gn_b), bonus, g, yb, yc, z, x, mods, mod_of_tile, p)
            gates = moe_router(h2, p['router_wt'], p['router_bias'])
            q['x'] = moe_ffn(h2, gates, x1, mods, mod_of_tile, p)
            if name == 'ctx':
                new_k.append(k_h)
                new_v.append(v_h)
                new_rwkv.append(jnp.swapaxes(s_fin, -1, -2))
                new_s5.append([jnp.swapaxes(s5_unrows(part, bsz), 0, 1)
                               for part in (x_fin[..., :S5_GW], x_fin[..., S5_GW:])])

    s5_re, s5_im = (jnp.stack([layer[part] for layer in new_s5], axis=1).reshape(bc, depth, 2, C_GROUPS, C_STATE)
                    for part in range(2))
    return (paths['ctx']['x'].reshape(bc, tc, D_MODEL), paths['lat']['x'].reshape(bl, tl, D_MODEL),
            jnp.stack(new_k, axis=1), jnp.stack(new_v, axis=1), jnp.stack(new_rwkv, axis=1), s5_re, s5_im)
```

```python
import functools
import math

import numpy as np
import jax
import jax.numpy as jnp
from jax import lax
from jax.experimental import pallas as pl
from jax.experimental.pallas import tpu as pltpu

F32 = jnp.float32
BF16 = jnp.bfloat16

D_MODEL = 1024
DEPTH = 2
GRID_W = 64
D_MIX = 512
HEADS = 8
HD = 64
LORA = 128
WIN_R = 8
WIN_C = 16
C_GROUP = 16
C_GROUPS = D_MIX // C_GROUP
C_STATE = 64
S5_N = C_GROUPS * C_STATE
N_EXPERTS = 64
TOP_K = 8
N_GROUPS = 8
TOPK_GROUPS = 4
D_EXPERT = 128
D_SHARED = 128
ROUTED_SCALE = 2.5
LN_EPS = 1e-5
GN_EPS = 64e-5
NEG = -1e30
DN_ALPHA = (2 * DEPTH) ** 0.25
A_COLS = 3 * D_MIX + 3 * LORA
A_PAD = 2048
IN_COLS_P = A_PAD + 4 * D_MIX + 3 * D_MODEL

LANE = 128
SUBLANE = 8
TM = 256
CH = HD
RWKV_CHUNKS = 4
NA_QROWS = 8
NA_KROWS = NA_QROWS + WIN_R
S5_GB = 4
S5_GW = S5_N // S5_GB
S5_GRP = SUBLANE // S5_GB
S5_ROWS = 512
S5_CARRY_VREGS = 32
VMEM_LIMIT = 56 * 1024 * 1024


def _cparams(sem, vmem=None):
    return pltpu.CompilerParams(dimension_semantics=sem, vmem_limit_bytes=vmem)


def _sigmoid(x):
    return 1.0 / (1.0 + jnp.exp(-x))


def _dot(a, b, precision=None):
    return jnp.dot(a, b, preferred_element_type=F32, precision=precision)


def _dot_nt(a, b, precision=None):
    return lax.dot_general(a, b, (((1,), (1,)), ((), ())), preferred_element_type=F32, precision=precision)


def _split2(x):
    hi = x.astype(BF16)
    return hi, (x - hi.astype(F32)).astype(BF16)


def _split3(x):
    hi = x.astype(BF16)
    r1 = x - hi.astype(F32)
    mid = r1.astype(BF16)
    return hi, mid, (r1 - mid.astype(F32)).astype(BF16)


def _seg_sum(x, ones_bf16):
    return sum(_dot(part, ones_bf16) for part in _split3(x))


def _mm3(a, b, dot=_dot):
    return dot(a[0], b[0]) + (dot(a[1], b[0]) + dot(a[0], b[1]))


def _layer_norm(x, g, b):
    mu = jnp.mean(x, axis=-1, keepdims=True)
    xc = x - mu
    var = jnp.mean(xc * xc, axis=-1, keepdims=True)
    return xc * lax.rsqrt(var + LN_EPS) * g + b


def _ada_kernel(c_ref, w_ref, b_ref, o_ref):
    c = c_ref[...]
    s = c * _sigmoid(c)
    o_ref[0] = _dot(s.astype(BF16), w_ref[0]) + b_ref[0]


def ada_modulation(cond, w_ada, b_ada):
    n_l, d, n6 = w_ada.shape
    tn = 1536
    return pl.pallas_call(
        _ada_kernel,
        grid=(n_l, n6 // tn),
        in_specs=[pl.BlockSpec((SUBLANE, d), lambda l, j: (0, 0)),
                  pl.BlockSpec((1, d, tn), lambda l, j: (l, 0, j)),
                  pl.BlockSpec((1, 1, tn), lambda l, j: (l, 0, j))],
        out_specs=pl.BlockSpec((1, SUBLANE, tn), lambda l, j: (l, 0, j)),
        out_shape=jax.ShapeDtypeStruct((n_l, SUBLANE, n6), F32),
        compiler_params=_cparams(("arbitrary", "arbitrary")),
        name="ada_modulation",
    )(cond, w_ada.astype(BF16), b_ada.reshape(n_l, 1, n6))


def _inproj_kernel(x_ref, mod_ref, w_ref, z_ref, h_scr):
    @pl.when(pl.program_id(1) == 0)
    def _():
        m = mod_ref[0]
        h_scr[...] = (x_ref[...] * (1.0 + m[1:2]) + m[0:1]).astype(BF16)
    z_ref[...] = _dot(h_scr[...], w_ref[...])


def in_projection(x, mods, w_in_p, mod_of_tile, tm=2048, tn=1024):
    n, d = x.shape
    cols = w_in_p.shape[1]
    return pl.pallas_call(
        _inproj_kernel,
        grid=(n // tm, cols // tn),
        in_specs=[pl.BlockSpec((tm, d), lambda i, j: (i, 0)),
                  pl.BlockSpec((1, SUBLANE, d), lambda i, j: (mod_of_tile(i, tm), 0, 0)),
                  pl.BlockSpec((d, tn), lambda i, j: (0, j))],
        out_specs=pl.BlockSpec((tm, tn), lambda i, j: (i, j)),
        out_shape=jax.ShapeDtypeStruct((n, cols), F32),
        scratch_shapes=[pltpu.VMEM((tm, d), BF16)],
        compiler_params=_cparams(("arbitrary", "arbitrary"), VMEM_LIMIT),
        name="in_projection",
    )(x, mods, w_in_p)


def _rwkv_prep_kernel(seq_tiles_ref, z_ref, zp_ref, zn_ref, mu_ref, w2_ref, a2_ref, g2_ref, w0_ref, a0_ref,
                      kkp_ref, ka_ref, rk_ref, e_ref,
                      r_ref, v_ref, kk_ref, ld_ref, kd_ref, bd_ref, g_ref, bonus_ref):
    i = pl.program_id(0)
    tiles = seq_tiles_ref[i]
    pos = seq_tiles_ref[i + pl.num_programs(0)]
    x = z_ref[...]
    tm = x.shape[0]
    rows = lax.broadcasted_iota(jnp.int32, x.shape, 0)
    prev_row = jnp.where(pos == 0, 0.0, zp_ref[SUBLANE - 1:SUBLANE, :])
    next_row = jnp.where(pos == tiles - 1, 0.0, zn_ref[0:1, :])
    xm1 = jnp.where(rows == 0, prev_row, pltpu.roll(x, 1, axis=0))
    xp1 = jnp.where(rows == tm - 1, next_row, pltpu.roll(x, tm - 1, axis=0))
    za = x + mu_ref[...] * (0.5 * (xm1 + xp1) - x)

    r = za[:, 0:D_MIX]
    k = za[:, D_MIX:2 * D_MIX]
    v = za[:, 2 * D_MIX:3 * D_MIX]
    lw = za[:, 3 * D_MIX:3 * D_MIX + LORA]
    la = za[:, 3 * D_MIX + LORA:3 * D_MIX + 2 * LORA]
    lg = za[:, 3 * D_MIX + 2 * LORA:3 * D_MIX + 3 * LORA]

    w_both = w0_ref[...] + _dot(jnp.tanh(lw).astype(BF16), w2_ref[...])
    a_both = _sigmoid(a0_ref[...] + _dot(la.astype(BF16), a2_ref[...]))
    g_ref[...] = _dot(_sigmoid(lg).astype(BF16), g2_ref[...])

    e = e_ref[...]
    kks = k * kkp_ref[...]
    nrm = jnp.sqrt(_seg_sum(kks * kks, e))
    kk = kks / jnp.maximum(nrm, 1e-12)
    bonus = jnp.zeros_like(v)
    r_ref[...] = r
    v_ref[...] = v
    kk_ref[...] = kk
    for d in range(2):
        w = w_both[:, d * D_MIX:(d + 1) * D_MIX]
        a = a_both[:, d * D_MIX:(d + 1) * D_MIX]
        ld = -math.exp(-0.5) * _sigmoid(w)
        kd = k * (1.0 + (a - 1.0) * ka_ref[...])
        bd = kk * a
        bonus = bonus + _seg_sum(r * kd * rk_ref[...], e) * v
        ld_ref[d] = ld
        kd_ref[d] = kd
        bd_ref[d] = bd
    bonus_ref[...] = bonus


def rwkv_prep(z, seq_tiles, p):
    n = z.shape[0]
    nt = n // TM
    halo = TM // SUBLANE
    nb8 = n // SUBLANE
    tok = jax.ShapeDtypeStruct((n, D_MIX), F32)
    tok2 = jax.ShapeDtypeStruct((2, n, D_MIX), F32)
    full = lambda shape: pl.BlockSpec(shape, lambda i, s: (0,) * len(shape))
    tok_spec = pl.BlockSpec((TM, D_MIX), lambda i, s: (i, 0))
    tok2_spec = pl.BlockSpec((2, TM, D_MIX), lambda i, s: (0, i, 0))
    grid_spec = pltpu.PrefetchScalarGridSpec(
        num_scalar_prefetch=1,
        grid=(nt,),
        in_specs=[pl.BlockSpec((TM, A_PAD), lambda i, s: (i, 0)),
                  pl.BlockSpec((SUBLANE, A_PAD), lambda i, s: (jnp.maximum(i * halo - 1, 0), 0)),
                  pl.BlockSpec((SUBLANE, A_PAD), lambda i, s: (jnp.minimum((i + 1) * halo, nb8 - 1), 0)),
                  full((1, A_PAD)), full((LORA, 2 * D_MIX)), full((LORA, 2 * D_MIX)), full((LORA, D_MIX)),
                  full((1, 2 * D_MIX)), full((1, 2 * D_MIX)), full((1, D_MIX)), full((1, D_MIX)),
                  full((1, D_MIX)), full((D_MIX, D_MIX))],
        out_specs=[tok_spec, tok_spec, tok_spec, tok2_spec, tok2_spec, tok2_spec, tok_spec, tok_spec],
    )
    return pl.pallas_call(
        _rwkv_prep_kernel,
        grid_spec=grid_spec,
        out_shape=[tok, tok, tok, tok2, tok2, tok2, tok, tok],
        compiler_params=_cparams(("arbitrary",), VMEM_LIMIT),
        name="rwkv_prep",
    )(seq_tiles, z, z, z, p['mu'], p['w2bd'], p['a2bd'], p['g2'], p['w0'], p['a0'],
      p['kkp'], p['ka'], p['rk'], p['seg_ones'])


def _chunk_masks():
    t = np.arange(CH)
    fwd_incl = (t[:, None] >= t[None, :])
    out = []
    for incl in (fwd_incl, fwd_incl.T):
        strict = incl & (t[:, None] != t[None, :])
        ms = [incl, strict, strict & ((t[:, None] // 8) == (t[None, :] // 8))]
        for m in (8, 16, 32):
            ms.append(strict & ((t[:, None] // (2 * m)) == (t[None, :] // (2 * m)))
                      & ((t[:, None] // m) != (t[None, :] // m)))
        out.append(np.stack(ms))
    masks = np.stack(out).astype(np.float32)
    return np.concatenate([masks, masks], axis=-1)


def _rwkv_pair_kernel(cb, sched_ref, *refs):
    dir_refs = (refs[0:6], refs[6:12])
    m_ref, gng_ref, gnb_ref, s0_ref = refs[12:16]
    y_refs = refs[16:18]
    sfin_ref, s_scr = refs[18:20]
    step_id = pl.program_id(0)

    @pl.when(sched_ref[_SCHED_FIRST, step_id] == 1)
    def _():
        s_scr[...] = s0_ref[0]

    pw = 2 * HD
    ri = lax.broadcasted_iota(jnp.int32, (pw, pw), 0)
    ci = lax.broadcasted_iota(jnp.int32, (pw, pw), 1)
    eye_bd = (ri == ci).astype(F32)
    mask_bd = ((ri // HD) == (ci // HD)).astype(F32)
    eye12 = (lax.broadcasted_iota(jnp.int32, (CH, pw), 0)
             == lax.broadcasted_iota(jnp.int32, (CH, pw), 1) % HD).astype(F32)
    left = lax.broadcasted_iota(jnp.int32, (1, pw), 1) < HD

    bf = lambda x: x.astype(BF16)
    each = lambda f, *cols: [f(*args) for args in zip(*cols)]
    rows = lambda j: slice(j * CH, (j + 1) * CH)
    lanes_of = lambda p: slice(p * pw, (p + 1) * pw)
    stack = lambda *xs: jnp.concatenate(xs, axis=0)
    side = lambda *xs: jnp.concatenate(xs, axis=1)
    top, mid = slice(0, CH), slice(CH, 2 * CH)
    zero = jnp.zeros((), BF16)
    bd = lambda x: stack(jnp.where(left, x, zero), jnp.where(left, zero, x))

    lanes = [(d, p) for d in range(2) for p in range(HEADS // 2)]
    chains = [(d, j, p) for d, p in lanes for j in range(cb)]
    msk = lambda k: [m_ref[d, k] for d, _, _ in chains]
    incl, strict, m8 = msk(0), msk(1), msk(2)
    incl_b = each(lambda m: bf(m[:, :CH]), incl)
    get = lambda k: [dir_refs[d][k][rows(j), lanes_of(p)] if k < 3 else dir_refs[d][k][0, rows(j), lanes_of(p)]
                     for d, j, p in chains]
    R, V, KK, LD, Kd, Bd = (get(k) for k in range(6))
    L = each(lambda m, x: sum(_dot(m, part) for part in _split3(x)), incl_b, LD)
    ltot = each(lambda x: jnp.sum(x, axis=0, keepdims=True), LD)
    e_nl = each(lambda l: jnp.exp(-l), L)
    e_rest = each(lambda l, lt: jnp.exp(lt - l), L, ltot)
    Qb = each(lambda kk, l, ld: bf(kk * jnp.exp(l - ld)), KK, L, LD)
    Rh = each(lambda r, l: r * jnp.exp(l), R, L)
    QRb = each(lambda q, rh: stack(q, bf(rh)), Qb, Rh)
    Btd = each(lambda b, e: bd(bf(b * e)), Bd, e_nl)
    Ktd = each(lambda k, e: bd(bf(k * e)), Kd, e_nl)
    BcTb = each(lambda b, e: bf((b * e).T), Bd, e_rest)
    KcTb = each(lambda k, e: bf((k * e).T), Kd, e_rest)
    Vb = each(bf, V)
    QRB = each(_dot_nt, QRb, Btd)
    QRK = each(_dot_nt, QRb, Ktd)
    Nl = each(lambda m, x: m * x[top], strict, QRB)
    Mrbb = each(lambda m, x: bf(m * x[mid]), incl, QRB)
    Mkb = each(lambda m, x: bf(m * x[top]), strict, QRK)
    Mrkb = each(lambda m, x: bf(m * x[mid]), incl, QRK)
    N8 = each(lambda m, n: m * n, m8, Nl)
    N8b = each(bf, N8)
    N2 = each(lambda a: _dot(a, bd(a)), N8b)
    N2b = each(bf, N2)
    N4 = each(lambda a: _dot(a, bd(a)), N2b)
    W = each(lambda a, b: _dot(bf(eye12 - a), bd(bf(eye12 + b))), N8, N2)
    W = each(lambda w, n4: _dot(bf(w), bd(bf(eye12 + n4))), W, N4)
    for lvl in range(3):
        Wb = each(bf, W)
        T = each(lambda m, n, w: _dot(bf(m * n), bd(w)), msk(3 + lvl), Nl, Wb)
        W = each(lambda w, wb, t: w - _dot(wb, bd(bf(t))), W, Wb, T)
    Wb = each(bf, W)
    Whb = each(lambda w, q: bf(_dot(w, bd(q))), Wb, Qb)
    XV = each(lambda mk, mrk, v: _dot(stack(mk, mrk), bd(v)), Mkb, Mrkb, Vb)
    U0b = each(lambda w, x: bf(-_dot(w, bd(bf(x[top])))), Wb, XV)
    XWU = each(lambda m, wh, u: _dot(m, side(bd(wh), bd(u))), Mrbb, Whb, U0b)
    BWU = each(lambda b, wh, u: _dot(b, side(wh, u)), BcTb, Whb, U0b)
    KV = each(_dot, KcTb, Vb)
    Y0 = each(lambda xwu, xv: xwu[:, pw:] + xv[mid], XWU, XV)
    Hd = each(lambda bwu, kv: mask_bd * (bwu[:, pw:] + kv), BWU, KV)
    RG = each(lambda rh, lt, xwu, bwu: _split2(stack(rh - xwu[:, :pw],
                                                     eye_bd * jnp.exp(lt) - mask_bd * bwu[:, :pw])),
              Rh, ltot, XWU, BWU)

    ST = [s_scr[d, p] for d, p in lanes]
    for step in range(cb):
        idx = [chains.index((d, step if d == 0 else cb - 1 - step, p)) for d, p in lanes]
        XS = [_mm3(RG[i], _split2(st)) for i, st in zip(idx, ST)]
        ST = [x[CH:] + Hd[i] for i, x in zip(idx, XS)]
        for i, x in zip(idx, XS):
            d, j, p = chains[i]
            y = x[top] + Y0[i]
            half_mean = lambda a: jnp.where(left, jnp.sum(jnp.where(left, a, 0.0), axis=-1, keepdims=True),
                                            jnp.sum(jnp.where(left, 0.0, a), axis=-1, keepdims=True)) * (1.0 / HD)
            yc = y - half_mean(y)
            var = half_mean(yc * yc)
            y_refs[d][rows(j), lanes_of(p)] = yc * lax.rsqrt(var + GN_EPS) * gng_ref[p] + gnb_ref[p]
    for (d, p), st in zip(lanes, ST):
        s_scr[d, p] = st

    @pl.when(sched_ref[_SCHED_LAST, step_id] == 1)
    def _():
        for d, p in lanes:
            st = s_scr[d, p]
            sfin_ref[0, d, 2 * p] = st[:HD, :HD]
            sfin_ref[0, d, 2 * p + 1] = st[HD:, HD:]


_SCHED_FWD, _SCHED_BWD, _SCHED_SEQ, _SCHED_FIRST, _SCHED_LAST = range(5)


def rwkv_schedule(seq_lens):
    blk_rows = RWKV_CHUNKS * CH
    cols, base = [], 0
    for s, t_len in enumerate(seq_lens):
        n_b = t_len // blk_rows
        for i in range(n_b):
            cols.append((base + i, base + n_b - 1 - i, s, int(i == 0), int(i == n_b - 1)))
        base += n_b
    return np.asarray(cols, np.int32).T


def rwkv_scan(prep, masks, gng, gnb, s0t, sched):
    r, v, kk, ld, kd, bd = prep
    n = r.shape[0]
    n_seq = s0t.shape[0]
    blk_rows = RWKV_CHUNKS * CH
    pairs, pw = HEADS // 2, 2 * HD
    in_specs, args = [], []
    for d, row in ((0, _SCHED_FWD), (1, _SCHED_BWD)):
        for a in (r, v, kk):
            in_specs.append(pl.BlockSpec((blk_rows, D_MIX), lambda i, s, row=row: (s[row, i], 0)))
            args.append(a)
        for a in (ld, kd, bd):
            in_specs.append(pl.BlockSpec((1, blk_rows, D_MIX), lambda i, s, row=row, d=d: (d, s[row, i], 0)))
            args.append(a)
    state_spec = pl.BlockSpec((1, 2, pairs, pw, pw), lambda i, s: (s[_SCHED_SEQ, i], 0, 0, 0, 0))
    in_specs += [pl.BlockSpec((2, 6, CH, pw), lambda i, s: (0, 0, 0, 0)),
                 pl.BlockSpec((pairs, 1, pw), lambda i, s: (0, 0, 0)),
                 pl.BlockSpec((pairs, 1, pw), lambda i, s: (0, 0, 0)),
                 state_spec]
    args += [masks, gng, gnb, s0t]
    grid_spec = pltpu.PrefetchScalarGridSpec(
        num_scalar_prefetch=1,
        grid=(sched.shape[1],),
        in_specs=in_specs,
        out_specs=[pl.BlockSpec((blk_rows, D_MIX), lambda i, s: (s[_SCHED_FWD, i], 0)),
                   pl.BlockSpec((blk_rows, D_MIX), lambda i, s: (s[_SCHED_BWD, i], 0)),
                   pl.BlockSpec((1, 2, HEADS, HD, HD), lambda i, s: (s[_SCHED_SEQ, i], 0, 0, 0, 0))],
        scratch_shapes=[pltpu.VMEM((2, pairs, pw, pw), F32)],
    )
    return pl.pallas_call(
        functools.partial(_rwkv_pair_kernel, RWKV_CHUNKS),
        grid_spec=grid_spec,
        out_shape=[jax.ShapeDtypeStruct((n, D_MIX), F32), jax.ShapeDtypeStruct((n, D_MIX), F32),
                   jax.ShapeDtypeStruct((n_seq, 2, HEADS, HD, HD), F32)],
        compiler_params=_cparams(("arbitrary",), VMEM_LIMIT),
        name="rwkv_scan",
    )(jnp.asarray(sched), *args)


def rwkv_pack_state(s):
    n = s.shape[0]
    st = jnp.swapaxes(s, -1, -2).reshape(n, 2, HEADS // 2, 2, HD, HD)
    eye2 = jnp.eye(2, dtype=s.dtype)
    return jnp.einsum('ndpakv,ab->ndpakbv', st, eye2).reshape(n, 2, HEADS // 2, 2 * HD, 2 * HD)


def _ctx_attn_kernel(q_ref, k_ref, v_ref, y_ref, ko_ref, vo_ref):
    scale = HD ** -0.5
    for h in range(HEADS):
        sl = slice(h * HD, (h + 1) * HD)
        q = q_ref[:, sl]
        k = k_ref[:, sl]
        v = v_ref[:, sl]
        ko_ref[0, h] = k
        vo_ref[0, h] = v
        s = _dot_nt(q.astype(BF16), k.astype(BF16)) * scale
        m = jnp.max(s, axis=-1, keepdims=True)
        e = jnp.exp(s - m)
        p = e / jnp.sum(e, axis=-1, keepdims=True)
        y_ref[:, sl] = _dot(p.astype(BF16), v.astype(BF16))


def ctx_attention(z, bsz, t_len):
    qb = A_PAD // D_MIX
    return pl.pallas_call(
        _ctx_attn_kernel,
        grid=(bsz,),
        in_specs=[pl.BlockSpec((t_len, D_MIX), lambda b: (b, qb)),
                  pl.BlockSpec((t_len, D_MIX), lambda b: (b, qb + 1)),
                  pl.BlockSpec((t_len, D_MIX), lambda b: (b, qb + 2))],
        out_specs=[pl.BlockSpec((t_len, D_MIX), lambda b: (b, 0)),
                   pl.BlockSpec((1, HEADS, t_len, HD), lambda b: (b, 0, 0, 0)),
                   pl.BlockSpec((1, HEADS, t_len, HD), lambda b: (b, 0, 0, 0))],
        out_shape=[jax.ShapeDtypeStruct((bsz * t_len, D_MIX), F32),
                   jax.ShapeDtypeStruct((bsz, HEADS, t_len, HD), F32),
                   jax.ShapeDtypeStruct((bsz, HEADS, t_len, HD), F32)],
        compiler_params=_cparams(("arbitrary",)),
        name="ctx_attention",
    )(z, z, z)


def _na_kernel(rows, nblk, *refs):
    q_ref = refs[0]
    k_refs = refs[1:1 + nblk]
    v_refs = refs[1 + nblk:1 + 2 * nblk]
    kc_ref, vc_ref, tab_ref, y_ref = refs[1 + 2 * nblk:]
    r0 = pl.program_id(1) * NA_QROWS
    u0 = jnp.clip(r0 - WIN_R // 2, 0, rows - NA_KROWS)
    scale = HD ** -0.5
    kwin = jnp.concatenate([kr[...] for kr in k_refs], axis=0).astype(BF16)
    vwin = jnp.concatenate([vr[...] for vr in v_refs], axis=0).astype(BF16)
    left = lax.broadcasted_iota(jnp.int32, (1, 2 * GRID_W), 1) < GRID_W
    bias = []
    for i in range(NA_QROWS):
        r = r0 + i
        rs = jnp.clip(r - WIN_R // 2, 0, rows - WIN_R)
        per_pair = []
        for jp in range(NA_KROWS // 2):
            kr = u0 + 2 * jp
            off = [jnp.where(jnp.logical_and(kr + e >= rs, kr + e < rs + WIN_R), 0.0, NEG) for e in range(2)]
            per_pair.append((jnp.clip(kr - r + WIN_R, 0, 2 * WIN_R - 1), jnp.where(left, off[0], off[1])))
        bias.append(per_pair)
    for h in range(HEADS):
        sl = slice(h * HD, (h + 1) * HD)
        q = q_ref[:, sl].astype(BF16)
        s_raw = _dot_nt(q, kwin[:, sl]) * scale
        s_loc = jnp.concatenate([
            jnp.concatenate([s_raw[i * GRID_W:(i + 1) * GRID_W, jp * 2 * GRID_W:(jp + 1) * 2 * GRID_W]
                             + tab_ref[h, bias[i][jp][0]] + bias[i][jp][1]
                             for jp in range(NA_KROWS // 2)], axis=1)
            for i in range(NA_QROWS)], axis=0)
        s_ctx = _dot_nt(q, kc_ref[0, 0, h].astype(BF16)) * scale
        m = jnp.maximum(jnp.max(s_loc, axis=-1, keepdims=True), jnp.max(s_ctx, axis=-1, keepdims=True))
        e_loc = jnp.exp(s_loc - m)
        e_ctx = jnp.exp(s_ctx - m)
        den = jnp.sum(e_loc, axis=-1, keepdims=True) + jnp.sum(e_ctx, axis=-1, keepdims=True)
        p_loc = (e_loc / den).astype(BF16)
        p_ctx = (e_ctx / den).astype(BF16)
        y_ref[:, sl] = _dot(p_loc, vwin[:, sl]) + _dot(p_ctx, vc_ref[0, 0, h].astype(BF16))


def na_bias_table(rpb):
    cq = np.arange(GRID_W)[:, None]
    ck = np.arange(GRID_W)[None, :]
    cs = np.clip(cq - WIN_C // 2, 0, GRID_W - WIN_C)
    col_bias = np.where((ck >= cs) & (ck < cs + WIN_C), 0.0, NEG).astype(np.float32)
    col_idx = np.clip(ck - cq + WIN_C - 1, 0, 2 * WIN_C - 2)
    rpb_col = rpb.astype(F32)[:, :, col_idx] + col_bias
    padded = jnp.pad(rpb_col, ((0, 0), (1, 1), (0, 0), (0, 0)), constant_values=NEG)
    return jnp.concatenate([padded[:, :-1], padded[:, 1:]], axis=-1)


def na_attention(z, k_ctx, v_ctx, layer, bias_tab, bsz, t_len):
    rows = t_len // GRID_W
    assert rows >= NA_KROWS and rows % NA_QROWS == 0, "latent grid too small for the row-group tiling"
    qb = A_PAD // D_MIX
    past = k_ctx.shape[3]
    blk_rows = WIN_R // 2
    nblk = NA_KROWS // blk_rows
    blk_tok = blk_rows * GRID_W
    q_tok = NA_QROWS * GRID_W

    def win_spec(j, col):
        def index(b, g):
            u0 = jnp.clip(g * NA_QROWS - WIN_R // 2, 0, rows - NA_KROWS)
            return (b * (rows // blk_rows) + u0 // blk_rows + j, col)
        return pl.BlockSpec((blk_tok, D_MIX), index)

    in_specs = ([pl.BlockSpec((q_tok, D_MIX), lambda b, g: (b * (rows // NA_QROWS) + g, qb))]
                + [win_spec(j, qb + 1) for j in range(nblk)]
                + [win_spec(j, qb + 2) for j in range(nblk)]
                + [pl.BlockSpec((1, 1, HEADS, past, HD), lambda b, g: (b, layer, 0, 0, 0)),
                   pl.BlockSpec((1, 1, HEADS, past, HD), lambda b, g: (b, layer, 0, 0, 0)),
                   pl.BlockSpec(bias_tab.shape, lambda b, g: (0, 0, 0, 0))])
    return pl.pallas_call(
        functools.partial(_na_kernel, rows, nblk),
        grid=(bsz, rows // NA_QROWS),
        in_specs=in_specs,
        out_specs=pl.BlockSpec((q_tok, D_MIX), lambda b, g: (b * (rows // NA_QROWS) + g, 0)),
        out_shape=jax.ShapeDtypeStruct((bsz * t_len, D_MIX), F32),
        compiler_params=_cparams(("arbitrary", "arbitrary"), VMEM_LIMIT),
        name="na_attention",
    )(*([z] * (1 + 2 * nblk)), k_ctx, v_ctx, bias_tab)


def _s5_kernel(bsz, tb, uf_ref, ub_ref, bm_ref, cm_ref, ar_ref, ai_ref, x0_ref, yf_ref, yb_ref, xf_ref,
               x_scr, xo_scr, carry_scr):
    i = pl.program_id(1)

    @pl.when(i == 0)
    def _():
        carry_scr[...] = x0_ref[0]

    n_re = S5_GW // LANE
    n_rows = S5_GB * bsz
    assert n_rows == SUBLANE
    blk = bsz * tb
    group = min(n_re, S5_CARRY_VREGS // 2)
    ch = D_MIX // S5_GB
    lane = lambda c: slice(c * LANE, (c + 1) * LANE)
    steps_of = lambda row: pl.ds(row, tb, stride=n_rows)
    rows_of = lambda t: pl.ds(pl.multiple_of(t * n_rows, SUBLANE), n_rows)
    for d, (u_ref, y_ref) in enumerate(((uf_ref, yf_ref), (ub_ref, yb_ref))):
        u = u_ref[...].reshape(blk, D_MIX).astype(BF16)
        for k in range(S5_GB):
            bu = _dot(u[:, k * ch:(k + 1) * ch], bm_ref[d, k])
            for c in range(2 * n_re):
                for b in range(bsz):
                    x_scr[c, steps_of(k * bsz + b), :] = bu[b * tb:(b + 1) * tb, lane(c)]
        for c0 in range(0, n_re, group):
            tiles = list(range(c0, c0 + group))
            ar = [ar_ref[d, :, lane(c)] for c in tiles]
            ai = [ai_ref[d, :, lane(c)] for c in tiles]

            def step(s, x, d=d, tiles=tiles, ar=ar, ai=ai):
                t = s if d == 0 else tb - 1 - s
                rows = rows_of(t)
                out = []
                for c, a_r, a_i, (xr, xi) in zip(tiles, ar, ai, x):
                    nr = a_r * xr - a_i * xi + x_scr[c, rows, :]
                    ni = a_r * xi + a_i * xr + x_scr[n_re + c, rows, :]
                    xo_scr[c, rows, :] = nr
                    xo_scr[n_re + c, rows, :] = ni
                    out.append((nr, ni))
                return tuple(out)

            init = tuple((carry_scr[d, :, lane(c)], carry_scr[d, :, lane(n_re + c)]) for c in tiles)
            fin = lax.fori_loop(0, tb, step, init, unroll=4)
            for c, (xr, xi) in zip(tiles, fin):
                carry_scr[d, :, lane(c)] = xr
                carry_scr[d, :, lane(n_re + c)] = xi
        ys = []
        for k in range(S5_GB):
            xs = jnp.concatenate(
                [jnp.concatenate([xo_scr[c, steps_of(k * bsz + b), :] for b in range(bsz)], axis=0)
                 for c in range(2 * n_re)], axis=-1)
            ys.append(_dot(xs.astype(BF16), cm_ref[k]))
        y_ref[...] = jnp.concatenate(ys, axis=-1).reshape(bsz, tb, D_MIX)

    @pl.when(i == pl.num_programs(1) - 1)
    def _():
        xf_ref[0] = carry_scr[...]


def s5_scan(z3, bmat, cmat, a_re, a_im, x0, tb, grp):
    bsz, t_len, _ = z3.shape
    n_t = t_len // tb
    n_rows = S5_GB * grp
    ucol = (A_PAD + 3 * D_MIX) // D_MIX
    full = lambda shape: pl.BlockSpec(shape, lambda g, i: (0,) * len(shape))
    state_spec = pl.BlockSpec((1, 2, n_rows, 2 * S5_GW), lambda g, i: (g, 0, 0, 0))
    y_shape = jax.ShapeDtypeStruct((bsz, t_len, D_MIX), F32)
    return pl.pallas_call(
        functools.partial(_s5_kernel, grp, tb),
        grid=(bsz // grp, n_t),
        in_specs=[pl.BlockSpec((grp, tb, D_MIX), lambda g, i: (g, i, ucol)),
                  pl.BlockSpec((grp, tb, D_MIX), lambda g, i: (g, n_t - 1 - i, ucol)),
                  full((2, S5_GB, D_MIX // S5_GB, 2 * S5_GW)), full((S5_GB, 2 * S5_GW, D_MIX // S5_GB)),
                  full((2, n_rows, S5_GW)), full((2, n_rows, S5_GW)), state_spec],
        out_specs=[pl.BlockSpec((grp, tb, D_MIX), lambda g, i: (g, i, 0)),
                   pl.BlockSpec((grp, tb, D_MIX), lambda g, i: (g, n_t - 1 - i, 0)),
                   state_spec],
        out_shape=[y_shape, y_shape, jax.ShapeDtypeStruct((bsz // grp, 2, n_rows, 2 * S5_GW), F32)],
        scratch_shapes=[pltpu.VMEM((2 * S5_GW // LANE, n_rows * tb, LANE), F32),
                        pltpu.VMEM((2 * S5_GW // LANE, n_rows * tb, LANE), F32),
                        pltpu.VMEM((2, n_rows, 2 * S5_GW), F32)],
        compiler_params=_cparams(("arbitrary", "arbitrary"), VMEM_LIMIT),
        name="s5_scan",
    )(z3, z3, bmat, cmat, a_re, a_im, x0)


def s5_rows(x, grp):
    n_g = x.shape[1] // grp
    return (x.reshape(2, n_g, grp, S5_GB, S5_GW).transpose(1, 0, 3, 2, 4)
            .reshape(n_g, 2, S5_GB * grp, S5_GW))


def s5_unrows(x, grp):
    n_g = x.shape[0]
    return x.reshape(n_g, 2, S5_GB, grp, S5_GW).transpose(1, 0, 3, 2, 4).reshape(2, n_g * grp, S5_N)


def _merge_kernel(gnf_ref, gnb_ref, bonus_ref, g_ref, yb_ref, ycf_ref, ycb_ref, u_ref, zg0_ref, zg1_ref, zg2_ref,
                  x_ref, mod_ref, s5d_ref, wglu_ref, bglu_ref, wb_ref, wout_ref, lng_ref, lnb_ref, x1_ref, h2_ref):
    ya = (gnf_ref[...] + gnb_ref[...] + bonus_ref[...]) * g_ref[...]
    yc = ycf_ref[...] + ycb_ref[...] + s5d_ref[...] * u_ref[...]
    yc = 0.5 * yc * (1.0 + jnp.tanh(math.sqrt(2.0 / math.pi) * (yc + 0.044715 * (yc * yc * yc))))
    yc = yc * _sigmoid(_dot(yc.astype(BF16), wglu_ref[...]) + bglu_ref[...])
    merged = (_dot(ya.astype(BF16), wb_ref[0]) * _sigmoid(zg0_ref[...])
              + _dot(yb_ref[...].astype(BF16), wb_ref[1]) * _sigmoid(zg1_ref[...])
              + _dot(yc.astype(BF16), wb_ref[2]) * _sigmoid(zg2_ref[...]))
    mo = _dot(merged.astype(BF16), wout_ref[...])
    m = mod_ref[0]
    x1 = _layer_norm(DN_ALPHA * x_ref[...] + m[2:3] * mo, lng_ref[...], lnb_ref[...])
    x1_ref[...] = x1
    h2_ref[...] = (x1 * (1.0 + m[4:5]) + m[3:4]).astype(BF16)


def merge_branches(gn, bonus, g, yb, yc, z, x, mods, mod_of_tile, p):
    n = x.shape[0]
    row = lambda w, col=0: pl.BlockSpec((TM, w), lambda i, col=col: (i, col))
    full = lambda shape: pl.BlockSpec(shape, lambda i: (0,) * len(shape))
    gb = (A_PAD + 4 * D_MIX) // D_MODEL
    return pl.pallas_call(
        _merge_kernel,
        grid=(n // TM,),
        in_specs=[row(D_MIX), row(D_MIX), row(D_MIX), row(D_MIX), row(D_MIX), row(D_MIX), row(D_MIX),
                  row(D_MIX, A_PAD // D_MIX + 3),
                  row(D_MODEL, gb), row(D_MODEL, gb + 1), row(D_MODEL, gb + 2),
                  row(D_MODEL),
                  pl.BlockSpec((1, SUBLANE, D_MODEL), lambda i: (mod_of_tile(i, TM), 0, 0)),
                  full((1, D_MIX)), full((D_MIX, D_MIX)), full((1, D_MIX)),
                  full((3, D_MIX, D_MODEL)), full((D_MODEL, D_MODEL)),
                  full((1, D_MODEL)), full((1, D_MODEL))],
        out_specs=[row(D_MODEL), row(D_MODEL)],
        out_shape=[jax.ShapeDtypeStruct((n, D_MODEL), F32), jax.ShapeDtypeStruct((n, D_MODEL), BF16)],
        compiler_params=_cparams(("arbitrary",), VMEM_LIMIT),
        name="merge_branches",
    )(gn[0], gn[1], bonus, g, yb, yc[0], yc[1], z, z, z, z, x, mods,
      p['s5_d'], p['w_glu'], p['b_glu'], p['w_branch'], p['w_out'], p['ln1_g'], p['ln1_b'])


def _first_max(val, idx, big):
    m = jnp.max(jnp.max(val, axis=1, keepdims=True), axis=0, keepdims=True)
    cand = jnp.where(val == m, idx, big)
    first = jnp.min(jnp.min(cand, axis=1, keepdims=True), axis=0, keepdims=True)
    return m, idx == first


def _router_kernel(h_ref, wt_ref, bias_ref, gates_ref):
    per = N_EXPERTS // N_GROUPS
    logits = _dot_nt(wt_ref[...], h_ref[...])
    n = logits.shape[1]
    scores = _sigmoid(logits).reshape(N_GROUPS, per, n)
    sel = scores + bias_ref[...]
    e_idx = (lax.broadcasted_iota(jnp.int32, (N_GROUPS, per, n), 0) * per
             + lax.broadcasted_iota(jnp.int32, (N_GROUPS, per, n), 1))
    in_grp = lax.broadcasted_iota(jnp.int32, (N_GROUPS, per, n), 1)
    m1 = jnp.max(sel, axis=1, keepdims=True)
    first = jnp.min(jnp.where(sel == m1, in_grp, per), axis=1, keepdims=True)
    m2 = jnp.max(jnp.where(in_grp == first, -jnp.inf, sel), axis=1, keepdims=True)
    grp = m1 + m2
    g_idx = lax.broadcasted_iota(jnp.int32, (N_GROUPS, 1, n), 0)
    gmask = jnp.zeros((N_GROUPS, 1, n), F32)
    for _ in range(TOPK_GROUPS):
        _, hit = _first_max(grp, g_idx, N_GROUPS)
        gmask = jnp.where(hit, 1.0, gmask)
        grp = jnp.where(hit, -jnp.inf, grp)
    cur = jnp.where(jnp.broadcast_to(gmask, sel.shape) > 0.0, sel, NEG)
    w = jnp.zeros((N_GROUPS, per, n), F32)
    for _ in range(TOP_K):
        _, hit = _first_max(cur, e_idx, N_EXPERTS)
        w = jnp.where(hit, scores, w)
        cur = jnp.where(hit, -jnp.inf, cur)
    tot = jnp.sum(jnp.sum(w, axis=1, keepdims=True), axis=0, keepdims=True)
    gates = (ROUTED_SCALE * w / tot).reshape(N_EXPERTS, n)
    hi = gates.astype(BF16).astype(F32)
    lo = (gates - hi).astype(BF16).astype(F32)
    gates_ref[...] = jnp.concatenate([hi, lo], axis=0).T.astype(BF16)


def moe_router(h2, router_wt, router_bias):
    n = h2.shape[0]
    return pl.pallas_call(
        _router_kernel,
        grid=(n // TM,),
        in_specs=[pl.BlockSpec((TM, D_MODEL), lambda i: (i, 0)),
                  pl.BlockSpec((N_EXPERTS, D_MODEL), lambda i: (0, 0)),
                  pl.BlockSpec((N_GROUPS, N_EXPERTS // N_GROUPS, 1), lambda i: (0, 0, 0))],
        out_specs=pl.BlockSpec((TM, 2 * N_EXPERTS), lambda i: (i, 0)),
        out_shape=jax.ShapeDtypeStruct((n, 2 * N_EXPERTS), BF16),
        compiler_params=_cparams(("arbitrary",)),
        name="moe_router",
    )(h2, router_wt, router_bias)


def _moe_kernel(ec, h_ref, gates_ref, x1_ref, mod_ref, wg_ref, wu_ref, wd_ref, ex_ref,
                sg_ref, su_ref, sd_ref, lng_ref, lnb_ref, out_ref, acc_scr):
    j = pl.program_id(1)
    h = h_ref[...]

    @pl.when(j == 0)
    def _():
        sh = _dot(h, sg_ref[...])
        sh = sh * _sigmoid(sh) * _dot(h, su_ref[...])
        acc_scr[...] = _dot(sh.astype(BF16), sd_ref[...])

    gexp = _dot(gates_ref[...], ex_ref[0])
    wg = jnp.concatenate([wg_ref[e] for e in range(ec)], axis=1)
    wu = jnp.concatenate([wu_ref[e] for e in range(ec)], axis=1)
    hg = _dot(h, wg)
    hu = _dot(h, wu)
    act = hg * _sigmoid(hg) * hu * gexp
    acc_scr[...] += _dot(act.astype(BF16), wd_ref[...])

    @pl.when(j == pl.num_programs(1) - 1)
    def _():
        m = mod_ref[0]
        out_ref[...] = _layer_norm(DN_ALPHA * x1_ref[...] + m[5:6] * acc_scr[...], lng_ref[...], lnb_ref[...])


def moe_ffn(h2, gates, x1, mods, mod_of_tile, p, tm=512, ec=8):
    n = h2.shape[0]
    wcols = ec * D_EXPERT
    n_j = N_EXPERTS // ec
    full = lambda shape: pl.BlockSpec(shape, lambda i, j: (0,) * len(shape))
    return pl.pallas_call(
        functools.partial(_moe_kernel, ec),
        grid=(n // tm, n_j),
        in_specs=[pl.BlockSpec((tm, D_MODEL), lambda i, j: (i, 0)),
                  pl.BlockSpec((tm, 2 * N_EXPERTS), lambda i, j: (i, 0)),
                  pl.BlockSpec((tm, D_MODEL), lambda i, j: (i, 0)),
                  pl.BlockSpec((1, SUBLANE, D_MODEL), lambda i, j: (mod_of_tile(i, tm), 0, 0)),
                  pl.BlockSpec((ec, D_MODEL, D_EXPERT), lambda i, j: (j, 0, 0)),
                  pl.BlockSpec((ec, D_MODEL, D_EXPERT), lambda i, j: (j, 0, 0)),
                  pl.BlockSpec((wcols, D_MODEL), lambda i, j: (j, 0)),
                  pl.BlockSpec((1, 2 * N_EXPERTS, wcols), lambda i, j: (j, 0, 0)),
                  full((D_MODEL, D_SHARED)), full((D_MODEL, D_SHARED)), full((D_SHARED, D_MODEL)),
                  full((1, D_MODEL)), full((1, D_MODEL))],
        out_specs=pl.BlockSpec((tm, D_MODEL), lambda i, j: (i, 0)),
        out_shape=jax.ShapeDtypeStruct((n, D_MODEL), F32),
        scratch_shapes=[pltpu.VMEM((tm, D_MODEL), F32)],
        compiler_params=_cparams(("arbitrary", "arbitrary"), VMEM_LIMIT),
        name="moe_ffn",
    )(h2, gates, x1, mods, p['wg'], p['wu'], p['wd'], p['expand'],
      p['sh_g'], p['sh_u'], p['sh_d'], p['ln2_g'], p['ln2_b'])


def _block_diag2(m):
    z = jnp.zeros_like(m[0])
    return jnp.concatenate([jnp.concatenate([m[0], z], axis=1), jnp.concatenate([z, m[1]], axis=1)], axis=0)


def _layer_params(P, l):
    g = lambda name: P[name][l]
    w_in = g('w_in')
    pad = jnp.zeros((D_MODEL, A_PAD - A_COLS), F32)
    p = {}
    p['w_in'] = jnp.concatenate([w_in[:, :A_COLS], pad, w_in[:, A_COLS:]], axis=1).astype(BF16)
    p['mu'] = jnp.concatenate([g('rwkv_mu'), jnp.zeros((A_PAD - A_COLS,), F32)])[None, :]
    p['w2bd'] = _block_diag2(g('rwkv_w2')).astype(BF16)
    p['a2bd'] = _block_diag2(g('rwkv_a2')).astype(BF16)
    p['g2'] = g('rwkv_g2').astype(BF16)
    p['w0'] = g('rwkv_w0').reshape(1, 2 * D_MIX)
    p['a0'] = g('rwkv_a0').reshape(1, 2 * D_MIX)
    p['kkp'] = g('rwkv_kk')[None, :]
    p['ka'] = g('rwkv_ka')[None, :]
    p['rk'] = g('rwkv_rk')[None, :]
    hid = np.arange(D_MIX) // HD
    p['seg_ones'] = jnp.asarray((hid[:, None] == hid[None, :]).astype(np.float32), dtype=BF16)
    p['gng'] = g('rwkv_gn_g').reshape(HEADS // 2, 1, 2 * HD)
    p['gnb'] = g('rwkv_gn_b').reshape(HEADS // 2, 1, 2 * HD)
    a = lax.complex(g('s5_a_re'), g('s5_a_im'))
    dt = jnp.exp(g('s5_log_dt'))[..., None]
    a_bar = jnp.exp(dt * a)
    b_bar = ((a_bar - 1.0) / a)[..., None] * lax.complex(g('s5_b_re'), g('s5_b_im'))
    eye_g = jnp.eye(C_GROUPS // S5_GB, dtype=F32)
    gpb = C_GROUPS // S5_GB
    bd = lambda m: jnp.einsum('dkgph,gj->dkghjp', m.reshape(2, S5_GB, gpb, C_STATE, C_GROUP),
                              eye_g).reshape(2, S5_GB, D_MIX // S5_GB, S5_GW)
    p['s5_bmat'] = jnp.concatenate([bd(b_bar.real), bd(b_bar.imag)], axis=-1).astype(BF16)
    cd = lambda m: jnp.einsum('kghp,gj->kjpgh', m.reshape(S5_GB, gpb, C_GROUP, C_STATE),
                              eye_g).reshape(S5_GB, S5_GW, D_MIX // S5_GB)
    p['s5_cmat'] = jnp.concatenate([cd(g('s5_c_re')), -cd(g('s5_c_im'))], axis=1).astype(BF16)
    p['s5_ar'] = a_bar.real.reshape(2, 1, S5_N)
    p['s5_ai'] = a_bar.imag.reshape(2, 1, S5_N)
    p['s5_d'] = g('s5_d')[None, :]
    p['w_glu'] = g('s5_w_glu').astype(BF16)
    p['b_glu'] = g('s5_b_glu')[None, :]
    p['w_branch'] = g('w_branch').astype(BF16)
    p['w_out'] = g('w_out').astype(BF16)
    p['ln1_g'] = g('ln1_g')[None, :]
    p['ln1_b'] = g('ln1_b')[None, :]
    p['router_wt'] = g('router_w').T.astype(BF16)
    p['router_bias'] = g('router_bias').reshape(N_GROUPS, N_EXPERTS // N_GROUPS, 1)
    ecols = N_EXPERTS * D_EXPERT
    p['wg'] = g('exp_w_gate').astype(BF16)
    p['wu'] = g('exp_w_up').astype(BF16)
    p['wd'] = g('exp_w_down').reshape(ecols, D_MODEL).astype(BF16)
    p['sh_g'] = g('sh_w_gate').astype(BF16)
    p['sh_u'] = g('sh_w_up').astype(BF16)
    p['sh_d'] = g('sh_w_down').astype(BF16)
    p['ln2_g'] = g('ln2_g')[None, :]
    p['ln2_b'] = g('ln2_b')[None, :]
    return p


def _expand_table(ec):
    n_j = N_EXPERTS // ec
    t = np.zeros((n_j, 2 * N_EXPERTS, ec * D_EXPERT), np.float32)
    for e in range(N_EXPERTS):
        j, q = divmod(e, ec)
        t[j, e, q * D_EXPERT:(q + 1) * D_EXPERT] = 1.0
        t[j, N_EXPERTS + e, q * D_EXPERT:(q + 1) * D_EXPERT] = 1.0
    return jnp.asarray(t, dtype=BF16)


def kernel(x_prompt, x_sample, c, cache_na_k, cache_na_v, state_rwkv, state_s5_re, state_s5_im, c_ctx, w_ada, b_ada, w_in, rwkv_mu, rwkv_w0, rwkv_w2, rwkv_a0, rwkv_a2, rwkv_g2, rwkv_kk, rwkv_ka, rwkv_rk, rwkv_gn_g, rwkv_gn_b, na_rpb, s5_a_re, s5_a_im, s5_log_dt, s5_b_re, s5_b_im, s5_c_re, s5_c_im, s5_d, s5_w_glu, s5_b_glu, w_branch, w_out, ln1_g, ln1_b, router_w, router_bias, exp_w_gate, exp_w_up, exp_w_down, sh_w_gate, sh_w_up, sh_w_down, ln2_g, ln2_b):
    P = dict(w_in=w_in, rwkv_mu=rwkv_mu, rwkv_w0=rwkv_w0, rwkv_w2=rwkv_w2, rwkv_a0=rwkv_a0, rwkv_a2=rwkv_a2,
             rwkv_g2=rwkv_g2, rwkv_kk=rwkv_kk, rwkv_ka=rwkv_ka, rwkv_rk=rwkv_rk, rwkv_gn_g=rwkv_gn_g,
             rwkv_gn_b=rwkv_gn_b, s5_a_re=s5_a_re, s5_a_im=s5_a_im, s5_log_dt=s5_log_dt, s5_b_re=s5_b_re,
             s5_b_im=s5_b_im, s5_c_re=s5_c_re, s5_c_im=s5_c_im, s5_d=s5_d, s5_w_glu=s5_w_glu,
             s5_b_glu=s5_b_glu, w_branch=w_branch, w_out=w_out, ln1_g=ln1_g, ln1_b=ln1_b, router_w=router_w,
             router_bias=router_bias, exp_w_gate=exp_w_gate, exp_w_up=exp_w_up, exp_w_down=exp_w_down,
             sh_w_gate=sh_w_gate, sh_w_up=sh_w_up, sh_w_down=sh_w_down, ln2_g=ln2_g, ln2_b=ln2_b)
    bc, tc, _ = x_prompt.shape
    bl, tl, _ = x_sample.shape
    depth = w_in.shape[0]

    cond = jnp.concatenate([c_ctx[None, :], c, jnp.zeros((SUBLANE - 1 - bl, D_MODEL), F32)], axis=0)
    ada = ada_modulation(cond, w_ada, b_ada)
    masks = jnp.asarray(_chunk_masks())
    expand = _expand_table(8)

    paths = {
        'ctx': dict(bsz=bc, t=tc, x=x_prompt.reshape(bc * tc, D_MODEL), mod_of_tile=lambda i, tm: 0),
        'lat': dict(bsz=bl, t=tl, x=x_sample.reshape(bl * tl, D_MODEL),
                    mod_of_tile=lambda i, tm: 1 + (i * tm) // tl),
    }
    for q in paths.values():
        tiles = q['t'] // TM
        pos = np.arange(q['bsz'] * tiles) % tiles
        q['seq_tiles'] = jnp.asarray(np.concatenate([np.full_like(pos, tiles), pos]).astype(np.int32))
        q['sched'] = rwkv_schedule([q['t']] * q['bsz'])
        q['s5_tb'] = min(S5_ROWS // S5_GRP, q['t'])
        assert q['bsz'] % S5_GRP == 0 and q['s5_tb'] % SUBLANE == 0 and q['t'] % q['s5_tb'] == 0

    new_k, new_v, new_rwkv, new_s5 = [], [], [], []
    for l in range(depth):
        p = _layer_params(P, l)
        p['expand'] = expand
        mods = ada[l, :1 + bl].reshape(1 + bl, 6, D_MODEL)
        mods = jnp.concatenate([mods, jnp.zeros((1 + bl, SUBLANE - 6, D_MODEL), F32)], axis=1)
        bias_tab = na_bias_table(na_rpb[l])
        for name, q in paths.items():
            bsz, t_len, x, mod_of_tile = q['bsz'], q['t'], q['x'], q['mod_of_tile']
            z = in_projection(x, mods, p['w_in'], mod_of_tile)

            r, v, kk, ld, kd, bd, g, bonus = rwkv_prep(z, q['seq_tiles'], p)
            if name == 'ctx':
                s0t = jnp.zeros((bsz, 2, HEADS // 2, 2 * HD, 2 * HD), F32)
            else:
                s0t = rwkv_pack_state(state_rwkv[:, l])
            gn_f, gn_b, s_fin = rwkv_scan((r, v, kk, ld, kd, bd), masks, p['gng'], p['gnb'], s0t, q['sched'])

            if name == 'ctx':
                yb, k_h, v_h = ctx_attention(z, bsz, t_len)
            else:
                yb = na_attention(z, cache_na_k, cache_na_v, l, bias_tab, bsz, t_len)

            if name == 'ctx':
                x0 = jnp.zeros((bsz // S5_GRP, 2, S5_GB * S5_GRP, 2 * S5_GW), F32)
            else:
                x0 = jnp.concatenate(
                    [s5_rows(jnp.swapaxes(s[:, l].reshape(bsz, 2, S5_N), 0, 1), S5_GRP)
                     for s in (state_s5_re, state_s5_im)], axis=-1)
            a_re, a_im = (s5_rows(jnp.broadcast_to(a, (2, S5_GRP, S5_N)), S5_GRP)[0]
                          for a in (p['s5_ar'], p['s5_ai']))
            yc_f, yc_b, x_fin = s5_scan(z.reshape(bsz, t_len, IN_COLS_P), p['s5_bmat'], p['s5_cmat'],
                                        a_re, a_im, x0, q['s5_tb'], S5_GRP)
            yc = (yc_f.reshape(bsz * t_len, D_MIX), yc_b.reshape(bsz * t_len, D_MIX))

            x1, h2 = merge_branches((gn_f, gn_b), bonus, g, yb, yc, z, x, mods, mod_of_tile, p)
            gates = moe_router(h2, p['router_wt'], p['router_bias'])
            q['x'] = moe_ffn(h2, gates, x1, mods, mod_of_tile, p)
            if name == 'ctx':
                new_k.append(k_h)
                new_v.append(v_h)
                new_rwkv.append(jnp.swapaxes(s_fin, -1, -2))
                new_s5.append([jnp.swapaxes(s5_unrows(part, S5_GRP), 0, 1)
                               for part in (x_fin[..., :S5_GW], x_fin[..., S5_GW:])])

    s5_re, s5_im = (jnp.stack([layer[part] for layer in new_s5], axis=1).reshape(bc, depth, 2, C_GROUPS, C_STATE)
                    for part in range(2))
    return (paths['ctx']['x'].reshape(bc, tc, D_MODEL), paths['lat']['x'].reshape(bl, tl, D_MODEL),
            jnp.stack(new_k, axis=1), jnp.stack(new_v, axis=1), jnp.stack(new_rwkv, axis=1), s5_re, s5_im)
```

```python
import functools
import math

import numpy as np
import jax
import jax.numpy as jnp
from jax import lax
from jax.experimental import pallas as pl
from jax.experimental.pallas import tpu as pltpu

F32 = jnp.float32
BF16 = jnp.bfloat16

D_MODEL = 1024
DEPTH = 2
GRID_W = 64
D_MIX = 512
HEADS = 8
HD = 64
LORA = 128
WIN_R = 8
WIN_C = 16
C_GROUP = 16
C_GROUPS = D_MIX // C_GROUP
C_STATE = 64
S5_N = C_GROUPS * C_STATE
N_EXPERTS = 64
TOP_K = 8
N_GROUPS = 8
TOPK_GROUPS = 4
D_EXPERT = 128
D_SHARED = 128
ROUTED_SCALE = 2.5
LN_EPS = 1e-5
GN_EPS = 64e-5
NEG = -1e30
DN_ALPHA = (2 * DEPTH) ** 0.25
A_COLS = 3 * D_MIX + 3 * LORA
A_PAD = 2048
IN_COLS_P = A_PAD + 4 * D_MIX + 3 * D_MODEL

LANE = 128
SUBLANE = 8
TM = 256
CH = HD
RWKV_CHUNKS = 4
MOE_EC = 4
NA_QROWS = 8
NA_KROWS = NA_QROWS + WIN_R
S5_GB = 4
S5_GW = S5_N // S5_GB
S5_GRP = SUBLANE // S5_GB
S5_ROWS = 512
S5_CARRY_VREGS = 32
VMEM_LIMIT = 56 * 1024 * 1024


def _cparams(sem, vmem=None):
    return pltpu.CompilerParams(dimension_semantics=sem, vmem_limit_bytes=vmem)


def _sigmoid(x):
    return 1.0 / (1.0 + jnp.exp(-x))


def _dot(a, b, precision=None):
    return jnp.dot(a, b, preferred_element_type=F32, precision=precision)


def _dot_nt(a, b, precision=None):
    return lax.dot_general(a, b, (((1,), (1,)), ((), ())), preferred_element_type=F32, precision=precision)


def _split2(x):
    hi = x.astype(BF16)
    return hi, (x - hi.astype(F32)).astype(BF16)


def _split3(x):
    hi = x.astype(BF16)
    r1 = x - hi.astype(F32)
    mid = r1.astype(BF16)
    return hi, mid, (r1 - mid.astype(F32)).astype(BF16)


def _seg_sum(x, ones_bf16):
    return sum(_dot(part, ones_bf16) for part in _split3(x))


def _mm3(a, b, dot=_dot):
    return dot(a[0], b[0]) + (dot(a[1], b[0]) + dot(a[0], b[1]))


def _layer_norm(x, g, b):
    mu = jnp.mean(x, axis=-1, keepdims=True)
    xc = x - mu
    var = jnp.mean(xc * xc, axis=-1, keepdims=True)
    return xc * lax.rsqrt(var + LN_EPS) * g + b


def _ada_kernel(c_ref, w_ref, b_ref, o_ref):
    c = c_ref[...]
    s = c * _sigmoid(c)
    o_ref[0] = _dot(s.astype(BF16), w_ref[0].astype(BF16)) + b_ref[0]


def ada_modulation(cond, w_ada, b_ada):
    n_l, d, n6 = w_ada.shape
    tn = 1536
    return pl.pallas_call(
        _ada_kernel,
        grid=(n_l, n6 // tn),
        in_specs=[pl.BlockSpec((SUBLANE, d), lambda l, j: (0, 0)),
                  pl.BlockSpec((1, d, tn), lambda l, j: (l, 0, j)),
                  pl.BlockSpec((1, 1, tn), lambda l, j: (l, 0, j))],
        out_specs=pl.BlockSpec((1, SUBLANE, tn), lambda l, j: (l, 0, j)),
        out_shape=jax.ShapeDtypeStruct((n_l, SUBLANE, n6), F32),
        compiler_params=_cparams(("arbitrary", "arbitrary")),
        name="ada_modulation",
    )(cond, w_ada, b_ada.reshape(n_l, 1, n6))


def _inproj_kernel(x_ref, mod_ref, w_ref, z_ref, h_scr):
    @pl.when(pl.program_id(1) == 0)
    def _():
        m = mod_ref[0]
        h_scr[...] = (x_ref[...] * (1.0 + m[1:2]) + m[0:1]).astype(BF16)
    z_ref[...] = _dot(h_scr[...], w_ref[...])


def in_projection(x, mods, w_in_p, mod_of_tile, tm=2048, tn=1024):
    n, d = x.shape
    cols = w_in_p.shape[1]
    return pl.pallas_call(
        _inproj_kernel,
        grid=(n // tm, cols // tn),
        in_specs=[pl.BlockSpec((tm, d), lambda i, j: (i, 0)),
                  pl.BlockSpec((1, SUBLANE, d), lambda i, j: (mod_of_tile(i, tm), 0, 0)),
                  pl.BlockSpec((d, tn), lambda i, j: (0, j))],
        out_specs=pl.BlockSpec((tm, tn), lambda i, j: (i, j)),
        out_shape=jax.ShapeDtypeStruct((n, cols), F32),
        scratch_shapes=[pltpu.VMEM((tm, d), BF16)],
        compiler_params=_cparams(("arbitrary", "arbitrary"), VMEM_LIMIT),
        name="in_projection",
    )(x, mods, w_in_p)


def _rwkv_prep_kernel(seq_tiles_ref, z_ref, zp_ref, zn_ref, mu_ref, w2_ref, a2_ref, g2_ref, w0_ref, a0_ref,
                      kkp_ref, ka_ref, rk_ref, e_ref,
                      r_ref, v_ref, kk_ref, ld_ref, kd_ref, bd_ref, g_ref, bonus_ref):
    i = pl.program_id(0)
    tiles = seq_tiles_ref[i]
    pos = seq_tiles_ref[i + pl.num_programs(0)]
    x = z_ref[...]
    tm = x.shape[0]
    rows = lax.broadcasted_iota(jnp.int32, x.shape, 0)
    prev_row = jnp.where(pos == 0, 0.0, zp_ref[SUBLANE - 1:SUBLANE, :])
    next_row = jnp.where(pos == tiles - 1, 0.0, zn_ref[0:1, :])
    xm1 = jnp.where(rows == 0, prev_row, pltpu.roll(x, 1, axis=0))
    xp1 = jnp.where(rows == tm - 1, next_row, pltpu.roll(x, tm - 1, axis=0))
    za = x + mu_ref[...] * (0.5 * (xm1 + xp1) - x)

    r = za[:, 0:D_MIX]
    k = za[:, D_MIX:2 * D_MIX]
    v = za[:, 2 * D_MIX:3 * D_MIX]
    lw = za[:, 3 * D_MIX:3 * D_MIX + LORA]
    la = za[:, 3 * D_MIX + LORA:3 * D_MIX + 2 * LORA]
    lg = za[:, 3 * D_MIX + 2 * LORA:3 * D_MIX + 3 * LORA]

    w_both = w0_ref[...] + _dot(jnp.tanh(lw).astype(BF16), w2_ref[...])
    a_both = _sigmoid(a0_ref[...] + _dot(la.astype(BF16), a2_ref[...]))
    g_ref[...] = _dot(_sigmoid(lg).astype(BF16), g2_ref[...])

    e = e_ref[...]
    kks = k * kkp_ref[...]
    nrm = jnp.sqrt(_seg_sum(kks * kks, e))
    kk = kks / jnp.maximum(nrm, 1e-12)
    bonus = jnp.zeros_like(v)
    r_ref[...] = r
    v_ref[...] = v
    kk_ref[...] = kk
    for d in range(2):
        w = w_both[:, d * D_MIX:(d + 1) * D_MIX]
        a = a_both[:, d * D_MIX:(d + 1) * D_MIX]
        ld = -math.exp(-0.5) * _sigmoid(w)
        kd = k * (1.0 + (a - 1.0) * ka_ref[...])
        bd = kk * a
        bonus = bonus + _seg_sum(r * kd * rk_ref[...], e) * v
        ld_ref[d] = ld
        kd_ref[d] = kd
        bd_ref[d] = bd
    bonus_ref[...] = bonus


def rwkv_prep(z, seq_tiles, p):
    n = z.shape[0]
    nt = n // TM
    halo = TM // SUBLANE
    nb8 = n // SUBLANE
    tok = jax.ShapeDtypeStruct((n, D_MIX), F32)
    tok2 = jax.ShapeDtypeStruct((2, n, D_MIX), F32)
    full = lambda shape: pl.BlockSpec(shape, lambda i, s: (0,) * len(shape))
    tok_spec = pl.BlockSpec((TM, D_MIX), lambda i, s: (i, 0))
    tok2_spec = pl.BlockSpec((2, TM, D_MIX), lambda i, s: (0, i, 0))
    grid_spec = pltpu.PrefetchScalarGridSpec(
        num_scalar_prefetch=1,
        grid=(nt,),
        in_specs=[pl.BlockSpec((TM, A_PAD), lambda i, s: (i, 0)),
                  pl.BlockSpec((SUBLANE, A_PAD), lambda i, s: (jnp.maximum(i * halo - 1, 0), 0)),
                  pl.BlockSpec((SUBLANE, A_PAD), lambda i, s: (jnp.minimum((i + 1) * halo, nb8 - 1), 0)),
                  full((1, A_PAD)), full((LORA, 2 * D_MIX)), full((LORA, 2 * D_MIX)), full((LORA, D_MIX)),
                  full((1, 2 * D_MIX)), full((1, 2 * D_MIX)), full((1, D_MIX)), full((1, D_MIX)),
                  full((1, D_MIX)), full((D_MIX, D_MIX))],
        out_specs=[tok_spec, tok_spec, tok_spec, tok2_spec, tok2_spec, tok2_spec, tok_spec, tok_spec],
    )
    return pl.pallas_call(
        _rwkv_prep_kernel,
        grid_spec=grid_spec,
        out_shape=[tok, tok, tok, tok2, tok2, tok2, tok, tok],
        compiler_params=_cparams(("arbitrary",), VMEM_LIMIT),
        name="rwkv_prep",
    )(seq_tiles, z, z, z, p['mu'], p['w2bd'], p['a2bd'], p['g2'], p['w0'], p['a0'],
      p['kkp'], p['ka'], p['rk'], p['seg_ones'])


def _chunk_masks():
    t = np.arange(CH)
    fwd_incl = (t[:, None] >= t[None, :])
    out = []
    for incl in (fwd_incl, fwd_incl.T):
        strict = incl & (t[:, None] != t[None, :])
        ms = [incl, strict, strict & ((t[:, None] // 8) == (t[None, :] // 8))]
        for m in (8, 16, 32):
            ms.append(strict & ((t[:, None] // (2 * m)) == (t[None, :] // (2 * m)))
                      & ((t[:, None] // m) != (t[None, :] // m)))
        out.append(np.stack(ms))
    masks = np.stack(out).astype(np.float32)
    return np.concatenate([masks, masks], axis=-1)


def _rwkv_pair_kernel(cb, sched_ref, *refs):
    dir_refs = (refs[0:6], refs[6:12])
    m_ref, gng_ref, gnb_ref, s0_ref = refs[12:16]
    y_refs = refs[16:18]
    sfin_ref, s_scr = refs[18:20]
    step_id = pl.program_id(0)

    @pl.when(sched_ref[_SCHED_FIRST, step_id] == 1)
    def _():
        s_scr[...] = s0_ref[0]

    pw = 2 * HD
    ri = lax.broadcasted_iota(jnp.int32, (pw, pw), 0)
    ci = lax.broadcasted_iota(jnp.int32, (pw, pw), 1)
    eye_bd = (ri == ci).astype(F32)
    mask_bd = ((ri // HD) == (ci // HD)).astype(F32)
    eye12 = (lax.broadcasted_iota(jnp.int32, (CH, pw), 0)
             == lax.broadcasted_iota(jnp.int32, (CH, pw), 1) % HD).astype(F32)
    left = lax.broadcasted_iota(jnp.int32, (1, pw), 1) < HD

    bf = lambda x: x.astype(BF16)
    each = lambda f, *cols: [f(*args) for args in zip(*cols)]
    rows = lambda j: slice(j * CH, (j + 1) * CH)
    lanes_of = lambda p: slice(p * pw, (p + 1) * pw)
    stack = lambda *xs: jnp.concatenate(xs, axis=0)
    side = lambda *xs: jnp.concatenate(xs, axis=1)
    top, mid = slice(0, CH), slice(CH, 2 * CH)
    zero = jnp.zeros((), BF16)
    bd = lambda x: stack(jnp.where(left, x, zero), jnp.where(left, zero, x))

    lanes = [(d, p) for d in range(2) for p in range(HEADS // 2)]
    chains = [(d, j, p) for d, p in lanes for j in range(cb)]
    msk = lambda k: [m_ref[d, k] for d, _, _ in chains]
    incl, strict, m8 = msk(0), msk(1), msk(2)
    incl_b = each(lambda m: bf(m[:, :CH]), incl)
    get = lambda k: [dir_refs[d][k][rows(j), lanes_of(p)] if k < 3 else dir_refs[d][k][0, rows(j), lanes_of(p)]
                     for d, j, p in chains]
    R, V, KK, LD, Kd, Bd = (get(k) for k in range(6))
    L = each(lambda m, x: sum(_dot(m, part) for part in _split3(x)), incl_b, LD)
    ltot = each(lambda x: jnp.sum(x, axis=0, keepdims=True), LD)
    e_nl = each(lambda l: jnp.exp(-l), L)
    e_rest = each(lambda l, lt: jnp.exp(lt - l), L, ltot)
    Qb = each(lambda kk, l, ld: bf(kk * jnp.exp(l - ld)), KK, L, LD)
    Rh = each(lambda r, l: r * jnp.exp(l), R, L)
    QRb = each(lambda q, rh: stack(q, bf(rh)), Qb, Rh)
    Btd = each(lambda b, e: bd(bf(b * e)), Bd, e_nl)
    Ktd = each(lambda k, e: bd(bf(k * e)), Kd, e_nl)
    BcTb = each(lambda b, e: bf((b * e).T), Bd, e_rest)
    KcTb = each(lambda k, e: bf((k * e).T), Kd, e_rest)
    Vb = each(bf, V)
    QRB = each(_dot_nt, QRb, Btd)
    QRK = each(_dot_nt, QRb, Ktd)
    Nl = each(lambda m, x: m * x[top], strict, QRB)
    Mrbb = each(lambda m, x: bf(m * x[mid]), incl, QRB)
    Mkb = each(lambda m, x: bf(m * x[top]), strict, QRK)
    Mrkb = each(lambda m, x: bf(m * x[mid]), incl, QRK)
    N8 = each(lambda m, n: m * n, m8, Nl)
    N8b = each(bf, N8)
    N2 = each(lambda a: _dot(a, bd(a)), N8b)
    N2b = each(bf, N2)
    N4 = each(lambda a: _dot(a, bd(a)), N2b)
    W = each(lambda a, b: _dot(bf(eye12 - a), bd(bf(eye12 + b))), N8, N2)
    W = each(lambda w, n4: _dot(bf(w), bd(bf(eye12 + n4))), W, N4)
    for lvl in range(3):
        Wb = each(bf, W)
        T = each(lambda m, n, w: _dot(bf(m * n), bd(w)), msk(3 + lvl), Nl, Wb)
        W = each(lambda w, wb, t: w - _dot(wb, bd(bf(t))), W, Wb, T)
    Wb = each(bf, W)
    Whb = each(lambda w, q: bf(_dot(w, bd(q))), Wb, Qb)
    XV = each(lambda mk, mrk, v: _dot(stack(mk, mrk), bd(v)), Mkb, Mrkb, Vb)
    U0b = each(lambda w, x: bf(-_dot(w, bd(bf(x[top])))), Wb, XV)
    XWU = each(lambda m, wh, u: _dot(m, side(bd(wh), bd(u))), Mrbb, Whb, U0b)
    BWU = each(lambda b, wh, u: _dot(b, side(wh, u)), BcTb, Whb, U0b)
    KV = each(_dot, KcTb, Vb)
    Y0 = each(lambda xwu, xv: xwu[:, pw:] + xv[mid], XWU, XV)
    Hd = each(lambda bwu, kv: mask_bd * (bwu[:, pw:] + kv), BWU, KV)
    RG = each(lambda rh, lt, xwu, bwu: _split2(stack(rh - xwu[:, :pw],
                                                     eye_bd * jnp.exp(lt) - mask_bd * bwu[:, :pw])),
              Rh, ltot, XWU, BWU)

    ST = [s_scr[d, p] for d, p in lanes]
    for step in range(cb):
        idx = [chains.index((d, step if d == 0 else cb - 1 - step, p)) for d, p in lanes]
        XS = [_mm3(RG[i], _split2(st)) for i, st in zip(idx, ST)]
        ST = [x[CH:] + Hd[i] for i, x in zip(idx, XS)]
        for i, x in zip(idx, XS):
            d, j, p = chains[i]
            y = x[top] + Y0[i]
            half_mean = lambda a: jnp.where(left, jnp.sum(jnp.where(left, a, 0.0), axis=-1, keepdims=True),
                                            jnp.sum(jnp.where(left, 0.0, a), axis=-1, keepdims=True)) * (1.0 / HD)
            yc = y - half_mean(y)
            var = half_mean(yc * yc)
            y_refs[d][rows(j), lanes_of(p)] = yc * lax.rsqrt(var + GN_EPS) * gng_ref[p] + gnb_ref[p]
    for (d, p), st in zip(lanes, ST):
        s_scr[d, p] = st

    @pl.when(sched_ref[_SCHED_LAST, step_id] == 1)
    def _():
        for d, p in lanes:
            st = s_scr[d, p]
            sfin_ref[0, d, 2 * p] = st[:HD, :HD]
            sfin_ref[0, d, 2 * p + 1] = st[HD:, HD:]


_SCHED_FWD, _SCHED_BWD, _SCHED_SEQ, _SCHED_FIRST, _SCHED_LAST = range(5)


def rwkv_schedule(seq_lens):
    blk_rows = RWKV_CHUNKS * CH
    cols, base = [], 0
    for s, t_len in enumerate(seq_lens):
        n_b = t_len // blk_rows
        for i in range(n_b):
            cols.append((base + i, base + n_b - 1 - i, s, int(i == 0), int(i == n_b - 1)))
        base += n_b
    return np.asarray(cols, np.int32).T


def rwkv_scan(prep, masks, gng, gnb, s0t, sched):
    r, v, kk, ld, kd, bd = prep
    n = r.shape[0]
    n_seq = s0t.shape[0]
    blk_rows = RWKV_CHUNKS * CH
    pairs, pw = HEADS // 2, 2 * HD
    in_specs, args = [], []
    for d, row in ((0, _SCHED_FWD), (1, _SCHED_BWD)):
        for a in (r, v, kk):
            in_specs.append(pl.BlockSpec((blk_rows, D_MIX), lambda i, s, row=row: (s[row, i], 0)))
            args.append(a)
        for a in (ld, kd, bd):
            in_specs.append(pl.BlockSpec((1, blk_rows, D_MIX), lambda i, s, row=row, d=d: (d, s[row, i], 0)))
            args.append(a)
    state_spec = pl.BlockSpec((1, 2, pairs, pw, pw), lambda i, s: (s[_SCHED_SEQ, i], 0, 0, 0, 0))
    in_specs += [pl.BlockSpec((2, 6, CH, pw), lambda i, s: (0, 0, 0, 0)),
                 pl.BlockSpec((pairs, 1, pw), lambda i, s: (0, 0, 0)),
                 pl.BlockSpec((pairs, 1, pw), lambda i, s: (0, 0, 0)),
                 state_spec]
    args += [masks, gng, gnb, s0t]
    grid_spec = pltpu.PrefetchScalarGridSpec(
        num_scalar_prefetch=1,
        grid=(sched.shape[1],),
        in_specs=in_specs,
        out_specs=[pl.BlockSpec((blk_rows, D_MIX), lambda i, s: (s[_SCHED_FWD, i], 0)),
                   pl.BlockSpec((blk_rows, D_MIX), lambda i, s: (s[_SCHED_BWD, i], 0)),
                   pl.BlockSpec((1, 2, HEADS, HD, HD), lambda i, s: (s[_SCHED_SEQ, i], 0, 0, 0, 0))],
        scratch_shapes=[pltpu.VMEM((2, pairs, pw, pw), F32)],
    )
    return pl.pallas_call(
        functools.partial(_rwkv_pair_kernel, RWKV_CHUNKS),
        grid_spec=grid_spec,
        out_shape=[jax.ShapeDtypeStruct((n, D_MIX), F32), jax.ShapeDtypeStruct((n, D_MIX), F32),
                   jax.ShapeDtypeStruct((n_seq, 2, HEADS, HD, HD), F32)],
        compiler_params=_cparams(("arbitrary",), VMEM_LIMIT),
        name="rwkv_scan",
    )(jnp.asarray(sched), *args)


def rwkv_pack_state(s):
    n = s.shape[0]
    st = jnp.swapaxes(s, -1, -2).reshape(n, 2, HEADS // 2, 2, HD, HD)
    eye2 = jnp.eye(2, dtype=s.dtype)
    return jnp.einsum('ndpakv,ab->ndpakbv', st, eye2).reshape(n, 2, HEADS // 2, 2 * HD, 2 * HD)


def _ctx_attn_kernel(q_ref, k_ref, v_ref, y_ref, ko_ref, vo_ref):
    scale = HD ** -0.5
    for h in range(HEADS):
        sl = slice(h * HD, (h + 1) * HD)
        q = q_ref[:, sl]
        k = k_ref[:, sl]
        v = v_ref[:, sl]
        ko_ref[0, h] = k
        vo_ref[0, h] = v
        s = _dot_nt(q.astype(BF16), k.astype(BF16)) * scale
        m = jnp.max(s, axis=-1, keepdims=True)
        e = jnp.exp(s - m)
        p = e / jnp.sum(e, axis=-1, keepdims=True)
        y_ref[:, sl] = _dot(p.astype(BF16), v.astype(BF16))


def ctx_attention(z, bsz, t_len):
    qb = A_PAD // D_MIX
    return pl.pallas_call(
        _ctx_attn_kernel,
        grid=(bsz,),
        in_specs=[pl.BlockSpec((t_len, D_MIX), lambda b: (b, qb)),
                  pl.BlockSpec((t_len, D_MIX), lambda b: (b, qb + 1)),
                  pl.BlockSpec((t_len, D_MIX), lambda b: (b, qb + 2))],
        out_specs=[pl.BlockSpec((t_len, D_MIX), lambda b: (b, 0)),
                   pl.BlockSpec((1, HEADS, t_len, HD), lambda b: (b, 0, 0, 0)),
                   pl.BlockSpec((1, HEADS, t_len, HD), lambda b: (b, 0, 0, 0))],
        out_shape=[jax.ShapeDtypeStruct((bsz * t_len, D_MIX), F32),
                   jax.ShapeDtypeStruct((bsz, HEADS, t_len, HD), F32),
                   jax.ShapeDtypeStruct((bsz, HEADS, t_len, HD), F32)],
        compiler_params=_cparams(("arbitrary",)),
        name="ctx_attention",
    )(z, z, z)


def _na_kernel(rows, nblk, *refs):
    q_ref = refs[0]
    k_refs = refs[1:1 + nblk]
    v_refs = refs[1 + nblk:1 + 2 * nblk]
    kc_ref, vc_ref, tab_ref, y_ref = refs[1 + 2 * nblk:]
    r0 = pl.program_id(1) * NA_QROWS
    u0 = jnp.clip(r0 - WIN_R // 2, 0, rows - NA_KROWS)
    scale = HD ** -0.5
    kwin = jnp.concatenate([kr[...] for kr in k_refs], axis=0).astype(BF16)
    vwin = jnp.concatenate([vr[...] for vr in v_refs], axis=0).astype(BF16)
    left = lax.broadcasted_iota(jnp.int32, (1, 2 * GRID_W), 1) < GRID_W
    bias = []
    for i in range(NA_QROWS):
        r = r0 + i
        rs = jnp.clip(r - WIN_R // 2, 0, rows - WIN_R)
        per_pair = []
        for jp in range(NA_KROWS // 2):
            kr = u0 + 2 * jp
            off = [jnp.where(jnp.logical_and(kr + e >= rs, kr + e < rs + WIN_R), 0.0, NEG) for e in range(2)]
            per_pair.append((jnp.clip(kr - r + WIN_R, 0, 2 * WIN_R - 1), jnp.where(left, off[0], off[1])))
        bias.append(per_pair)
    for h in range(HEADS):
        sl = slice(h * HD, (h + 1) * HD)
        q = q_ref[:, sl].astype(BF16)
        s_raw = _dot_nt(q, kwin[:, sl]) * scale
        s_loc = jnp.concatenate([
            jnp.concatenate([s_raw[i * GRID_W:(i + 1) * GRID_W, jp * 2 * GRID_W:(jp + 1) * 2 * GRID_W]
                             + tab_ref[h, bias[i][jp][0]] + bias[i][jp][1]
                             for jp in range(NA_KROWS // 2)], axis=1)
            for i in range(NA_QROWS)], axis=0)
        s_ctx = _dot_nt(q, kc_ref[0, 0, h].astype(BF16)) * scale
        m = jnp.maximum(jnp.max(s_loc, axis=-1, keepdims=True), jnp.max(s_ctx, axis=-1, keepdims=True))
        e_loc = jnp.exp(s_loc - m)
        e_ctx = jnp.exp(s_ctx - m)
        den = jnp.sum(e_loc, axis=-1, keepdims=True) + jnp.sum(e_ctx, axis=-1, keepdims=True)
        p_loc = (e_loc / den).astype(BF16)
        p_ctx = (e_ctx / den).astype(BF16)
        y_ref[:, sl] = _dot(p_loc, vwin[:, sl]) + _dot(p_ctx, vc_ref[0, 0, h].astype(BF16))


def na_bias_table(rpb):
    cq = np.arange(GRID_W)[:, None]
    ck = np.arange(GRID_W)[None, :]
    cs = np.clip(cq - WIN_C // 2, 0, GRID_W - WIN_C)
    col_bias = np.where((ck >= cs) & (ck < cs + WIN_C), 0.0, NEG).astype(np.float32)
    col_idx = np.clip(ck - cq + WIN_C - 1, 0, 2 * WIN_C - 2)
    rpb_col = rpb.astype(F32)[:, :, col_idx] + col_bias
    padded = jnp.pad(rpb_col, ((0, 0), (1, 1), (0, 0), (0, 0)), constant_values=NEG)
    return jnp.concatenate([padded[:, :-1], padded[:, 1:]], axis=-1)


def na_attention(z, k_ctx, v_ctx, layer, bias_tab, bsz, t_len):
    rows = t_len // GRID_W
    assert rows >= NA_KROWS and rows % NA_QROWS == 0, "latent grid too small for the row-group tiling"
    qb = A_PAD // D_MIX
    past = k_ctx.shape[3]
    blk_rows = WIN_R // 2
    nblk = NA_KROWS // blk_rows
    blk_tok = blk_rows * GRID_W
    q_tok = NA_QROWS * GRID_W

    def win_spec(j, col):
        def index(b, g):
            u0 = jnp.clip(g * NA_QROWS - WIN_R // 2, 0, rows - NA_KROWS)
            return (b * (rows // blk_rows) + u0 // blk_rows + j, col)
        return pl.BlockSpec((blk_tok, D_MIX), index)

    in_specs = ([pl.BlockSpec((q_tok, D_MIX), lambda b, g: (b * (rows // NA_QROWS) + g, qb))]
                + [win_spec(j, qb + 1) for j in range(nblk)]
                + [win_spec(j, qb + 2) for j in range(nblk)]
                + [pl.BlockSpec((1, 1, HEADS, past, HD), lambda b, g: (b, layer, 0, 0, 0)),
                   pl.BlockSpec((1, 1, HEADS, past, HD), lambda b, g: (b, layer, 0, 0, 0)),
                   pl.BlockSpec(bias_tab.shape, lambda b, g: (0, 0, 0, 0))])
    return pl.pallas_call(
        functools.partial(_na_kernel, rows, nblk),
        grid=(bsz, rows // NA_QROWS),
        in_specs=in_specs,
        out_specs=pl.BlockSpec((q_tok, D_MIX), lambda b, g: (b * (rows // NA_QROWS) + g, 0)),
        out_shape=jax.ShapeDtypeStruct((bsz * t_len, D_MIX), F32),
        compiler_params=_cparams(("arbitrary", "arbitrary"), VMEM_LIMIT),
        name="na_attention",
    )(*([z] * (1 + 2 * nblk)), k_ctx, v_ctx, bias_tab)


def _s5_kernel(bsz, tb, uf_ref, ub_ref, bm_ref, cm_ref, ar_ref, ai_ref, x0_ref, yf_ref, yb_ref, xf_ref,
               x_scr, xo_scr, carry_scr):
    i = pl.program_id(1)

    @pl.when(i == 0)
    def _():
        carry_scr[...] = x0_ref[0]

    n_re = S5_GW // LANE
    n_rows = S5_GB * bsz
    assert n_rows == SUBLANE
    blk = bsz * tb
    group = min(n_re, S5_CARRY_VREGS // 2)
    ch = D_MIX // S5_GB
    lane = lambda c: slice(c * LANE, (c + 1) * LANE)
    steps_of = lambda row: pl.ds(row, tb, stride=n_rows)
    rows_of = lambda t: pl.ds(pl.multiple_of(t * n_rows, SUBLANE), n_rows)
    for d, (u_ref, y_ref) in enumerate(((uf_ref, yf_ref), (ub_ref, yb_ref))):
        u = u_ref[...].reshape(blk, D_MIX).astype(BF16)
        for k in range(S5_GB):
            bu = _dot(u[:, k * ch:(k + 1) * ch], bm_ref[d, k])
            for c in range(2 * n_re):
                for b in range(bsz):
                    x_scr[c, steps_of(k * bsz + b), :] = bu[b * tb:(b + 1) * tb, lane(c)]
        for c0 in range(0, n_re, group):
            tiles = list(range(c0, c0 + group))
            ar = [ar_ref[d, :, lane(c)] for c in tiles]
            ai = [ai_ref[d, :, lane(c)] for c in tiles]

            def step(s, x, d=d, tiles=tiles, ar=ar, ai=ai):
                t = s if d == 0 else tb - 1 - s
                rows = rows_of(t)
                out = []
                for c, a_r, a_i, (xr, xi) in zip(tiles, ar, ai, x):
                    nr = a_r * xr - a_i * xi + x_scr[c, rows, :]
                    ni = a_r * xi + a_i * xr + x_scr[n_re + c, rows, :]
                    xo_scr[c, rows, :] = nr
                    xo_scr[n_re + c, rows, :] = ni
                    out.append((nr, ni))
                return tuple(out)

            init = tuple((carry_scr[d, :, lane(c)], carry_scr[d, :, lane(n_re + c)]) for c in tiles)
            fin = lax.fori_loop(0, tb, step, init, unroll=4)
            for c, (xr, xi) in zip(tiles, fin):
                carry_scr[d, :, lane(c)] = xr
                carry_scr[d, :, lane(n_re + c)] = xi
        ys = []
        for k in range(S5_GB):
            xs = jnp.concatenate(
                [jnp.concatenate([xo_scr[c, steps_of(k * bsz + b), :] for b in range(bsz)], axis=0)
                 for c in range(2 * n_re)], axis=-1)
            ys.append(_dot(xs.astype(BF16), cm_ref[k]))
        y_ref[...] = jnp.concatenate(ys, axis=-1).reshape(bsz, tb, D_MIX)

    @pl.when(i == pl.num_programs(1) - 1)
    def _():
        xf_ref[0] = carry_scr[...]


def s5_scan(z3, bmat, cmat, a_re, a_im, x0, tb, grp):
    bsz, t_len, _ = z3.shape
    n_t = t_len // tb
    n_rows = S5_GB * grp
    ucol = (A_PAD + 3 * D_MIX) // D_MIX
    full = lambda shape: pl.BlockSpec(shape, lambda g, i: (0,) * len(shape))
    state_spec = pl.BlockSpec((1, 2, n_rows, 2 * S5_GW), lambda g, i: (g, 0, 0, 0))
    y_shape = jax.ShapeDtypeStruct((bsz, t_len, D_MIX), F32)
    return pl.pallas_call(
        functools.partial(_s5_kernel, grp, tb),
        grid=(bsz // grp, n_t),
        in_specs=[pl.BlockSpec((grp, tb, D_MIX), lambda g, i: (g, i, ucol)),
                  pl.BlockSpec((grp, tb, D_MIX), lambda g, i: (g, n_t - 1 - i, ucol)),
                  full((2, S5_GB, D_MIX // S5_GB, 2 * S5_GW)), full((S5_GB, 2 * S5_GW, D_MIX // S5_GB)),
                  full((2, n_rows, S5_GW)), full((2, n_rows, S5_GW)), state_spec],
        out_specs=[pl.BlockSpec((grp, tb, D_MIX), lambda g, i: (g, i, 0)),
                   pl.BlockSpec((grp, tb, D_MIX), lambda g, i: (g, n_t - 1 - i, 0)),
                   state_spec],
        out_shape=[y_shape, y_shape, jax.ShapeDtypeStruct((bsz // grp, 2, n_rows, 2 * S5_GW), F32)],
        scratch_shapes=[pltpu.VMEM((2 * S5_GW // LANE, n_rows * tb, LANE), F32),
                        pltpu.VMEM((2 * S5_GW // LANE, n_rows * tb, LANE), F32),
                        pltpu.VMEM((2, n_rows, 2 * S5_GW), F32)],
        compiler_params=_cparams(("arbitrary", "arbitrary"), VMEM_LIMIT),
        name="s5_scan",
    )(z3, z3, bmat, cmat, a_re, a_im, x0)


def s5_rows(x, grp):
    n_g = x.shape[1] // grp
    return (x.reshape(2, n_g, grp, S5_GB, S5_GW).transpose(1, 0, 3, 2, 4)
            .reshape(n_g, 2, S5_GB * grp, S5_GW))


def s5_unrows(x, grp):
    n_g = x.shape[0]
    return x.reshape(n_g, 2, S5_GB, grp, S5_GW).transpose(1, 0, 3, 2, 4).reshape(2, n_g * grp, S5_N)


def _merge_kernel(gnf_ref, gnb_ref, bonus_ref, g_ref, yb_ref, ycf_ref, ycb_ref, u_ref, zg0_ref, zg1_ref, zg2_ref,
                  x_ref, mod_ref, s5d_ref, wglu_ref, bglu_ref, wb_ref, wout_ref, lng_ref, lnb_ref, x1_ref, h2_ref):
    ya = (gnf_ref[...] + gnb_ref[...] + bonus_ref[...]) * g_ref[...]
    yc = ycf_ref[...] + ycb_ref[...] + s5d_ref[...] * u_ref[...]
    yc = 0.5 * yc * (1.0 + jnp.tanh(math.sqrt(2.0 / math.pi) * (yc + 0.044715 * (yc * yc * yc))))
    yc = yc * _sigmoid(_dot(yc.astype(BF16), wglu_ref[...]) + bglu_ref[...])
    merged = (_dot(ya.astype(BF16), wb_ref[0]) * _sigmoid(zg0_ref[...])
              + _dot(yb_ref[...].astype(BF16), wb_ref[1]) * _sigmoid(zg1_ref[...])
              + _dot(yc.astype(BF16), wb_ref[2]) * _sigmoid(zg2_ref[...]))
    mo = _dot(merged.astype(BF16), wout_ref[...])
    m = mod_ref[0]
    x1 = _layer_norm(DN_ALPHA * x_ref[...] + m[2:3] * mo, lng_ref[...], lnb_ref[...])
    x1_ref[...] = x1
    h2_ref[...] = (x1 * (1.0 + m[4:5]) + m[3:4]).astype(BF16)


def merge_branches(gn, bonus, g, yb, yc, z, x, mods, mod_of_tile, p):
    n = x.shape[0]
    row = lambda w, col=0: pl.BlockSpec((TM, w), lambda i, col=col: (i, col))
    full = lambda shape: pl.BlockSpec(shape, lambda i: (0,) * len(shape))
    gb = (A_PAD + 4 * D_MIX) // D_MODEL
    return pl.pallas_call(
        _merge_kernel,
        grid=(n // TM,),
        in_specs=[row(D_MIX), row(D_MIX), row(D_MIX), row(D_MIX), row(D_MIX), row(D_MIX), row(D_MIX),
                  row(D_MIX, A_PAD // D_MIX + 3),
                  row(D_MODEL, gb), row(D_MODEL, gb + 1), row(D_MODEL, gb + 2),
                  row(D_MODEL),
                  pl.BlockSpec((1, SUBLANE, D_MODEL), lambda i: (mod_of_tile(i, TM), 0, 0)),
                  full((1, D_MIX)), full((D_MIX, D_MIX)), full((1, D_MIX)),
                  full((3, D_MIX, D_MODEL)), full((D_MODEL, D_MODEL)),
                  full((1, D_MODEL)), full((1, D_MODEL))],
        out_specs=[row(D_MODEL), row(D_MODEL)],
        out_shape=[jax.ShapeDtypeStruct((n, D_MODEL), F32), jax.ShapeDtypeStruct((n, D_MODEL), BF16)],
        compiler_params=_cparams(("arbitrary",), VMEM_LIMIT),
        name="merge_branches",
    )(gn[0], gn[1], bonus, g, yb, yc[0], yc[1], z, z, z, z, x, mods,
      p['s5_d'], p['w_glu'], p['b_glu'], p['w_branch'], p['w_out'], p['ln1_g'], p['ln1_b'])


def _first_max(val, idx, big):
    m = jnp.max(jnp.max(val, axis=1, keepdims=True), axis=0, keepdims=True)
    cand = jnp.where(val == m, idx, big)
    first = jnp.min(jnp.min(cand, axis=1, keepdims=True), axis=0, keepdims=True)
    return m, idx == first


def _router_kernel(h_ref, wt_ref, bias_ref, gates_ref):
    per = N_EXPERTS // N_GROUPS
    logits = _dot_nt(wt_ref[...], h_ref[...])
    n = logits.shape[1]
    scores = _sigmoid(logits).reshape(N_GROUPS, per, n)
    sel = scores + bias_ref[...]
    e_idx = (lax.broadcasted_iota(jnp.int32, (N_GROUPS, per, n), 0) * per
             + lax.broadcasted_iota(jnp.int32, (N_GROUPS, per, n), 1))
    in_grp = lax.broadcasted_iota(jnp.int32, (N_GROUPS, per, n), 1)
    m1 = jnp.max(sel, axis=1, keepdims=True)
    first = jnp.min(jnp.where(sel == m1, in_grp, per), axis=1, keepdims=True)
    m2 = jnp.max(jnp.where(in_grp == first, -jnp.inf, sel), axis=1, keepdims=True)
    grp = m1 + m2
    g_idx = lax.broadcasted_iota(jnp.int32, (N_GROUPS, 1, n), 0)
    gmask = jnp.zeros((N_GROUPS, 1, n), F32)
    for _ in range(TOPK_GROUPS):
        _, hit = _first_max(grp, g_idx, N_GROUPS)
        gmask = jnp.where(hit, 1.0, gmask)
        grp = jnp.where(hit, -jnp.inf, grp)
    cur = jnp.where(jnp.broadcast_to(gmask, sel.shape) > 0.0, sel, NEG)
    w = jnp.zeros((N_GROUPS, per, n), F32)
    for _ in range(TOP_K):
        _, hit = _first_max(cur, e_idx, N_EXPERTS)
        w = jnp.where(hit, scores, w)
        cur = jnp.where(hit, -jnp.inf, cur)
    tot = jnp.sum(jnp.sum(w, axis=1, keepdims=True), axis=0, keepdims=True)
    gates = (ROUTED_SCALE * w / tot).reshape(N_EXPERTS, n)
    hi = gates.astype(BF16).astype(F32)
    lo = (gates - hi).astype(BF16).astype(F32)
    gates_ref[...] = jnp.concatenate([hi, lo], axis=0).T.astype(BF16)


def moe_router(h2, router_wt, router_bias):
    n = h2.shape[0]
    return pl.pallas_call(
        _router_kernel,
        grid=(n // TM,),
        in_specs=[pl.BlockSpec((TM, D_MODEL), lambda i: (i, 0)),
                  pl.BlockSpec((N_EXPERTS, D_MODEL), lambda i: (0, 0)),
                  pl.BlockSpec((N_GROUPS, N_EXPERTS // N_GROUPS, 1), lambda i: (0, 0, 0))],
        out_specs=pl.BlockSpec((TM, 2 * N_EXPERTS), lambda i: (i, 0)),
        out_shape=jax.ShapeDtypeStruct((n, 2 * N_EXPERTS), BF16),
        compiler_params=_cparams(("arbitrary",)),
        name="moe_router",
    )(h2, router_wt, router_bias)


def _moe_kernel(ec, h_ref, gates_ref, x1_ref, mod_ref, wg_ref, wu_ref, wd_ref, ex_ref,
                sg_ref, su_ref, sd_ref, lng_ref, lnb_ref, out_ref, acc_scr):
    j = pl.program_id(1)
    h = h_ref[...]

    @pl.when(j == 0)
    def _():
        sh = _dot(h, sg_ref[...])
        sh = sh * _sigmoid(sh) * _dot(h, su_ref[...])
        acc_scr[...] = _dot(sh.astype(BF16), sd_ref[...])

    gexp = _dot(gates_ref[...], ex_ref[0])
    wg = jnp.concatenate([wg_ref[e].astype(BF16) for e in range(ec)], axis=1)
    wu = jnp.concatenate([wu_ref[e].astype(BF16) for e in range(ec)], axis=1)
    hg = _dot(h, wg)
    hu = _dot(h, wu)
    act = hg * _sigmoid(hg) * hu * gexp
    acc_scr[...] += _dot(act.astype(BF16), wd_ref[...].astype(BF16))

    @pl.when(j == pl.num_programs(1) - 1)
    def _():
        m = mod_ref[0]
        out_ref[...] = _layer_norm(DN_ALPHA * x1_ref[...] + m[5:6] * acc_scr[...], lng_ref[...], lnb_ref[...])


def moe_ffn(h2, gates, x1, mods, mod_of_tile, p, tm=1024, ec=MOE_EC):
    n = h2.shape[0]
    wcols = ec * D_EXPERT
    n_j = N_EXPERTS // ec
    full = lambda shape: pl.BlockSpec(shape, lambda i, j: (0,) * len(shape))
    return pl.pallas_call(
        functools.partial(_moe_kernel, ec),
        grid=(n // tm, n_j),
        in_specs=[pl.BlockSpec((tm, D_MODEL), lambda i, j: (i, 0)),
                  pl.BlockSpec((tm, 2 * N_EXPERTS), lambda i, j: (i, 0)),
                  pl.BlockSpec((tm, D_MODEL), lambda i, j: (i, 0)),
                  pl.BlockSpec((1, SUBLANE, D_MODEL), lambda i, j: (mod_of_tile(i, tm), 0, 0)),
                  pl.BlockSpec((ec, D_MODEL, D_EXPERT), lambda i, j: (j, 0, 0)),
                  pl.BlockSpec((ec, D_MODEL, D_EXPERT), lambda i, j: (j, 0, 0)),
                  pl.BlockSpec((wcols, D_MODEL), lambda i, j: (j, 0)),
                  pl.BlockSpec((1, 2 * N_EXPERTS, wcols), lambda i, j: (j, 0, 0)),
                  full((D_MODEL, D_SHARED)), full((D_MODEL, D_SHARED)), full((D_SHARED, D_MODEL)),
                  full((1, D_MODEL)), full((1, D_MODEL))],
        out_specs=pl.BlockSpec((tm, D_MODEL), lambda i, j: (i, 0)),
        out_shape=jax.ShapeDtypeStruct((n, D_MODEL), F32),
        scratch_shapes=[pltpu.VMEM((tm, D_MODEL), F32)],
        compiler_params=_cparams(("arbitrary", "arbitrary"), VMEM_LIMIT),
        name="moe_ffn",
    )(h2, gates, x1, mods, p['wg'], p['wu'], p['wd'], p['expand'],
      p['sh_g'], p['sh_u'], p['sh_d'], p['ln2_g'], p['ln2_b'])


def _block_diag2(m):
    z = jnp.zeros_like(m[0])
    return jnp.concatenate([jnp.concatenate([m[0], z], axis=1), jnp.concatenate([z, m[1]], axis=1)], axis=0)


def _layer_params(P, l):
    g = lambda name: P[name][l]
    w_in = g('w_in')
    pad = jnp.zeros((D_MODEL, A_PAD - A_COLS), F32)
    p = {}
    p['w_in'] = jnp.concatenate([w_in[:, :A_COLS], pad, w_in[:, A_COLS:]], axis=1).astype(BF16)
    p['mu'] = jnp.concatenate([g('rwkv_mu'), jnp.zeros((A_PAD - A_COLS,), F32)])[None, :]
    p['w2bd'] = _block_diag2(g('rwkv_w2')).astype(BF16)
    p['a2bd'] = _block_diag2(g('rwkv_a2')).astype(BF16)
    p['g2'] = g('rwkv_g2').astype(BF16)
    p['w0'] = g('rwkv_w0').reshape(1, 2 * D_MIX)
    p['a0'] = g('rwkv_a0').reshape(1, 2 * D_MIX)
    p['kkp'] = g('rwkv_kk')[None, :]
    p['ka'] = g('rwkv_ka')[None, :]
    p['rk'] = g('rwkv_rk')[None, :]
    hid = np.arange(D_MIX) // HD
    p['seg_ones'] = jnp.asarray((hid[:, None] == hid[None, :]).astype(np.float32), dtype=BF16)
    p['gng'] = g('rwkv_gn_g').reshape(HEADS // 2, 1, 2 * HD)
    p['gnb'] = g('rwkv_gn_b').reshape(HEADS // 2, 1, 2 * HD)
    a = lax.complex(g('s5_a_re'), g('s5_a_im'))
    dt = jnp.exp(g('s5_log_dt'))[..., None]
    a_bar = jnp.exp(dt * a)
    b_bar = ((a_bar - 1.0) / a)[..., None] * lax.complex(g('s5_b_re'), g('s5_b_im'))
    eye_g = jnp.eye(C_GROUPS // S5_GB, dtype=F32)
    gpb = C_GROUPS // S5_GB
    bd = lambda m: jnp.einsum('dkgph,gj->dkghjp', m.reshape(2, S5_GB, gpb, C_STATE, C_GROUP),
                              eye_g).reshape(2, S5_GB, D_MIX // S5_GB, S5_GW)
    p['s5_bmat'] = jnp.concatenate([bd(b_bar.real), bd(b_bar.imag)], axis=-1).astype(BF16)
    cd = lambda m: jnp.einsum('kghp,gj->kjpgh', m.reshape(S5_GB, gpb, C_GROUP, C_STATE),
                              eye_g).reshape(S5_GB, S5_GW, D_MIX // S5_GB)
    p['s5_cmat'] = jnp.concatenate([cd(g('s5_c_re')), -cd(g('s5_c_im'))], axis=1).astype(BF16)
    p['s5_ar'] = a_bar.real.reshape(2, 1, S5_N)
    p['s5_ai'] = a_bar.imag.reshape(2, 1, S5_N)
    p['s5_d'] = g('s5_d')[None, :]
    p['w_glu'] = g('s5_w_glu').astype(BF16)
    p['b_glu'] = g('s5_b_glu')[None, :]
    p['w_branch'] = g('w_branch').astype(BF16)
    p['w_out'] = g('w_out').astype(BF16)
    p['ln1_g'] = g('ln1_g')[None, :]
    p['ln1_b'] = g('ln1_b')[None, :]
    p['router_wt'] = g('router_w').T.astype(BF16)
    p['router_bias'] = g('router_bias').reshape(N_GROUPS, N_EXPERTS // N_GROUPS, 1)
    ecols = N_EXPERTS * D_EXPERT
    p['wg'] = g('exp_w_gate')
    p['wu'] = g('exp_w_up')
    p['wd'] = g('exp_w_down').reshape(ecols, D_MODEL)
    p['sh_g'] = g('sh_w_gate').astype(BF16)
    p['sh_u'] = g('sh_w_up').astype(BF16)
    p['sh_d'] = g('sh_w_down').astype(BF16)
    p['ln2_g'] = g('ln2_g')[None, :]
    p['ln2_b'] = g('ln2_b')[None, :]
    return p


def _expand_table(ec):
    n_j = N_EXPERTS // ec
    t = np.zeros((n_j, 2 * N_EXPERTS, ec * D_EXPERT), np.float32)
    for e in range(N_EXPERTS):
        j, q = divmod(e, ec)
        t[j, e, q * D_EXPERT:(q + 1) * D_EXPERT] = 1.0
        t[j, N_EXPERTS + e, q * D_EXPERT:(q + 1) * D_EXPERT] = 1.0
    return jnp.asarray(t, dtype=BF16)


def kernel(x_prompt, x_sample, c, cache_na_k, cache_na_v, state_rwkv, state_s5_re, state_s5_im, c_ctx, w_ada, b_ada, w_in, rwkv_mu, rwkv_w0, rwkv_w2, rwkv_a0, rwkv_a2, rwkv_g2, rwkv_kk, rwkv_ka, rwkv_rk, rwkv_gn_g, rwkv_gn_b, na_rpb, s5_a_re, s5_a_im, s5_log_dt, s5_b_re, s5_b_im, s5_c_re, s5_c_im, s5_d, s5_w_glu, s5_b_glu, w_branch, w_out, ln1_g, ln1_b, router_w, router_bias, exp_w_gate, exp_w_up, exp_w_down, sh_w_gate, sh_w_up, sh_w_down, ln2_g, ln2_b):
    P = dict(w_in=w_in, rwkv_mu=rwkv_mu, rwkv_w0=rwkv_w0, rwkv_w2=rwkv_w2, rwkv_a0=rwkv_a0, rwkv_a2=rwkv_a2,
             rwkv_g2=rwkv_g2, rwkv_kk=rwkv_kk, rwkv_ka=rwkv_ka, rwkv_rk=rwkv_rk, rwkv_gn_g=rwkv_gn_g,
             rwkv_gn_b=rwkv_gn_b, s5_a_re=s5_a_re, s5_a_im=s5_a_im, s5_log_dt=s5_log_dt, s5_b_re=s5_b_re,
             s5_b_im=s5_b_im, s5_c_re=s5_c_re, s5_c_im=s5_c_im, s5_d=s5_d, s5_w_glu=s5_w_glu,
             s5_b_glu=s5_b_glu, w_branch=w_branch, w_out=w_out, ln1_g=ln1_g, ln1_b=ln1_b, router_w=router_w,
             router_bias=router_bias, exp_w_gate=exp_w_gate, exp_w_up=exp_w_up, exp_w_down=exp_w_down,
             sh_w_gate=sh_w_gate, sh_w_up=sh_w_up, sh_w_down=sh_w_down, ln2_g=ln2_g, ln2_b=ln2_b)
    bc, tc, _ = x_prompt.shape
    bl, tl, _ = x_sample.shape
    depth = w_in.shape[0]

    cond = jnp.concatenate([c_ctx[None, :], c, jnp.zeros((SUBLANE - 1 - bl, D_MODEL), F32)], axis=0)
    ada = ada_modulation(cond, w_ada, b_ada)
    masks = jnp.asarray(_chunk_masks())
    expand = _expand_table(MOE_EC)

    paths = {
        'ctx': dict(bsz=bc, t=tc, x=x_prompt.reshape(bc * tc, D_MODEL), mod_of_tile=lambda i, tm: 0),
        'lat': dict(bsz=bl, t=tl, x=x_sample.reshape(bl * tl, D_MODEL),
                    mod_of_tile=lambda i, tm: 1 + (i * tm) // tl),
    }
    for q in paths.values():
        tiles = q['t'] // TM
        pos = np.arange(q['bsz'] * tiles) % tiles
        q['seq_tiles'] = jnp.asarray(np.concatenate([np.full_like(pos, tiles), pos]).astype(np.int32))
        q['sched'] = rwkv_schedule([q['t']] * q['bsz'])
        q['s5_tb'] = min(S5_ROWS // S5_GRP, q['t'])
        assert q['bsz'] % S5_GRP == 0 and q['s5_tb'] % SUBLANE == 0 and q['t'] % q['s5_tb'] == 0

    new_k, new_v, new_rwkv, new_s5 = [], [], [], []
    for l in range(depth):
        p = _layer_params(P, l)
        p['expand'] = expand
        mods = ada[l, :1 + bl].reshape(1 + bl, 6, D_MODEL)
        mods = jnp.concatenate([mods, jnp.zeros((1 + bl, SUBLANE - 6, D_MODEL), F32)], axis=1)
        bias_tab = na_bias_table(na_rpb[l])
        for name, q in paths.items():
            bsz, t_len, x, mod_of_tile = q['bsz'], q['t'], q['x'], q['mod_of_tile']
            z = in_projection(x, mods, p['w_in'], mod_of_tile)

            r, v, kk, ld, kd, bd, g, bonus = rwkv_prep(z, q['seq_tiles'], p)
            if name == 'ctx':
                s0t = jnp.zeros((bsz, 2, HEADS // 2, 2 * HD, 2 * HD), F32)
            else:
                s0t = rwkv_pack_state(state_rwkv[:, l])
            gn_f, gn_b, s_fin = rwkv_scan((r, v, kk, ld, kd, bd), masks, p['gng'], p['gnb'], s0t, q['sched'])

            if name == 'ctx':
                yb, k_h, v_h = ctx_attention(z, bsz, t_len)
            else:
                yb = na_attention(z, cache_na_k, cache_na_v, l, bias_tab, bsz, t_len)

            if name == 'ctx':
                x0 = jnp.zeros((bsz // S5_GRP, 2, S5_GB * S5_GRP, 2 * S5_GW), F32)
            else:
                x0 = jnp.concatenate(
                    [s5_rows(jnp.swapaxes(s[:, l].reshape(bsz, 2, S5_N), 0, 1), S5_GRP)
                     for s in (state_s5_re, state_s5_im)], axis=-1)
            a_re, a_im = (s5_rows(jnp.broadcast_to(a, (2, S5_GRP, S5_N)), S5_GRP)[0]
                          for a in (p['s5_ar'], p['s5_ai']))
            yc_f, yc_b, x_fin = s5_scan(z.reshape(bsz, t_len, IN_COLS_P), p['s5_bmat'], p['s5_cmat'],
                                        a_re, a_im, x0, q['s5_tb'], S5_GRP)
            yc = (yc_f.reshape(bsz * t_len, D_MIX), yc_b.reshape(bsz * t_len, D_MIX))

            x1, h2 = merge_branches((gn_f, gn_b), bonus, g, yb, yc, z, x, mods, mod_of_tile, p)
            gates = moe_router(h2, p['router_wt'], p['router_bias'])
            q['x'] = moe_ffn(h2, gates, x1, mods, mod_of_tile, p)
            if name == 'ctx':
                new_k.append(k_h)
                new_v.append(v_h)
                new_rwkv.append(jnp.swapaxes(s_fin, -1, -2))
                new_s5.append([jnp.swapaxes(s5_unrows(part, S5_GRP), 0, 1)
                               for part in (x_fin[..., :S5_GW], x_fin[..., S5_GW:])])

    s5_re, s5_im = (jnp.stack([layer[part] for layer in new_s5], axis=1).reshape(bc, depth, 2, C_GROUPS, C_STATE)
                    for part in range(2))
    return (paths['ctx']['x'].reshape(bc, tc, D_MODEL), paths['lat']['x'].reshape(bl, tl, D_MODEL),
            jnp.stack(new_k, axis=1), jnp.stack(new_v, axis=1), jnp.stack(new_rwkv, axis=1), s5_re, s5_im)
```

```python
import functools
import math

import numpy as np
import jax
import jax.numpy as jnp
from jax import lax
from jax.experimental import pallas as pl
from jax.experimental.pallas import tpu as pltpu

F32 = jnp.float32
BF16 = jnp.bfloat16

D_MODEL = 1024
DEPTH = 2
GRID_W = 64
D_MIX = 512
HEADS = 8
HD = 64
LORA = 128
WIN_R = 8
WIN_C = 16
C_GROUP = 16
C_GROUPS = D_MIX // C_GROUP
C_STATE = 64
S5_N = C_GROUPS * C_STATE
N_EXPERTS = 64
TOP_K = 8
N_GROUPS = 8
TOPK_GROUPS = 4
D_EXPERT = 128
D_SHARED = 128
ROUTED_SCALE = 2.5
LN_EPS = 1e-5
GN_EPS = 64e-5
NEG = -1e30
DN_ALPHA = (2 * DEPTH) ** 0.25
A_COLS = 3 * D_MIX + 3 * LORA
A_PAD = 2048
IN_COLS_P = A_PAD + 4 * D_MIX + 3 * D_MODEL

LANE = 128
SUBLANE = 8
TM = 256
CH = HD
RWKV_CHUNKS = 4
MOE_EC = 8
NA_QROWS = 8
NA_KROWS = NA_QROWS + WIN_R
S5_GB = 4
S5_GW = S5_N // S5_GB
S5_GRP = SUBLANE // S5_GB
S5_ROWS = 512
S5_CARRY_VREGS = 32
VMEM_LIMIT = 56 * 1024 * 1024


def _cparams(sem, vmem=None):
    return pltpu.CompilerParams(dimension_semantics=sem, vmem_limit_bytes=vmem)


def _sigmoid(x):
    return 1.0 / (1.0 + jnp.exp(-x))


def _dot(a, b, precision=None):
    return jnp.dot(a, b, preferred_element_type=F32, precision=precision)


def _dot_nt(a, b, precision=None):
    return lax.dot_general(a, b, (((1,), (1,)), ((), ())), preferred_element_type=F32, precision=precision)


def _split2(x):
    hi = x.astype(BF16)
    return hi, (x - hi.astype(F32)).astype(BF16)


def _split3(x):
    hi = x.astype(BF16)
    r1 = x - hi.astype(F32)
    mid = r1.astype(BF16)
    return hi, mid, (r1 - mid.astype(F32)).astype(BF16)


def _seg_sum(x, ones_bf16):
    return sum(_dot(part, ones_bf16) for part in _split3(x))


def _mm3(a, b, dot=_dot):
    return dot(a[0], b[0]) + (dot(a[1], b[0]) + dot(a[0], b[1]))


def _layer_norm(x, g, b):
    mu = jnp.mean(x, axis=-1, keepdims=True)
    xc = x - mu
    var = jnp.mean(xc * xc, axis=-1, keepdims=True)
    return xc * lax.rsqrt(var + LN_EPS) * g + b


def _ada_kernel(c_ref, w_ref, b_ref, o_ref):
    c = c_ref[...]
    s = c * _sigmoid(c)
    o_ref[0] = _dot(s.astype(BF16), w_ref[0].astype(BF16)) + b_ref[0]


def ada_modulation(cond, w_ada, b_ada):
    n_l, d, n6 = w_ada.shape
    tn = 1536
    return pl.pallas_call(
        _ada_kernel,
        grid=(n_l, n6 // tn),
        in_specs=[pl.BlockSpec((SUBLANE, d), lambda l, j: (0, 0)),
                  pl.BlockSpec((1, d, tn), lambda l, j: (l, 0, j)),
                  pl.BlockSpec((1, 1, tn), lambda l, j: (l, 0, j))],
        out_specs=pl.BlockSpec((1, SUBLANE, tn), lambda l, j: (l, 0, j)),
        out_shape=jax.ShapeDtypeStruct((n_l, SUBLANE, n6), F32),
        compiler_params=_cparams(("arbitrary", "arbitrary")),
        name="ada_modulation",
    )(cond, w_ada, b_ada.reshape(n_l, 1, n6))


def _inproj_kernel(x_ref, mod_ref, w_ref, z_ref, h_scr):
    @pl.when(pl.program_id(1) == 0)
    def _():
        m = mod_ref[0]
        h_scr[...] = (x_ref[...] * (1.0 + m[1:2]) + m[0:1]).astype(BF16)
    z_ref[...] = _dot(h_scr[...], w_ref[...])


def in_projection(x, mods, w_in_p, mod_of_tile, tm=2048, tn=1024):
    n, d = x.shape
    cols = w_in_p.shape[1]
    return pl.pallas_call(
        _inproj_kernel,
        grid=(n // tm, cols // tn),
        in_specs=[pl.BlockSpec((tm, d), lambda i, j: (i, 0)),
                  pl.BlockSpec((1, SUBLANE, d), lambda i, j: (mod_of_tile(i, tm), 0, 0)),
                  pl.BlockSpec((d, tn), lambda i, j: (0, j))],
        out_specs=pl.BlockSpec((tm, tn), lambda i, j: (i, j)),
        out_shape=jax.ShapeDtypeStruct((n, cols), F32),
        scratch_shapes=[pltpu.VMEM((tm, d), BF16)],
        compiler_params=_cparams(("arbitrary", "arbitrary"), VMEM_LIMIT),
        name="in_projection",
    )(x, mods, w_in_p)


def _rwkv_prep_kernel(seq_tiles_ref, z_ref, zp_ref, zn_ref, mu_ref, w2_ref, a2_ref, g2_ref, w0_ref, a0_ref,
                      kkp_ref, ka_ref, rk_ref, e_ref,
                      r_ref, v_ref, kk_ref, ld_ref, kd_ref, bd_ref, g_ref, bonus_ref):
    i = pl.program_id(0)
    tiles = seq_tiles_ref[i]
    pos = seq_tiles_ref[i + pl.num_programs(0)]
    x = z_ref[...]
    tm = x.shape[0]
    rows = lax.broadcasted_iota(jnp.int32, x.shape, 0)
    prev_row = jnp.where(pos == 0, 0.0, zp_ref[SUBLANE - 1:SUBLANE, :])
    next_row = jnp.where(pos == tiles - 1, 0.0, zn_ref[0:1, :])
    xm1 = jnp.where(rows == 0, prev_row, pltpu.roll(x, 1, axis=0))
    xp1 = jnp.where(rows == tm - 1, next_row, pltpu.roll(x, tm - 1, axis=0))
    za = x + mu_ref[...] * (0.5 * (xm1 + xp1) - x)

    r = za[:, 0:D_MIX]
    k = za[:, D_MIX:2 * D_MIX]
    v = za[:, 2 * D_MIX:3 * D_MIX]
    lw = za[:, 3 * D_MIX:3 * D_MIX + LORA]
    la = za[:, 3 * D_MIX + LORA:3 * D_MIX + 2 * LORA]
    lg = za[:, 3 * D_MIX + 2 * LORA:3 * D_MIX + 3 * LORA]

    w_both = w0_ref[...] + _dot(jnp.tanh(lw).astype(BF16), w2_ref[...])
    a_both = _sigmoid(a0_ref[...] + _dot(la.astype(BF16), a2_ref[...]))
    g_ref[...] = _dot(_sigmoid(lg).astype(BF16), g2_ref[...])

    e = e_ref[...]
    kks = k * kkp_ref[...]
    nrm = jnp.sqrt(_seg_sum(kks * kks, e))
    kk = kks / jnp.maximum(nrm, 1e-12)
    bonus = jnp.zeros_like(v)
    r_ref[...] = r
    v_ref[...] = v
    kk_ref[...] = kk
    for d in range(2):
        w = w_both[:, d * D_MIX:(d + 1) * D_MIX]
        a = a_both[:, d * D_MIX:(d + 1) * D_MIX]
        ld = -math.exp(-0.5) * _sigmoid(w)
        kd = k * (1.0 + (a - 1.0) * ka_ref[...])
        bd = kk * a
        bonus = bonus + _seg_sum(r * kd * rk_ref[...], e) * v
        ld_ref[d] = ld
        kd_ref[d] = kd
        bd_ref[d] = bd
    bonus_ref[...] = bonus


def rwkv_prep(z, seq_tiles, p):
    n = z.shape[0]
    nt = n // TM
    halo = TM // SUBLANE
    nb8 = n // SUBLANE
    tok = jax.ShapeDtypeStruct((n, D_MIX), F32)
    tok2 = jax.ShapeDtypeStruct((2, n, D_MIX), F32)
    full = lambda shape: pl.BlockSpec(shape, lambda i, s: (0,) * len(shape))
    tok_spec = pl.BlockSpec((TM, D_MIX), lambda i, s: (i, 0))
    tok2_spec = pl.BlockSpec((2, TM, D_MIX), lambda i, s: (0, i, 0))
    grid_spec = pltpu.PrefetchScalarGridSpec(
        num_scalar_prefetch=1,
        grid=(nt,),
        in_specs=[pl.BlockSpec((TM, A_PAD), lambda i, s: (i, 0)),
                  pl.BlockSpec((SUBLANE, A_PAD), lambda i, s: (jnp.maximum(i * halo - 1, 0), 0)),
                  pl.BlockSpec((SUBLANE, A_PAD), lambda i, s: (jnp.minimum((i + 1) * halo, nb8 - 1), 0)),
                  full((1, A_PAD)), full((LORA, 2 * D_MIX)), full((LORA, 2 * D_MIX)), full((LORA, D_MIX)),
                  full((1, 2 * D_MIX)), full((1, 2 * D_MIX)), full((1, D_MIX)), full((1, D_MIX)),
                  full((1, D_MIX)), full((D_MIX, D_MIX))],
        out_specs=[tok_spec, tok_spec, tok_spec, tok2_spec, tok2_spec, tok2_spec, tok_spec, tok_spec],
    )
    return pl.pallas_call(
        _rwkv_prep_kernel,
        grid_spec=grid_spec,
        out_shape=[tok, tok, tok, tok2, tok2, tok2, tok, tok],
        compiler_params=_cparams(("arbitrary",), VMEM_LIMIT),
        name="rwkv_prep",
    )(seq_tiles, z, z, z, p['mu'], p['w2bd'], p['a2bd'], p['g2'], p['w0'], p['a0'],
      p['kkp'], p['ka'], p['rk'], p['seg_ones'])


def _chunk_masks():
    t = np.arange(CH)
    fwd_incl = (t[:, None] >= t[None, :])
    out = []
    for incl in (fwd_incl, fwd_incl.T):
        strict = incl & (t[:, None] != t[None, :])
        ms = [incl, strict, strict & ((t[:, None] // 8) == (t[None, :] // 8))]
        for m in (8, 16, 32):
            ms.append(strict & ((t[:, None] // (2 * m)) == (t[None, :] // (2 * m)))
                      & ((t[:, None] // m) != (t[None, :] // m)))
        out.append(np.stack(ms))
    masks = np.stack(out).astype(np.float32)
    return np.concatenate([masks, masks], axis=-1)


def _rwkv_pair_kernel(cb, sched_ref, *refs):
    dir_refs = (refs[0:6], refs[6:12])
    m_ref, gng_ref, gnb_ref, s0_ref = refs[12:16]
    y_refs = refs[16:18]
    sfin_ref, s_scr = refs[18:20]
    step_id = pl.program_id(0)

    @pl.when(sched_ref[_SCHED_FIRST, step_id] == 1)
    def _():
        s_scr[...] = s0_ref[0]

    pw = 2 * HD
    ri = lax.broadcasted_iota(jnp.int32, (pw, pw), 0)
    ci = lax.broadcasted_iota(jnp.int32, (pw, pw), 1)
    eye_bd = (ri == ci).astype(F32)
    mask_bd = ((ri // HD) == (ci // HD)).astype(F32)
    eye12 = (lax.broadcasted_iota(jnp.int32, (CH, pw), 0)
             == lax.broadcasted_iota(jnp.int32, (CH, pw), 1) % HD).astype(F32)
    left = lax.broadcasted_iota(jnp.int32, (1, pw), 1) < HD

    bf = lambda x: x.astype(BF16)
    each = lambda f, *cols: [f(*args) for args in zip(*cols)]
    rows = lambda j: slice(j * CH, (j + 1) * CH)
    lanes_of = lambda p: slice(p * pw, (p + 1) * pw)
    stack = lambda *xs: jnp.concatenate(xs, axis=0)
    side = lambda *xs: jnp.concatenate(xs, axis=1)
    top, mid = slice(0, CH), slice(CH, 2 * CH)
    zero = jnp.zeros((), BF16)
    bd = lambda x: stack(jnp.where(left, x, zero), jnp.where(left, zero, x))

    lanes = [(d, p) for d in range(2) for p in range(HEADS // 2)]
    chains = [(d, j, p) for d, p in lanes for j in range(cb)]
    msk = lambda k: [m_ref[d, k] for d, _, _ in chains]
    incl, strict, m8 = msk(0), msk(1), msk(2)
    incl_b = each(lambda m: bf(m[:, :CH]), incl)
    get = lambda k: [dir_refs[d][k][rows(j), lanes_of(p)] if k < 3 else dir_refs[d][k][0, rows(j), lanes_of(p)]
                     for d, j, p in chains]
    R, V, KK, LD, Kd, Bd = (get(k) for k in range(6))
    L = each(lambda m, x: sum(_dot(m, part) for part in _split3(x)), incl_b, LD)
    ltot = each(lambda x: jnp.sum(x, axis=0, keepdims=True), LD)
    e_nl = each(lambda l: jnp.exp(-l), L)
    e_rest = each(lambda l, lt: jnp.exp(lt - l), L, ltot)
    Qb = each(lambda kk, l, ld: bf(kk * jnp.exp(l - ld)), KK, L, LD)
    Rh = each(lambda r, l: r * jnp.exp(l), R, L)
    QRb = each(lambda q, rh: stack(q, bf(rh)), Qb, Rh)
    Btd = each(lambda b, e: bd(bf(b * e)), Bd, e_nl)
    Ktd = each(lambda k, e: bd(bf(k * e)), Kd, e_nl)
    BcTb = each(lambda b, e: bf((b * e).T), Bd, e_rest)
    KcTb = each(lambda k, e: bf((k * e).T), Kd, e_rest)
    Vb = each(bf, V)
    QRB = each(_dot_nt, QRb, Btd)
    QRK = each(_dot_nt, QRb, Ktd)
    Nl = each(lambda m, x: m * x[top], strict, QRB)
    Mrbb = each(lambda m, x: bf(m * x[mid]), incl, QRB)
    Mkb = each(lambda m, x: bf(m * x[top]), strict, QRK)
    Mrkb = each(lambda m, x: bf(m * x[mid]), incl, QRK)
    N8 = each(lambda m, n: m * n, m8, Nl)
    N8b = each(bf, N8)
    N2 = each(lambda a: _dot(a, bd(a)), N8b)
    N2b = each(bf, N2)
    N4 = each(lambda a: _dot(a, bd(a)), N2b)
    W = each(lambda a, b: _dot(bf(eye12 - a), bd(bf(eye12 + b))), N8, N2)
    W = each(lambda w, n4: _dot(bf(w), bd(bf(eye12 + n4))), W, N4)
    for lvl in range(3):
        Wb = each(bf, W)
        T = each(lambda m, n, w: _dot(bf(m * n), bd(w)), msk(3 + lvl), Nl, Wb)
        W = each(lambda w, wb, t: w - _dot(wb, bd(bf(t))), W, Wb, T)
    Wb = each(bf, W)
    Whb = each(lambda w, q: bf(_dot(w, bd(q))), Wb, Qb)
    XV = each(lambda mk, mrk, v: _dot(stack(mk, mrk), bd(v)), Mkb, Mrkb, Vb)
    U0b = each(lambda w, x: bf(-_dot(w, bd(bf(x[top])))), Wb, XV)
    XWU = each(lambda m, wh, u: _dot(m, side(bd(wh), bd(u))), Mrbb, Whb, U0b)
    BWU = each(lambda b, wh, u: _dot(b, side(wh, u)), BcTb, Whb, U0b)
    KV = each(_dot, KcTb, Vb)
    Y0 = each(lambda xwu, xv: xwu[:, pw:] + xv[mid], XWU, XV)
    Hd = each(lambda bwu, kv: mask_bd * (bwu[:, pw:] + kv), BWU, KV)
    RG = each(lambda rh, lt, xwu, bwu: _split2(stack(rh - xwu[:, :pw],
                                                     eye_bd * jnp.exp(lt) - mask_bd * bwu[:, :pw])),
              Rh, ltot, XWU, BWU)

    ST = [s_scr[d, p] for d, p in lanes]
    for step in range(cb):
        idx = [chains.index((d, step if d == 0 else cb - 1 - step, p)) for d, p in lanes]
        XS = [_mm3(RG[i], _split2(st)) for i, st in zip(idx, ST)]
        ST = [x[CH:] + Hd[i] for i, x in zip(idx, XS)]
        for i, x in zip(idx, XS):
            d, j, p = chains[i]
            y = x[top] + Y0[i]
            half_mean = lambda a: jnp.where(left, jnp.sum(jnp.where(left, a, 0.0), axis=-1, keepdims=True),
                                            jnp.sum(jnp.where(left, 0.0, a), axis=-1, keepdims=True)) * (1.0 / HD)
            yc = y - half_mean(y)
            var = half_mean(yc * yc)
            y_refs[d][rows(j), lanes_of(p)] = yc * lax.rsqrt(var + GN_EPS) * gng_ref[p] + gnb_ref[p]
    for (d, p), st in zip(lanes, ST):
        s_scr[d, p] = st

    @pl.when(sched_ref[_SCHED_LAST, step_id] == 1)
    def _():
        for d, p in lanes:
            st = s_scr[d, p]
            sfin_ref[0, d, 2 * p] = st[:HD, :HD]
            sfin_ref[0, d, 2 * p + 1] = st[HD:, HD:]


_SCHED_FWD, _SCHED_BWD, _SCHED_SEQ, _SCHED_FIRST, _SCHED_LAST = range(5)


def rwkv_schedule(seq_lens):
    blk_rows = RWKV_CHUNKS * CH
    cols, base = [], 0
    for s, t_len in enumerate(seq_lens):
        n_b = t_len // blk_rows
        for i in range(n_b):
            cols.append((base + i, base + n_b - 1 - i, s, int(i == 0), int(i == n_b - 1)))
        base += n_b
    return np.asarray(cols, np.int32).T


def rwkv_scan(prep, masks, gng, gnb, s0t, sched):
    r, v, kk, ld, kd, bd = prep
    n = r.shape[0]
    n_seq = s0t.shape[0]
    blk_rows = RWKV_CHUNKS * CH
    pairs, pw = HEADS // 2, 2 * HD
    in_specs, args = [], []
    for d, row in ((0, _SCHED_FWD), (1, _SCHED_BWD)):
        for a in (r, v, kk):
            in_specs.append(pl.BlockSpec((blk_rows, D_MIX), lambda i, s, row=row: (s[row, i], 0)))
            args.append(a)
        for a in (ld, kd, bd):
            in_specs.append(pl.BlockSpec((1, blk_rows, D_MIX), lambda i, s, row=row, d=d: (d, s[row, i], 0)))
            args.append(a)
    state_spec = pl.BlockSpec((1, 2, pairs, pw, pw), lambda i, s: (s[_SCHED_SEQ, i], 0, 0, 0, 0))
    in_specs += [pl.BlockSpec((2, 6, CH, pw), lambda i, s: (0, 0, 0, 0)),
                 pl.BlockSpec((pairs, 1, pw), lambda i, s: (0, 0, 0)),
                 pl.BlockSpec((pairs, 1, pw), lambda i, s: (0, 0, 0)),
                 state_spec]
    args += [masks, gng, gnb, s0t]
    grid_spec = pltpu.PrefetchScalarGridSpec(
        num_scalar_prefetch=1,
        grid=(sched.shape[1],),
        in_specs=in_specs,
        out_specs=[pl.BlockSpec((blk_rows, D_MIX), lambda i, s: (s[_SCHED_FWD, i], 0)),
                   pl.BlockSpec((blk_rows, D_MIX), lambda i, s: (s[_SCHED_BWD, i], 0)),
                   pl.BlockSpec((1, 2, HEADS, HD, HD), lambda i, s: (s[_SCHED_SEQ, i], 0, 0, 0, 0))],
        scratch_shapes=[pltpu.VMEM((2, pairs, pw, pw), F32)],
    )
    return pl.pallas_call(
        functools.partial(_rwkv_pair_kernel, RWKV_CHUNKS),
        grid_spec=grid_spec,
        out_shape=[jax.ShapeDtypeStruct((n, D_MIX), F32), jax.ShapeDtypeStruct((n, D_MIX), F32),
                   jax.ShapeDtypeStruct((n_seq, 2, HEADS, HD, HD), F32)],
        compiler_params=_cparams(("arbitrary",), VMEM_LIMIT),
        name="rwkv_scan",
    )(jnp.asarray(sched), *args)


def rwkv_pack_state(s):
    n = s.shape[0]
    st = jnp.swapaxes(s, -1, -2).reshape(n, 2, HEADS // 2, 2, HD, HD)
    eye2 = jnp.eye(2, dtype=s.dtype)
    return jnp.einsum('ndpakv,ab->ndpakbv', st, eye2).reshape(n, 2, HEADS // 2, 2 * HD, 2 * HD)


def _ctx_attn_kernel(q_ref, k_ref, v_ref, y_ref, ko_ref, vo_ref):
    scale = HD ** -0.5
    for h in range(HEADS):
        sl = slice(h * HD, (h + 1) * HD)
        q = q_ref[:, sl]
        k = k_ref[:, sl]
        v = v_ref[:, sl]
        ko_ref[0, h] = k
        vo_ref[0, h] = v
        s = _dot_nt(q.astype(BF16), k.astype(BF16)) * scale
        m = jnp.max(s, axis=-1, keepdims=True)
        e = jnp.exp(s - m)
        p = e / jnp.sum(e, axis=-1, keepdims=True)
        y_ref[:, sl] = _dot(p.astype(BF16), v.astype(BF16))


def ctx_attention(z, bsz, t_len):
    qb = A_PAD // D_MIX
    return pl.pallas_call(
        _ctx_attn_kernel,
        grid=(bsz,),
        in_specs=[pl.BlockSpec((t_len, D_MIX), lambda b: (b, qb)),
                  pl.BlockSpec((t_len, D_MIX), lambda b: (b, qb + 1)),
                  pl.BlockSpec((t_len, D_MIX), lambda b: (b, qb + 2))],
        out_specs=[pl.BlockSpec((t_len, D_MIX), lambda b: (b, 0)),
                   pl.BlockSpec((1, HEADS, t_len, HD), lambda b: (b, 0, 0, 0)),
                   pl.BlockSpec((1, HEADS, t_len, HD), lambda b: (b, 0, 0, 0))],
        out_shape=[jax.ShapeDtypeStruct((bsz * t_len, D_MIX), F32),
                   jax.ShapeDtypeStruct((bsz, HEADS, t_len, HD), F32),
                   jax.ShapeDtypeStruct((bsz, HEADS, t_len, HD), F32)],
        compiler_params=_cparams(("arbitrary",)),
        name="ctx_attention",
    )(z, z, z)


def _na_kernel(rows, nblk, *refs):
    q_ref = refs[0]
    k_refs = refs[1:1 + nblk]
    v_refs = refs[1 + nblk:1 + 2 * nblk]
    kc_ref, vc_ref, tab_ref, y_ref = refs[1 + 2 * nblk:]
    r0 = pl.program_id(1) * NA_QROWS
    u0 = jnp.clip(r0 - WIN_R // 2, 0, rows - NA_KROWS)
    scale = HD ** -0.5
    kwin = jnp.concatenate([kr[...] for kr in k_refs], axis=0).astype(BF16)
    vwin = jnp.concatenate([vr[...] for vr in v_refs], axis=0).astype(BF16)
    left = lax.broadcasted_iota(jnp.int32, (1, 2 * GRID_W), 1) < GRID_W
    bias = []
    for i in range(NA_QROWS):
        r = r0 + i
        rs = jnp.clip(r - WIN_R // 2, 0, rows - WIN_R)
        per_pair = []
        for jp in range(NA_KROWS // 2):
            kr = u0 + 2 * jp
            off = [jnp.where(jnp.logical_and(kr + e >= rs, kr + e < rs + WIN_R), 0.0, NEG) for e in range(2)]
            per_pair.append((jnp.clip(kr - r + WIN_R, 0, 2 * WIN_R - 1), jnp.where(left, off[0], off[1])))
        bias.append(per_pair)
    for h in range(HEADS):
        sl = slice(h * HD, (h + 1) * HD)
        q = q_ref[:, sl].astype(BF16)
        s_raw = _dot_nt(q, kwin[:, sl]) * scale
        s_loc = jnp.concatenate([
            jnp.concatenate([s_raw[i * GRID_W:(i + 1) * GRID_W, jp * 2 * GRID_W:(jp + 1) * 2 * GRID_W]
                             + tab_ref[h, bias[i][jp][0]] + bias[i][jp][1]
                             for jp in range(NA_KROWS // 2)], axis=1)
            for i in range(NA_QROWS)], axis=0)
        s_ctx = _dot_nt(q, kc_ref[0, 0, h].astype(BF16)) * scale
        m = jnp.maximum(jnp.max(s_loc, axis=-1, keepdims=True), jnp.max(s_ctx, axis=-1, keepdims=True))
        e_loc = jnp.exp(s_loc - m)
        e_ctx = jnp.exp(s_ctx - m)
        den = jnp.sum(e_loc, axis=-1, keepdims=True) + jnp.sum(e_ctx, axis=-1, keepdims=True)
        p_loc = (e_loc / den).astype(BF16)
        p_ctx = (e_ctx / den).astype(BF16)
        y_ref[:, sl] = _dot(p_loc, vwin[:, sl]) + _dot(p_ctx, vc_ref[0, 0, h].astype(BF16))


def na_bias_table(rpb):
    cq = np.arange(GRID_W)[:, None]
    ck = np.arange(GRID_W)[None, :]
    cs = np.clip(cq - WIN_C // 2, 0, GRID_W - WIN_C)
    col_bias = np.where((ck >= cs) & (ck < cs + WIN_C), 0.0, NEG).astype(np.float32)
    col_idx = np.clip(ck - cq + WIN_C - 1, 0, 2 * WIN_C - 2)
    rpb_col = rpb.astype(F32)[:, :, col_idx] + col_bias
    padded = jnp.pad(rpb_col, ((0, 0), (1, 1), (0, 0), (0, 0)), constant_values=NEG)
    return jnp.concatenate([padded[:, :-1], padded[:, 1:]], axis=-1)


def na_attention(z, k_ctx, v_ctx, layer, bias_tab, bsz, t_len):
    rows = t_len // GRID_W
    assert rows >= NA_KROWS and rows % NA_QROWS == 0, "latent grid too small for the row-group tiling"
    qb = A_PAD // D_MIX
    past = k_ctx.shape[3]
    blk_rows = WIN_R // 2
    nblk = NA_KROWS // blk_rows
    blk_tok = blk_rows * GRID_W
    q_tok = NA_QROWS * GRID_W

    def win_spec(j, col):
        def index(b, g):
            u0 = jnp.clip(g * NA_QROWS - WIN_R // 2, 0, rows - NA_KROWS)
            return (b * (rows // blk_rows) + u0 // blk_rows + j, col)
        return pl.BlockSpec((blk_tok, D_MIX), index)

    in_specs = ([pl.BlockSpec((q_tok, D_MIX), lambda b, g: (b * (rows // NA_QROWS) + g, qb))]
                + [win_spec(j, qb + 1) for j in range(nblk)]
                + [win_spec(j, qb + 2) for j in range(nblk)]
                + [pl.BlockSpec((1, 1, HEADS, past, HD), lambda b, g: (b, layer, 0, 0, 0)),
                   pl.BlockSpec((1, 1, HEADS, past, HD), lambda b, g: (b, layer, 0, 0, 0)),
                   pl.BlockSpec(bias_tab.shape, lambda b, g: (0, 0, 0, 0))])
    return pl.pallas_call(
        functools.partial(_na_kernel, rows, nblk),
        grid=(bsz, rows // NA_QROWS),
        in_specs=in_specs,
        out_specs=pl.BlockSpec((q_tok, D_MIX), lambda b, g: (b * (rows // NA_QROWS) + g, 0)),
        out_shape=jax.ShapeDtypeStruct((bsz * t_len, D_MIX), F32),
        compiler_params=_cparams(("arbitrary", "arbitrary"), VMEM_LIMIT),
        name="na_attention",
    )(*([z] * (1 + 2 * nblk)), k_ctx, v_ctx, bias_tab)


def _s5_kernel(bsz, tb, uf_ref, ub_ref, bm_ref, cm_ref, ar_ref, ai_ref, x0_ref, yf_ref, yb_ref, xf_ref,
               x_scr, xo_scr, carry_scr):
    i = pl.program_id(1)

    @pl.when(i == 0)
    def _():
        carry_scr[...] = x0_ref[0]

    n_re = S5_GW // LANE
    n_rows = S5_GB * bsz
    assert n_rows == SUBLANE
    blk = bsz * tb
    group = min(n_re, S5_CARRY_VREGS // 2)
    ch = D_MIX // S5_GB
    lane = lambda c: slice(c * LANE, (c + 1) * LANE)
    steps_of = lambda row: pl.ds(row, tb, stride=n_rows)
    rows_of = lambda t: pl.ds(pl.multiple_of(t * n_rows, SUBLANE), n_rows)
    for d, (u_ref, y_ref) in enumerate(((uf_ref, yf_ref), (ub_ref, yb_ref))):
        u = u_ref[...].reshape(blk, D_MIX).astype(BF16)
        for k in range(S5_GB):
            bu = _dot(u[:, k * ch:(k + 1) * ch], bm_ref[d, k])
            for c in range(2 * n_re):
                for b in range(bsz):
                    x_scr[c, steps_of(k * bsz + b), :] = bu[b * tb:(b + 1) * tb, lane(c)]
        for c0 in range(0, n_re, group):
            tiles = list(range(c0, c0 + group))
            ar = [ar_ref[d, :, lane(c)] for c in tiles]
            ai = [ai_ref[d, :, lane(c)] for c in tiles]

            def step(s, x, d=d, tiles=tiles, ar=ar, ai=ai):
                t = s if d == 0 else tb - 1 - s
                rows = rows_of(t)
                out = []
                for c, a_r, a_i, (xr, xi) in zip(tiles, ar, ai, x):
                    nr = a_r * xr - a_i * xi + x_scr[c, rows, :]
                    ni = a_r * xi + a_i * xr + x_scr[n_re + c, rows, :]
                    xo_scr[c, rows, :] = nr
                    xo_scr[n_re + c, rows, :] = ni
                    out.append((nr, ni))
                return tuple(out)

            init = tuple((carry_scr[d, :, lane(c)], carry_scr[d, :, lane(n_re + c)]) for c in tiles)
            fin = lax.fori_loop(0, tb, step, init, unroll=4)
            for c, (xr, xi) in zip(tiles, fin):
                carry_scr[d, :, lane(c)] = xr
                carry_scr[d, :, lane(n_re + c)] = xi
        ys = []
        for k in range(S5_GB):
            xs = jnp.concatenate(
                [jnp.concatenate([xo_scr[c, steps_of(k * bsz + b), :] for b in range(bsz)], axis=0)
                 for c in range(2 * n_re)], axis=-1)
            ys.append(_dot(xs.astype(BF16), cm_ref[k]))
        y_ref[...] = jnp.concatenate(ys, axis=-1).reshape(bsz, tb, D_MIX)

    @pl.when(i == pl.num_programs(1) - 1)
    def _():
        xf_ref[0] = carry_scr[...]


def s5_scan(z3, bmat, cmat, a_re, a_im, x0, tb, grp):
    bsz, t_len, _ = z3.shape
    n_t = t_len // tb
    n_rows = S5_GB * grp
    ucol = (A_PAD + 3 * D_MIX) // D_MIX
    full = lambda shape: pl.BlockSpec(shape, lambda g, i: (0,) * len(shape))
    state_spec = pl.BlockSpec((1, 2, n_rows, 2 * S5_GW), lambda g, i: (g, 0, 0, 0))
    y_shape = jax.ShapeDtypeStruct((bsz, t_len, D_MIX), F32)
    return pl.pallas_call(
        functools.partial(_s5_kernel, grp, tb),
        grid=(bsz // grp, n_t),
        in_specs=[pl.BlockSpec((grp, tb, D_MIX), lambda g, i: (g, i, ucol)),
                  pl.BlockSpec((grp, tb, D_MIX), lambda g, i: (g, n_t - 1 - i, ucol)),
                  full((2, S5_GB, D_MIX // S5_GB, 2 * S5_GW)), full((S5_GB, 2 * S5_GW, D_MIX // S5_GB)),
                  full((2, n_rows, S5_GW)), full((2, n_rows, S5_GW)), state_spec],
        out_specs=[pl.BlockSpec((grp, tb, D_MIX), lambda g, i: (g, i, 0)),
                   pl.BlockSpec((grp, tb, D_MIX), lambda g, i: (g, n_t - 1 - i, 0)),
                   state_spec],
        out_shape=[y_shape, y_shape, jax.ShapeDtypeStruct((bsz // grp, 2, n_rows, 2 * S5_GW), F32)],
        scratch_shapes=[pltpu.VMEM((2 * S5_GW // LANE, n_rows * tb, LANE), F32),
                        pltpu.VMEM((2 * S5_GW // LANE, n_rows * tb, LANE), F32),
                        pltpu.VMEM((2, n_rows, 2 * S5_GW), F32)],
        compiler_params=_cparams(("arbitrary", "arbitrary"), VMEM_LIMIT),
        name="s5_scan",
    )(z3, z3, bmat, cmat, a_re, a_im, x0)


def s5_rows(x, grp):
    n_g = x.shape[1] // grp
    return (x.reshape(2, n_g, grp, S5_GB, S5_GW).transpose(1, 0, 3, 2, 4)
            .reshape(n_g, 2, S5_GB * grp, S5_GW))


def s5_unrows(x, grp):
    n_g = x.shape[0]
    return x.reshape(n_g, 2, S5_GB, grp, S5_GW).transpose(1, 0, 3, 2, 4).reshape(2, n_g * grp, S5_N)


def _merge_kernel(gnf_ref, gnb_ref, bonus_ref, g_ref, yb_ref, ycf_ref, ycb_ref, u_ref, zg0_ref, zg1_ref, zg2_ref,
                  x_ref, mod_ref, s5d_ref, wglu_ref, bglu_ref, wb_ref, wout_ref, lng_ref, lnb_ref, x1_ref, h2_ref):
    ya = (gnf_ref[...] + gnb_ref[...] + bonus_ref[...]) * g_ref[...]
    yc = ycf_ref[...] + ycb_ref[...] + s5d_ref[...] * u_ref[...]
    yc = 0.5 * yc * (1.0 + jnp.tanh(math.sqrt(2.0 / math.pi) * (yc + 0.044715 * (yc * yc * yc))))
    yc = yc * _sigmoid(_dot(yc.astype(BF16), wglu_ref[...]) + bglu_ref[...])
    merged = (_dot(ya.astype(BF16), wb_ref[0]) * _sigmoid(zg0_ref[...])
              + _dot(yb_ref[...].astype(BF16), wb_ref[1]) * _sigmoid(zg1_ref[...])
              + _dot(yc.astype(BF16), wb_ref[2]) * _sigmoid(zg2_ref[...]))
    mo = _dot(merged.astype(BF16), wout_ref[...])
    m = mod_ref[0]
    x1 = _layer_norm(DN_ALPHA * x_ref[...] + m[2:3] * mo, lng_ref[...], lnb_ref[...])
    x1_ref[...] = x1
    h2_ref[...] = (x1 * (1.0 + m[4:5]) + m[3:4]).astype(BF16)


def merge_branches(gn, bonus, g, yb, yc, z, x, mods, mod_of_tile, p):
    n = x.shape[0]
    row = lambda w, col=0: pl.BlockSpec((TM, w), lambda i, col=col: (i, col))
    full = lambda shape: pl.BlockSpec(shape, lambda i: (0,) * len(shape))
    gb = (A_PAD + 4 * D_MIX) // D_MODEL
    return pl.pallas_call(
        _merge_kernel,
        grid=(n // TM,),
        in_specs=[row(D_MIX), row(D_MIX), row(D_MIX), row(D_MIX), row(D_MIX), row(D_MIX), row(D_MIX),
                  row(D_MIX, A_PAD // D_MIX + 3),
                  row(D_MODEL, gb), row(D_MODEL, gb + 1), row(D_MODEL, gb + 2),
                  row(D_MODEL),
                  pl.BlockSpec((1, SUBLANE, D_MODEL), lambda i: (mod_of_tile(i, TM), 0, 0)),
                  full((1, D_MIX)), full((D_MIX, D_MIX)), full((1, D_MIX)),
                  full((3, D_MIX, D_MODEL)), full((D_MODEL, D_MODEL)),
                  full((1, D_MODEL)), full((1, D_MODEL))],
        out_specs=[row(D_MODEL), row(D_MODEL)],
        out_shape=[jax.ShapeDtypeStruct((n, D_MODEL), F32), jax.ShapeDtypeStruct((n, D_MODEL), BF16)],
        compiler_params=_cparams(("arbitrary",), VMEM_LIMIT),
        name="merge_branches",
    )(gn[0], gn[1], bonus, g, yb, yc[0], yc[1], z, z, z, z, x, mods,
      p['s5_d'], p['w_glu'], p['b_glu'], p['w_branch'], p['w_out'], p['ln1_g'], p['ln1_b'])


def _first_max(val, idx, big):
    m = jnp.max(jnp.max(val, axis=1, keepdims=True), axis=0, keepdims=True)
    cand = jnp.where(val == m, idx, big)
    first = jnp.min(jnp.min(cand, axis=1, keepdims=True), axis=0, keepdims=True)
    return m, idx == first


def _router_kernel(h_ref, wt_ref, bias_ref, gates_ref):
    per = N_EXPERTS // N_GROUPS
    logits = _dot_nt(wt_ref[...], h_ref[...])
    n = logits.shape[1]
    scores = _sigmoid(logits).reshape(N_GROUPS, per, n)
    sel = scores + bias_ref[...]
    e_idx = (lax.broadcasted_iota(jnp.int32, (N_GROUPS, per, n), 0) * per
             + lax.broadcasted_iota(jnp.int32, (N_GROUPS, per, n), 1))
    in_grp = lax.broadcasted_iota(jnp.int32, (N_GROUPS, per, n), 1)
    m1 = jnp.max(sel, axis=1, keepdims=True)
    first = jnp.min(jnp.where(sel == m1, in_grp, per), axis=1, keepdims=True)
    m2 = jnp.max(jnp.where(in_grp == first, -jnp.inf, sel), axis=1, keepdims=True)
    grp = m1 + m2
    g_idx = lax.broadcasted_iota(jnp.int32, (N_GROUPS, 1, n), 0)
    gmask = jnp.zeros((N_GROUPS, 1, n), F32)
    for _ in range(TOPK_GROUPS):
        _, hit = _first_max(grp, g_idx, N_GROUPS)
        gmask = jnp.where(hit, 1.0, gmask)
        grp = jnp.where(hit, -jnp.inf, grp)
    cur = jnp.where(jnp.broadcast_to(gmask, sel.shape) > 0.0, sel, NEG)
    w = jnp.zeros((N_GROUPS, per, n), F32)
    for _ in range(TOP_K):
        _, hit = _first_max(cur, e_idx, N_EXPERTS)
        w = jnp.where(hit, scores, w)
        cur = jnp.where(hit, -jnp.inf, cur)
    tot = jnp.sum(jnp.sum(w, axis=1, keepdims=True), axis=0, keepdims=True)
    gates = (ROUTED_SCALE * w / tot).reshape(N_EXPERTS, n)
    hi = gates.astype(BF16).astype(F32)
    lo = (gates - hi).astype(BF16).astype(F32)
    gates_ref[...] = jnp.concatenate([hi, lo], axis=0).T.astype(BF16)


def moe_router(h2, router_wt, router_bias):
    n = h2.shape[0]
    return pl.pallas_call(
        _router_kernel,
        grid=(n // TM,),
        in_specs=[pl.BlockSpec((TM, D_MODEL), lambda i: (i, 0)),
                  pl.BlockSpec((N_EXPERTS, D_MODEL), lambda i: (0, 0)),
                  pl.BlockSpec((N_GROUPS, N_EXPERTS // N_GROUPS, 1), lambda i: (0, 0, 0))],
        out_specs=pl.BlockSpec((TM, 2 * N_EXPERTS), lambda i: (i, 0)),
        out_shape=jax.ShapeDtypeStruct((n, 2 * N_EXPERTS), BF16),
        compiler_params=_cparams(("arbitrary",)),
        name="moe_router",
    )(h2, router_wt, router_bias)


def _moe_kernel(ec, h_ref, gates_ref, x1_ref, mod_ref, wg_ref, wu_ref, wd_ref, ex_ref,
                sg_ref, su_ref, sd_ref, lng_ref, lnb_ref, out_ref, acc_scr):
    j = pl.program_id(1)
    h = h_ref[...]

    @pl.when(j == 0)
    def _():
        sh = _dot(h, sg_ref[...])
        sh = sh * _sigmoid(sh) * _dot(h, su_ref[...])
        acc_scr[...] = _dot(sh.astype(BF16), sd_ref[...])

    gexp = _dot(gates_ref[...], ex_ref[0])
    wg = jnp.concatenate([wg_ref[e] for e in range(ec)], axis=1)
    wu = jnp.concatenate([wu_ref[e] for e in range(ec)], axis=1)
    hg = _dot(h, wg)
    hu = _dot(h, wu)
    act = hg * _sigmoid(hg) * hu * gexp
    acc_scr[...] += _dot(act.astype(BF16), wd_ref[...])

    @pl.when(j == pl.num_programs(1) - 1)
    def _():
        m = mod_ref[0]
        out_ref[...] = _layer_norm(DN_ALPHA * x1_ref[...] + m[5:6] * acc_scr[...], lng_ref[...], lnb_ref[...])


def moe_ffn(h2, gates, x1, mods, mod_of_tile, p, tm=1024, ec=MOE_EC):
    n = h2.shape[0]
    wcols = ec * D_EXPERT
    n_j = N_EXPERTS // ec
    full = lambda shape: pl.BlockSpec(shape, lambda i, j: (0,) * len(shape))
    return pl.pallas_call(
        functools.partial(_moe_kernel, ec),
        grid=(n // tm, n_j),
        in_specs=[pl.BlockSpec((tm, D_MODEL), lambda i, j: (i, 0)),
                  pl.BlockSpec((tm, 2 * N_EXPERTS), lambda i, j: (i, 0)),
                  pl.BlockSpec((tm, D_MODEL), lambda i, j: (i, 0)),
                  pl.BlockSpec((1, SUBLANE, D_MODEL), lambda i, j: (mod_of_tile(i, tm), 0, 0)),
                  pl.BlockSpec((ec, D_MODEL, D_EXPERT), lambda i, j: (j, 0, 0)),
                  pl.BlockSpec((ec, D_MODEL, D_EXPERT), lambda i, j: (j, 0, 0)),
                  pl.BlockSpec((wcols, D_MODEL), lambda i, j: (j, 0)),
                  pl.BlockSpec((1, 2 * N_EXPERTS, wcols), lambda i, j: (j, 0, 0)),
                  full((D_MODEL, D_SHARED)), full((D_MODEL, D_SHARED)), full((D_SHARED, D_MODEL)),
                  full((1, D_MODEL)), full((1, D_MODEL))],
        out_specs=pl.BlockSpec((tm, D_MODEL), lambda i, j: (i, 0)),
        out_shape=jax.ShapeDtypeStruct((n, D_MODEL), F32),
        scratch_shapes=[pltpu.VMEM((tm, D_MODEL), F32)],
        compiler_params=_cparams(("arbitrary", "arbitrary"), VMEM_LIMIT),
        name="moe_ffn",
    )(h2, gates, x1, mods, p['wg'], p['wu'], p['wd'], p['expand'],
      p['sh_g'], p['sh_u'], p['sh_d'], p['ln2_g'], p['ln2_b'])


def _block_diag2(m):
    z = jnp.zeros_like(m[0])
    return jnp.concatenate([jnp.concatenate([m[0], z], axis=1), jnp.concatenate([z, m[1]], axis=1)], axis=0)


def _layer_params(P, l):
    g = lambda name: P[name][l]
    w_in = g('w_in')
    pad = jnp.zeros((D_MODEL, A_PAD - A_COLS), F32)
    p = {}
    p['w_in'] = jnp.concatenate([w_in[:, :A_COLS], pad, w_in[:, A_COLS:]], axis=1).astype(BF16)
    p['mu'] = jnp.concatenate([g('rwkv_mu'), jnp.zeros((A_PAD - A_COLS,), F32)])[None, :]
    p['w2bd'] = _block_diag2(g('rwkv_w2')).astype(BF16)
    p['a2bd'] = _block_diag2(g('rwkv_a2')).astype(BF16)
    p['g2'] = g('rwkv_g2').astype(BF16)
    p['w0'] = g('rwkv_w0').reshape(1, 2 * D_MIX)
    p['a0'] = g('rwkv_a0').reshape(1, 2 * D_MIX)
    p['kkp'] = g('rwkv_kk')[None, :]
    p['ka'] = g('rwkv_ka')[None, :]
    p['rk'] = g('rwkv_rk')[None, :]
    hid = np.arange(D_MIX) // HD
    p['seg_ones'] = jnp.asarray((hid[:, None] == hid[None, :]).astype(np.float32), dtype=BF16)
    p['gng'] = g('rwkv_gn_g').reshape(HEADS // 2, 1, 2 * HD)
    p['gnb'] = g('rwkv_gn_b').reshape(HEADS // 2, 1, 2 * HD)
    a = lax.complex(g('s5_a_re'), g('s5_a_im'))
    dt = jnp.exp(g('s5_log_dt'))[..., None]
    a_bar = jnp.exp(dt * a)
    b_bar = ((a_bar - 1.0) / a)[..., None] * lax.complex(g('s5_b_re'), g('s5_b_im'))
    eye_g = jnp.eye(C_GROUPS // S5_GB, dtype=F32)
    gpb = C_GROUPS // S5_GB
    bd = lambda m: jnp.einsum('dkgph,gj->dkghjp', m.reshape(2, S5_GB, gpb, C_STATE, C_GROUP),
                              eye_g).reshape(2, S5_GB, D_MIX // S5_GB, S5_GW)
    p['s5_bmat'] = jnp.concatenate([bd(b_bar.real), bd(b_bar.imag)], axis=-1).astype(BF16)
    cd = lambda m: jnp.einsum('kghp,gj->kjpgh', m.reshape(S5_GB, gpb, C_GROUP, C_STATE),
                              eye_g).reshape(S5_GB, S5_GW, D_MIX // S5_GB)
    p['s5_cmat'] = jnp.concatenate([cd(g('s5_c_re')), -cd(g('s5_c_im'))], axis=1).astype(BF16)
    p['s5_ar'] = a_bar.real.reshape(2, 1, S5_N)
    p['s5_ai'] = a_bar.imag.reshape(2, 1, S5_N)
    p['s5_d'] = g('s5_d')[None, :]
    p['w_glu'] = g('s5_w_glu').astype(BF16)
    p['b_glu'] = g('s5_b_glu')[None, :]
    p['w_branch'] = g('w_branch').astype(BF16)
    p['w_out'] = g('w_out').astype(BF16)
    p['ln1_g'] = g('ln1_g')[None, :]
    p['ln1_b'] = g('ln1_b')[None, :]
    p['router_wt'] = g('router_w').T.astype(BF16)
    p['router_bias'] = g('router_bias').reshape(N_GROUPS, N_EXPERTS // N_GROUPS, 1)
    ecols = N_EXPERTS * D_EXPERT
    p['wg'] = g('exp_w_gate').astype(BF16)
    p['wu'] = g('exp_w_up').astype(BF16)
    p['wd'] = g('exp_w_down').reshape(ecols, D_MODEL).astype(BF16)
    p['sh_g'] = g('sh_w_gate').astype(BF16)
    p['sh_u'] = g('sh_w_up').astype(BF16)
    p['sh_d'] = g('sh_w_down').astype(BF16)
    p['ln2_g'] = g('ln2_g')[None, :]
    p['ln2_b'] = g('ln2_b')[None, :]
    return p


def _expand_table(ec):
    n_j = N_EXPERTS // ec
    t = np.zeros((n_j, 2 * N_EXPERTS, ec * D_EXPERT), np.float32)
    for e in range(N_EXPERTS):
        j, q = divmod(e, ec)
        t[j, e, q * D_EXPERT:(q + 1) * D_EXPERT] = 1.0
        t[j, N_EXPERTS + e, q * D_EXPERT:(q + 1) * D_EXPERT] = 1.0
    return jnp.asarray(t, dtype=BF16)


def kernel(x_prompt, x_sample, c, cache_na_k, cache_na_v, state_rwkv, state_s5_re, state_s5_im, c_ctx, w_ada, b_ada, w_in, rwkv_mu, rwkv_w0, rwkv_w2, rwkv_a0, rwkv_a2, rwkv_g2, rwkv_kk, rwkv_ka, rwkv_rk, rwkv_gn_g, rwkv_gn_b, na_rpb, s5_a_re, s5_a_im, s5_log_dt, s5_b_re, s5_b_im, s5_c_re, s5_c_im, s5_d, s5_w_glu, s5_b_glu, w_branch, w_out, ln1_g, ln1_b, router_w, router_bias, exp_w_gate, exp_w_up, exp_w_down, sh_w_gate, sh_w_up, sh_w_down, ln2_g, ln2_b):
    P = dict(w_in=w_in, rwkv_mu=rwkv_mu, rwkv_w0=rwkv_w0, rwkv_w2=rwkv_w2, rwkv_a0=rwkv_a0, rwkv_a2=rwkv_a2,
             rwkv_g2=rwkv_g2, rwkv_kk=rwkv_kk, rwkv_ka=rwkv_ka, rwkv_rk=rwkv_rk, rwkv_gn_g=rwkv_gn_g,
             rwkv_gn_b=rwkv_gn_b, s5_a_re=s5_a_re, s5_a_im=s5_a_im, s5_log_dt=s5_log_dt, s5_b_re=s5_b_re,
             s5_b_im=s5_b_im, s5_c_re=s5_c_re, s5_c_im=s5_c_im, s5_d=s5_d, s5_w_glu=s5_w_glu,
             s5_b_glu=s5_b_glu, w_branch=w_branch, w_out=w_out, ln1_g=ln1_g, ln1_b=ln1_b, router_w=router_w,
             router_bias=router_bias, exp_w_gate=exp_w_gate, exp_w_up=exp_w_up, exp_w_down=exp_w_down,
             sh_w_gate=sh_w_gate, sh_w_up=sh_w_up, sh_w_down=sh_w_down, ln2_g=ln2_g, ln2_b=ln2_b)
    bc, tc, _ = x_prompt.shape
    bl, tl, _ = x_sample.shape
    depth = w_in.shape[0]

    cond = jnp.concatenate([c_ctx[None, :], c, jnp.zeros((SUBLANE - 1 - bl, D_MODEL), F32)], axis=0)
    ada = ada_modulation(cond, w_ada, b_ada)
    masks = jnp.asarray(_chunk_masks())
    expand = _expand_table(MOE_EC)

    paths = {
        'ctx': dict(bsz=bc, t=tc, x=x_prompt.reshape(bc * tc, D_MODEL), mod_of_tile=lambda i, tm: 0),
        'lat': dict(bsz=bl, t=tl, x=x_sample.reshape(bl * tl, D_MODEL),
                    mod_of_tile=lambda i, tm: 1 + (i * tm) // tl),
    }
    for q in paths.values():
        tiles = q['t'] // TM
        pos = np.arange(q['bsz'] * tiles) % tiles
        q['seq_tiles'] = jnp.asarray(np.concatenate([np.full_like(pos, tiles), pos]).astype(np.int32))
        q['sched'] = rwkv_schedule([q['t']] * q['bsz'])
        q['s5_tb'] = min(S5_ROWS // S5_GRP, q['t'])
        assert q['bsz'] % S5_GRP == 0 and q['s5_tb'] % SUBLANE == 0 and q['t'] % q['s5_tb'] == 0

    new_k, new_v, new_rwkv, new_s5 = [], [], [], []
    for l in range(depth):
        p = _layer_params(P, l)
        p['expand'] = expand
        mods = ada[l, :1 + bl].reshape(1 + bl, 6, D_MODEL)
        mods = jnp.concatenate([mods, jnp.zeros((1 + bl, SUBLANE - 6, D_MODEL), F32)], axis=1)
        bias_tab = na_bias_table(na_rpb[l])
        for name, q in paths.items():
            bsz, t_len, x, mod_of_tile = q['bsz'], q['t'], q['x'], q['mod_of_tile']
            z = in_projection(x, mods, p['w_in'], mod_of_tile)

            r, v, kk, ld, kd, bd, g, bonus = rwkv_prep(z, q['seq_tiles'], p)
            if name == 'ctx':
                s0t = jnp.zeros((bsz, 2, HEADS // 2, 2 * HD, 2 * HD), F32)
            else:
                s0t = rwkv_pack_state(state_rwkv[:, l])
            gn_f, gn_b, s_fin = rwkv_scan((r, v, kk, ld, kd, bd), masks, p['gng'], p['gnb'], s0t, q['sched'])

            if name == 'ctx':
                yb, k_h, v_h = ctx_attention(z, bsz, t_len)
            else:
                yb = na_attention(z, cache_na_k, cache_na_v, l, bias_tab, bsz, t_len)

            if name == 'ctx':
                x0 = jnp.zeros((bsz // S5_GRP, 2, S5_GB * S5_GRP, 2 * S5_GW), F32)
            else:
                x0 = jnp.concatenate(
                    [s5_rows(jnp.swapaxes(s[:, l].reshape(bsz, 2, S5_N), 0, 1), S5_GRP)
                     for s in (state_s5_re, state_s5_im)], axis=-1)
            a_re, a_im = (s5_rows(jnp.broadcast_to(a, (2, S5_GRP, S5_N)), S5_GRP)[0]
                          for a in (p['s5_ar'], p['s5_ai']))
            yc_f, yc_b, x_fin = s5_scan(z.reshape(bsz, t_len, IN_COLS_P), p['s5_bmat'], p['s5_cmat'],
                                        a_re, a_im, x0, q['s5_tb'], S5_GRP)
            yc = (yc_f.reshape(bsz * t_len, D_MIX), yc_b.reshape(bsz * t_len, D_MIX))

            x1, h2 = merge_branches((gn_f, gn_b), bonus, g, yb, yc, z, x, mods, mod_of_tile, p)
            gates = moe_router(h2, p['router_wt'], p['router_bias'])
            q['x'] = moe_ffn(h2, gates, x1, mods, mod_of_tile, p)
            if name == 'ctx':
                new_k.append(k_h)
                new_v.append(v_h)
                new_rwkv.append(jnp.swapaxes(s_fin, -1, -2))
                new_s5.append([jnp.swapaxes(s5_unrows(part, S5_GRP), 0, 1)
                               for part in (x_fin[..., :S5_GW], x_fin[..., S5_GW:])])

    s5_re, s5_im = (jnp.stack([layer[part] for layer in new_s5], axis=1).reshape(bc, depth, 2, C_GROUPS, C_STATE)
                    for part in range(2))
    return (paths['ctx']['x'].reshape(bc, tc, D_MODEL), paths['lat']['x'].reshape(bl, tl, D_MODEL),
            jnp.stack(new_k, axis=1), jnp.stack(new_v, axis=1), jnp.stack(new_rwkv, axis=1), s5_re, s5_im)
```

```python
import functools
import math

import numpy as np
import jax
import jax.numpy as jnp
from jax import lax
from jax.experimental import pallas as pl
from jax.experimental.pallas import tpu as pltpu

F32 = jnp.float32
BF16 = jnp.bfloat16

D_MODEL = 1024
DEPTH = 2
GRID_W = 64
D_MIX = 512
HEADS = 8
HD = 64
LORA = 128
WIN_R = 8
WIN_C = 16
C_GROUP = 16
C_GROUPS = D_MIX // C_GROUP
C_STATE = 64
S5_N = C_GROUPS * C_STATE
N_EXPERTS = 64
TOP_K = 8
N_GROUPS = 8
TOPK_GROUPS = 4
D_EXPERT = 128
D_SHARED = 128
ROUTED_SCALE = 2.5
LN_EPS = 1e-5
GN_EPS = 64e-5
NEG = -1e30
DN_ALPHA = (2 * DEPTH) ** 0.25
A_COLS = 3 * D_MIX + 3 * LORA
A_PAD = 2048
IN_COLS_P = A_PAD + 4 * D_MIX + 3 * D_MODEL

LANE = 128
SUBLANE = 8
TM = 256
CH = HD
RWKV_CHUNKS = 4
MOE_EC = 8
NA_QROWS = 8
NA_KROWS = NA_QROWS + WIN_R
S5_GB = 4
S5_GW = S5_N // S5_GB
S5_GRP = SUBLANE // S5_GB
S5_ROWS = 512
S5_CARRY_VREGS = 32
VMEM_LIMIT = 56 * 1024 * 1024


def _cparams(sem, vmem=None):
    return pltpu.CompilerParams(dimension_semantics=sem, vmem_limit_bytes=vmem)


def _sigmoid(x):
    return 1.0 / (1.0 + jnp.exp(-x))


def _dot(a, b, precision=None):
    return jnp.dot(a, b, preferred_element_type=F32, precision=precision)


def _dot_nt(a, b, precision=None):
    return lax.dot_general(a, b, (((1,), (1,)), ((), ())), preferred_element_type=F32, precision=precision)


def _split2(x):
    hi = x.astype(BF16)
    return hi, (x - hi.astype(F32)).astype(BF16)


def _split3(x):
    hi = x.astype(BF16)
    r1 = x - hi.astype(F32)
    mid = r1.astype(BF16)
    return hi, mid, (r1 - mid.astype(F32)).astype(BF16)


def _seg_sum(x, ones_bf16):
    return sum(_dot(part, ones_bf16) for part in _split3(x))


def _mm3(a, b, dot=_dot):
    return dot(a[0], b[0]) + (dot(a[1], b[0]) + dot(a[0], b[1]))


def _layer_norm(x, g, b):
    mu = jnp.mean(x, axis=-1, keepdims=True)
    xc = x - mu
    var = jnp.mean(xc * xc, axis=-1, keepdims=True)
    return xc * lax.rsqrt(var + LN_EPS) * g + b


def _ada_kernel(c_ref, w_ref, b_ref, o_ref):
    c = c_ref[...]
    s = c * _sigmoid(c)
    o_ref[0] = _dot(s.astype(BF16), w_ref[0].astype(BF16)) + b_ref[0]


def ada_modulation(cond, w_ada, b_ada):
    n_l, d, n6 = w_ada.shape
    tn = 1536
    return pl.pallas_call(
        _ada_kernel,
        grid=(n_l, n6 // tn),
        in_specs=[pl.BlockSpec((SUBLANE, d), lambda l, j: (0, 0)),
                  pl.BlockSpec((1, d, tn), lambda l, j: (l, 0, j)),
                  pl.BlockSpec((1, 1, tn), lambda l, j: (l, 0, j))],
        out_specs=pl.BlockSpec((1, SUBLANE, tn), lambda l, j: (l, 0, j)),
        out_shape=jax.ShapeDtypeStruct((n_l, SUBLANE, n6), F32),
        compiler_params=_cparams(("arbitrary", "arbitrary")),
        name="ada_modulation",
    )(cond, w_ada, b_ada.reshape(n_l, 1, n6))


def _inproj_kernel(x_ref, mod_ref, w_ref, z_ref, h_scr):
    @pl.when(pl.program_id(1) == 0)
    def _():
        m = mod_ref[0]
        h_scr[...] = (x_ref[...] * (1.0 + m[1:2]) + m[0:1]).astype(BF16)
    z_ref[...] = _dot(h_scr[...], w_ref[...])


def in_projection(x, mods, w_in_p, mod_of_tile, tm=2048, tn=1024):
    n, d = x.shape
    cols = w_in_p.shape[1]
    return pl.pallas_call(
        _inproj_kernel,
        grid=(n // tm, cols // tn),
        in_specs=[pl.BlockSpec((tm, d), lambda i, j: (i, 0)),
                  pl.BlockSpec((1, SUBLANE, d), lambda i, j: (mod_of_tile(i, tm), 0, 0)),
                  pl.BlockSpec((d, tn), lambda i, j: (0, j))],
        out_specs=pl.BlockSpec((tm, tn), lambda i, j: (i, j)),
        out_shape=jax.ShapeDtypeStruct((n, cols), F32),
        scratch_shapes=[pltpu.VMEM((tm, d), BF16)],
        compiler_params=_cparams(("arbitrary", "arbitrary"), VMEM_LIMIT),
        name="in_projection",
    )(x, mods, w_in_p)


def _rwkv_prep_kernel(seq_tiles_ref, z_ref, zp_ref, zn_ref, mu_ref, w2_ref, a2_ref, g2_ref, w0_ref, a0_ref,
                      kkp_ref, ka_ref, rk_ref, e_ref,
                      r_ref, v_ref, kk_ref, ld_ref, kd_ref, bd_ref, g_ref, bonus_ref):
    i = pl.program_id(0)
    tiles = seq_tiles_ref[i]
    pos = seq_tiles_ref[i + pl.num_programs(0)]
    x = z_ref[...]
    tm = x.shape[0]
    rows = lax.broadcasted_iota(jnp.int32, x.shape, 0)
    prev_row = jnp.where(pos == 0, 0.0, zp_ref[SUBLANE - 1:SUBLANE, :])
    next_row = jnp.where(pos == tiles - 1, 0.0, zn_ref[0:1, :])
    xm1 = jnp.where(rows == 0, prev_row, pltpu.roll(x, 1, axis=0))
    xp1 = jnp.where(rows == tm - 1, next_row, pltpu.roll(x, tm - 1, axis=0))
    za = x + mu_ref[...] * (0.5 * (xm1 + xp1) - x)

    r = za[:, 0:D_MIX]
    k = za[:, D_MIX:2 * D_MIX]
    v = za[:, 2 * D_MIX:3 * D_MIX]
    lw = za[:, 3 * D_MIX:3 * D_MIX + LORA]
    la = za[:, 3 * D_MIX + LORA:3 * D_MIX + 2 * LORA]
    lg = za[:, 3 * D_MIX + 2 * LORA:3 * D_MIX + 3 * LORA]

    w_both = w0_ref[...] + _dot(jnp.tanh(lw).astype(BF16), w2_ref[...])
    a_both = _sigmoid(a0_ref[...] + _dot(la.astype(BF16), a2_ref[...]))
    g_ref[...] = _dot(_sigmoid(lg).astype(BF16), g2_ref[...])

    e = e_ref[...]
    kks = k * kkp_ref[...]
    nrm = jnp.sqrt(_seg_sum(kks * kks, e))
    kk = kks / jnp.maximum(nrm, 1e-12)
    bonus = jnp.zeros_like(v)
    r_ref[...] = r
    v_ref[...] = v
    kk_ref[...] = kk
    for d in range(2):
        w = w_both[:, d * D_MIX:(d + 1) * D_MIX]
        a = a_both[:, d * D_MIX:(d + 1) * D_MIX]
        ld = -math.exp(-0.5) * _sigmoid(w)
        kd = k * (1.0 + (a - 1.0) * ka_ref[...])
        bd = kk * a
        bonus = bonus + _seg_sum(r * kd * rk_ref[...], e) * v
        ld_ref[d] = ld
        kd_ref[d] = kd
        bd_ref[d] = bd
    bonus_ref[...] = bonus


def rwkv_prep(z, seq_tiles, p):
    n = z.shape[0]
    nt = n // TM
    halo = TM // SUBLANE
    nb8 = n // SUBLANE
    tok = jax.ShapeDtypeStruct((n, D_MIX), F32)
    tok2 = jax.ShapeDtypeStruct((2, n, D_MIX), F32)
    full = lambda shape: pl.BlockSpec(shape, lambda i, s: (0,) * len(shape))
    tok_spec = pl.BlockSpec((TM, D_MIX), lambda i, s: (i, 0))
    tok2_spec = pl.BlockSpec((2, TM, D_MIX), lambda i, s: (0, i, 0))
    grid_spec = pltpu.PrefetchScalarGridSpec(
        num_scalar_prefetch=1,
        grid=(nt,),
        in_specs=[pl.BlockSpec((TM, A_PAD), lambda i, s: (i, 0)),
                  pl.BlockSpec((SUBLANE, A_PAD), lambda i, s: (jnp.maximum(i * halo - 1, 0), 0)),
                  pl.BlockSpec((SUBLANE, A_PAD), lambda i, s: (jnp.minimum((i + 1) * halo, nb8 - 1), 0)),
                  full((1, A_PAD)), full((LORA, 2 * D_MIX)), full((LORA, 2 * D_MIX)), full((LORA, D_MIX)),
                  full((1, 2 * D_MIX)), full((1, 2 * D_MIX)), full((1, D_MIX)), full((1, D_MIX)),
                  full((1, D_MIX)), full((D_MIX, D_MIX))],
        out_specs=[tok_spec, tok_spec, tok_spec, tok2_spec, tok2_spec, tok2_spec, tok_spec, tok_spec],
    )
    return pl.pallas_call(
        _rwkv_prep_kernel,
        grid_spec=grid_spec,
        out_shape=[tok, tok, tok, tok2, tok2, tok2, tok, tok],
        compiler_params=_cparams(("arbitrary",), VMEM_LIMIT),
        name="rwkv_prep",
    )(seq_tiles, z, z, z, p['mu'], p['w2bd'], p['a2bd'], p['g2'], p['w0'], p['a0'],
      p['kkp'], p['ka'], p['rk'], p['seg_ones'])


def _chunk_masks():
    t = np.arange(CH)
    fwd_incl = (t[:, None] >= t[None, :])
    out = []
    for incl in (fwd_incl, fwd_incl.T):
        strict = incl & (t[:, None] != t[None, :])
        ms = [incl, strict, strict & ((t[:, None] // 8) == (t[None, :] // 8))]
        for m in (8, 16, 32):
            ms.append(strict & ((t[:, None] // (2 * m)) == (t[None, :] // (2 * m)))
                      & ((t[:, None] // m) != (t[None, :] // m)))
        out.append(np.stack(ms))
    masks = np.stack(out).astype(np.float32)
    return np.concatenate([masks, masks], axis=-1)


def _rwkv_pair_kernel(cb, sched_ref, *refs):
    dir_refs = (refs[0:6], refs[6:12])
    m_ref, gng_ref, gnb_ref, s0_ref = refs[12:16]
    y_refs = refs[16:18]
    sfin_ref, s_scr = refs[18:20]
    step_id = pl.program_id(0)

    @pl.when(sched_ref[_SCHED_FIRST, step_id] == 1)
    def _():
        s_scr[...] = s0_ref[0]

    pw = 2 * HD
    ri = lax.broadcasted_iota(jnp.int32, (pw, pw), 0)
    ci = lax.broadcasted_iota(jnp.int32, (pw, pw), 1)
    eye_bd = (ri == ci).astype(F32)
    mask_bd = ((ri // HD) == (ci // HD)).astype(F32)
    eye12 = (lax.broadcasted_iota(jnp.int32, (CH, pw), 0)
             == lax.broadcasted_iota(jnp.int32, (CH, pw), 1) % HD).astype(F32)
    left = lax.broadcasted_iota(jnp.int32, (1, pw), 1) < HD

    bf = lambda x: x.astype(BF16)
    each = lambda f, *cols: [f(*args) for args in zip(*cols)]
    rows = lambda j: slice(j * CH, (j + 1) * CH)
    lanes_of = lambda p: slice(p * pw, (p + 1) * pw)
    stack = lambda *xs: jnp.concatenate(xs, axis=0)
    side = lambda *xs: jnp.concatenate(xs, axis=1)
    top, mid = slice(0, CH), slice(CH, 2 * CH)
    zero = jnp.zeros((), BF16)
    bd = lambda x: stack(jnp.where(left, x, zero), jnp.where(left, zero, x))

    lanes = [(d, p) for d in range(2) for p in range(HEADS // 2)]
    chains = [(d, j, p) for d, p in lanes for j in range(cb)]
    msk = lambda k: [m_ref[d, k] for d, _, _ in chains]
    incl, strict, m8 = msk(0), msk(1), msk(2)
    incl_b = each(lambda m: bf(m[:, :CH]), incl)
    get = lambda k: [dir_refs[d][k][rows(j), lanes_of(p)] if k < 3 else dir_refs[d][k][0, rows(j), lanes_of(p)]
                     for d, j, p in chains]
    R, V, KK, LD, Kd, Bd = (get(k) for k in range(6))
    L = each(lambda m, x: sum(_dot(m, part) for part in _split3(x)), incl_b, LD)
    ltot = each(lambda x: jnp.sum(x, axis=0, keepdims=True), LD)
    e_nl = each(lambda l: jnp.exp(-l), L)
    e_rest = each(lambda l, lt: jnp.exp(lt - l), L, ltot)
    Qb = each(lambda kk, l, ld: bf(kk * jnp.exp(l - ld)), KK, L, LD)
    Rh = each(lambda r, l: r * jnp.exp(l), R, L)
    QRb = each(lambda q, rh: stack(q, bf(rh)), Qb, Rh)
    Btd = each(lambda b, e: bd(bf(b * e)), Bd, e_nl)
    Ktd = each(lambda k, e: bd(bf(k * e)), Kd, e_nl)
    BcTb = each(lambda b, e: bf((b * e).T), Bd, e_rest)
    KcTb = each(lambda k, e: bf((k * e).T), Kd, e_rest)
    Vb = each(bf, V)
    QRB = each(_dot_nt, QRb, Btd)
    QRK = each(_dot_nt, QRb, Ktd)
    Nl = each(lambda m, x: m * x[top], strict, QRB)
    Mrbb = each(lambda m, x: bf(m * x[mid]), incl, QRB)
    Mkb = each(lambda m, x: bf(m * x[top]), strict, QRK)
    Mrkb = each(lambda m, x: bf(m * x[mid]), incl, QRK)
    N8 = each(lambda m, n: m * n, m8, Nl)
    N8b = each(bf, N8)
    N2 = each(lambda a: _dot(a, bd(a)), N8b)
    N2b = each(bf, N2)
    N4 = each(lambda a: _dot(a, bd(a)), N2b)
    W = each(lambda a, b: _dot(bf(eye12 - a), bd(bf(eye12 + b))), N8, N2)
    W = each(lambda w, n4: _dot(bf(w), bd(bf(eye12 + n4))), W, N4)
    for lvl in range(3):
        Wb = each(bf, W)
        T = each(lambda m, n, w: _dot(bf(m * n), bd(w)), msk(3 + lvl), Nl, Wb)
        W = each(lambda w, wb, t: w - _dot(wb, bd(bf(t))), W, Wb, T)
    Wb = each(bf, W)
    Whb = each(lambda w, q: bf(_dot(w, bd(q))), Wb, Qb)
    XV = each(lambda mk, mrk, v: _dot(stack(mk, mrk), bd(v)), Mkb, Mrkb, Vb)
    U0b = each(lambda w, x: bf(-_dot(w, bd(bf(x[top])))), Wb, XV)
    XWU = each(lambda m, wh, u: _dot(m, side(bd(wh), bd(u))), Mrbb, Whb, U0b)
    BWU = each(lambda b, wh, u: _dot(b, side(wh, u)), BcTb, Whb, U0b)
    KV = each(_dot, KcTb, Vb)
    Y0 = each(lambda xwu, xv: xwu[:, pw:] + xv[mid], XWU, XV)
    Hd = each(lambda bwu, kv: mask_bd * (bwu[:, pw:] + kv), BWU, KV)
    RG = each(lambda rh, lt, xwu, bwu: _split2(stack(rh - xwu[:, :pw],
                                                     eye_bd * jnp.exp(lt) - mask_bd * bwu[:, :pw])),
              Rh, ltot, XWU, BWU)

    ST = [s_scr[d, p] for d, p in lanes]
    for step in range(cb):
        idx = [chains.index((d, step if d == 0 else cb - 1 - step, p)) for d, p in lanes]
        XS = [_mm3(RG[i], _split2(st)) for i, st in zip(idx, ST)]
        ST = [x[CH:] + Hd[i] for i, x in zip(idx, XS)]
        for i, x in zip(idx, XS):
            d, j, p = chains[i]
            y = x[top] + Y0[i]
            half_mean = lambda a: jnp.where(left, jnp.sum(jnp.where(left, a, 0.0), axis=-1, keepdims=True),
                                            jnp.sum(jnp.where(left, 0.0, a), axis=-1, keepdims=True)) * (1.0 / HD)
            yc = y - half_mean(y)
            var = half_mean(yc * yc)
            y_refs[d][rows(j), lanes_of(p)] = yc * lax.rsqrt(var + GN_EPS) * gng_ref[p] + gnb_ref[p]
    for (d, p), st in zip(lanes, ST):
        s_scr[d, p] = st

    @pl.when(sched_ref[_SCHED_LAST, step_id] == 1)
    def _():
        for d, p in lanes:
            st = s_scr[d, p]
            sfin_ref[0, d, 2 * p] = st[:HD, :HD]
            sfin_ref[0, d, 2 * p + 1] = st[HD:, HD:]


_SCHED_FWD, _SCHED_BWD, _SCHED_SEQ, _SCHED_FIRST, _SCHED_LAST = range(5)


def rwkv_schedule(seq_lens):
    blk_rows = RWKV_CHUNKS * CH
    cols, base = [], 0
    for s, t_len in enumerate(seq_lens):
        n_b = t_len // blk_rows
        for i in range(n_b):
            cols.append((base + i, base + n_b - 1 - i, s, int(i == 0), int(i == n_b - 1)))
        base += n_b
    return np.asarray(cols, np.int32).T


def rwkv_scan(prep, masks, gng, gnb, s0t, sched):
    r, v, kk, ld, kd, bd = prep
    n = r.shape[0]
    n_seq = s0t.shape[0]
    blk_rows = RWKV_CHUNKS * CH
    pairs, pw = HEADS // 2, 2 * HD
    in_specs, args = [], []
    for d, row in ((0, _SCHED_FWD), (1, _SCHED_BWD)):
        for a in (r, v, kk):
            in_specs.append(pl.BlockSpec((blk_rows, D_MIX), lambda i, s, row=row: (s[row, i], 0)))
            args.append(a)
        for a in (ld, kd, bd):
            in_specs.append(pl.BlockSpec((1, blk_rows, D_MIX), lambda i, s, row=row, d=d: (d, s[row, i], 0)))
            args.append(a)
    state_spec = pl.BlockSpec((1, 2, pairs, pw, pw), lambda i, s: (s[_SCHED_SEQ, i], 0, 0, 0, 0))
    in_specs += [pl.BlockSpec((2, 6, CH, pw), lambda i, s: (0, 0, 0, 0)),
                 pl.BlockSpec((pairs, 1, pw), lambda i, s: (0, 0, 0)),
                 pl.BlockSpec((pairs, 1, pw), lambda i, s: (0, 0, 0)),
                 state_spec]
    args += [masks, gng, gnb, s0t]
    grid_spec = pltpu.PrefetchScalarGridSpec(
        num_scalar_prefetch=1,
        grid=(sched.shape[1],),
        in_specs=in_specs,
        out_specs=[pl.BlockSpec((blk_rows, D_MIX), lambda i, s: (s[_SCHED_FWD, i], 0)),
                   pl.BlockSpec((blk_rows, D_MIX), lambda i, s: (s[_SCHED_BWD, i], 0)),
                   pl.BlockSpec((1, 2, HEADS, HD, HD), lambda i, s: (s[_SCHED_SEQ, i], 0, 0, 0, 0))],
        scratch_shapes=[pltpu.VMEM((2, pairs, pw, pw), F32)],
    )
    return pl.pallas_call(
        functools.partial(_rwkv_pair_kernel, RWKV_CHUNKS),
        grid_spec=grid_spec,
        out_shape=[jax.ShapeDtypeStruct((n, D_MIX), F32), jax.ShapeDtypeStruct((n, D_MIX), F32),
                   jax.ShapeDtypeStruct((n_seq, 2, HEADS, HD, HD), F32)],
        compiler_params=_cparams(("arbitrary",), VMEM_LIMIT),
        name="rwkv_scan",
    )(jnp.asarray(sched), *args)


def rwkv_pack_state(s):
    n = s.shape[0]
    st = jnp.swapaxes(s, -1, -2).reshape(n, 2, HEADS // 2, 2, HD, HD)
    eye2 = jnp.eye(2, dtype=s.dtype)
    return jnp.einsum('ndpakv,ab->ndpakbv', st, eye2).reshape(n, 2, HEADS // 2, 2 * HD, 2 * HD)


def _ctx_attn_kernel(q_ref, k_ref, v_ref, y_ref, ko_ref, vo_ref):
    scale = HD ** -0.5
    for h in range(HEADS):
        sl = slice(h * HD, (h + 1) * HD)
        q = q_ref[:, sl]
        k = k_ref[:, sl]
        v = v_ref[:, sl]
        ko_ref[0, h] = k
        vo_ref[0, h] = v
        s = _dot_nt(q.astype(BF16), k.astype(BF16)) * scale
        m = jnp.max(s, axis=-1, keepdims=True)
        e = jnp.exp(s - m)
        p = e / jnp.sum(e, axis=-1, keepdims=True)
        y_ref[:, sl] = _dot(p.astype(BF16), v.astype(BF16))


def ctx_attention(z, bsz, t_len):
    qb = A_PAD // D_MIX
    return pl.pallas_call(
        _ctx_attn_kernel,
        grid=(bsz,),
        in_specs=[pl.BlockSpec((t_len, D_MIX), lambda b: (b, qb)),
                  pl.BlockSpec((t_len, D_MIX), lambda b: (b, qb + 1)),
                  pl.BlockSpec((t_len, D_MIX), lambda b: (b, qb + 2))],
        out_specs=[pl.BlockSpec((t_len, D_MIX), lambda b: (b, 0)),
                   pl.BlockSpec((1, HEADS, t_len, HD), lambda b: (b, 0, 0, 0)),
                   pl.BlockSpec((1, HEADS, t_len, HD), lambda b: (b, 0, 0, 0))],
        out_shape=[jax.ShapeDtypeStruct((bsz * t_len, D_MIX), F32),
                   jax.ShapeDtypeStruct((bsz, HEADS, t_len, HD), F32),
                   jax.ShapeDtypeStruct((bsz, HEADS, t_len, HD), F32)],
        compiler_params=_cparams(("arbitrary",)),
        name="ctx_attention",
    )(z, z, z)


def _na_kernel(rows, nblk, *refs):
    q_ref = refs[0]
    k_refs = refs[1:1 + nblk]
    v_refs = refs[1 + nblk:1 + 2 * nblk]
    kc_ref, vc_ref, tab_ref, y_ref = refs[1 + 2 * nblk:]
    r0 = pl.program_id(1) * NA_QROWS
    u0 = jnp.clip(r0 - WIN_R // 2, 0, rows - NA_KROWS)
    scale = HD ** -0.5
    kwin = jnp.concatenate([kr[...] for kr in k_refs], axis=0).astype(BF16)
    vwin = jnp.concatenate([vr[...] for vr in v_refs], axis=0).astype(BF16)
    left = lax.broadcasted_iota(jnp.int32, (1, 2 * GRID_W), 1) < GRID_W
    bias = []
    for i in range(NA_QROWS):
        r = r0 + i
        rs = jnp.clip(r - WIN_R // 2, 0, rows - WIN_R)
        per_pair = []
        for jp in range(NA_KROWS // 2):
            kr = u0 + 2 * jp
            off = [jnp.where(jnp.logical_and(kr + e >= rs, kr + e < rs + WIN_R), 0.0, NEG) for e in range(2)]
            per_pair.append((jnp.clip(kr - r + WIN_R, 0, 2 * WIN_R - 1), jnp.where(left, off[0], off[1])))
        bias.append(per_pair)
    for h in range(HEADS):
        sl = slice(h * HD, (h + 1) * HD)
        q = q_ref[:, sl].astype(BF16)
        s_raw = _dot_nt(q, kwin[:, sl]) * scale
        s_loc = jnp.concatenate([
            jnp.concatenate([s_raw[i * GRID_W:(i + 1) * GRID_W, jp * 2 * GRID_W:(jp + 1) * 2 * GRID_W]
                             + tab_ref[h, bias[i][jp][0]] + bias[i][jp][1]
                             for jp in range(NA_KROWS // 2)], axis=1)
            for i in range(NA_QROWS)], axis=0)
        s_ctx = _dot_nt(q, kc_ref[0, 0, h].astype(BF16)) * scale
        m = jnp.maximum(jnp.max(s_loc, axis=-1, keepdims=True), jnp.max(s_ctx, axis=-1, keepdims=True))
        e_loc = jnp.exp(s_loc - m)
        e_ctx = jnp.exp(s_ctx - m)
        den = jnp.sum(e_loc, axis=-1, keepdims=True) + jnp.sum(e_ctx, axis=-1, keepdims=True)
        p_loc = (e_loc / den).astype(BF16)
        p_ctx = (e_ctx / den).astype(BF16)
        y_ref[:, sl] = _dot(p_loc, vwin[:, sl]) + _dot(p_ctx, vc_ref[0, 0, h].astype(BF16))


def na_bias_table(rpb):
    cq = np.arange(GRID_W)[:, None]
    ck = np.arange(GRID_W)[None, :]
    cs = np.clip(cq - WIN_C // 2, 0, GRID_W - WIN_C)
    col_bias = np.where((ck >= cs) & (ck < cs + WIN_C), 0.0, NEG).astype(np.float32)
    col_idx = np.clip(ck - cq + WIN_C - 1, 0, 2 * WIN_C - 2)
    rpb_col = rpb.astype(F32)[:, :, col_idx] + col_bias
    padded = jnp.pad(rpb_col, ((0, 0), (1, 1), (0, 0), (0, 0)), constant_values=NEG)
    return jnp.concatenate([padded[:, :-1], padded[:, 1:]], axis=-1)


def na_attention(z, k_ctx, v_ctx, layer, bias_tab, bsz, t_len):
    rows = t_len // GRID_W
    assert rows >= NA_KROWS and rows % NA_QROWS == 0, "latent grid too small for the row-group tiling"
    qb = A_PAD // D_MIX
    past = k_ctx.shape[3]
    blk_rows = WIN_R // 2
    nblk = NA_KROWS // blk_rows
    blk_tok = blk_rows * GRID_W
    q_tok = NA_QROWS * GRID_W

    def win_spec(j, col):
        def index(b, g):
            u0 = jnp.clip(g * NA_QROWS - WIN_R // 2, 0, rows - NA_KROWS)
            return (b * (rows // blk_rows) + u0 // blk_rows + j, col)
        return pl.BlockSpec((blk_tok, D_MIX), index)

    in_specs = ([pl.BlockSpec((q_tok, D_MIX), lambda b, g: (b * (rows // NA_QROWS) + g, qb))]
                + [win_spec(j, qb + 1) for j in range(nblk)]
                + [win_spec(j, qb + 2) for j in range(nblk)]
                + [pl.BlockSpec((1, 1, HEADS, past, HD), lambda b, g: (b, layer, 0, 0, 0)),
                   pl.BlockSpec((1, 1, HEADS, past, HD), lambda b, g: (b, layer, 0, 0, 0)),
                   pl.BlockSpec(bias_tab.shape, lambda b, g: (0, 0, 0, 0))])
    return pl.pallas_call(
        functools.partial(_na_kernel, rows, nblk),
        grid=(bsz, rows // NA_QROWS),
        in_specs=in_specs,
        out_specs=pl.BlockSpec((q_tok, D_MIX), lambda b, g: (b * (rows // NA_QROWS) + g, 0)),
        out_shape=jax.ShapeDtypeStruct((bsz * t_len, D_MIX), F32),
        compiler_params=_cparams(("arbitrary", "arbitrary"), VMEM_LIMIT),
        name="na_attention",
    )(*([z] * (1 + 2 * nblk)), k_ctx, v_ctx, bias_tab)


def _s5_kernel(bsz, tb, uf_ref, ub_ref, bm_ref, cm_ref, ar_ref, ai_ref, x0_ref, yf_ref, yb_ref, xf_ref,
               x_scr, xo_scr, carry_scr):
    i = pl.program_id(1)

    @pl.when(i == 0)
    def _():
        carry_scr[...] = x0_ref[0]

    n_re = S5_GW // LANE
    n_rows = S5_GB * bsz
    assert n_rows == SUBLANE
    blk = bsz * tb
    group = min(n_re, S5_CARRY_VREGS // 2)
    ch = D_MIX // S5_GB
    lane = lambda c: slice(c * LANE, (c + 1) * LANE)
    steps_of = lambda row: pl.ds(row, tb, stride=n_rows)
    rows_of = lambda t: pl.ds(pl.multiple_of(t * n_rows, SUBLANE), n_rows)
    for d, (u_ref, y_ref) in enumerate(((uf_ref, yf_ref), (ub_ref, yb_ref))):
        u = u_ref[...].reshape(blk, D_MIX).astype(BF16)
        for k in range(S5_GB):
            bu = _dot(u[:, k * ch:(k + 1) * ch], bm_ref[d, k])
            for c in range(2 * n_re):
                for b in range(bsz):
                    x_scr[c, steps_of(k * bsz + b), :] = bu[b * tb:(b + 1) * tb, lane(c)]
        for c0 in range(0, n_re, group):
            tiles = list(range(c0, c0 + group))
            ar = [ar_ref[d, :, lane(c)] for c in tiles]
            ai = [ai_ref[d, :, lane(c)] for c in tiles]

            def step(s, x, d=d, tiles=tiles, ar=ar, ai=ai):
                t = s if d == 0 else tb - 1 - s
                rows = rows_of(t)
                out = []
                for c, a_r, a_i, (xr, xi) in zip(tiles, ar, ai, x):
                    nr = a_r * xr - a_i * xi + x_scr[c, rows, :]
                    ni = a_r * xi + a_i * xr + x_scr[n_re + c, rows, :]
                    xo_scr[c, rows, :] = nr
                    xo_scr[n_re + c, rows, :] = ni
                    out.append((nr, ni))
                return tuple(out)

            init = tuple((carry_scr[d, :, lane(c)], carry_scr[d, :, lane(n_re + c)]) for c in tiles)
            fin = lax.fori_loop(0, tb, step, init, unroll=4)
            for c, (xr, xi) in zip(tiles, fin):
                carry_scr[d, :, lane(c)] = xr
                carry_scr[d, :, lane(n_re + c)] = xi
        ys = []
        for k in range(S5_GB):
            xs = jnp.concatenate(
                [jnp.concatenate([xo_scr[c, steps_of(k * bsz + b), :] for b in range(bsz)], axis=0)
                 for c in range(2 * n_re)], axis=-1)
            ys.append(_dot(xs.astype(BF16), cm_ref[k]))
        y_ref[...] = jnp.concatenate(ys, axis=-1).reshape(bsz, tb, D_MIX)

    @pl.when(i == pl.num_programs(1) - 1)
    def _():
        xf_ref[0] = carry_scr[...]


def s5_scan(z3, bmat, cmat, a_re, a_im, x0, tb, grp):
    bsz, t_len, _ = z3.shape
    n_t = t_len // tb
    n_rows = S5_GB * grp
    ucol = (A_PAD + 3 * D_MIX) // D_MIX
    full = lambda shape: pl.BlockSpec(shape, lambda g, i: (0,) * len(shape))
    state_spec = pl.BlockSpec((1, 2, n_rows, 2 * S5_GW), lambda g, i: (g, 0, 0, 0))
    y_shape = jax.ShapeDtypeStruct((bsz, t_len, D_MIX), F32)
    return pl.pallas_call(
        functools.partial(_s5_kernel, grp, tb),
        grid=(bsz // grp, n_t),
        in_specs=[pl.BlockSpec((grp, tb, D_MIX), lambda g, i: (g, i, ucol)),
                  pl.BlockSpec((grp, tb, D_MIX), lambda g, i: (g, n_t - 1 - i, ucol)),
                  full((2, S5_GB, D_MIX // S5_GB, 2 * S5_GW)), full((S5_GB, 2 * S5_GW, D_MIX // S5_GB)),
                  full((2, n_rows, S5_GW)), full((2, n_rows, S5_GW)), state_spec],
        out_specs=[pl.BlockSpec((grp, tb, D_MIX), lambda g, i: (g, i, 0)),
                   pl.BlockSpec((grp, tb, D_MIX), lambda g, i: (g, n_t - 1 - i, 0)),
                   state_spec],
        out_shape=[y_shape, y_shape, jax.ShapeDtypeStruct((bsz // grp, 2, n_rows, 2 * S5_GW), F32)],
        scratch_shapes=[pltpu.VMEM((2 * S5_GW // LANE, n_rows * tb, LANE), F32),
                        pltpu.VMEM((2 * S5_GW // LANE, n_rows * tb, LANE), F32),
                        pltpu.VMEM((2, n_rows, 2 * S5_GW), F32)],
        compiler_params=_cparams(("arbitrary", "arbitrary"), VMEM_LIMIT),
        name="s5_scan",
    )(z3, z3, bmat, cmat, a_re, a_im, x0)


def s5_rows(x, grp):
    n_g = x.shape[1] // grp
    return (x.reshape(2, n_g, grp, S5_GB, S5_GW).transpose(1, 0, 3, 2, 4)
            .reshape(n_g, 2, S5_GB * grp, S5_GW))


def s5_unrows(x, grp):
    n_g = x.shape[0]
    return x.reshape(n_g, 2, S5_GB, grp, S5_GW).transpose(1, 0, 3, 2, 4).reshape(2, n_g * grp, S5_N)


def _merge_kernel(gnf_ref, gnb_ref, bonus_ref, g_ref, yb_ref, ycf_ref, ycb_ref, u_ref, zg0_ref, zg1_ref, zg2_ref,
                  x_ref, mod_ref, s5d_ref, wglu_ref, bglu_ref, wb_ref, wout_ref, lng_ref, lnb_ref, x1_ref, h2_ref):
    ya = (gnf_ref[...] + gnb_ref[...] + bonus_ref[...]) * g_ref[...]
    yc = ycf_ref[...] + ycb_ref[...] + s5d_ref[...] * u_ref[...]
    yc = 0.5 * yc * (1.0 + jnp.tanh(math.sqrt(2.0 / math.pi) * (yc + 0.044715 * (yc * yc * yc))))
    yc = yc * _sigmoid(_dot(yc.astype(BF16), wglu_ref[...]) + bglu_ref[...])
    merged = (_dot(ya.astype(BF16), wb_ref[0]) * _sigmoid(zg0_ref[...])
              + _dot(yb_ref[...].astype(BF16), wb_ref[1]) * _sigmoid(zg1_ref[...])
              + _dot(yc.astype(BF16), wb_ref[2]) * _sigmoid(zg2_ref[...]))
    mo = _dot(merged.astype(BF16), wout_ref[...])
    m = mod_ref[0]
    x1 = _layer_norm(DN_ALPHA * x_ref[...] + m[2:3] * mo, lng_ref[...], lnb_ref[...])
    x1_ref[...] = x1
    h2_ref[...] = (x1 * (1.0 + m[4:5]) + m[3:4]).astype(BF16)


def merge_branches(gn, bonus, g, yb, yc, z, x, mods, mod_of_tile, p):
    n = x.shape[0]
    row = lambda w, col=0: pl.BlockSpec((TM, w), lambda i, col=col: (i, col))
    full = lambda shape: pl.BlockSpec(shape, lambda i: (0,) * len(shape))
    gb = (A_PAD + 4 * D_MIX) // D_MODEL
    return pl.pallas_call(
        _merge_kernel,
        grid=(n // TM,),
        in_specs=[row(D_MIX), row(D_MIX), row(D_MIX), row(D_MIX), row(D_MIX), row(D_MIX), row(D_MIX),
                  row(D_MIX, A_PAD // D_MIX + 3),
                  row(D_MODEL, gb), row(D_MODEL, gb + 1), row(D_MODEL, gb + 2),
                  row(D_MODEL),
                  pl.BlockSpec((1, SUBLANE, D_MODEL), lambda i: (mod_of_tile(i, TM), 0, 0)),
                  full((1, D_MIX)), full((D_MIX, D_MIX)), full((1, D_MIX)),
                  full((3, D_MIX, D_MODEL)), full((D_MODEL, D_MODEL)),
                  full((1, D_MODEL)), full((1, D_MODEL))],
        out_specs=[row(D_MODEL), row(D_MODEL)],
        out_shape=[jax.ShapeDtypeStruct((n, D_MODEL), F32), jax.ShapeDtypeStruct((n, D_MODEL), BF16)],
        compiler_params=_cparams(("arbitrary",), VMEM_LIMIT),
        name="merge_branches",
    )(gn[0], gn[1], bonus, g, yb, yc[0], yc[1], z, z, z, z, x, mods,
      p['s5_d'], p['w_glu'], p['b_glu'], p['w_branch'], p['w_out'], p['ln1_g'], p['ln1_b'])


def _first_max(val, idx, big):
    m = jnp.max(jnp.max(val, axis=1, keepdims=True), axis=0, keepdims=True)
    cand = jnp.where(val == m, idx, big)
    first = jnp.min(jnp.min(cand, axis=1, keepdims=True), axis=0, keepdims=True)
    return m, idx == first


def _router_kernel(h_ref, wt_ref, bias_ref, gates_ref):
    per = N_EXPERTS // N_GROUPS
    logits = _dot_nt(wt_ref[...], h_ref[...])
    n = logits.shape[1]
    scores = _sigmoid(logits).reshape(N_GROUPS, per, n)
    sel = scores + bias_ref[...]
    e_idx = (lax.broadcasted_iota(jnp.int32, (N_GROUPS, per, n), 0) * per
             + lax.broadcasted_iota(jnp.int32, (N_GROUPS, per, n), 1))
    in_grp = lax.broadcasted_iota(jnp.int32, (N_GROUPS, per, n), 1)
    m1 = jnp.max(sel, axis=1, keepdims=True)
    first = jnp.min(jnp.where(sel == m1, in_grp, per), axis=1, keepdims=True)
    m2 = jnp.max(jnp.where(in_grp == first, -jnp.inf, sel), axis=1, keepdims=True)
    grp = m1 + m2
    g_idx = lax.broadcasted_iota(jnp.int32, (N_GROUPS, 1, n), 0)
    gmask = jnp.zeros((N_GROUPS, 1, n), F32)
    for _ in range(TOPK_GROUPS):
        _, hit = _first_max(grp, g_idx, N_GROUPS)
        gmask = jnp.where(hit, 1.0, gmask)
        grp = jnp.where(hit, -jnp.inf, grp)
    cur = jnp.where(jnp.broadcast_to(gmask, sel.shape) > 0.0, sel, NEG)
    w = jnp.zeros((N_GROUPS, per, n), F32)
    for _ in range(TOP_K):
        _, hit = _first_max(cur, e_idx, N_EXPERTS)
        w = jnp.where(hit, scores, w)
        cur = jnp.where(hit, -jnp.inf, cur)
    tot = jnp.sum(jnp.sum(w, axis=1, keepdims=True), axis=0, keepdims=True)
    gates = (ROUTED_SCALE * w / tot).reshape(N_EXPERTS, n)
    hi = gates.astype(BF16).astype(F32)
    lo = (gates - hi).astype(BF16).astype(F32)
    gates_ref[...] = jnp.concatenate([hi, lo], axis=0).T.astype(BF16)


def moe_router(h2, router_wt, router_bias):
    n = h2.shape[0]
    return pl.pallas_call(
        _router_kernel,
        grid=(n // TM,),
        in_specs=[pl.BlockSpec((TM, D_MODEL), lambda i: (i, 0)),
                  pl.BlockSpec((N_EXPERTS, D_MODEL), lambda i: (0, 0)),
                  pl.BlockSpec((N_GROUPS, N_EXPERTS // N_GROUPS, 1), lambda i: (0, 0, 0))],
        out_specs=pl.BlockSpec((TM, 2 * N_EXPERTS), lambda i: (i, 0)),
        out_shape=jax.ShapeDtypeStruct((n, 2 * N_EXPERTS), BF16),
        compiler_params=_cparams(("arbitrary",)),
        name="moe_router",
    )(h2, router_wt, router_bias)


def _moe_kernel(ec, h_ref, gates_ref, x1_ref, mod_ref, wg_ref, wu_ref, wd_ref, ex_ref,
                sg_ref, su_ref, sd_ref, lng_ref, lnb_ref, out_ref, acc_scr):
    j = pl.program_id(1)
    h = h_ref[...]

    @pl.when(j == 0)
    def _():
        sh = _dot(h, sg_ref[...])
        sh = sh * _sigmoid(sh) * _dot(h, su_ref[...])
        acc_scr[...] = _dot(sh.astype(BF16), sd_ref[...])

    gexp = _dot(gates_ref[...], ex_ref[0])
    wg = jnp.concatenate([wg_ref[e] for e in range(ec)], axis=1)
    wu = jnp.concatenate([wu_ref[e] for e in range(ec)], axis=1)
    hg = _dot(h, wg)
    hu = _dot(h, wu)
    act = hg * _sigmoid(hg) * hu * gexp
    acc_scr[...] += _dot(act.astype(BF16), wd_ref[...])

    @pl.when(j == pl.num_programs(1) - 1)
    def _():
        m = mod_ref[0]
        out_ref[...] = _layer_norm(DN_ALPHA * x1_ref[...] + m[5:6] * acc_scr[...], lng_ref[...], lnb_ref[...])


def moe_ffn(h2, gates, x1, mods, mod_of_tile, p, tm=1024, ec=MOE_EC):
    n = h2.shape[0]
    wcols = ec * D_EXPERT
    n_j = N_EXPERTS // ec
    full = lambda shape: pl.BlockSpec(shape, lambda i, j: (0,) * len(shape))
    return pl.pallas_call(
        functools.partial(_moe_kernel, ec),
        grid=(n // tm, n_j),
        in_specs=[pl.BlockSpec((tm, D_MODEL), lambda i, j: (i, 0)),
                  pl.BlockSpec((tm, 2 * N_EXPERTS), lambda i, j: (i, 0)),
                  pl.BlockSpec((tm, D_MODEL), lambda i, j: (i, 0)),
                  pl.BlockSpec((1, SUBLANE, D_MODEL), lambda i, j: (mod_of_tile(i, tm), 0, 0)),
                  pl.BlockSpec((ec, D_MODEL, D_EXPERT), lambda i, j: (j, 0, 0)),
                  pl.BlockSpec((ec, D_MODEL, D_EXPERT), lambda i, j: (j, 0, 0)),
                  pl.BlockSpec((wcols, D_MODEL), lambda i, j: (j, 0)),
                  pl.BlockSpec((1, 2 * N_EXPERTS, wcols), lambda i, j: (j, 0, 0)),
                  full((D_MODEL, D_SHARED)), full((D_MODEL, D_SHARED)), full((D_SHARED, D_MODEL)),
                  full((1, D_MODEL)), full((1, D_MODEL))],
        out_specs=pl.BlockSpec((tm, D_MODEL), lambda i, j: (i, 0)),
        out_shape=jax.ShapeDtypeStruct((n, D_MODEL), F32),
        scratch_shapes=[pltpu.VMEM((tm, D_MODEL), F32)],
        compiler_params=_cparams(("arbitrary", "arbitrary"), VMEM_LIMIT),
        name="moe_ffn",
    )(h2, gates, x1, mods, p['wg'], p['wu'], p['wd'], p['expand'],
      p['sh_g'], p['sh_u'], p['sh_d'], p['ln2_g'], p['ln2_b'])


def _block_diag2(m):
    z = jnp.zeros_like(m[0])
    return jnp.concatenate([jnp.concatenate([m[0], z], axis=1), jnp.concatenate([z, m[1]], axis=1)], axis=0)


_SMALL_PARAMS = ('rwkv_mu', 'rwkv_w2', 'rwkv_a2', 'rwkv_w0', 'rwkv_a0', 'rwkv_kk', 'rwkv_ka', 'rwkv_rk',
                 'rwkv_gn_g', 'rwkv_gn_b', 's5_a_re', 's5_a_im', 's5_log_dt', 's5_b_re', 's5_b_im', 's5_c_re',
                 's5_c_im', 's5_d', 's5_b_glu', 'ln1_g', 'ln1_b', 'router_bias', 'ln2_g', 'ln2_b', 'na_rpb')


def _small_params(g):
    p = {}
    p['mu'] = jnp.concatenate([g['rwkv_mu'], jnp.zeros((A_PAD - A_COLS,), F32)])[None, :]
    p['w2bd'] = _block_diag2(g['rwkv_w2']).astype(BF16)
    p['a2bd'] = _block_diag2(g['rwkv_a2']).astype(BF16)
    p['w0'] = g['rwkv_w0'].reshape(1, 2 * D_MIX)
    p['a0'] = g['rwkv_a0'].reshape(1, 2 * D_MIX)
    p['kkp'] = g['rwkv_kk'][None, :]
    p['ka'] = g['rwkv_ka'][None, :]
    p['rk'] = g['rwkv_rk'][None, :]
    p['gng'] = g['rwkv_gn_g'].reshape(HEADS // 2, 1, 2 * HD)
    p['gnb'] = g['rwkv_gn_b'].reshape(HEADS // 2, 1, 2 * HD)
    p['na_bias'] = na_bias_table(g['na_rpb'])
    a = lax.complex(g['s5_a_re'], g['s5_a_im'])
    dt = jnp.exp(g['s5_log_dt'])[..., None]
    a_bar = jnp.exp(dt * a)
    b_bar = ((a_bar - 1.0) / a)[..., None] * lax.complex(g['s5_b_re'], g['s5_b_im'])
    eye_g = jnp.eye(C_GROUPS // S5_GB, dtype=F32)
    gpb = C_GROUPS // S5_GB
    bd = lambda m: jnp.einsum('dkgph,gj->dkghjp', m.reshape(2, S5_GB, gpb, C_STATE, C_GROUP),
                              eye_g).reshape(2, S5_GB, D_MIX // S5_GB, S5_GW)
    p['s5_bmat'] = jnp.concatenate([bd(b_bar.real), bd(b_bar.imag)], axis=-1).astype(BF16)
    cd = lambda m: jnp.einsum('kghp,gj->kjpgh', m.reshape(S5_GB, gpb, C_GROUP, C_STATE),
                              eye_g).reshape(S5_GB, S5_GW, D_MIX // S5_GB)
    p['s5_cmat'] = jnp.concatenate([cd(g['s5_c_re']), -cd(g['s5_c_im'])], axis=1).astype(BF16)
    p['s5_ar'] = a_bar.real.reshape(2, 1, S5_N)
    p['s5_ai'] = a_bar.imag.reshape(2, 1, S5_N)
    p['s5_d'] = g['s5_d'][None, :]
    p['b_glu'] = g['s5_b_glu'][None, :]
    p['ln1_g'] = g['ln1_g'][None, :]
    p['ln1_b'] = g['ln1_b'][None, :]
    p['router_bias'] = g['router_bias'].reshape(N_GROUPS, N_EXPERTS // N_GROUPS, 1)
    p['ln2_g'] = g['ln2_g'][None, :]
    p['ln2_b'] = g['ln2_b'][None, :]
    return p


def _layer_weights(P, l):
    g = lambda name: P[name][l]
    w_in = g('w_in')
    pad = jnp.zeros((D_MODEL, A_PAD - A_COLS), F32)
    p = {}
    p['w_in'] = jnp.concatenate([w_in[:, :A_COLS], pad, w_in[:, A_COLS:]], axis=1).astype(BF16)
    p['g2'] = g('rwkv_g2').astype(BF16)
    hid = np.arange(D_MIX) // HD
    p['seg_ones'] = jnp.asarray((hid[:, None] == hid[None, :]).astype(np.float32), dtype=BF16)
    p['w_glu'] = g('s5_w_glu').astype(BF16)
    p['w_branch'] = g('w_branch').astype(BF16)
    p['w_out'] = g('w_out').astype(BF16)
    p['router_wt'] = g('router_w').T.astype(BF16)
    ecols = N_EXPERTS * D_EXPERT
    p['wg'] = g('exp_w_gate').astype(BF16)
    p['wu'] = g('exp_w_up').astype(BF16)
    p['wd'] = g('exp_w_down').reshape(ecols, D_MODEL).astype(BF16)
    p['sh_g'] = g('sh_w_gate').astype(BF16)
    p['sh_u'] = g('sh_w_up').astype(BF16)
    p['sh_d'] = g('sh_w_down').astype(BF16)
    return p


def _expand_table(ec):
    n_j = N_EXPERTS // ec
    t = np.zeros((n_j, 2 * N_EXPERTS, ec * D_EXPERT), np.float32)
    for e in range(N_EXPERTS):
        j, q = divmod(e, ec)
        t[j, e, q * D_EXPERT:(q + 1) * D_EXPERT] = 1.0
        t[j, N_EXPERTS + e, q * D_EXPERT:(q + 1) * D_EXPERT] = 1.0
    return jnp.asarray(t, dtype=BF16)


def kernel(x_prompt, x_sample, c, cache_na_k, cache_na_v, state_rwkv, state_s5_re, state_s5_im, c_ctx, w_ada, b_ada, w_in, rwkv_mu, rwkv_w0, rwkv_w2, rwkv_a0, rwkv_a2, rwkv_g2, rwkv_kk, rwkv_ka, rwkv_rk, rwkv_gn_g, rwkv_gn_b, na_rpb, s5_a_re, s5_a_im, s5_log_dt, s5_b_re, s5_b_im, s5_c_re, s5_c_im, s5_d, s5_w_glu, s5_b_glu, w_branch, w_out, ln1_g, ln1_b, router_w, router_bias, exp_w_gate, exp_w_up, exp_w_down, sh_w_gate, sh_w_up, sh_w_down, ln2_g, ln2_b):
    P = dict(w_in=w_in, rwkv_mu=rwkv_mu, rwkv_w0=rwkv_w0, rwkv_w2=rwkv_w2, rwkv_a0=rwkv_a0, rwkv_a2=rwkv_a2,
             rwkv_g2=rwkv_g2, rwkv_kk=rwkv_kk, rwkv_ka=rwkv_ka, rwkv_rk=rwkv_rk, rwkv_gn_g=rwkv_gn_g,
             rwkv_gn_b=rwkv_gn_b, s5_a_re=s5_a_re, s5_a_im=s5_a_im, s5_log_dt=s5_log_dt, s5_b_re=s5_b_re,
             s5_b_im=s5_b_im, s5_c_re=s5_c_re, s5_c_im=s5_c_im, s5_d=s5_d, s5_w_glu=s5_w_glu,
             s5_b_glu=s5_b_glu, w_branch=w_branch, w_out=w_out, ln1_g=ln1_g, ln1_b=ln1_b, router_w=router_w,
             router_bias=router_bias, exp_w_gate=exp_w_gate, exp_w_up=exp_w_up, exp_w_down=exp_w_down,
             sh_w_gate=sh_w_gate, sh_w_up=sh_w_up, sh_w_down=sh_w_down, ln2_g=ln2_g, ln2_b=ln2_b)
    bc, tc, _ = x_prompt.shape
    bl, tl, _ = x_sample.shape
    depth = w_in.shape[0]

    cond = jnp.concatenate([c_ctx[None, :], c, jnp.zeros((SUBLANE - 1 - bl, D_MODEL), F32)], axis=0)
    ada = ada_modulation(cond, w_ada, b_ada)
    masks = jnp.asarray(_chunk_masks())
    expand = _expand_table(MOE_EC)

    paths = {
        'ctx': dict(bsz=bc, t=tc, x=x_prompt.reshape(bc * tc, D_MODEL), mod_of_tile=lambda i, tm: 0),
        'lat': dict(bsz=bl, t=tl, x=x_sample.reshape(bl * tl, D_MODEL),
                    mod_of_tile=lambda i, tm: 1 + (i * tm) // tl),
    }
    for q in paths.values():
        tiles = q['t'] // TM
        pos = np.arange(q['bsz'] * tiles) % tiles
        q['seq_tiles'] = jnp.asarray(np.concatenate([np.full_like(pos, tiles), pos]).astype(np.int32))
        q['sched'] = rwkv_schedule([q['t']] * q['bsz'])
        q['s5_tb'] = min(S5_ROWS // S5_GRP, q['t'])
        assert q['bsz'] % S5_GRP == 0 and q['s5_tb'] % SUBLANE == 0 and q['t'] % q['s5_tb'] == 0

    P['na_rpb'] = na_rpb
    small = jax.vmap(_small_params)({k: P[k] for k in _SMALL_PARAMS})
    mods_all = jnp.concatenate([ada[:, :1 + bl].reshape(depth, 1 + bl, 6, D_MODEL),
                                jnp.zeros((depth, 1 + bl, SUBLANE - 6, D_MODEL), F32)], axis=2)
    new_k, new_v, new_rwkv, new_s5 = [], [], [], []
    for l in range(depth):
        p = _layer_weights(P, l)
        p.update({k: v[l] for k, v in small.items()})
        p['expand'] = expand
        mods = mods_all[l]
        bias_tab = p['na_bias']
        for name, q in paths.items():
            bsz, t_len, x, mod_of_tile = q['bsz'], q['t'], q['x'], q['mod_of_tile']
            z = in_projection(x, mods, p['w_in'], mod_of_tile)

            r, v, kk, ld, kd, bd, g, bonus = rwkv_prep(z, q['seq_tiles'], p)
            if name == 'ctx':
                s0t = jnp.zeros((bsz, 2, HEADS // 2, 2 * HD, 2 * HD), F32)
            else:
                s0t = rwkv_pack_state(state_rwkv[:, l])
            gn_f, gn_b, s_fin = rwkv_scan((r, v, kk, ld, kd, bd), masks, p['gng'], p['gnb'], s0t, q['sched'])

            if name == 'ctx':
                yb, k_h, v_h = ctx_attention(z, bsz, t_len)
            else:
                yb = na_attention(z, cache_na_k, cache_na_v, l, bias_tab, bsz, t_len)

            if name == 'ctx':
                x0 = jnp.zeros((bsz // S5_GRP, 2, S5_GB * S5_GRP, 2 * S5_GW), F32)
            else:
                x0 = jnp.concatenate(
                    [s5_rows(jnp.swapaxes(s[:, l].reshape(bsz, 2, S5_N), 0, 1), S5_GRP)
                     for s in (state_s5_re, state_s5_im)], axis=-1)
            a_re, a_im = (s5_rows(jnp.broadcast_to(a, (2, S5_GRP, S5_N)), S5_GRP)[0]
                          for a in (p['s5_ar'], p['s5_ai']))
            yc_f, yc_b, x_fin = s5_scan(z.reshape(bsz, t_len, IN_COLS_P), p['s5_bmat'], p['s5_cmat'],
                                        a_re, a_im, x0, q['s5_tb'], S5_GRP)
            yc = (yc_f.reshape(bsz * t_len, D_MIX), yc_b.reshape(bsz * t_len, D_MIX))

            x1, h2 = merge_branches((gn_f, gn_b), bonus, g, yb, yc, z, x, mods, mod_of_tile, p)
            gates = moe_router(h2, p['router_wt'], p['router_bias'])
            q['x'] = moe_ffn(h2, gates, x1, mods, mod_of_tile, p)
            if name == 'ctx':
                new_k.append(k_h)
                new_v.append(v_h)
                new_rwkv.append(jnp.swapaxes(s_fin, -1, -2))
                new_s5.append([jnp.swapaxes(s5_unrows(part, S5_GRP), 0, 1)
                               for part in (x_fin[..., :S5_GW], x_fin[..., S5_GW:])])

    s5_re, s5_im = (jnp.stack([layer[part] for layer in new_s5], axis=1).reshape(bc, depth, 2, C_GROUPS, C_STATE)
                    for part in range(2))
    return (paths['ctx']['x'].reshape(bc, tc, D_MODEL), paths['lat']['x'].reshape(bl, tl, D_MODEL),
            jnp.stack(new_k, axis=1), jnp.stack(new_v, axis=1), jnp.stack(new_rwkv, axis=1), s5_re, s5_im)
```

```python
import functools
import math

import numpy as np
import jax
import jax.numpy as jnp
from jax import lax
from jax.experimental import pallas as pl
from jax.experimental.pallas import tpu as pltpu

F32 = jnp.float32
BF16 = jnp.bfloat16

D_MODEL = 1024
DEPTH = 2
GRID_W = 64
D_MIX = 512
HEADS = 8
HD = 64
LORA = 128
WIN_R = 8
WIN_C = 16
C_GROUP = 16
C_GROUPS = D_MIX // C_GROUP
C_STATE = 64
S5_N = C_GROUPS * C_STATE
N_EXPERTS = 64
TOP_K = 8
N_GROUPS = 8
TOPK_GROUPS = 4
D_EXPERT = 128
D_SHARED = 128
ROUTED_SCALE = 2.5
LN_EPS = 1e-5
GN_EPS = 64e-5
NEG = -1e30
DN_ALPHA = (2 * DEPTH) ** 0.25
A_COLS = 3 * D_MIX + 3 * LORA
A_PAD = 2048
IN_COLS_P = A_PAD + 4 * D_MIX + 3 * D_MODEL

LANE = 128
SUBLANE = 8
TM = 256
CH = HD
RWKV_CHUNKS = 4
MOE_EC = 8
NA_QROWS = 8
NA_KROWS = NA_QROWS + WIN_R
S5_GB = 4
S5_GW = S5_N // S5_GB
S5_GRP = SUBLANE // S5_GB
S5_ROWS = 512
S5_CARRY_VREGS = 32
VMEM_LIMIT = 56 * 1024 * 1024


def _cparams(sem, vmem=None):
    return pltpu.CompilerParams(dimension_semantics=sem, vmem_limit_bytes=vmem)


def _sigmoid(x):
    return 1.0 / (1.0 + jnp.exp(-x))


def _dot(a, b, precision=None):
    return jnp.dot(a, b, preferred_element_type=F32, precision=precision)


def _dot_nt(a, b, precision=None):
    return lax.dot_general(a, b, (((1,), (1,)), ((), ())), preferred_element_type=F32, precision=precision)


def _split2(x):
    hi = x.astype(BF16)
    return hi, (x - hi.astype(F32)).astype(BF16)


def _split3(x):
    hi = x.astype(BF16)
    r1 = x - hi.astype(F32)
    mid = r1.astype(BF16)
    return hi, mid, (r1 - mid.astype(F32)).astype(BF16)


def _seg_sum(x, ones_bf16):
    return sum(_dot(part, ones_bf16) for part in _split3(x))


def _mm3(a, b, dot=_dot):
    return dot(a[0], b[0]) + (dot(a[1], b[0]) + dot(a[0], b[1]))


def _layer_norm(x, g, b):
    mu = jnp.mean(x, axis=-1, keepdims=True)
    xc = x - mu
    var = jnp.mean(xc * xc, axis=-1, keepdims=True)
    return xc * lax.rsqrt(var + LN_EPS) * g + b


def _ada_kernel(c_ref, w_ref, b_ref, o_ref):
    c = c_ref[...]
    s = c * _sigmoid(c)
    o_ref[0] = _dot(s.astype(BF16), w_ref[0].astype(BF16)) + b_ref[0]


def ada_modulation(cond, w_ada, b_ada):
    n_l, d, n6 = w_ada.shape
    tn = 1536
    return pl.pallas_call(
        _ada_kernel,
        grid=(n_l, n6 // tn),
        in_specs=[pl.BlockSpec((SUBLANE, d), lambda l, j: (0, 0)),
                  pl.BlockSpec((1, d, tn), lambda l, j: (l, 0, j)),
                  pl.BlockSpec((1, 1, tn), lambda l, j: (l, 0, j))],
        out_specs=pl.BlockSpec((1, SUBLANE, tn), lambda l, j: (l, 0, j)),
        out_shape=jax.ShapeDtypeStruct((n_l, SUBLANE, n6), F32),
        compiler_params=_cparams(("arbitrary", "arbitrary")),
        name="ada_modulation",
    )(cond, w_ada, b_ada.reshape(n_l, 1, n6))


def _inproj_kernel(x_ref, mod_ref, w_ref, z_ref, h_scr):
    @pl.when(pl.program_id(1) == 0)
    def _():
        m = mod_ref[0]
        h_scr[...] = (x_ref[...] * (1.0 + m[1:2]) + m[0:1]).astype(BF16)
    z_ref[...] = _dot(h_scr[...], w_ref[...])


def in_projection(x, mods, w_in_p, mod_of_tile, tm=2048, tn=1024):
    n, d = x.shape
    cols = w_in_p.shape[1]
    return pl.pallas_call(
        _inproj_kernel,
        grid=(n // tm, cols // tn),
        in_specs=[pl.BlockSpec((tm, d), lambda i, j: (i, 0)),
                  pl.BlockSpec((1, SUBLANE, d), lambda i, j: (mod_of_tile(i, tm), 0, 0)),
                  pl.BlockSpec((d, tn), lambda i, j: (0, j))],
        out_specs=pl.BlockSpec((tm, tn), lambda i, j: (i, j)),
        out_shape=jax.ShapeDtypeStruct((n, cols), F32),
        scratch_shapes=[pltpu.VMEM((tm, d), BF16)],
        compiler_params=_cparams(("arbitrary", "arbitrary"), VMEM_LIMIT),
        name="in_projection",
    )(x, mods, w_in_p)


def _rwkv_prep_kernel(seq_tiles_ref, z_ref, zp_ref, zn_ref, mu_ref, w2_ref, a2_ref, g2_ref, w0_ref, a0_ref,
                      kkp_ref, ka_ref, rk_ref, e_ref,
                      r_ref, v_ref, kk_ref, ld_ref, kd_ref, bd_ref, g_ref, bonus_ref):
    i = pl.program_id(0)
    tiles = seq_tiles_ref[i]
    pos = seq_tiles_ref[i + pl.num_programs(0)]
    x = z_ref[...]
    tm = x.shape[0]
    rows = lax.broadcasted_iota(jnp.int32, x.shape, 0)
    prev_row = jnp.where(pos == 0, 0.0, zp_ref[SUBLANE - 1:SUBLANE, :])
    next_row = jnp.where(pos == tiles - 1, 0.0, zn_ref[0:1, :])
    xm1 = jnp.where(rows == 0, prev_row, pltpu.roll(x, 1, axis=0))
    xp1 = jnp.where(rows == tm - 1, next_row, pltpu.roll(x, tm - 1, axis=0))
    za = x + mu_ref[...] * (0.5 * (xm1 + xp1) - x)

    r = za[:, 0:D_MIX]
    k = za[:, D_MIX:2 * D_MIX]
    v = za[:, 2 * D_MIX:3 * D_MIX]
    lw = za[:, 3 * D_MIX:3 * D_MIX + LORA]
    la = za[:, 3 * D_MIX + LORA:3 * D_MIX + 2 * LORA]
    lg = za[:, 3 * D_MIX + 2 * LORA:3 * D_MIX + 3 * LORA]

    w_both = w0_ref[...] + _dot(jnp.tanh(lw).astype(BF16), w2_ref[...])
    a_both = _sigmoid(a0_ref[...] + _dot(la.astype(BF16), a2_ref[...]))
    g_ref[...] = _dot(_sigmoid(lg).astype(BF16), g2_ref[...])

    e = e_ref[...]
    kks = k * kkp_ref[...]
    nrm = jnp.sqrt(_seg_sum(kks * kks, e))
    kk = kks / jnp.maximum(nrm, 1e-12)
    bonus = jnp.zeros_like(v)
    r_ref[...] = r
    v_ref[...] = v
    kk_ref[...] = kk
    for d in range(2):
        w = w_both[:, d * D_MIX:(d + 1) * D_MIX]
        a = a_both[:, d * D_MIX:(d + 1) * D_MIX]
        ld = -math.exp(-0.5) * _sigmoid(w)
        kd = k * (1.0 + (a - 1.0) * ka_ref[...])
        bd = kk * a
        bonus = bonus + _seg_sum(r * kd * rk_ref[...], e) * v
        ld_ref[d] = ld
        kd_ref[d] = kd
        bd_ref[d] = bd
    bonus_ref[...] = bonus


def rwkv_prep(z, seq_tiles, p):
    n = z.shape[0]
    nt = n // TM
    halo = TM // SUBLANE
    nb8 = n // SUBLANE
    tok = jax.ShapeDtypeStruct((n, D_MIX), F32)
    tok2 = jax.ShapeDtypeStruct((2, n, D_MIX), F32)
    full = lambda shape: pl.BlockSpec(shape, lambda i, s: (0,) * len(shape))
    tok_spec = pl.BlockSpec((TM, D_MIX), lambda i, s: (i, 0))
    tok2_spec = pl.BlockSpec((2, TM, D_MIX), lambda i, s: (0, i, 0))
    grid_spec = pltpu.PrefetchScalarGridSpec(
        num_scalar_prefetch=1,
        grid=(nt,),
        in_specs=[pl.BlockSpec((TM, A_PAD), lambda i, s: (i, 0)),
                  pl.BlockSpec((SUBLANE, A_PAD), lambda i, s: (jnp.maximum(i * halo - 1, 0), 0)),
                  pl.BlockSpec((SUBLANE, A_PAD), lambda i, s: (jnp.minimum((i + 1) * halo, nb8 - 1), 0)),
                  full((1, A_PAD)), full((LORA, 2 * D_MIX)), full((LORA, 2 * D_MIX)), full((LORA, D_MIX)),
                  full((1, 2 * D_MIX)), full((1, 2 * D_MIX)), full((1, D_MIX)), full((1, D_MIX)),
                  full((1, D_MIX)), full((D_MIX, D_MIX))],
        out_specs=[tok_spec, tok_spec, tok_spec, tok2_spec, tok2_spec, tok2_spec, tok_spec, tok_spec],
    )
    return pl.pallas_call(
        _rwkv_prep_kernel,
        grid_spec=grid_spec,
        out_shape=[tok, tok, tok, tok2, tok2, tok2, tok, tok],
        compiler_params=_cparams(("arbitrary",), VMEM_LIMIT),
        name="rwkv_prep",
    )(seq_tiles, z, z, z, p['mu'], p['w2bd'], p['a2bd'], p['g2'], p['w0'], p['a0'],
      p['kkp'], p['ka'], p['rk'], p['seg_ones'])


def _chunk_masks():
    t = np.arange(CH)
    fwd_incl = (t[:, None] >= t[None, :])
    out = []
    for incl in (fwd_incl, fwd_incl.T):
        strict = incl & (t[:, None] != t[None, :])
        ms = [incl, strict, strict & ((t[:, None] // 8) == (t[None, :] // 8))]
        for m in (8, 16, 32):
            ms.append(strict & ((t[:, None] // (2 * m)) == (t[None, :] // (2 * m)))
                      & ((t[:, None] // m) != (t[None, :] // m)))
        out.append(np.stack(ms))
    masks = np.stack(out).astype(np.float32)
    return np.concatenate([masks, masks], axis=-1)


def _rwkv_pair_kernel(cb, sched_ref, *refs):
    dir_refs = (refs[0:6], refs[6:12])
    m_ref, gng_ref, gnb_ref, s0_ref = refs[12:16]
    y_refs = refs[16:18]
    sfin_ref, s_scr = refs[18:20]
    step_id = pl.program_id(0)

    @pl.when(sched_ref[_SCHED_FIRST, step_id] == 1)
    def _():
        s_scr[...] = s0_ref[0]

    pw = 2 * HD
    ri = lax.broadcasted_iota(jnp.int32, (pw, pw), 0)
    ci = lax.broadcasted_iota(jnp.int32, (pw, pw), 1)
    eye_bd = (ri == ci).astype(F32)
    mask_bd = ((ri // HD) == (ci // HD)).astype(F32)
    eye12 = (lax.broadcasted_iota(jnp.int32, (CH, pw), 0)
             == lax.broadcasted_iota(jnp.int32, (CH, pw), 1) % HD).astype(F32)
    left = lax.broadcasted_iota(jnp.int32, (1, pw), 1) < HD

    bf = lambda x: x.astype(BF16)
    each = lambda f, *cols: [f(*args) for args in zip(*cols)]
    rows = lambda j: slice(j * CH, (j + 1) * CH)
    lanes_of = lambda p: slice(p * pw, (p + 1) * pw)
    stack = lambda *xs: jnp.concatenate(xs, axis=0)
    side = lambda *xs: jnp.concatenate(xs, axis=1)
    top, mid = slice(0, CH), slice(CH, 2 * CH)
    zero = jnp.zeros((), BF16)
    bd = lambda x: stack(jnp.where(left, x, zero), jnp.where(left, zero, x))

    lanes = [(d, p) for d in range(2) for p in range(HEADS // 2)]
    chains = [(d, j, p) for d, p in lanes for j in range(cb)]
    msk = lambda k: [m_ref[d, k] for d, _, _ in chains]
    incl, strict, m8 = msk(0), msk(1), msk(2)
    incl_b = each(lambda m: bf(m[:, :CH]), incl)
    get = lambda k: [dir_refs[d][k][rows(j), lanes_of(p)] if k < 3 else dir_refs[d][k][0, rows(j), lanes_of(p)]
                     for d, j, p in chains]
    R, V, KK, LD, Kd, Bd = (get(k) for k in range(6))
    L = each(lambda m, x: sum(_dot(m, part) for part in _split3(x)), incl_b, LD)
    ltot = each(lambda x: jnp.sum(x, axis=0, keepdims=True), LD)
    e_nl = each(lambda l: jnp.exp(-l), L)
    e_rest = each(lambda l, lt: jnp.exp(lt - l), L, ltot)
    Qb = each(lambda kk, l, ld: bf(kk * jnp.exp(l - ld)), KK, L, LD)
    Rh = each(lambda r, l: r * jnp.exp(l), R, L)
    QRb = each(lambda q, rh: stack(q, bf(rh)), Qb, Rh)
    Btd = each(lambda b, e: bd(bf(b * e)), Bd, e_nl)
    Ktd = each(lambda k, e: bd(bf(k * e)), Kd, e_nl)
    BcTb = each(lambda b, e: bf((b * e).T), Bd, e_rest)
    KcTb = each(lambda k, e: bf((k * e).T), Kd, e_rest)
    Vb = each(bf, V)
    QRB = each(_dot_nt, QRb, Btd)
    QRK = each(_dot_nt, QRb, Ktd)
    Nl = each(lambda m, x: m * x[top], strict, QRB)
    Mrbb = each(lambda m, x: bf(m * x[mid]), incl, QRB)
    Mkb = each(lambda m, x: bf(m * x[top]), strict, QRK)
    Mrkb = each(lambda m, x: bf(m * x[mid]), incl, QRK)
    N8 = each(lambda m, n: m * n, m8, Nl)
    N8b = each(bf, N8)
    N2 = each(lambda a: _dot(a, bd(a)), N8b)
    N2b = each(bf, N2)
    N4 = each(lambda a: _dot(a, bd(a)), N2b)
    W = each(lambda a, b: _dot(bf(eye12 - a), bd(bf(eye12 + b))), N8, N2)
    W = each(lambda w, n4: _dot(bf(w), bd(bf(eye12 + n4))), W, N4)
    for lvl in range(3):
        Wb = each(bf, W)
        T = each(lambda m, n, w: _dot(bf(m * n), bd(w)), msk(3 + lvl), Nl, Wb)
        W = each(lambda w, wb, t: w - _dot(wb, bd(bf(t))), W, Wb, T)
    Wb = each(bf, W)
    Whb = each(lambda w, q: bf(_dot(w, bd(q))), Wb, Qb)
    XV = each(lambda mk, mrk, v: _dot(stack(mk, mrk), bd(v)), Mkb, Mrkb, Vb)
    U0b = each(lambda w, x: bf(-_dot(w, bd(bf(x[top])))), Wb, XV)
    XWU = each(lambda m, wh, u: _dot(m, side(bd(wh), bd(u))), Mrbb, Whb, U0b)
    BWU = each(lambda b, wh, u: _dot(b, side(wh, u)), BcTb, Whb, U0b)
    KV = each(_dot, KcTb, Vb)
    Y0 = each(lambda xwu, xv: xwu[:, pw:] + xv[mid], XWU, XV)
    Hd = each(lambda bwu, kv: mask_bd * (bwu[:, pw:] + kv), BWU, KV)
    RG = each(lambda rh, lt, xwu, bwu: _split2(stack(rh - xwu[:, :pw],
                                                     eye_bd * jnp.exp(lt) - mask_bd * bwu[:, :pw])),
              Rh, ltot, XWU, BWU)

    ST = [s_scr[d, p] for d, p in lanes]
    for step in range(cb):
        idx = [chains.index((d, step if d == 0 else cb - 1 - step, p)) for d, p in lanes]
        XS = [_mm3(RG[i], _split2(st)) for i, st in zip(idx, ST)]
        ST = [x[CH:] + Hd[i] for i, x in zip(idx, XS)]
        for i, x in zip(idx, XS):
            d, j, p = chains[i]
            y = x[top] + Y0[i]
            half_mean = lambda a: jnp.where(left, jnp.sum(jnp.where(left, a, 0.0), axis=-1, keepdims=True),
                                            jnp.sum(jnp.where(left, 0.0, a), axis=-1, keepdims=True)) * (1.0 / HD)
            yc = y - half_mean(y)
            var = half_mean(yc * yc)
            y_refs[d][rows(j), lanes_of(p)] = yc * lax.rsqrt(var + GN_EPS) * gng_ref[p] + gnb_ref[p]
    for (d, p), st in zip(lanes, ST):
        s_scr[d, p] = st

    @pl.when(sched_ref[_SCHED_LAST, step_id] == 1)
    def _():
        for d, p in lanes:
            st = s_scr[d, p]
            sfin_ref[0, d, 2 * p] = st[:HD, :HD]
            sfin_ref[0, d, 2 * p + 1] = st[HD:, HD:]


_SCHED_FWD, _SCHED_BWD, _SCHED_SEQ, _SCHED_FIRST, _SCHED_LAST = range(5)


def rwkv_schedule(seq_lens):
    blk_rows = RWKV_CHUNKS * CH
    cols, base = [], 0
    for s, t_len in enumerate(seq_lens):
        n_b = t_len // blk_rows
        for i in range(n_b):
            cols.append((base + i, base + n_b - 1 - i, s, int(i == 0), int(i == n_b - 1)))
        base += n_b
    return np.asarray(cols, np.int32).T


def rwkv_scan(prep, masks, gng, gnb, s0t, sched):
    r, v, kk, ld, kd, bd = prep
    n = r.shape[0]
    n_seq = s0t.shape[0]
    blk_rows = RWKV_CHUNKS * CH
    pairs, pw = HEADS // 2, 2 * HD
    in_specs, args = [], []
    for d, row in ((0, _SCHED_FWD), (1, _SCHED_BWD)):
        for a in (r, v, kk):
            in_specs.append(pl.BlockSpec((blk_rows, D_MIX), lambda i, s, row=row: (s[row, i], 0)))
            args.append(a)
        for a in (ld, kd, bd):
            in_specs.append(pl.BlockSpec((1, blk_rows, D_MIX), lambda i, s, row=row, d=d: (d, s[row, i], 0)))
            args.append(a)
    state_spec = pl.BlockSpec((1, 2, pairs, pw, pw), lambda i, s: (s[_SCHED_SEQ, i], 0, 0, 0, 0))
    in_specs += [pl.BlockSpec((2, 6, CH, pw), lambda i, s: (0, 0, 0, 0)),
                 pl.BlockSpec((pairs, 1, pw), lambda i, s: (0, 0, 0)),
                 pl.BlockSpec((pairs, 1, pw), lambda i, s: (0, 0, 0)),
                 state_spec]
    args += [masks, gng, gnb, s0t]
    grid_spec = pltpu.PrefetchScalarGridSpec(
        num_scalar_prefetch=1,
        grid=(sched.shape[1],),
        in_specs=in_specs,
        out_specs=[pl.BlockSpec((blk_rows, D_MIX), lambda i, s: (s[_SCHED_FWD, i], 0)),
                   pl.BlockSpec((blk_rows, D_MIX), lambda i, s: (s[_SCHED_BWD, i], 0)),
                   pl.BlockSpec((1, 2, HEADS, HD, HD), lambda i, s: (s[_SCHED_SEQ, i], 0, 0, 0, 0))],
        scratch_shapes=[pltpu.VMEM((2, pairs, pw, pw), F32)],
    )
    return pl.pallas_call(
        functools.partial(_rwkv_pair_kernel, RWKV_CHUNKS),
        grid_spec=grid_spec,
        out_shape=[jax.ShapeDtypeStruct((n, D_MIX), F32), jax.ShapeDtypeStruct((n, D_MIX), F32),
                   jax.ShapeDtypeStruct((n_seq, 2, HEADS, HD, HD), F32)],
        compiler_params=_cparams(("arbitrary",), VMEM_LIMIT),
        name="rwkv_scan",
    )(jnp.asarray(sched), *args)


def rwkv_pack_state(s):
    n = s.shape[0]
    st = jnp.swapaxes(s, -1, -2).reshape(n, 2, HEADS // 2, 2, HD, HD)
    eye2 = jnp.eye(2, dtype=s.dtype)
    return jnp.einsum('ndpakv,ab->ndpakbv', st, eye2).reshape(n, 2, HEADS // 2, 2 * HD, 2 * HD)


def _ctx_attn_kernel(q_ref, k_ref, v_ref, y_ref, ko_ref, vo_ref):
    scale = HD ** -0.5
    for h in range(HEADS):
        sl = slice(h * HD, (h + 1) * HD)
        q = q_ref[:, sl]
        k = k_ref[:, sl]
        v = v_ref[:, sl]
        ko_ref[0, h] = k
        vo_ref[0, h] = v
        s = _dot_nt(q.astype(BF16), k.astype(BF16)) * scale
        m = jnp.max(s, axis=-1, keepdims=True)
        e = jnp.exp(s - m)
        p = e / jnp.sum(e, axis=-1, keepdims=True)
        y_ref[:, sl] = _dot(p.astype(BF16), v.astype(BF16))


def ctx_attention(z, bsz, t_len):
    qb = A_PAD // D_MIX
    return pl.pallas_call(
        _ctx_attn_kernel,
        grid=(bsz,),
        in_specs=[pl.BlockSpec((t_len, D_MIX), lambda b: (b, qb)),
                  pl.BlockSpec((t_len, D_MIX), lambda b: (b, qb + 1)),
                  pl.BlockSpec((t_len, D_MIX), lambda b: (b, qb + 2))],
        out_specs=[pl.BlockSpec((t_len, D_MIX), lambda b: (b, 0)),
                   pl.BlockSpec((1, HEADS, t_len, HD), lambda b: (b, 0, 0, 0)),
                   pl.BlockSpec((1, HEADS, t_len, HD), lambda b: (b, 0, 0, 0))],
        out_shape=[jax.ShapeDtypeStruct((bsz * t_len, D_MIX), F32),
                   jax.ShapeDtypeStruct((bsz, HEADS, t_len, HD), F32),
                   jax.ShapeDtypeStruct((bsz, HEADS, t_len, HD), F32)],
        compiler_params=_cparams(("arbitrary",)),
        name="ctx_attention",
    )(z, z, z)


def _na_kernel(rows, nblk, *refs):
    q_ref = refs[0]
    k_refs = refs[1:1 + nblk]
    v_refs = refs[1 + nblk:1 + 2 * nblk]
    kc_ref, vc_ref, tab_ref, y_ref = refs[1 + 2 * nblk:]
    r0 = pl.program_id(1) * NA_QROWS
    u0 = jnp.clip(r0 - WIN_R // 2, 0, rows - NA_KROWS)
    scale = HD ** -0.5
    kwin = jnp.concatenate([kr[...] for kr in k_refs], axis=0).astype(BF16)
    vwin = jnp.concatenate([vr[...] for vr in v_refs], axis=0).astype(BF16)
    left = lax.broadcasted_iota(jnp.int32, (1, 2 * GRID_W), 1) < GRID_W
    bias = []
    for i in range(NA_QROWS):
        r = r0 + i
        rs = jnp.clip(r - WIN_R // 2, 0, rows - WIN_R)
        per_pair = []
        for jp in range(NA_KROWS // 2):
            kr = u0 + 2 * jp
            off = [jnp.where(jnp.logical_and(kr + e >= rs, kr + e < rs + WIN_R), 0.0, NEG) for e in range(2)]
            per_pair.append((jnp.clip(kr - r + WIN_R, 0, 2 * WIN_R - 1), jnp.where(left, off[0], off[1])))
        bias.append(per_pair)
    for h in range(HEADS):
        sl = slice(h * HD, (h + 1) * HD)
        q = q_ref[:, sl].astype(BF16)
        s_raw = _dot_nt(q, kwin[:, sl]) * scale
        s_loc = jnp.concatenate([
            jnp.concatenate([s_raw[i * GRID_W:(i + 1) * GRID_W, jp * 2 * GRID_W:(jp + 1) * 2 * GRID_W]
                             + tab_ref[h, bias[i][jp][0]] + bias[i][jp][1]
                             for jp in range(NA_KROWS // 2)], axis=1)
            for i in range(NA_QROWS)], axis=0)
        s_ctx = _dot_nt(q, kc_ref[0, 0, h].astype(BF16)) * scale
        m = jnp.maximum(jnp.max(s_loc, axis=-1, keepdims=True), jnp.max(s_ctx, axis=-1, keepdims=True))
        e_loc = jnp.exp(s_loc - m)
        e_ctx = jnp.exp(s_ctx - m)
        den = jnp.sum(e_loc, axis=-1, keepdims=True) + jnp.sum(e_ctx, axis=-1, keepdims=True)
        p_loc = (e_loc / den).astype(BF16)
        p_ctx = (e_ctx / den).astype(BF16)
        y_ref[:, sl] = _dot(p_loc, vwin[:, sl]) + _dot(p_ctx, vc_ref[0, 0, h].astype(BF16))


def na_bias_table(rpb):
    cq = np.arange(GRID_W)[:, None]
    ck = np.arange(GRID_W)[None, :]
    cs = np.clip(cq - WIN_C // 2, 0, GRID_W - WIN_C)
    col_bias = np.where((ck >= cs) & (ck < cs + WIN_C), 0.0, NEG).astype(np.float32)
    col_idx = np.clip(ck - cq + WIN_C - 1, 0, 2 * WIN_C - 2)
    rpb_col = rpb.astype(F32)[:, :, col_idx] + col_bias
    padded = jnp.pad(rpb_col, ((0, 0), (1, 1), (0, 0), (0, 0)), constant_values=NEG)
    return jnp.concatenate([padded[:, :-1], padded[:, 1:]], axis=-1)


def na_attention(z, k_ctx, v_ctx, layer, bias_tab, bsz, t_len):
    rows = t_len // GRID_W
    assert rows >= NA_KROWS and rows % NA_QROWS == 0, "latent grid too small for the row-group tiling"
    qb = A_PAD // D_MIX
    past = k_ctx.shape[3]
    blk_rows = WIN_R // 2
    nblk = NA_KROWS // blk_rows
    blk_tok = blk_rows * GRID_W
    q_tok = NA_QROWS * GRID_W

    def win_spec(j, col):
        def index(b, g):
            u0 = jnp.clip(g * NA_QROWS - WIN_R // 2, 0, rows - NA_KROWS)
            return (b * (rows // blk_rows) + u0 // blk_rows + j, col)
        return pl.BlockSpec((blk_tok, D_MIX), index)

    in_specs = ([pl.BlockSpec((q_tok, D_MIX), lambda b, g: (b * (rows // NA_QROWS) + g, qb))]
                + [win_spec(j, qb + 1) for j in range(nblk)]
                + [win_spec(j, qb + 2) for j in range(nblk)]
                + [pl.BlockSpec((1, 1, HEADS, past, HD), lambda b, g: (b, layer, 0, 0, 0)),
                   pl.BlockSpec((1, 1, HEADS, past, HD), lambda b, g: (b, layer, 0, 0, 0)),
                   pl.BlockSpec(bias_tab.shape, lambda b, g: (0, 0, 0, 0))])
    return pl.pallas_call(
        functools.partial(_na_kernel, rows, nblk),
        grid=(bsz, rows // NA_QROWS),
        in_specs=in_specs,
        out_specs=pl.BlockSpec((q_tok, D_MIX), lambda b, g: (b * (rows // NA_QROWS) + g, 0)),
        out_shape=jax.ShapeDtypeStruct((bsz * t_len, D_MIX), F32),
        compiler_params=_cparams(("arbitrary", "arbitrary"), VMEM_LIMIT),
        name="na_attention",
    )(*([z] * (1 + 2 * nblk)), k_ctx, v_ctx, bias_tab)


def _s5_kernel(bsz, tb, uf_ref, ub_ref, bm_ref, cm_ref, ar_ref, ai_ref, x0_ref, yf_ref, yb_ref, xf_ref,
               x_scr, xo_scr, carry_scr):
    i = pl.program_id(1)

    @pl.when(i == 0)
    def _():
        carry_scr[...] = x0_ref[0]

    n_re = S5_GW // LANE
    n_rows = S5_GB * bsz
    assert n_rows == SUBLANE
    blk = bsz * tb
    group = min(n_re, S5_CARRY_VREGS // 2)
    ch = D_MIX // S5_GB
    lane = lambda c: slice(c * LANE, (c + 1) * LANE)
    steps_of = lambda row: pl.ds(row, tb, stride=n_rows)
    rows_of = lambda t: pl.ds(pl.multiple_of(t * n_rows, SUBLANE), n_rows)
    for d, (u_ref, y_ref) in enumerate(((uf_ref, yf_ref), (ub_ref, yb_ref))):
        u = u_ref[...].reshape(blk, D_MIX).astype(BF16)
        for k in range(S5_GB):
            bu = _dot(u[:, k * ch:(k + 1) * ch], bm_ref[d, k])
            for c in range(2 * n_re):
                for b in range(bsz):
                    x_scr[c, steps_of(k * bsz + b), :] = bu[b * tb:(b + 1) * tb, lane(c)]
        for c0 in range(0, n_re, group):
            tiles = list(range(c0, c0 + group))
            ar = [ar_ref[d, :, lane(c)] for c in tiles]
            ai = [ai_ref[d, :, lane(c)] for c in tiles]

            def step(s, x, d=d, tiles=tiles, ar=ar, ai=ai):
                t = s if d == 0 else tb - 1 - s
                rows = rows_of(t)
                out = []
                for c, a_r, a_i, (xr, xi) in zip(tiles, ar, ai, x):
                    nr = a_r * xr - a_i * xi + x_scr[c, rows, :]
                    ni = a_r * xi + a_i * xr + x_scr[n_re + c, rows, :]
                    xo_scr[c, rows, :] = nr
                    xo_scr[n_re + c, rows, :] = ni
                    out.append((nr, ni))
                return tuple(out)

            init = tuple((carry_scr[d, :, lane(c)], carry_scr[d, :, lane(n_re + c)]) for c in tiles)
            fin = lax.fori_loop(0, tb, step, init, unroll=4)
            for c, (xr, xi) in zip(tiles, fin):
                carry_scr[d, :, lane(c)] = xr
                carry_scr[d, :, lane(n_re + c)] = xi
        ys = []
        for k in range(S5_GB):
            xs = jnp.concatenate(
                [jnp.concatenate([xo_scr[c, steps_of(k * bsz + b), :] for b in range(bsz)], axis=0)
                 for c in range(2 * n_re)], axis=-1)
            ys.append(_dot(xs.astype(BF16), cm_ref[k]))
        y_ref[...] = jnp.concatenate(ys, axis=-1).reshape(bsz, tb, D_MIX)

    @pl.when(i == pl.num_programs(1) - 1)
    def _():
        xf_ref[0] = carry_scr[...]


def s5_scan(z3, bmat, cmat, a_re, a_im, x0, tb, grp):
    bsz, t_len, _ = z3.shape
    n_t = t_len // tb
    n_rows = S5_GB * grp
    ucol = (A_PAD + 3 * D_MIX) // D_MIX
    full = lambda shape: pl.BlockSpec(shape, lambda g, i: (0,) * len(shape))
    state_spec = pl.BlockSpec((1, 2, n_rows, 2 * S5_GW), lambda g, i: (g, 0, 0, 0))
    y_shape = jax.ShapeDtypeStruct((bsz, t_len, D_MIX), F32)
    return pl.pallas_call(
        functools.partial(_s5_kernel, grp, tb),
        grid=(bsz // grp, n_t),
        in_specs=[pl.BlockSpec((grp, tb, D_MIX), lambda g, i: (g, i, ucol)),
                  pl.BlockSpec((grp, tb, D_MIX), lambda g, i: (g, n_t - 1 - i, ucol)),
                  full((2, S5_GB, D_MIX // S5_GB, 2 * S5_GW)), full((S5_GB, 2 * S5_GW, D_MIX // S5_GB)),
                  full((2, n_rows, S5_GW)), full((2, n_rows, S5_GW)), state_spec],
        out_specs=[pl.BlockSpec((grp, tb, D_MIX), lambda g, i: (g, i, 0)),
                   pl.BlockSpec((grp, tb, D_MIX), lambda g, i: (g, n_t - 1 - i, 0)),
                   state_spec],
        out_shape=[y_shape, y_shape, jax.ShapeDtypeStruct((bsz // grp, 2, n_rows, 2 * S5_GW), F32)],
        scratch_shapes=[pltpu.VMEM((2 * S5_GW // LANE, n_rows * tb, LANE), F32),
                        pltpu.VMEM((2 * S5_GW // LANE, n_rows * tb, LANE), F32),
                        pltpu.VMEM((2, n_rows, 2 * S5_GW), F32)],
        compiler_params=_cparams(("arbitrary", "arbitrary"), VMEM_LIMIT),
        name="s5_scan",
    )(z3, z3, bmat, cmat, a_re, a_im, x0)


def s5_rows(x, grp):
    n_g = x.shape[1] // grp
    return (x.reshape(2, n_g, grp, S5_GB, S5_GW).transpose(1, 0, 3, 2, 4)
            .reshape(n_g, 2, S5_GB * grp, S5_GW))


def s5_unrows(x, grp):
    n_g = x.shape[0]
    return x.reshape(n_g, 2, S5_GB, grp, S5_GW).transpose(1, 0, 3, 2, 4).reshape(2, n_g * grp, S5_N)


def _merge_kernel(gnf_ref, gnb_ref, bonus_ref, g_ref, yb_ref, ycf_ref, ycb_ref, u_ref, zg0_ref, zg1_ref, zg2_ref,
                  x_ref, mod_ref, s5d_ref, wglu_ref, bglu_ref, wb_ref, wout_ref, lng_ref, lnb_ref, x1_ref, h2_ref):
    ya = (gnf_ref[...] + gnb_ref[...] + bonus_ref[...]) * g_ref[...]
    yc = ycf_ref[...] + ycb_ref[...] + s5d_ref[...] * u_ref[...]
    yc = 0.5 * yc * (1.0 + jnp.tanh(math.sqrt(2.0 / math.pi) * (yc + 0.044715 * (yc * yc * yc))))
    yc = yc * _sigmoid(_dot(yc.astype(BF16), wglu_ref[...]) + bglu_ref[...])
    merged = (_dot(ya.astype(BF16), wb_ref[0]) * _sigmoid(zg0_ref[...])
              + _dot(yb_ref[...].astype(BF16), wb_ref[1]) * _sigmoid(zg1_ref[...])
              + _dot(yc.astype(BF16), wb_ref[2]) * _sigmoid(zg2_ref[...]))
    mo = _dot(merged.astype(BF16), wout_ref[...])
    m = mod_ref[0]
    x1 = _layer_norm(DN_ALPHA * x_ref[...] + m[2:3] * mo, lng_ref[...], lnb_ref[...])
    x1_ref[...] = x1
    h2_ref[...] = (x1 * (1.0 + m[4:5]) + m[3:4]).astype(BF16)


def merge_branches(gn, bonus, g, yb, yc, z, x, mods, mod_of_tile, p):
    n = x.shape[0]
    row = lambda w, col=0: pl.BlockSpec((TM, w), lambda i, col=col: (i, col))
    full = lambda shape: pl.BlockSpec(shape, lambda i: (0,) * len(shape))
    gb = (A_PAD + 4 * D_MIX) // D_MODEL
    return pl.pallas_call(
        _merge_kernel,
        grid=(n // TM,),
        in_specs=[row(D_MIX), row(D_MIX), row(D_MIX), row(D_MIX), row(D_MIX), row(D_MIX), row(D_MIX),
                  row(D_MIX, A_PAD // D_MIX + 3),
                  row(D_MODEL, gb), row(D_MODEL, gb + 1), row(D_MODEL, gb + 2),
                  row(D_MODEL),
                  pl.BlockSpec((1, SUBLANE, D_MODEL), lambda i: (mod_of_tile(i, TM), 0, 0)),
                  full((1, D_MIX)), full((D_MIX, D_MIX)), full((1, D_MIX)),
                  full((3, D_MIX, D_MODEL)), full((D_MODEL, D_MODEL)),
                  full((1, D_MODEL)), full((1, D_MODEL))],
        out_specs=[row(D_MODEL), row(D_MODEL)],
        out_shape=[jax.ShapeDtypeStruct((n, D_MODEL), F32), jax.ShapeDtypeStruct((n, D_MODEL), BF16)],
        compiler_params=_cparams(("arbitrary",), VMEM_LIMIT),
        name="merge_branches",
    )(gn[0], gn[1], bonus, g, yb, yc[0], yc[1], z, z, z, z, x, mods,
      p['s5_d'], p['w_glu'], p['b_glu'], p['w_branch'], p['w_out'], p['ln1_g'], p['ln1_b'])


def _first_max(val, idx, big):
    m = jnp.max(jnp.max(val, axis=1, keepdims=True), axis=0, keepdims=True)
    cand = jnp.where(val == m, idx, big)
    first = jnp.min(jnp.min(cand, axis=1, keepdims=True), axis=0, keepdims=True)
    return m, idx == first


def _router_kernel(h_ref, wt_ref, bias_ref, gates_ref):
    per = N_EXPERTS // N_GROUPS
    logits = _dot_nt(wt_ref[...], h_ref[...])
    n = logits.shape[1]
    scores = _sigmoid(logits).reshape(N_GROUPS, per, n)
    sel = scores + bias_ref[...]
    e_idx = (lax.broadcasted_iota(jnp.int32, (N_GROUPS, per, n), 0) * per
             + lax.broadcasted_iota(jnp.int32, (N_GROUPS, per, n), 1))
    in_grp = lax.broadcasted_iota(jnp.int32, (N_GROUPS, per, n), 1)
    m1 = jnp.max(sel, axis=1, keepdims=True)
    first = jnp.min(jnp.where(sel == m1, in_grp, per), axis=1, keepdims=True)
    m2 = jnp.max(jnp.where(in_grp == first, -jnp.inf, sel), axis=1, keepdims=True)
    grp = m1 + m2
    g_idx = lax.broadcasted_iota(jnp.int32, (N_GROUPS, 1, n), 0)
    gmask = jnp.zeros((N_GROUPS, 1, n), F32)
    for _ in range(TOPK_GROUPS):
        _, hit = _first_max(grp, g_idx, N_GROUPS)
        gmask = jnp.where(hit, 1.0, gmask)
        grp = jnp.where(hit, -jnp.inf, grp)
    cur = jnp.where(jnp.broadcast_to(gmask, sel.shape) > 0.0, sel, NEG)
    w = jnp.zeros((N_GROUPS, per, n), F32)
    for _ in range(TOP_K):
        _, hit = _first_max(cur, e_idx, N_EXPERTS)
        w = jnp.where(hit, scores, w)
        cur = jnp.where(hit, -jnp.inf, cur)
    tot = jnp.sum(jnp.sum(w, axis=1, keepdims=True), axis=0, keepdims=True)
    gates = (ROUTED_SCALE * w / tot).reshape(N_EXPERTS, n)
    hi = gates.astype(BF16).astype(F32)
    lo = (gates - hi).astype(BF16).astype(F32)
    gates_ref[...] = jnp.concatenate([hi, lo], axis=0).T.astype(BF16)


def moe_router(h2, router_wt, router_bias):
    n = h2.shape[0]
    return pl.pallas_call(
        _router_kernel,
        grid=(n // TM,),
        in_specs=[pl.BlockSpec((TM, D_MODEL), lambda i: (i, 0)),
                  pl.BlockSpec((N_EXPERTS, D_MODEL), lambda i: (0, 0)),
                  pl.BlockSpec((N_GROUPS, N_EXPERTS // N_GROUPS, 1), lambda i: (0, 0, 0))],
        out_specs=pl.BlockSpec((TM, 2 * N_EXPERTS), lambda i: (i, 0)),
        out_shape=jax.ShapeDtypeStruct((n, 2 * N_EXPERTS), BF16),
        compiler_params=_cparams(("arbitrary",)),
        name="moe_router",
    )(h2, router_wt, router_bias)


def _moe_kernel(ec, h_ref, gates_ref, x1_ref, mod_ref, wg_ref, wu_ref, wd_ref, ex_ref,
                sg_ref, su_ref, sd_ref, lng_ref, lnb_ref, out_ref, acc_scr):
    j = pl.program_id(1)
    h = h_ref[...]

    @pl.when(j == 0)
    def _():
        sh = _dot(h, sg_ref[...])
        sh = sh * _sigmoid(sh) * _dot(h, su_ref[...])
        acc_scr[...] = _dot(sh.astype(BF16), sd_ref[...])

    gexp = _dot(gates_ref[...], ex_ref[0])
    wg = jnp.concatenate([wg_ref[e] for e in range(ec)], axis=1)
    wu = jnp.concatenate([wu_ref[e] for e in range(ec)], axis=1)
    hg = _dot(h, wg)
    hu = _dot(h, wu)
    act = hg * _sigmoid(hg) * hu * gexp
    acc_scr[...] += _dot(act.astype(BF16), wd_ref[...])

    @pl.when(j == pl.num_programs(1) - 1)
    def _():
        m = mod_ref[0]
        out_ref[...] = _layer_norm(DN_ALPHA * x1_ref[...] + m[5:6] * acc_scr[...], lng_ref[...], lnb_ref[...])


def moe_ffn(h2, gates, x1, mods, mod_of_tile, p, tm=1024, ec=MOE_EC):
    n = h2.shape[0]
    wcols = ec * D_EXPERT
    n_j = N_EXPERTS // ec
    full = lambda shape: pl.BlockSpec(shape, lambda i, j: (0,) * len(shape))
    return pl.pallas_call(
        functools.partial(_moe_kernel, ec),
        grid=(n // tm, n_j),
        in_specs=[pl.BlockSpec((tm, D_MODEL), lambda i, j: (i, 0)),
                  pl.BlockSpec((tm, 2 * N_EXPERTS), lambda i, j: (i, 0)),
                  pl.BlockSpec((tm, D_MODEL), lambda i, j: (i, 0)),
                  pl.BlockSpec((1, SUBLANE, D_MODEL), lambda i, j: (mod_of_tile(i, tm), 0, 0)),
                  pl.BlockSpec((ec, D_MODEL, D_EXPERT), lambda i, j: (j, 0, 0)),
                  pl.BlockSpec((ec, D_MODEL, D_EXPERT), lambda i, j: (j, 0, 0)),
                  pl.BlockSpec((wcols, D_MODEL), lambda i, j: (j, 0)),
                  pl.BlockSpec((1, 2 * N_EXPERTS, wcols), lambda i, j: (j, 0, 0)),
                  full((D_MODEL, D_SHARED)), full((D_MODEL, D_SHARED)), full((D_SHARED, D_MODEL)),
                  full((1, D_MODEL)), full((1, D_MODEL))],
        out_specs=pl.BlockSpec((tm, D_MODEL), lambda i, j: (i, 0)),
        out_shape=jax.ShapeDtypeStruct((n, D_MODEL), F32),
        scratch_shapes=[pltpu.VMEM((tm, D_MODEL), F32)],
        compiler_params=_cparams(("arbitrary", "arbitrary"), VMEM_LIMIT),
        name="moe_ffn",
    )(h2, gates, x1, mods, p['wg'], p['wu'], p['wd'], p['expand'],
      p['sh_g'], p['sh_u'], p['sh_d'], p['ln2_g'], p['ln2_b'])


def _block_diag2(m):
    z = jnp.zeros_like(m[0])
    return jnp.concatenate([jnp.concatenate([m[0], z], axis=1), jnp.concatenate([z, m[1]], axis=1)], axis=0)


_SMALL_PARAMS = ('rwkv_mu', 'rwkv_w2', 'rwkv_a2', 'rwkv_w0', 'rwkv_a0', 'rwkv_kk', 'rwkv_ka', 'rwkv_rk',
                 'rwkv_gn_g', 'rwkv_gn_b', 's5_a_re', 's5_a_im', 's5_log_dt', 's5_b_re', 's5_b_im', 's5_c_re',
                 's5_c_im', 's5_d', 's5_b_glu', 'ln1_g', 'ln1_b', 'router_bias', 'ln2_g', 'ln2_b', 'na_rpb')


def _small_params(g):
    p = {}
    p['mu'] = jnp.concatenate([g['rwkv_mu'], jnp.zeros((A_PAD - A_COLS,), F32)])[None, :]
    p['w2bd'] = _block_diag2(g['rwkv_w2']).astype(BF16)
    p['a2bd'] = _block_diag2(g['rwkv_a2']).astype(BF16)
    p['w0'] = g['rwkv_w0'].reshape(1, 2 * D_MIX)
    p['a0'] = g['rwkv_a0'].reshape(1, 2 * D_MIX)
    p['kkp'] = g['rwkv_kk'][None, :]
    p['ka'] = g['rwkv_ka'][None, :]
    p['rk'] = g['rwkv_rk'][None, :]
    p['gng'] = g['rwkv_gn_g'].reshape(HEADS // 2, 1, 2 * HD)
    p['gnb'] = g['rwkv_gn_b'].reshape(HEADS // 2, 1, 2 * HD)
    p['na_bias'] = na_bias_table(g['na_rpb'])
    a = lax.complex(g['s5_a_re'], g['s5_a_im'])
    dt = jnp.exp(g['s5_log_dt'])[..., None]
    a_bar = jnp.exp(dt * a)
    b_bar = ((a_bar - 1.0) / a)[..., None] * lax.complex(g['s5_b_re'], g['s5_b_im'])
    eye_g = jnp.eye(C_GROUPS // S5_GB, dtype=F32)
    gpb = C_GROUPS // S5_GB
    bd = lambda m: jnp.einsum('dkgph,gj->dkghjp', m.reshape(2, S5_GB, gpb, C_STATE, C_GROUP),
                              eye_g).reshape(2, S5_GB, D_MIX // S5_GB, S5_GW)
    p['s5_bmat'] = jnp.concatenate([bd(b_bar.real), bd(b_bar.imag)], axis=-1).astype(BF16)
    cd = lambda m: jnp.einsum('kghp,gj->kjpgh', m.reshape(S5_GB, gpb, C_GROUP, C_STATE),
                              eye_g).reshape(S5_GB, S5_GW, D_MIX // S5_GB)
    p['s5_cmat'] = jnp.concatenate([cd(g['s5_c_re']), -cd(g['s5_c_im'])], axis=1).astype(BF16)
    p['s5_ar'] = a_bar.real.reshape(2, 1, S5_N)
    p['s5_ai'] = a_bar.imag.reshape(2, 1, S5_N)
    p['s5_d'] = g['s5_d'][None, :]
    p['b_glu'] = g['s5_b_glu'][None, :]
    p['ln1_g'] = g['ln1_g'][None, :]
    p['ln1_b'] = g['ln1_b'][None, :]
    p['router_bias'] = g['router_bias'].reshape(N_GROUPS, N_EXPERTS // N_GROUPS, 1)
    p['ln2_g'] = g['ln2_g'][None, :]
    p['ln2_b'] = g['ln2_b'][None, :]
    return p


def _layer_weights(P, l):
    g = lambda name: P[name][l]
    w_in = g('w_in').astype(BF16)
    p = {}
    p['w_in'] = (jnp.zeros((D_MODEL, IN_COLS_P), BF16)
                 .at[:, :A_COLS].set(w_in[:, :A_COLS]).at[:, A_PAD:].set(w_in[:, A_COLS:]))
    p['g2'] = g('rwkv_g2').astype(BF16)
    hid = np.arange(D_MIX) // HD
    p['seg_ones'] = jnp.asarray((hid[:, None] == hid[None, :]).astype(np.float32), dtype=BF16)
    p['w_glu'] = g('s5_w_glu').astype(BF16)
    p['w_branch'] = g('w_branch').astype(BF16)
    p['w_out'] = g('w_out').astype(BF16)
    p['router_wt'] = g('router_w').T.astype(BF16)
    ecols = N_EXPERTS * D_EXPERT
    p['wg'] = g('exp_w_gate').astype(BF16)
    p['wu'] = g('exp_w_up').astype(BF16)
    p['wd'] = g('exp_w_down').reshape(ecols, D_MODEL).astype(BF16)
    p['sh_g'] = g('sh_w_gate').astype(BF16)
    p['sh_u'] = g('sh_w_up').astype(BF16)
    p['sh_d'] = g('sh_w_down').astype(BF16)
    return p


def _expand_table(ec):
    n_j = N_EXPERTS // ec
    t = np.zeros((n_j, 2 * N_EXPERTS, ec * D_EXPERT), np.float32)
    for e in range(N_EXPERTS):
        j, q = divmod(e, ec)
        t[j, e, q * D_EXPERT:(q + 1) * D_EXPERT] = 1.0
        t[j, N_EXPERTS + e, q * D_EXPERT:(q + 1) * D_EXPERT] = 1.0
    return jnp.asarray(t, dtype=BF16)


def kernel(x_prompt, x_sample, c, cache_na_k, cache_na_v, state_rwkv, state_s5_re, state_s5_im, c_ctx, w_ada, b_ada, w_in, rwkv_mu, rwkv_w0, rwkv_w2, rwkv_a0, rwkv_a2, rwkv_g2, rwkv_kk, rwkv_ka, rwkv_rk, rwkv_gn_g, rwkv_gn_b, na_rpb, s5_a_re, s5_a_im, s5_log_dt, s5_b_re, s5_b_im, s5_c_re, s5_c_im, s5_d, s5_w_glu, s5_b_glu, w_branch, w_out, ln1_g, ln1_b, router_w, router_bias, exp_w_gate, exp_w_up, exp_w_down, sh_w_gate, sh_w_up, sh_w_down, ln2_g, ln2_b):
    P = dict(w_in=w_in, rwkv_mu=rwkv_mu, rwkv_w0=rwkv_w0, rwkv_w2=rwkv_w2, rwkv_a0=rwkv_a0, rwkv_a2=rwkv_a2,
             rwkv_g2=rwkv_g2, rwkv_kk=rwkv_kk, rwkv_ka=rwkv_ka, rwkv_rk=rwkv_rk, rwkv_gn_g=rwkv_gn_g,
             rwkv_gn_b=rwkv_gn_b, s5_a_re=s5_a_re, s5_a_im=s5_a_im, s5_log_dt=s5_log_dt, s5_b_re=s5_b_re,
             s5_b_im=s5_b_im, s5_c_re=s5_c_re, s5_c_im=s5_c_im, s5_d=s5_d, s5_w_glu=s5_w_glu,
             s5_b_glu=s5_b_glu, w_branch=w_branch, w_out=w_out, ln1_g=ln1_g, ln1_b=ln1_b, router_w=router_w,
             router_bias=router_bias, exp_w_gate=exp_w_gate, exp_w_up=exp_w_up, exp_w_down=exp_w_down,
             sh_w_gate=sh_w_gate, sh_w_up=sh_w_up, sh_w_down=sh_w_down, ln2_g=ln2_g, ln2_b=ln2_b)
    bc, tc, _ = x_prompt.shape
    bl, tl, _ = x_sample.shape
    depth = w_in.shape[0]

    cond = jnp.concatenate([c_ctx[None, :], c, jnp.zeros((SUBLANE - 1 - bl, D_MODEL), F32)], axis=0)
    ada = ada_modulation(cond, w_ada, b_ada)
    masks = jnp.asarray(_chunk_masks())
    expand = _expand_table(MOE_EC)

    paths = {
        'ctx': dict(bsz=bc, t=tc, x=x_prompt.reshape(bc * tc, D_MODEL), mod_of_tile=lambda i, tm: 0),
        'lat': dict(bsz=bl, t=tl, x=x_sample.reshape(bl * tl, D_MODEL),
                    mod_of_tile=lambda i, tm: 1 + (i * tm) // tl),
    }
    for q in paths.values():
        tiles = q['t'] // TM
        pos = np.arange(q['bsz'] * tiles) % tiles
        q['seq_tiles'] = jnp.asarray(np.concatenate([np.full_like(pos, tiles), pos]).astype(np.int32))
        q['sched'] = rwkv_schedule([q['t']] * q['bsz'])
        q['s5_tb'] = min(S5_ROWS // S5_GRP, q['t'])
        assert q['bsz'] % S5_GRP == 0 and q['s5_tb'] % SUBLANE == 0 and q['t'] % q['s5_tb'] == 0

    P['na_rpb'] = na_rpb
    small = jax.vmap(_small_params)({k: P[k] for k in _SMALL_PARAMS})
    mods_all = jnp.concatenate([ada[:, :1 + bl].reshape(depth, 1 + bl, 6, D_MODEL),
                                jnp.zeros((depth, 1 + bl, SUBLANE - 6, D_MODEL), F32)], axis=2)
    new_k, new_v, new_rwkv, new_s5 = [], [], [], []
    for l in range(depth):
        p = _layer_weights(P, l)
        p.update({k: v[l] for k, v in small.items()})
        p['expand'] = expand
        mods = mods_all[l]
        bias_tab = p['na_bias']
        for name, q in paths.items():
            bsz, t_len, x, mod_of_tile = q['bsz'], q['t'], q['x'], q['mod_of_tile']
            z = in_projection(x, mods, p['w_in'], mod_of_tile)

            r, v, kk, ld, kd, bd, g, bonus = rwkv_prep(z, q['seq_tiles'], p)
            if name == 'ctx':
                s0t = jnp.zeros((bsz, 2, HEADS // 2, 2 * HD, 2 * HD), F32)
            else:
                s0t = rwkv_pack_state(state_rwkv[:, l])
            gn_f, gn_b, s_fin = rwkv_scan((r, v, kk, ld, kd, bd), masks, p['gng'], p['gnb'], s0t, q['sched'])

            if name == 'ctx':
                yb, k_h, v_h = ctx_attention(z, bsz, t_len)
            else:
                yb = na_attention(z, cache_na_k, cache_na_v, l, bias_tab, bsz, t_len)

            if name == 'ctx':
                x0 = jnp.zeros((bsz // S5_GRP, 2, S5_GB * S5_GRP, 2 * S5_GW), F32)
            else:
                x0 = jnp.concatenate(
                    [s5_rows(jnp.swapaxes(s[:, l].reshape(bsz, 2, S5_N), 0, 1), S5_GRP)
                     for s in (state_s5_re, state_s5_im)], axis=-1)
            a_re, a_im = (s5_rows(jnp.broadcast_to(a, (2, S5_GRP, S5_N)), S5_GRP)[0]
                          for a in (p['s5_ar'], p['s5_ai']))
            yc_f, yc_b, x_fin = s5_scan(z.reshape(bsz, t_len, IN_COLS_P), p['s5_bmat'], p['s5_cmat'],
                                        a_re, a_im, x0, q['s5_tb'], S5_GRP)
            yc = (yc_f.reshape(bsz * t_len, D_MIX), yc_b.reshape(bsz * t_len, D_MIX))

            x1, h2 = merge_branches((gn_f, gn_b), bonus, g, yb, yc, z, x, mods, mod_of_tile, p)
            gates = moe_router(h2, p['router_wt'], p['router_bias'])
            q['x'] = moe_ffn(h2, gates, x1, mods, mod_of_tile, p)
            if name == 'ctx':
                new_k.append(k_h)
                new_v.append(v_h)
                new_rwkv.append(jnp.swapaxes(s_fin, -1, -2))
                new_s5.append([jnp.swapaxes(s5_unrows(part, S5_GRP), 0, 1)
                               for part in (x_fin[..., :S5_GW], x_fin[..., S5_GW:])])

    s5_re, s5_im = (jnp.stack([layer[part] for layer in new_s5], axis=1).reshape(bc, depth, 2, C_GROUPS, C_STATE)
                    for part in range(2))
    return (paths['ctx']['x'].reshape(bc, tc, D_MODEL), paths['lat']['x'].reshape(bl, tl, D_MODEL),
            jnp.stack(new_k, axis=1), jnp.stack(new_v, axis=1), jnp.stack(new_rwkv, axis=1), s5_re, s5_im)
```

```python
import functools
import math

import numpy as np
import jax
import jax.numpy as jnp
from jax import lax
from jax.experimental import pallas as pl
from jax.experimental.pallas import tpu as pltpu

F32 = jnp.float32
BF16 = jnp.bfloat16

D_MODEL = 1024
DEPTH = 2
GRID_W = 64
D_MIX = 512
HEADS = 8
HD = 64
LORA = 128
WIN_R = 8
WIN_C = 16
C_GROUP = 16
C_GROUPS = D_MIX // C_GROUP
C_STATE = 64
S5_N = C_GROUPS * C_STATE
N_EXPERTS = 64
TOP_K = 8
N_GROUPS = 8
TOPK_GROUPS = 4
D_EXPERT = 128
D_SHARED = 128
ROUTED_SCALE = 2.5
LN_EPS = 1e-5
GN_EPS = 64e-5
NEG = -1e30
DN_ALPHA = (2 * DEPTH) ** 0.25
A_COLS = 3 * D_MIX + 3 * LORA
A_PAD = 2048
IN_COLS_P = A_PAD + 4 * D_MIX + 3 * D_MODEL

LANE = 128
SUBLANE = 8
TM = 256
CH = HD
RWKV_CHUNKS = 4
MOE_EC = 8
NA_QROWS = 8
NA_KROWS = NA_QROWS + WIN_R
S5_GB = 4
S5_GW = S5_N // S5_GB
S5_GRP = SUBLANE // S5_GB
S5_ROWS = 512
S5_CARRY_VREGS = 32
VMEM_LIMIT = 56 * 1024 * 1024


def _cparams(sem, vmem=None):
    return pltpu.CompilerParams(dimension_semantics=sem, vmem_limit_bytes=vmem)


def _sigmoid(x):
    return 1.0 / (1.0 + jnp.exp(-x))


def _dot(a, b):
    return jnp.dot(a, b, preferred_element_type=F32)


def _dot_nt(a, b):
    return lax.dot_general(a, b, (((1,), (1,)), ((), ())), preferred_element_type=F32)


def _split2(x):
    hi = x.astype(BF16)
    return hi, (x - hi.astype(F32)).astype(BF16)


def _split3(x):
    hi = x.astype(BF16)
    r1 = x - hi.astype(F32)
    mid = r1.astype(BF16)
    return hi, mid, (r1 - mid.astype(F32)).astype(BF16)


def _seg_sum(x, ones_bf16):
    return sum(_dot(part, ones_bf16) for part in _split3(x))


def _mm3(a, b):
    return _dot(a[0], b[0]) + (_dot(a[1], b[0]) + _dot(a[0], b[1]))


def _layer_norm(x, g, b):
    mu = jnp.mean(x, axis=-1, keepdims=True)
    xc = x - mu
    var = jnp.mean(xc * xc, axis=-1, keepdims=True)
    return xc * lax.rsqrt(var + LN_EPS) * g + b


def _ada_kernel(c_ref, w_ref, b_ref, o_ref):
    c = c_ref[...]
    s = c * _sigmoid(c)
    o_ref[0] = _dot(s.astype(BF16), w_ref[0].astype(BF16)) + b_ref[0]


def ada_modulation(cond, w_ada, b_ada):
    n_l, d, n6 = w_ada.shape
    tn = 1536
    return pl.pallas_call(
        _ada_kernel,
        grid=(n_l, n6 // tn),
        in_specs=[pl.BlockSpec((SUBLANE, d), lambda l, j: (0, 0)),
                  pl.BlockSpec((1, d, tn), lambda l, j: (l, 0, j)),
                  pl.BlockSpec((1, 1, tn), lambda l, j: (l, 0, j))],
        out_specs=pl.BlockSpec((1, SUBLANE, tn), lambda l, j: (l, 0, j)),
        out_shape=jax.ShapeDtypeStruct((n_l, SUBLANE, n6), F32),
        compiler_params=_cparams(("arbitrary", "arbitrary")),
        name="ada_modulation",
    )(cond, w_ada, b_ada.reshape(n_l, 1, n6))


def _inproj_kernel(x_ref, mod_ref, w_ref, z_ref, h_scr):
    @pl.when(pl.program_id(1) == 0)
    def _():
        m = mod_ref[0]
        h_scr[...] = (x_ref[...] * (1.0 + m[1:2]) + m[0:1]).astype(BF16)
    z_ref[...] = _dot(h_scr[...], w_ref[...])


def in_projection(x, mods, w_in_p, mod_of_tile, tm=2048, tn=1024):
    n, d = x.shape
    cols = w_in_p.shape[1]
    return pl.pallas_call(
        _inproj_kernel,
        grid=(n // tm, cols // tn),
        in_specs=[pl.BlockSpec((tm, d), lambda i, j: (i, 0)),
                  pl.BlockSpec((1, SUBLANE, d), lambda i, j: (mod_of_tile(i, tm), 0, 0)),
                  pl.BlockSpec((d, tn), lambda i, j: (0, j))],
        out_specs=pl.BlockSpec((tm, tn), lambda i, j: (i, j)),
        out_shape=jax.ShapeDtypeStruct((n, cols), F32),
        scratch_shapes=[pltpu.VMEM((tm, d), BF16)],
        compiler_params=_cparams(("arbitrary", "arbitrary"), VMEM_LIMIT),
        name="in_projection",
    )(x, mods, w_in_p)


def _rwkv_prep_kernel(seq_tiles_ref, z_ref, zp_ref, zn_ref, mu_ref, w2_ref, a2_ref, g2_ref, w0_ref, a0_ref,
                      kkp_ref, ka_ref, rk_ref, e_ref,
                      r_ref, v_ref, kk_ref, ld_ref, kd_ref, bd_ref, g_ref, bonus_ref):
    i = pl.program_id(0)
    tiles = seq_tiles_ref[i]
    pos = seq_tiles_ref[i + pl.num_programs(0)]
    x = z_ref[...]
    tm = x.shape[0]
    rows = lax.broadcasted_iota(jnp.int32, x.shape, 0)
    prev_row = jnp.where(pos == 0, 0.0, zp_ref[SUBLANE - 1:SUBLANE, :])
    next_row = jnp.where(pos == tiles - 1, 0.0, zn_ref[0:1, :])
    xm1 = jnp.where(rows == 0, prev_row, pltpu.roll(x, 1, axis=0))
    xp1 = jnp.where(rows == tm - 1, next_row, pltpu.roll(x, tm - 1, axis=0))
    za = x + mu_ref[...] * (0.5 * (xm1 + xp1) - x)

    r = za[:, 0:D_MIX]
    k = za[:, D_MIX:2 * D_MIX]
    v = za[:, 2 * D_MIX:3 * D_MIX]
    lw = za[:, 3 * D_MIX:3 * D_MIX + LORA]
    la = za[:, 3 * D_MIX + LORA:3 * D_MIX + 2 * LORA]
    lg = za[:, 3 * D_MIX + 2 * LORA:3 * D_MIX + 3 * LORA]

    w_both = w0_ref[...] + _dot(jnp.tanh(lw).astype(BF16), w2_ref[...])
    a_both = _sigmoid(a0_ref[...] + _dot(la.astype(BF16), a2_ref[...]))
    g_ref[...] = _dot(_sigmoid(lg).astype(BF16), g2_ref[...])

    e = e_ref[...]
    kks = k * kkp_ref[...]
    nrm = jnp.sqrt(_seg_sum(kks * kks, e))
    kk = kks / jnp.maximum(nrm, 1e-12)
    bonus = jnp.zeros_like(v)
    r_ref[...] = r
    v_ref[...] = v
    kk_ref[...] = kk
    for d in range(2):
        w = w_both[:, d * D_MIX:(d + 1) * D_MIX]
        a = a_both[:, d * D_MIX:(d + 1) * D_MIX]
        ld = -math.exp(-0.5) * _sigmoid(w)
        kd = k * (1.0 + (a - 1.0) * ka_ref[...])
        bd = kk * a
        bonus = bonus + _seg_sum(r * kd * rk_ref[...], e) * v
        ld_ref[d] = ld
        kd_ref[d] = kd
        bd_ref[d] = bd
    bonus_ref[...] = bonus


def rwkv_prep(z, seq_tiles, p):
    n = z.shape[0]
    nt = n // TM
    halo = TM // SUBLANE
    nb8 = n // SUBLANE
    tok = jax.ShapeDtypeStruct((n, D_MIX), F32)
    tok2 = jax.ShapeDtypeStruct((2, n, D_MIX), F32)
    full = lambda shape: pl.BlockSpec(shape, lambda i, s: (0,) * len(shape))
    tok_spec = pl.BlockSpec((TM, D_MIX), lambda i, s: (i, 0))
    tok2_spec = pl.BlockSpec((2, TM, D_MIX), lambda i, s: (0, i, 0))
    grid_spec = pltpu.PrefetchScalarGridSpec(
        num_scalar_prefetch=1,
        grid=(nt,),
        in_specs=[pl.BlockSpec((TM, A_PAD), lambda i, s: (i, 0)),
                  pl.BlockSpec((SUBLANE, A_PAD), lambda i, s: (jnp.maximum(i * halo - 1, 0), 0)),
                  pl.BlockSpec((SUBLANE, A_PAD), lambda i, s: (jnp.minimum((i + 1) * halo, nb8 - 1), 0)),
                  full((1, A_PAD)), full((LORA, 2 * D_MIX)), full((LORA, 2 * D_MIX)), full((LORA, D_MIX)),
                  full((1, 2 * D_MIX)), full((1, 2 * D_MIX)), full((1, D_MIX)), full((1, D_MIX)),
                  full((1, D_MIX)), full((D_MIX, D_MIX))],
        out_specs=[tok_spec, tok_spec, tok_spec, tok2_spec, tok2_spec, tok2_spec, tok_spec, tok_spec],
    )
    return pl.pallas_call(
        _rwkv_prep_kernel,
        grid_spec=grid_spec,
        out_shape=[tok, tok, tok, tok2, tok2, tok2, tok, tok],
        compiler_params=_cparams(("arbitrary",), VMEM_LIMIT),
        name="rwkv_prep",
    )(seq_tiles, z, z, z, p['mu'], p['w2bd'], p['a2bd'], p['g2'], p['w0'], p['a0'],
      p['kkp'], p['ka'], p['rk'], p['seg_ones'])


def _chunk_masks():
    t = np.arange(CH)
    fwd_incl = (t[:, None] >= t[None, :])
    out = []
    for incl in (fwd_incl, fwd_incl.T):
        strict = incl & (t[:, None] != t[None, :])
        ms = [incl, strict, strict & ((t[:, None] // 8) == (t[None, :] // 8))]
        for m in (8, 16, 32):
            ms.append(strict & ((t[:, None] // (2 * m)) == (t[None, :] // (2 * m)))
                      & ((t[:, None] // m) != (t[None, :] // m)))
        out.append(np.stack(ms))
    masks = np.stack(out).astype(np.float32)
    return np.concatenate([masks, masks], axis=-1)


def _rwkv_pair_kernel(cb, sched_ref, *refs):
    dir_refs = (refs[0:6], refs[6:12])
    m_ref, gng_ref, gnb_ref, s0_ref = refs[12:16]
    y_refs = refs[16:18]
    sfin_ref, s_scr = refs[18:20]
    step_id = pl.program_id(0)

    @pl.when(sched_ref[_SCHED_FIRST, step_id] == 1)
    def _():
        s_scr[...] = s0_ref[0]

    pw = 2 * HD
    ri = lax.broadcasted_iota(jnp.int32, (pw, pw), 0)
    ci = lax.broadcasted_iota(jnp.int32, (pw, pw), 1)
    eye_bd = (ri == ci).astype(F32)
    mask_bd = ((ri // HD) == (ci // HD)).astype(F32)
    eye12 = (lax.broadcasted_iota(jnp.int32, (CH, pw), 0)
             == lax.broadcasted_iota(jnp.int32, (CH, pw), 1) % HD).astype(F32)
    left = lax.broadcasted_iota(jnp.int32, (1, pw), 1) < HD

    bf = lambda x: x.astype(BF16)
    each = lambda f, *cols: [f(*args) for args in zip(*cols)]
    rows = lambda j: slice(j * CH, (j + 1) * CH)
    lanes_of = lambda p: slice(p * pw, (p + 1) * pw)
    stack = lambda *xs: jnp.concatenate(xs, axis=0)
    side = lambda *xs: jnp.concatenate(xs, axis=1)
    top, mid = slice(0, CH), slice(CH, 2 * CH)
    zero = jnp.zeros((), BF16)
    bd = lambda x: stack(jnp.where(left, x, zero), jnp.where(left, zero, x))

    lanes = [(d, p) for d in range(2) for p in range(HEADS // 2)]
    chains = [(d, j, p) for d, p in lanes for j in range(cb)]
    msk = lambda k: [m_ref[d, k] for d, _, _ in chains]
    incl, strict, m8 = msk(0), msk(1), msk(2)
    incl_b = each(lambda m: bf(m[:, :CH]), incl)
    get = lambda k: [dir_refs[d][k][rows(j), lanes_of(p)] if k < 3 else dir_refs[d][k][0, rows(j), lanes_of(p)]
                     for d, j, p in chains]
    R, V, KK, LD, Kd, Bd = (get(k) for k in range(6))
    L = each(lambda m, x: sum(_dot(m, part) for part in _split3(x)), incl_b, LD)
    ltot = each(lambda x: jnp.sum(x, axis=0, keepdims=True), LD)
    e_nl = each(lambda l: jnp.exp(-l), L)
    e_rest = each(lambda l, lt: jnp.exp(lt - l), L, ltot)
    Qb = each(lambda kk, l, ld: bf(kk * jnp.exp(l - ld)), KK, L, LD)
    Rh = each(lambda r, l: r * jnp.exp(l), R, L)
    QRb = each(lambda q, rh: stack(q, bf(rh)), Qb, Rh)
    Btd = each(lambda b, e: bd(bf(b * e)), Bd, e_nl)
    Ktd = each(lambda k, e: bd(bf(k * e)), Kd, e_nl)
    BcTb = each(lambda b, e: bf((b * e).T), Bd, e_rest)
    KcTb = each(lambda k, e: bf((k * e).T), Kd, e_rest)
    Vb = each(bf, V)
    QRB = each(_dot_nt, QRb, Btd)
    QRK = each(_dot_nt, QRb, Ktd)
    Nl = each(lambda m, x: m * x[top], strict, QRB)
    Mrbb = each(lambda m, x: bf(m * x[mid]), incl, QRB)
    Mkb = each(lambda m, x: bf(m * x[top]), strict, QRK)
    Mrkb = each(lambda m, x: bf(m * x[mid]), incl, QRK)
    N8 = each(lambda m, n: m * n, m8, Nl)
    N8b = each(bf, N8)
    N2 = each(lambda a: _dot(a, bd(a)), N8b)
    N2b = each(bf, N2)
    N4 = each(lambda a: _dot(a, bd(a)), N2b)
    W = each(lambda a, b: _dot(bf(eye12 - a), bd(bf(eye12 + b))), N8, N2)
    W = each(lambda w, n4: _dot(bf(w), bd(bf(eye12 + n4))), W, N4)
    for lvl in range(3):
        Wb = each(bf, W)
        T = each(lambda m, n, w: _dot(bf(m * n), bd(w)), msk(3 + lvl), Nl, Wb)
        W = each(lambda w, wb, t: w - _dot(wb, bd(bf(t))), W, Wb, T)
    Wb = each(bf, W)
    Whb = each(lambda w, q: bf(_dot(w, bd(q))), Wb, Qb)
    XV = each(lambda mk, mrk, v: _dot(stack(mk, mrk), bd(v)), Mkb, Mrkb, Vb)
    U0b = each(lambda w, x: bf(-_dot(w, bd(bf(x[top])))), Wb, XV)
    XWU = each(lambda m, wh, u: _dot(m, side(bd(wh), bd(u))), Mrbb, Whb, U0b)
    BWU = each(lambda b, wh, u: _dot(b, side(wh, u)), BcTb, Whb, U0b)
    KV = each(_dot, KcTb, Vb)
    Y0 = each(lambda xwu, xv: xwu[:, pw:] + xv[mid], XWU, XV)
    Hd = each(lambda bwu, kv: mask_bd * (bwu[:, pw:] + kv), BWU, KV)
    RG = each(lambda rh, lt, xwu, bwu: _split2(stack(rh - xwu[:, :pw],
                                                     eye_bd * jnp.exp(lt) - mask_bd * bwu[:, :pw])),
              Rh, ltot, XWU, BWU)

    ST = [s_scr[d, p] for d, p in lanes]
    for step in range(cb):
        idx = [chains.index((d, step if d == 0 else cb - 1 - step, p)) for d, p in lanes]
        XS = [_mm3(RG[i], _split2(st)) for i, st in zip(idx, ST)]
        ST = [x[CH:] + Hd[i] for i, x in zip(idx, XS)]
        for i, x in zip(idx, XS):
            d, j, p = chains[i]
            y = x[top] + Y0[i]
            half_mean = lambda a: jnp.where(left, jnp.sum(jnp.where(left, a, 0.0), axis=-1, keepdims=True),
                                            jnp.sum(jnp.where(left, 0.0, a), axis=-1, keepdims=True)) * (1.0 / HD)
            yc = y - half_mean(y)
            var = half_mean(yc * yc)
            y_refs[d][rows(j), lanes_of(p)] = yc * lax.rsqrt(var + GN_EPS) * gng_ref[p] + gnb_ref[p]
    for (d, p), st in zip(lanes, ST):
        s_scr[d, p] = st

    @pl.when(sched_ref[_SCHED_LAST, step_id] == 1)
    def _():
        for d, p in lanes:
            st = s_scr[d, p]
            sfin_ref[0, d, 2 * p] = st[:HD, :HD]
            sfin_ref[0, d, 2 * p + 1] = st[HD:, HD:]


_SCHED_FWD, _SCHED_BWD, _SCHED_SEQ, _SCHED_FIRST, _SCHED_LAST = range(5)


def rwkv_schedule(seq_lens):
    blk_rows = RWKV_CHUNKS * CH
    cols, base = [], 0
    for s, t_len in enumerate(seq_lens):
        n_b = t_len // blk_rows
        for i in range(n_b):
            cols.append((base + i, base + n_b - 1 - i, s, int(i == 0), int(i == n_b - 1)))
        base += n_b
    return np.asarray(cols, np.int32).T


def rwkv_scan(prep, masks, gng, gnb, s0t, sched):
    r, v, kk, ld, kd, bd = prep
    n = r.shape[0]
    n_seq = s0t.shape[0]
    blk_rows = RWKV_CHUNKS * CH
    pairs, pw = HEADS // 2, 2 * HD
    in_specs, args = [], []
    for d, row in ((0, _SCHED_FWD), (1, _SCHED_BWD)):
        for a in (r, v, kk):
            in_specs.append(pl.BlockSpec((blk_rows, D_MIX), lambda i, s, row=row: (s[row, i], 0)))
            args.append(a)
        for a in (ld, kd, bd):
            in_specs.append(pl.BlockSpec((1, blk_rows, D_MIX), lambda i, s, row=row, d=d: (d, s[row, i], 0)))
            args.append(a)
    state_spec = pl.BlockSpec((1, 2, pairs, pw, pw), lambda i, s: (s[_SCHED_SEQ, i], 0, 0, 0, 0))
    in_specs += [pl.BlockSpec((2, 6, CH, pw), lambda i, s: (0, 0, 0, 0)),
                 pl.BlockSpec((pairs, 1, pw), lambda i, s: (0, 0, 0)),
                 pl.BlockSpec((pairs, 1, pw), lambda i, s: (0, 0, 0)),
                 state_spec]
    args += [masks, gng, gnb, s0t]
    grid_spec = pltpu.PrefetchScalarGridSpec(
        num_scalar_prefetch=1,
        grid=(sched.shape[1],),
        in_specs=in_specs,
        out_specs=[pl.BlockSpec((blk_rows, D_MIX), lambda i, s: (s[_SCHED_FWD, i], 0)),
                   pl.BlockSpec((blk_rows, D_MIX), lambda i, s: (s[_SCHED_BWD, i], 0)),
                   pl.BlockSpec((1, 2, HEADS, HD, HD), lambda i, s: (s[_SCHED_SEQ, i], 0, 0, 0, 0))],
        scratch_shapes=[pltpu.VMEM((2, pairs, pw, pw), F32)],
    )
    return pl.pallas_call(
        functools.partial(_rwkv_pair_kernel, RWKV_CHUNKS),
        grid_spec=grid_spec,
        out_shape=[jax.ShapeDtypeStruct((n, D_MIX), F32), jax.ShapeDtypeStruct((n, D_MIX), F32),
                   jax.ShapeDtypeStruct((n_seq, 2, HEADS, HD, HD), F32)],
        compiler_params=_cparams(("arbitrary",), VMEM_LIMIT),
        name="rwkv_scan",
    )(jnp.asarray(sched), *args)


def rwkv_pack_state(s):
    n = s.shape[0]
    st = jnp.swapaxes(s, -1, -2).reshape(n, 2, HEADS // 2, 2, HD, HD)
    eye2 = jnp.eye(2, dtype=s.dtype)
    return jnp.einsum('ndpakv,ab->ndpakbv', st, eye2).reshape(n, 2, HEADS // 2, 2 * HD, 2 * HD)


def _ctx_attn_kernel(q_ref, k_ref, v_ref, y_ref, ko_ref, vo_ref):
    scale = HD ** -0.5
    for h in range(HEADS):
        sl = slice(h * HD, (h + 1) * HD)
        q = q_ref[:, sl]
        k = k_ref[:, sl]
        v = v_ref[:, sl]
        ko_ref[0, h] = k
        vo_ref[0, h] = v
        s = _dot_nt(q.astype(BF16), k.astype(BF16)) * scale
        m = jnp.max(s, axis=-1, keepdims=True)
        e = jnp.exp(s - m)
        p = e / jnp.sum(e, axis=-1, keepdims=True)
        y_ref[:, sl] = _dot(p.astype(BF16), v.astype(BF16))


def ctx_attention(z, bsz, t_len):
    qb = A_PAD // D_MIX
    return pl.pallas_call(
        _ctx_attn_kernel,
        grid=(bsz,),
        in_specs=[pl.BlockSpec((t_len, D_MIX), lambda b: (b, qb)),
                  pl.BlockSpec((t_len, D_MIX), lambda b: (b, qb + 1)),
                  pl.BlockSpec((t_len, D_MIX), lambda b: (b, qb + 2))],
        out_specs=[pl.BlockSpec((t_len, D_MIX), lambda b: (b, 0)),
                   pl.BlockSpec((1, HEADS, t_len, HD), lambda b: (b, 0, 0, 0)),
                   pl.BlockSpec((1, HEADS, t_len, HD), lambda b: (b, 0, 0, 0))],
        out_shape=[jax.ShapeDtypeStruct((bsz * t_len, D_MIX), F32),
                   jax.ShapeDtypeStruct((bsz, HEADS, t_len, HD), F32),
                   jax.ShapeDtypeStruct((bsz, HEADS, t_len, HD), F32)],
        compiler_params=_cparams(("arbitrary",)),
        name="ctx_attention",
    )(z, z, z)


def _na_kernel(rows, nblk, *refs):
    q_ref = refs[0]
    k_refs = refs[1:1 + nblk]
    v_refs = refs[1 + nblk:1 + 2 * nblk]
    kc_ref, vc_ref, tab_ref, y_ref = refs[1 + 2 * nblk:]
    r0 = pl.program_id(1) * NA_QROWS
    u0 = jnp.clip(r0 - WIN_R // 2, 0, rows - NA_KROWS)
    scale = HD ** -0.5
    kwin = jnp.concatenate([kr[...] for kr in k_refs], axis=0).astype(BF16)
    vwin = jnp.concatenate([vr[...] for vr in v_refs], axis=0).astype(BF16)
    left = lax.broadcasted_iota(jnp.int32, (1, 2 * GRID_W), 1) < GRID_W
    bias = []
    for i in range(NA_QROWS):
        r = r0 + i
        rs = jnp.clip(r - WIN_R // 2, 0, rows - WIN_R)
        per_pair = []
        for jp in range(NA_KROWS // 2):
            kr = u0 + 2 * jp
            off = [jnp.where(jnp.logical_and(kr + e >= rs, kr + e < rs + WIN_R), 0.0, NEG) for e in range(2)]
            per_pair.append((jnp.clip(kr - r + WIN_R, 0, 2 * WIN_R - 1), jnp.where(left, off[0], off[1])))
        bias.append(per_pair)
    for h in range(HEADS):
        sl = slice(h * HD, (h + 1) * HD)
        q = q_ref[:, sl].astype(BF16)
        s_raw = _dot_nt(q, kwin[:, sl]) * scale
        s_loc = jnp.concatenate([
            jnp.concatenate([s_raw[i * GRID_W:(i + 1) * GRID_W, jp * 2 * GRID_W:(jp + 1) * 2 * GRID_W]
                             + tab_ref[h, bias[i][jp][0]] + bias[i][jp][1]
                             for jp in range(NA_KROWS // 2)], axis=1)
            for i in range(NA_QROWS)], axis=0)
        s_ctx = _dot_nt(q, kc_ref[0, 0, h].astype(BF16)) * scale
        m = jnp.maximum(jnp.max(s_loc, axis=-1, keepdims=True), jnp.max(s_ctx, axis=-1, keepdims=True))
        e_loc = jnp.exp(s_loc - m)
        e_ctx = jnp.exp(s_ctx - m)
        den = jnp.sum(e_loc, axis=-1, keepdims=True) + jnp.sum(e_ctx, axis=-1, keepdims=True)
        p_loc = (e_loc / den).astype(BF16)
        p_ctx = (e_ctx / den).astype(BF16)
        y_ref[:, sl] = _dot(p_loc, vwin[:, sl]) + _dot(p_ctx, vc_ref[0, 0, h].astype(BF16))


def na_bias_table(rpb):
    cq = np.arange(GRID_W)[:, None]
    ck = np.arange(GRID_W)[None, :]
    cs = np.clip(cq - WIN_C // 2, 0, GRID_W - WIN_C)
    col_bias = np.where((ck >= cs) & (ck < cs + WIN_C), 0.0, NEG).astype(np.float32)
    col_idx = np.clip(ck - cq + WIN_C - 1, 0, 2 * WIN_C - 2)
    rpb_col = rpb.astype(F32)[:, :, col_idx] + col_bias
    padded = jnp.pad(rpb_col, ((0, 0), (1, 1), (0, 0), (0, 0)), constant_values=NEG)
    return jnp.concatenate([padded[:, :-1], padded[:, 1:]], axis=-1)


def na_attention(z, k_ctx, v_ctx, layer, bias_tab, bsz, t_len):
    rows = t_len // GRID_W
    assert rows >= NA_KROWS and rows % NA_QROWS == 0, "latent grid too small for the row-group tiling"
    qb = A_PAD // D_MIX
    past = k_ctx.shape[3]
    blk_rows = WIN_R // 2
    nblk = NA_KROWS // blk_rows
    blk_tok = blk_rows * GRID_W
    q_tok = NA_QROWS * GRID_W

    def win_spec(j, col):
        def index(b, g):
            u0 = jnp.clip(g * NA_QROWS - WIN_R // 2, 0, rows - NA_KROWS)
            return (b * (rows // blk_rows) + u0 // blk_rows + j, col)
        return pl.BlockSpec((blk_tok, D_MIX), index)

    in_specs = ([pl.BlockSpec((q_tok, D_MIX), lambda b, g: (b * (rows // NA_QROWS) + g, qb))]
                + [win_spec(j, qb + 1) for j in range(nblk)]
                + [win_spec(j, qb + 2) for j in range(nblk)]
                + [pl.BlockSpec((1, 1, HEADS, past, HD), lambda b, g: (b, layer, 0, 0, 0)),
                   pl.BlockSpec((1, 1, HEADS, past, HD), lambda b, g: (b, layer, 0, 0, 0)),
                   pl.BlockSpec(bias_tab.shape, lambda b, g: (0, 0, 0, 0))])
    return pl.pallas_call(
        functools.partial(_na_kernel, rows, nblk),
        grid=(bsz, rows // NA_QROWS),
        in_specs=in_specs,
        out_specs=pl.BlockSpec((q_tok, D_MIX), lambda b, g: (b * (rows // NA_QROWS) + g, 0)),
        out_shape=jax.ShapeDtypeStruct((bsz * t_len, D_MIX), F32),
        compiler_params=_cparams(("arbitrary", "arbitrary"), VMEM_LIMIT),
        name="na_attention",
    )(*([z] * (1 + 2 * nblk)), k_ctx, v_ctx, bias_tab)


def _s5_kernel(bsz, tb, uf_ref, ub_ref, bm_ref, cm_ref, ar_ref, ai_ref, x0_ref, yf_ref, yb_ref, xf_ref,
               x_scr, xo_scr, carry_scr):
    i = pl.program_id(1)

    @pl.when(i == 0)
    def _():
        carry_scr[...] = x0_ref[0]

    n_re = S5_GW // LANE
    n_rows = S5_GB * bsz
    assert n_rows == SUBLANE
    blk = bsz * tb
    group = min(n_re, S5_CARRY_VREGS // 2)
    ch = D_MIX // S5_GB
    lane = lambda c: slice(c * LANE, (c + 1) * LANE)
    steps_of = lambda row: pl.ds(row, tb, stride=n_rows)
    rows_of = lambda t: pl.ds(pl.multiple_of(t * n_rows, SUBLANE), n_rows)
    for d, (u_ref, y_ref) in enumerate(((uf_ref, yf_ref), (ub_ref, yb_ref))):
        u = u_ref[...].reshape(blk, D_MIX).astype(BF16)
        for k in range(S5_GB):
            bu = _dot(u[:, k * ch:(k + 1) * ch], bm_ref[d, k])
            for c in range(2 * n_re):
                for b in range(bsz):
                    x_scr[c, steps_of(k * bsz + b), :] = bu[b * tb:(b + 1) * tb, lane(c)]
        for c0 in range(0, n_re, group):
            tiles = list(range(c0, c0 + group))
            ar = [ar_ref[d, :, lane(c)] for c in tiles]
            ai = [ai_ref[d, :, lane(c)] for c in tiles]

            def step(s, x, d=d, tiles=tiles, ar=ar, ai=ai):
                t = s if d == 0 else tb - 1 - s
                rows = rows_of(t)
                out = []
                for c, a_r, a_i, (xr, xi) in zip(tiles, ar, ai, x):
                    nr = a_r * xr - a_i * xi + x_scr[c, rows, :]
                    ni = a_r * xi + a_i * xr + x_scr[n_re + c, rows, :]
                    xo_scr[c, rows, :] = nr
                    xo_scr[n_re + c, rows, :] = ni
                    out.append((nr, ni))
                return tuple(out)

            init = tuple((carry_scr[d, :, lane(c)], carry_scr[d, :, lane(n_re + c)]) for c in tiles)
            fin = lax.fori_loop(0, tb, step, init, unroll=4)
            for c, (xr, xi) in zip(tiles, fin):
                carry_scr[d, :, lane(c)] = xr
                carry_scr[d, :, lane(n_re + c)] = xi
        ys = []
        for k in range(S5_GB):
            xs = jnp.concatenate(
                [jnp.concatenate([xo_scr[c, steps_of(k * bsz + b), :] for b in range(bsz)], axis=0)
                 for c in range(2 * n_re)], axis=-1)
            ys.append(_dot(xs.astype(BF16), cm_ref[k]))
        y_ref[...] = jnp.concatenate(ys, axis=-1).reshape(bsz, tb, D_MIX)

    @pl.when(i == pl.num_programs(1) - 1)
    def _():
        xf_ref[0] = carry_scr[...]


def s5_scan(z3, bmat, cmat, a_re, a_im, x0, tb, grp):
    bsz, t_len, _ = z3.shape
    n_t = t_len // tb
    n_rows = S5_GB * grp
    ucol = (A_PAD + 3 * D_MIX) // D_MIX
    full = lambda shape: pl.BlockSpec(shape, lambda g, i: (0,) * len(shape))
    state_spec = pl.BlockSpec((1, 2, n_rows, 2 * S5_GW), lambda g, i: (g, 0, 0, 0))
    y_shape = jax.ShapeDtypeStruct((bsz, t_len, D_MIX), F32)
    return pl.pallas_call(
        functools.partial(_s5_kernel, grp, tb),
        grid=(bsz // grp, n_t),
        in_specs=[pl.BlockSpec((grp, tb, D_MIX), lambda g, i: (g, i, ucol)),
                  pl.BlockSpec((grp, tb, D_MIX), lambda g, i: (g, n_t - 1 - i, ucol)),
                  full((2, S5_GB, D_MIX // S5_GB, 2 * S5_GW)), full((S5_GB, 2 * S5_GW, D_MIX // S5_GB)),
                  full((2, n_rows, S5_GW)), full((2, n_rows, S5_GW)), state_spec],
        out_specs=[pl.BlockSpec((grp, tb, D_MIX), lambda g, i: (g, i, 0)),
                   pl.BlockSpec((grp, tb, D_MIX), lambda g, i: (g, n_t - 1 - i, 0)),
                   state_spec],
        out_shape=[y_shape, y_shape, jax.ShapeDtypeStruct((bsz // grp, 2, n_rows, 2 * S5_GW), F32)],
        scratch_shapes=[pltpu.VMEM((2 * S5_GW // LANE, n_rows * tb, LANE), F32),
                        pltpu.VMEM((2 * S5_GW // LANE, n_rows * tb, LANE), F32),
                        pltpu.VMEM((2, n_rows, 2 * S5_GW), F32)],
        compiler_params=_cparams(("arbitrary", "arbitrary"), VMEM_LIMIT),
        name="s5_scan",
    )(z3, z3, bmat, cmat, a_re, a_im, x0)


def s5_rows(x, grp):
    n_g = x.shape[1] // grp
    return (x.reshape(2, n_g, grp, S5_GB, S5_GW).transpose(1, 0, 3, 2, 4)
            .reshape(n_g, 2, S5_GB * grp, S5_GW))


def s5_unrows(x, grp):
    n_g = x.shape[0]
    return x.reshape(n_g, 2, S5_GB, grp, S5_GW).transpose(1, 0, 3, 2, 4).reshape(2, n_g * grp, S5_N)


def _merge_kernel(gnf_ref, gnb_ref, bonus_ref, g_ref, yb_ref, ycf_ref, ycb_ref, u_ref, zg0_ref, zg1_ref, zg2_ref,
                  x_ref, mod_ref, s5d_ref, wglu_ref, bglu_ref, wb_ref, wout_ref, lng_ref, lnb_ref, x1_ref, h2_ref):
    ya = (gnf_ref[...] + gnb_ref[...] + bonus_ref[...]) * g_ref[...]
    yc = ycf_ref[...] + ycb_ref[...] + s5d_ref[...] * u_ref[...]
    yc = 0.5 * yc * (1.0 + jnp.tanh(math.sqrt(2.0 / math.pi) * (yc + 0.044715 * (yc * yc * yc))))
    yc = yc * _sigmoid(_dot(yc.astype(BF16), wglu_ref[...]) + bglu_ref[...])
    merged = (_dot(ya.astype(BF16), wb_ref[0]) * _sigmoid(zg0_ref[...])
              + _dot(yb_ref[...].astype(BF16), wb_ref[1]) * _sigmoid(zg1_ref[...])
              + _dot(yc.astype(BF16), wb_ref[2]) * _sigmoid(zg2_ref[...]))
    mo = _dot(merged.astype(BF16), wout_ref[...])
    m = mod_ref[0]
    x1 = _layer_norm(DN_ALPHA * x_ref[...] + m[2:3] * mo, lng_ref[...], lnb_ref[...])
    x1_ref[...] = x1
    h2_ref[...] = (x1 * (1.0 + m[4:5]) + m[3:4]).astype(BF16)


def merge_branches(gn, bonus, g, yb, yc, z, x, mods, mod_of_tile, p):
    n = x.shape[0]
    row = lambda w, col=0: pl.BlockSpec((TM, w), lambda i, col=col: (i, col))
    full = lambda shape: pl.BlockSpec(shape, lambda i: (0,) * len(shape))
    gb = (A_PAD + 4 * D_MIX) // D_MODEL
    return pl.pallas_call(
        _merge_kernel,
        grid=(n // TM,),
        in_specs=[row(D_MIX), row(D_MIX), row(D_MIX), row(D_MIX), row(D_MIX), row(D_MIX), row(D_MIX),
                  row(D_MIX, A_PAD // D_MIX + 3),
                  row(D_MODEL, gb), row(D_MODEL, gb + 1), row(D_MODEL, gb + 2),
                  row(D_MODEL),
                  pl.BlockSpec((1, SUBLANE, D_MODEL), lambda i: (mod_of_tile(i, TM), 0, 0)),
                  full((1, D_MIX)), full((D_MIX, D_MIX)), full((1, D_MIX)),
                  full((3, D_MIX, D_MODEL)), full((D_MODEL, D_MODEL)),
                  full((1, D_MODEL)), full((1, D_MODEL))],
        out_specs=[row(D_MODEL), row(D_MODEL)],
        out_shape=[jax.ShapeDtypeStruct((n, D_MODEL), F32), jax.ShapeDtypeStruct((n, D_MODEL), BF16)],
        compiler_params=_cparams(("arbitrary",), VMEM_LIMIT),
        name="merge_branches",
    )(gn[0], gn[1], bonus, g, yb, yc[0], yc[1], z, z, z, z, x, mods,
      p['s5_d'], p['w_glu'], p['b_glu'], p['w_branch'], p['w_out'], p['ln1_g'], p['ln1_b'])


def _first_max(val, idx, big):
    m = jnp.max(jnp.max(val, axis=1, keepdims=True), axis=0, keepdims=True)
    cand = jnp.where(val == m, idx, big)
    first = jnp.min(jnp.min(cand, axis=1, keepdims=True), axis=0, keepdims=True)
    return m, idx == first


def _router_kernel(h_ref, wt_ref, bias_ref, gates_ref):
    per = N_EXPERTS // N_GROUPS
    logits = _dot_nt(wt_ref[...], h_ref[...])
    n = logits.shape[1]
    scores = _sigmoid(logits).reshape(N_GROUPS, per, n)
    sel = scores + bias_ref[...]
    e_idx = (lax.broadcasted_iota(jnp.int32, (N_GROUPS, per, n), 0) * per
             + lax.broadcasted_iota(jnp.int32, (N_GROUPS, per, n), 1))
    in_grp = lax.broadcasted_iota(jnp.int32, (N_GROUPS, per, n), 1)
    m1 = jnp.max(sel, axis=1, keepdims=True)
    first = jnp.min(jnp.where(sel == m1, in_grp, per), axis=1, keepdims=True)
    m2 = jnp.max(jnp.where(in_grp == first, -jnp.inf, sel), axis=1, keepdims=True)
    grp = m1 + m2
    g_idx = lax.broadcasted_iota(jnp.int32, (N_GROUPS, 1, n), 0)
    gmask = jnp.zeros((N_GROUPS, 1, n), F32)
    for _ in range(TOPK_GROUPS):
        _, hit = _first_max(grp, g_idx, N_GROUPS)
        gmask = jnp.where(hit, 1.0, gmask)
        grp = jnp.where(hit, -jnp.inf, grp)
    cur = jnp.where(jnp.broadcast_to(gmask, sel.shape) > 0.0, sel, NEG)
    w = jnp.zeros((N_GROUPS, per, n), F32)
    for _ in range(TOP_K):
        _, hit = _first_max(cur, e_idx, N_EXPERTS)
        w = jnp.where(hit, scores, w)
        cur = jnp.where(hit, -jnp.inf, cur)
    tot = jnp.sum(jnp.sum(w, axis=1, keepdims=True), axis=0, keepdims=True)
    gates = (ROUTED_SCALE * w / tot).reshape(N_EXPERTS, n)
    hi = gates.astype(BF16).astype(F32)
    lo = (gates - hi).astype(BF16).astype(F32)
    gates_ref[...] = jnp.concatenate([hi, lo], axis=0).T.astype(BF16)


def moe_router(h2, router_wt, router_bias):
    n = h2.shape[0]
    return pl.pallas_call(
        _router_kernel,
        grid=(n // TM,),
        in_specs=[pl.BlockSpec((TM, D_MODEL), lambda i: (i, 0)),
                  pl.BlockSpec((N_EXPERTS, D_MODEL), lambda i: (0, 0)),
                  pl.BlockSpec((N_GROUPS, N_EXPERTS // N_GROUPS, 1), lambda i: (0, 0, 0))],
        out_specs=pl.BlockSpec((TM, 2 * N_EXPERTS), lambda i: (i, 0)),
        out_shape=jax.ShapeDtypeStruct((n, 2 * N_EXPERTS), BF16),
        compiler_params=_cparams(("arbitrary",)),
        name="moe_router",
    )(h2, router_wt, router_bias)


def _moe_kernel(ec, h_ref, gates_ref, x1_ref, mod_ref, wg_ref, wu_ref, wd_ref, ex_ref,
                sg_ref, su_ref, sd_ref, lng_ref, lnb_ref, out_ref, acc_scr):
    j = pl.program_id(1)
    h = h_ref[...]

    @pl.when(j == 0)
    def _():
        sh = _dot(h, sg_ref[...])
        sh = sh * _sigmoid(sh) * _dot(h, su_ref[...])
        acc_scr[...] = _dot(sh.astype(BF16), sd_ref[...])

    gexp = _dot(gates_ref[...], ex_ref[0])
    wg = jnp.concatenate([wg_ref[e] for e in range(ec)], axis=1)
    wu = jnp.concatenate([wu_ref[e] for e in range(ec)], axis=1)
    hg = _dot(h, wg)
    hu = _dot(h, wu)
    act = hg * _sigmoid(hg) * hu * gexp
    acc_scr[...] += _dot(act.astype(BF16), wd_ref[...])

    @pl.when(j == pl.num_programs(1) - 1)
    def _():
        m = mod_ref[0]
        out_ref[...] = _layer_norm(DN_ALPHA * x1_ref[...] + m[5:6] * acc_scr[...], lng_ref[...], lnb_ref[...])


def moe_ffn(h2, gates, x1, mods, mod_of_tile, p, tm=1024, ec=MOE_EC):
    n = h2.shape[0]
    wcols = ec * D_EXPERT
    n_j = N_EXPERTS // ec
    full = lambda shape: pl.BlockSpec(shape, lambda i, j: (0,) * len(shape))
    return pl.pallas_call(
        functools.partial(_moe_kernel, ec),
        grid=(n // tm, n_j),
        in_specs=[pl.BlockSpec((tm, D_MODEL), lambda i, j: (i, 0)),
                  pl.BlockSpec((tm, 2 * N_EXPERTS), lambda i, j: (i, 0)),
                  pl.BlockSpec((tm, D_MODEL), lambda i, j: (i, 0)),
                  pl.BlockSpec((1, SUBLANE, D_MODEL), lambda i, j: (mod_of_tile(i, tm), 0, 0)),
                  pl.BlockSpec((ec, D_MODEL, D_EXPERT), lambda i, j: (j, 0, 0)),
                  pl.BlockSpec((ec, D_MODEL, D_EXPERT), lambda i, j: (j, 0, 0)),
                  pl.BlockSpec((wcols, D_MODEL), lambda i, j: (j, 0)),
                  pl.BlockSpec((1, 2 * N_EXPERTS, wcols), lambda i, j: (j, 0, 0)),
                  full((D_MODEL, D_SHARED)), full((D_MODEL, D_SHARED)), full((D_SHARED, D_MODEL)),
                  full((1, D_MODEL)), full((1, D_MODEL))],
        out_specs=pl.BlockSpec((tm, D_MODEL), lambda i, j: (i, 0)),
        out_shape=jax.ShapeDtypeStruct((n, D_MODEL), F32),
        scratch_shapes=[pltpu.VMEM((tm, D_MODEL), F32)],
        compiler_params=_cparams(("arbitrary", "arbitrary"), VMEM_LIMIT),
        name="moe_ffn",
    )(h2, gates, x1, mods, p['wg'], p['wu'], p['wd'], p['expand'],
      p['sh_g'], p['sh_u'], p['sh_d'], p['ln2_g'], p['ln2_b'])


def _block_diag2(m):
    z = jnp.zeros_like(m[0])
    return jnp.concatenate([jnp.concatenate([m[0], z], axis=1), jnp.concatenate([z, m[1]], axis=1)], axis=0)


_SMALL_PARAMS = ('rwkv_mu', 'rwkv_w2', 'rwkv_a2', 'rwkv_w0', 'rwkv_a0', 'rwkv_kk', 'rwkv_ka', 'rwkv_rk',
                 'rwkv_gn_g', 'rwkv_gn_b', 's5_a_re', 's5_a_im', 's5_log_dt', 's5_b_re', 's5_b_im', 's5_c_re',
                 's5_c_im', 's5_d', 's5_b_glu', 'ln1_g', 'ln1_b', 'router_bias', 'ln2_g', 'ln2_b', 'na_rpb')


def _small_params(g):
    p = {}
    p['mu'] = jnp.concatenate([g['rwkv_mu'], jnp.zeros((A_PAD - A_COLS,), F32)])[None, :]
    p['w2bd'] = _block_diag2(g['rwkv_w2']).astype(BF16)
    p['a2bd'] = _block_diag2(g['rwkv_a2']).astype(BF16)
    p['w0'] = g['rwkv_w0'].reshape(1, 2 * D_MIX)
    p['a0'] = g['rwkv_a0'].reshape(1, 2 * D_MIX)
    p['kkp'] = g['rwkv_kk'][None, :]
    p['ka'] = g['rwkv_ka'][None, :]
    p['rk'] = g['rwkv_rk'][None, :]
    p['gng'] = g['rwkv_gn_g'].reshape(HEADS // 2, 1, 2 * HD)
    p['gnb'] = g['rwkv_gn_b'].reshape(HEADS // 2, 1, 2 * HD)
    p['na_bias'] = na_bias_table(g['na_rpb'])
    a = lax.complex(g['s5_a_re'], g['s5_a_im'])
    dt = jnp.exp(g['s5_log_dt'])[..., None]
    a_bar = jnp.exp(dt * a)
    b_bar = ((a_bar - 1.0) / a)[..., None] * lax.complex(g['s5_b_re'], g['s5_b_im'])
    eye_g = jnp.eye(C_GROUPS // S5_GB, dtype=F32)
    gpb = C_GROUPS // S5_GB
    bd = lambda m: jnp.einsum('dkgph,gj->dkghjp', m.reshape(2, S5_GB, gpb, C_STATE, C_GROUP),
                              eye_g).reshape(2, S5_GB, D_MIX // S5_GB, S5_GW)
    p['s5_bmat'] = jnp.concatenate([bd(b_bar.real), bd(b_bar.imag)], axis=-1).astype(BF16)
    cd = lambda m: jnp.einsum('kghp,gj->kjpgh', m.reshape(S5_GB, gpb, C_GROUP, C_STATE),
                              eye_g).reshape(S5_GB, S5_GW, D_MIX // S5_GB)
    p['s5_cmat'] = jnp.concatenate([cd(g['s5_c_re']), -cd(g['s5_c_im'])], axis=1).astype(BF16)
    p['s5_ar'] = a_bar.real.reshape(2, 1, S5_N)
    p['s5_ai'] = a_bar.imag.reshape(2, 1, S5_N)
    p['s5_d'] = g['s5_d'][None, :]
    p['b_glu'] = g['s5_b_glu'][None, :]
    p['ln1_g'] = g['ln1_g'][None, :]
    p['ln1_b'] = g['ln1_b'][None, :]
    p['router_bias'] = g['router_bias'].reshape(N_GROUPS, N_EXPERTS // N_GROUPS, 1)
    p['ln2_g'] = g['ln2_g'][None, :]
    p['ln2_b'] = g['ln2_b'][None, :]
    return p


def _layer_weights(P, l):
    g = lambda name: P[name][l]
    w_in = g('w_in')
    pad = jnp.zeros((D_MODEL, A_PAD - A_COLS), F32)
    p = {}
    p['w_in'] = jnp.concatenate([w_in[:, :A_COLS], pad, w_in[:, A_COLS:]], axis=1).astype(BF16)
    p['g2'] = g('rwkv_g2').astype(BF16)
    hid = np.arange(D_MIX) // HD
    p['seg_ones'] = jnp.asarray((hid[:, None] == hid[None, :]).astype(np.float32), dtype=BF16)
    p['w_glu'] = g('s5_w_glu').astype(BF16)
    p['w_branch'] = g('w_branch').astype(BF16)
    p['w_out'] = g('w_out').astype(BF16)
    p['router_wt'] = g('router_w').T.astype(BF16)
    ecols = N_EXPERTS * D_EXPERT
    p['wg'] = g('exp_w_gate').astype(BF16)
    p['wu'] = g('exp_w_up').astype(BF16)
    p['wd'] = g('exp_w_down').reshape(ecols, D_MODEL).astype(BF16)
    p['sh_g'] = g('sh_w_gate').astype(BF16)
    p['sh_u'] = g('sh_w_up').astype(BF16)
    p['sh_d'] = g('sh_w_down').astype(BF16)
    return p


def _expand_table(ec):
    n_j = N_EXPERTS // ec
    t = np.zeros((n_j, 2 * N_EXPERTS, ec * D_EXPERT), np.float32)
    for e in range(N_EXPERTS):
        j, q = divmod(e, ec)
        t[j, e, q * D_EXPERT:(q + 1) * D_EXPERT] = 1.0
        t[j, N_EXPERTS + e, q * D_EXPERT:(q + 1) * D_EXPERT] = 1.0
    return jnp.asarray(t, dtype=BF16)


def kernel(x_prompt, x_sample, c, cache_na_k, cache_na_v, state_rwkv, state_s5_re, state_s5_im, c_ctx, w_ada, b_ada, w_in, rwkv_mu, rwkv_w0, rwkv_w2, rwkv_a0, rwkv_a2, rwkv_g2, rwkv_kk, rwkv_ka, rwkv_rk, rwkv_gn_g, rwkv_gn_b, na_rpb, s5_a_re, s5_a_im, s5_log_dt, s5_b_re, s5_b_im, s5_c_re, s5_c_im, s5_d, s5_w_glu, s5_b_glu, w_branch, w_out, ln1_g, ln1_b, router_w, router_bias, exp_w_gate, exp_w_up, exp_w_down, sh_w_gate, sh_w_up, sh_w_down, ln2_g, ln2_b):
    P = dict(w_in=w_in, rwkv_mu=rwkv_mu, rwkv_w0=rwkv_w0, rwkv_w2=rwkv_w2, rwkv_a0=rwkv_a0, rwkv_a2=rwkv_a2,
             rwkv_g2=rwkv_g2, rwkv_kk=rwkv_kk, rwkv_ka=rwkv_ka, rwkv_rk=rwkv_rk, rwkv_gn_g=rwkv_gn_g,
             rwkv_gn_b=rwkv_gn_b, s5_a_re=s5_a_re, s5_a_im=s5_a_im, s5_log_dt=s5_log_dt, s5_b_re=s5_b_re,
             s5_b_im=s5_b_im, s5_c_re=s5_c_re, s5_c_im=s5_c_im, s5_d=s5_d, s5_w_glu=s5_w_glu,
             s5_b_glu=s5_b_glu, w_branch=w_branch, w_out=w_out, ln1_g=ln1_g, ln1_b=ln1_b, router_w=router_w,
             router_bias=router_bias, exp_w_gate=exp_w_gate, exp_w_up=exp_w_up, exp_w_down=exp_w_down,
             sh_w_gate=sh_w_gate, sh_w_up=sh_w_up, sh_w_down=sh_w_down, ln2_g=ln2_g, ln2_b=ln2_b)
    bc, tc, _ = x_prompt.shape
    bl, tl, _ = x_sample.shape
    depth = w_in.shape[0]

    cond = jnp.concatenate([c_ctx[None, :], c, jnp.zeros((SUBLANE - 1 - bl, D_MODEL), F32)], axis=0)
    ada = ada_modulation(cond, w_ada, b_ada)
    masks = jnp.asarray(_chunk_masks())
    expand = _expand_table(MOE_EC)

    paths = {
        'ctx': dict(bsz=bc, t=tc, x=x_prompt.reshape(bc * tc, D_MODEL), mod_of_tile=lambda i, tm: 0),
        'lat': dict(bsz=bl, t=tl, x=x_sample.reshape(bl * tl, D_MODEL),
                    mod_of_tile=lambda i, tm: 1 + (i * tm) // tl),
    }
    for q in paths.values():
        tiles = q['t'] // TM
        pos = np.arange(q['bsz'] * tiles) % tiles
        q['seq_tiles'] = jnp.asarray(np.concatenate([np.full_like(pos, tiles), pos]).astype(np.int32))
        q['sched'] = rwkv_schedule([q['t']] * q['bsz'])
        q['s5_tb'] = min(S5_ROWS // S5_GRP, q['t'])
        assert q['bsz'] % S5_GRP == 0 and q['s5_tb'] % SUBLANE == 0 and q['t'] % q['s5_tb'] == 0

    P['na_rpb'] = na_rpb
    small = jax.vmap(_small_params)({k: P[k] for k in _SMALL_PARAMS})
    mods_all = jnp.concatenate([ada[:, :1 + bl].reshape(depth, 1 + bl, 6, D_MODEL),
                                jnp.zeros((depth, 1 + bl, SUBLANE - 6, D_MODEL), F32)], axis=2)
    new_k, new_v, new_rwkv, new_s5 = [], [], [], []
    for l in range(depth):
        p = _layer_weights(P, l)
        p.update({k: v[l] for k, v in small.items()})
        p['expand'] = expand
        mods = mods_all[l]
        bias_tab = p['na_bias']
        for name, q in paths.items():
            bsz, t_len, x, mod_of_tile = q['bsz'], q['t'], q['x'], q['mod_of_tile']
            z = in_projection(x, mods, p['w_in'], mod_of_tile)

            r, v, kk, ld, kd, bd, g, bonus = rwkv_prep(z, q['seq_tiles'], p)
            if name == 'ctx':
                s0t = jnp.zeros((bsz, 2, HEADS // 2, 2 * HD, 2 * HD), F32)
            else:
                s0t = rwkv_pack_state(state_rwkv[:, l])
            gn_f, gn_b, s_fin = rwkv_scan((r, v, kk, ld, kd, bd), masks, p['gng'], p['gnb'], s0t, q['sched'])

            if name == 'ctx':
                yb, k_h, v_h = ctx_attention(z, bsz, t_len)
            else:
                yb = na_attention(z, cache_na_k, cache_na_v, l, bias_tab, bsz, t_len)

            if name == 'ctx':
                x0 = jnp.zeros((bsz // S5_GRP, 2, S5_GB * S5_GRP, 2 * S5_GW), F32)
            else:
                x0 = jnp.concatenate(
                    [s5_rows(jnp.swapaxes(s[:, l].reshape(bsz, 2, S5_N), 0, 1), S5_GRP)
                     for s in (state_s5_re, state_s5_im)], axis=-1)
            a_re, a_im = (s5_rows(jnp.broadcast_to(a, (2, S5_GRP, S5_N)), S5_GRP)[0]
                          for a in (p['s5_ar'], p['s5_ai']))
            yc_f, yc_b, x_fin = s5_scan(z.reshape(bsz, t_len, IN_COLS_P), p['s5_bmat'], p['s5_cmat'],
                                        a_re, a_im, x0, q['s5_tb'], S5_GRP)
            yc = (yc_f.reshape(bsz * t_len, D_MIX), yc_b.reshape(bsz * t_len, D_MIX))

            x1, h2 = merge_branches((gn_f, gn_b), bonus, g, yb, yc, z, x, mods, mod_of_tile, p)
            gates = moe_router(h2, p['router_wt'], p['router_bias'])
            q['x'] = moe_ffn(h2, gates, x1, mods, mod_of_tile, p)
            if name == 'ctx':
                new_k.append(k_h)
                new_v.append(v_h)
                new_rwkv.append(jnp.swapaxes(s_fin, -1, -2))
                new_s5.append([jnp.swapaxes(s5_unrows(part, S5_GRP), 0, 1)
                               for part in (x_fin[..., :S5_GW], x_fin[..., S5_GW:])])

    s5_re, s5_im = (jnp.stack([layer[part] for layer in new_s5], axis=1).reshape(bc, depth, 2, C_GROUPS, C_STATE)
                    for part in range(2))
    return (paths['ctx']['x'].reshape(bc, tc, D_MODEL), paths['lat']['x'].reshape(bl, tl, D_MODEL),
            jnp.stack(new_k, axis=1), jnp.stack(new_v, axis=1), jnp.stack(new_rwkv, axis=1), s5_re, s5_im)
```

```python
import functools
import math

import numpy as np
import jax
import jax.numpy as jnp
from jax import lax
from jax.experimental import pallas as pl
from jax.experimental.pallas import tpu as pltpu

F32 = jnp.float32
BF16 = jnp.bfloat16

D_MODEL = 1024
DEPTH = 2
GRID_W = 64
D_MIX = 512
HEADS = 8
HD = 64
LORA = 128
WIN_R = 8
WIN_C = 16
C_GROUP = 16
C_GROUPS = D_MIX // C_GROUP
C_STATE = 64
S5_N = C_GROUPS * C_STATE
N_EXPERTS = 64
TOP_K = 8
N_GROUPS = 8
TOPK_GROUPS = 4
D_EXPERT = 128
D_SHARED = 128
ROUTED_SCALE = 2.5
LN_EPS = 1e-5
GN_EPS = 64e-5
NEG = -1e30
DN_ALPHA = (2 * DEPTH) ** 0.25
A_COLS = 3 * D_MIX + 3 * LORA
A_PAD = 2048
IN_COLS_P = A_PAD + 4 * D_MIX + 3 * D_MODEL

LANE = 128
SUBLANE = 8
TM = 256
CH = HD
RWKV_CHUNKS = 4
MOE_EC = 8
NA_QROWS = 8
NA_KROWS = NA_QROWS + WIN_R
S5_GB = 4
S5_GW = S5_N // S5_GB
S5_GRP = SUBLANE // S5_GB
S5_ROWS = 512
S5_CARRY_VREGS = 32
VMEM_LIMIT = 56 * 1024 * 1024


def _cparams(sem, vmem=None):
    return pltpu.CompilerParams(dimension_semantics=sem, vmem_limit_bytes=vmem)


def _sigmoid(x):
    return 1.0 / (1.0 + jnp.exp(-x))


def _dot(a, b):
    return jnp.dot(a, b, preferred_element_type=F32)


def _dot_nt(a, b):
    return lax.dot_general(a, b, (((1,), (1,)), ((), ())), preferred_element_type=F32)


def _split2(x):
    hi = x.astype(BF16)
    return hi, (x - hi.astype(F32)).astype(BF16)


def _split3(x):
    hi = x.astype(BF16)
    r1 = x - hi.astype(F32)
    mid = r1.astype(BF16)
    return hi, mid, (r1 - mid.astype(F32)).astype(BF16)


def _seg_sum(x, ones_bf16):
    return sum(_dot(part, ones_bf16) for part in _split3(x))


def _mm3(a, b):
    return _dot(a[0], b[0]) + (_dot(a[1], b[0]) + _dot(a[0], b[1]))


def _layer_norm(x, g, b):
    mu = jnp.mean(x, axis=-1, keepdims=True)
    xc = x - mu
    var = jnp.mean(xc * xc, axis=-1, keepdims=True)
    return xc * lax.rsqrt(var + LN_EPS) * g + b


def _ada_kernel(c_ref, w_ref, b_ref, o_ref):
    c = c_ref[...]
    s = c * _sigmoid(c)
    o_ref[0] = _dot(s.astype(BF16), w_ref[0].astype(BF16)) + b_ref[0]


def ada_modulation(cond, w_ada, b_ada):
    n_l, d, n6 = w_ada.shape
    tn = 1536
    return pl.pallas_call(
        _ada_kernel,
        grid=(n_l, n6 // tn),
        in_specs=[pl.BlockSpec((SUBLANE, d), lambda l, j: (0, 0)),
                  pl.BlockSpec((1, d, tn), lambda l, j: (l, 0, j)),
                  pl.BlockSpec((1, 1, tn), lambda l, j: (l, 0, j))],
        out_specs=pl.BlockSpec((1, SUBLANE, tn), lambda l, j: (l, 0, j)),
        out_shape=jax.ShapeDtypeStruct((n_l, SUBLANE, n6), F32),
        compiler_params=_cparams(("arbitrary", "arbitrary")),
        name="ada_modulation",
    )(cond, w_ada, b_ada.reshape(n_l, 1, n6))


def _inproj_kernel(x_ref, mod_ref, w_ref, z_ref, h_scr):
    @pl.when(pl.program_id(1) == 0)
    def _():
        m = mod_ref[0]
        h_scr[...] = (x_ref[...] * (1.0 + m[1:2]) + m[0:1]).astype(BF16)
    z_ref[...] = _dot(h_scr[...], w_ref[...])


def in_projection(x, mods, w_in_p, mod_of_tile, tm=2048, tn=1024):
    n, d = x.shape
    cols = w_in_p.shape[1]
    return pl.pallas_call(
        _inproj_kernel,
        grid=(n // tm, cols // tn),
        in_specs=[pl.BlockSpec((tm, d), lambda i, j: (i, 0)),
                  pl.BlockSpec((1, SUBLANE, d), lambda i, j: (mod_of_tile(i, tm), 0, 0)),
                  pl.BlockSpec((d, tn), lambda i, j: (0, j))],
        out_specs=pl.BlockSpec((tm, tn), lambda i, j: (i, j)),
        out_shape=jax.ShapeDtypeStruct((n, cols), F32),
        scratch_shapes=[pltpu.VMEM((tm, d), BF16)],
        compiler_params=_cparams(("arbitrary", "arbitrary"), VMEM_LIMIT),
        name="in_projection",
    )(x, mods, w_in_p)


def _rwkv_prep_kernel(seq_tiles_ref, z_ref, zp_ref, zn_ref, mu_ref, w2_ref, a2_ref, g2_ref, w0_ref, a0_ref,
                      kkp_ref, ka_ref, rk_ref, e_ref,
                      r_ref, v_ref, kk_ref, ld_ref, kd_ref, bd_ref, g_ref, bonus_ref):
    i = pl.program_id(0)
    tiles = seq_tiles_ref[i]
    pos = seq_tiles_ref[i + pl.num_programs(0)]
    x = z_ref[...]
    tm = x.shape[0]
    rows = lax.broadcasted_iota(jnp.int32, x.shape, 0)
    prev_row = jnp.where(pos == 0, 0.0, zp_ref[SUBLANE - 1:SUBLANE, :])
    next_row = jnp.where(pos == tiles - 1, 0.0, zn_ref[0:1, :])
    xm1 = jnp.where(rows == 0, prev_row, pltpu.roll(x, 1, axis=0))
    xp1 = jnp.where(rows == tm - 1, next_row, pltpu.roll(x, tm - 1, axis=0))
    za = x + mu_ref[...] * (0.5 * (xm1 + xp1) - x)

    r = za[:, 0:D_MIX]
    k = za[:, D_MIX:2 * D_MIX]
    v = za[:, 2 * D_MIX:3 * D_MIX]
    lw = za[:, 3 * D_MIX:3 * D_MIX + LORA]
    la = za[:, 3 * D_MIX + LORA:3 * D_MIX + 2 * LORA]
    lg = za[:, 3 * D_MIX + 2 * LORA:3 * D_MIX + 3 * LORA]

    w_both = w0_ref[...] + _dot(jnp.tanh(lw).astype(BF16), w2_ref[...])
    a_both = _sigmoid(a0_ref[...] + _dot(la.astype(BF16), a2_ref[...]))
    g_ref[...] = _dot(_sigmoid(lg).astype(BF16), g2_ref[...])

    e = e_ref[...]
    kks = k * kkp_ref[...]
    nrm = jnp.sqrt(_seg_sum(kks * kks, e))
    kk = kks / jnp.maximum(nrm, 1e-12)
    bonus = jnp.zeros_like(v)
    r_ref[...] = r
    v_ref[...] = v
    kk_ref[...] = kk
    for d in range(2):
        w = w_both[:, d * D_MIX:(d + 1) * D_MIX]
        a = a_both[:, d * D_MIX:(d + 1) * D_MIX]
        ld = -math.exp(-0.5) * _sigmoid(w)
        kd = k * (1.0 + (a - 1.0) * ka_ref[...])
        bd = kk * a
        bonus = bonus + _seg_sum(r * kd * rk_ref[...], e) * v
        ld_ref[d] = ld
        kd_ref[d] = kd
        bd_ref[d] = bd
    bonus_ref[...] = bonus


def rwkv_prep(z, seq_tiles, p):
    n = z.shape[0]
    nt = n // TM
    halo = TM // SUBLANE
    nb8 = n // SUBLANE
    tok = jax.ShapeDtypeStruct((n, D_MIX), F32)
    tok2 = jax.ShapeDtypeStruct((2, n, D_MIX), F32)
    full = lambda shape: pl.BlockSpec(shape, lambda i, s: (0,) * len(shape))
    tok_spec = pl.BlockSpec((TM, D_MIX), lambda i, s: (i, 0))
    tok2_spec = pl.BlockSpec((2, TM, D_MIX), lambda i, s: (0, i, 0))
    grid_spec = pltpu.PrefetchScalarGridSpec(
        num_scalar_prefetch=1,
        grid=(nt,),
        in_specs=[pl.BlockSpec((TM, A_PAD), lambda i, s: (i, 0)),
                  pl.BlockSpec((SUBLANE, A_PAD), lambda i, s: (jnp.maximum(i * halo - 1, 0), 0)),
                  pl.BlockSpec((SUBLANE, A_PAD), lambda i, s: (jnp.minimum((i + 1) * halo, nb8 - 1), 0)),
                  full((1, A_PAD)), full((LORA, 2 * D_MIX)), full((LORA, 2 * D_MIX)), full((LORA, D_MIX)),
                  full((1, 2 * D_MIX)), full((1, 2 * D_MIX)), full((1, D_MIX)), full((1, D_MIX)),
                  full((1, D_MIX)), full((D_MIX, D_MIX))],
        out_specs=[tok_spec, tok_spec, tok_spec, tok2_spec, tok2_spec, tok2_spec, tok_spec, tok_spec],
    )
    return pl.pallas_call(
        _rwkv_prep_kernel,
        grid_spec=grid_spec,
        out_shape=[tok, tok, tok, tok2, tok2, tok2, tok, tok],
        compiler_params=_cparams(("arbitrary",), VMEM_LIMIT),
        name="rwkv_prep",
    )(seq_tiles, z, z, z, p['mu'], p['w2bd'], p['a2bd'], p['g2'], p['w0'], p['a0'],
      p['kkp'], p['ka'], p['rk'], p['seg_ones'])


def _chunk_masks():
    t = np.arange(CH)
    fwd_incl = (t[:, None] >= t[None, :])
    out = []
    for incl in (fwd_incl, fwd_incl.T):
        strict = incl & (t[:, None] != t[None, :])
        ms = [incl, strict, strict & ((t[:, None] // 8) == (t[None, :] // 8))]
        for m in (8, 16, 32):
            ms.append(strict & ((t[:, None] // (2 * m)) == (t[None, :] // (2 * m)))
                      & ((t[:, None] // m) != (t[None, :] // m)))
        out.append(np.stack(ms))
    masks = np.stack(out).astype(np.float32)
    return np.concatenate([masks, masks], axis=-1)


def _rwkv_pair_kernel(cb, sched_ref, *refs):
    dir_refs = (refs[0:6], refs[6:12])
    m_ref, gng_ref, gnb_ref, s0_ref = refs[12:16]
    y_refs = refs[16:18]
    sfin_ref, s_scr = refs[18:20]
    step_id = pl.program_id(0)

    @pl.when(sched_ref[_SCHED_FIRST, step_id] == 1)
    def _():
        s_scr[...] = s0_ref[0]

    pw = 2 * HD
    ri = lax.broadcasted_iota(jnp.int32, (pw, pw), 0)
    ci = lax.broadcasted_iota(jnp.int32, (pw, pw), 1)
    eye_bd = (ri == ci).astype(F32)
    mask_bd = ((ri // HD) == (ci // HD)).astype(F32)
    eye12 = (lax.broadcasted_iota(jnp.int32, (CH, pw), 0)
             == lax.broadcasted_iota(jnp.int32, (CH, pw), 1) % HD).astype(F32)
    left = lax.broadcasted_iota(jnp.int32, (1, pw), 1) < HD

    bf = lambda x: x.astype(BF16)
    each = lambda f, *cols: [f(*args) for args in zip(*cols)]
    rows = lambda j: slice(j * CH, (j + 1) * CH)
    lanes_of = lambda p: slice(p * pw, (p + 1) * pw)
    stack = lambda *xs: jnp.concatenate(xs, axis=0)
    side = lambda *xs: jnp.concatenate(xs, axis=1)
    top, mid = slice(0, CH), slice(CH, 2 * CH)
    zero = jnp.zeros((), BF16)
    bd = lambda x: stack(jnp.where(left, x, zero), jnp.where(left, zero, x))

    lanes = [(d, p) for d in range(2) for p in range(HEADS // 2)]
    chains = [(d, j, p) for d, p in lanes for j in range(cb)]
    msk = lambda k: [m_ref[d, k] for d, _, _ in chains]
    incl, strict, m8 = msk(0), msk(1), msk(2)
    incl_b = each(lambda m: bf(m[:, :CH]), incl)
    get = lambda k: [dir_refs[d][k][rows(j), lanes_of(p)] if k < 3 else dir_refs[d][k][0, rows(j), lanes_of(p)]
                     for d, j, p in chains]
    R, V, KK, LD, Kd, Bd = (get(k) for k in range(6))
    L = each(lambda m, x: sum(_dot(m, part) for part in _split3(x)), incl_b, LD)
    ltot = each(lambda x: jnp.sum(x, axis=0, keepdims=True), LD)
    e_nl = each(lambda l: jnp.exp(-l), L)
    e_rest = each(lambda l, lt: jnp.exp(lt - l), L, ltot)
    Qb = each(lambda kk, l, ld: bf(kk * jnp.exp(l - ld)), KK, L, LD)
    Rh = each(lambda r, l: r * jnp.exp(l), R, L)
    QRb = each(lambda q, rh: stack(q, bf(rh)), Qb, Rh)
    Btd = each(lambda b, e: bd(bf(b * e)), Bd, e_nl)
    Ktd = each(lambda k, e: bd(bf(k * e)), Kd, e_nl)
    BcTb = each(lambda b, e: bf((b * e).T), Bd, e_rest)
    KcTb = each(lambda k, e: bf((k * e).T), Kd, e_rest)
    Vb = each(bf, V)
    QRB = each(_dot_nt, QRb, Btd)
    QRK = each(_dot_nt, QRb, Ktd)
    Nl = each(lambda m, x: m * x[top], strict, QRB)
    Mrbb = each(lambda m, x: bf(m * x[mid]), incl, QRB)
    Mkb = each(lambda m, x: bf(m * x[top]), strict, QRK)
    Mrkb = each(lambda m, x: bf(m * x[mid]), incl, QRK)
    N8 = each(lambda m, n: m * n, m8, Nl)
    N8b = each(bf, N8)
    N2 = each(lambda a: _dot(a, bd(a)), N8b)
    N2b = each(bf, N2)
    N4 = each(lambda a: _dot(a, bd(a)), N2b)
    W = each(lambda a, b: _dot(bf(eye12 - a), bd(bf(eye12 + b))), N8, N2)
    W = each(lambda w, n4: _dot(bf(w), bd(bf(eye12 + n4))), W, N4)
    for lvl in range(3):
        Wb = each(bf, W)
        T = each(lambda m, n, w: _dot(bf(m * n), bd(w)), msk(3 + lvl), Nl, Wb)
        W = each(lambda w, wb, t: w - _dot(wb, bd(bf(t))), W, Wb, T)
    Wb = each(bf, W)
    Whb = each(lambda w, q: bf(_dot(w, bd(q))), Wb, Qb)
    XV = each(lambda mk, mrk, v: _dot(stack(mk, mrk), bd(v)), Mkb, Mrkb, Vb)
    U0b = each(lambda w, x: bf(-_dot(w, bd(bf(x[top])))), Wb, XV)
    XWU = each(lambda m, wh, u: _dot(m, side(bd(wh), bd(u))), Mrbb, Whb, U0b)
    BWU = each(lambda b, wh, u: _dot(b, side(wh, u)), BcTb, Whb, U0b)
    KV = each(_dot, KcTb, Vb)
    Y0 = each(lambda xwu, xv: xwu[:, pw:] + xv[mid], XWU, XV)
    Hd = each(lambda bwu, kv: mask_bd * (bwu[:, pw:] + kv), BWU, KV)
    RG = each(lambda rh, lt, xwu, bwu: _split2(stack(rh - xwu[:, :pw],
                                                     eye_bd * jnp.exp(lt) - mask_bd * bwu[:, :pw])),
              Rh, ltot, XWU, BWU)

    ST = [s_scr[d, p] for d, p in lanes]
    for step in range(cb):
        idx = [chains.index((d, step if d == 0 else cb - 1 - step, p)) for d, p in lanes]
        XS = [_mm3(RG[i], _split2(st)) for i, st in zip(idx, ST)]
        ST = [x[CH:] + Hd[i] for i, x in zip(idx, XS)]
        for i, x in zip(idx, XS):
            d, j, p = chains[i]
            y = x[top] + Y0[i]
            half_mean = lambda a: jnp.where(left, jnp.sum(jnp.where(left, a, 0.0), axis=-1, keepdims=True),
                                            jnp.sum(jnp.where(left, 0.0, a), axis=-1, keepdims=True)) * (1.0 / HD)
            yc = y - half_mean(y)
            var = half_mean(yc * yc)
            y_refs[d][rows(j), lanes_of(p)] = yc * lax.rsqrt(var + GN_EPS) * gng_ref[p] + gnb_ref[p]
    for (d, p), st in zip(lanes, ST):
        s_scr[d, p] = st

    @pl.when(sched_ref[_SCHED_LAST, step_id] == 1)
    def _():
        for d, p in lanes:
            st = s_scr[d, p]
            sfin_ref[0, d, 2 * p] = st[:HD, :HD]
            sfin_ref[0, d, 2 * p + 1] = st[HD:, HD:]


_SCHED_FWD, _SCHED_BWD, _SCHED_SEQ, _SCHED_FIRST, _SCHED_LAST = range(5)


def rwkv_schedule(seq_lens):
    blk_rows = RWKV_CHUNKS * CH
    cols, base = [], 0
    for s, t_len in enumerate(seq_lens):
        n_b = t_len // blk_rows
        for i in range(n_b):
            cols.append((base + i, base + n_b - 1 - i, s, int(i == 0), int(i == n_b - 1)))
        base += n_b
    return np.asarray(cols, np.int32).T


def rwkv_scan(prep, masks, gng, gnb, s0t, sched):
    r, v, kk, ld, kd, bd = prep
    n = r.shape[0]
    n_seq = s0t.shape[0]
    blk_rows = RWKV_CHUNKS * CH
    pairs, pw = HEADS // 2, 2 * HD
    in_specs, args = [], []
    for d, row in ((0, _SCHED_FWD), (1, _SCHED_BWD)):
        for a in (r, v, kk):
            in_specs.append(pl.BlockSpec((blk_rows, D_MIX), lambda i, s, row=row: (s[row, i], 0)))
            args.append(a)
        for a in (ld, kd, bd):
            in_specs.append(pl.BlockSpec((1, blk_rows, D_MIX), lambda i, s, row=row, d=d: (d, s[row, i], 0)))
            args.append(a)
    state_spec = pl.BlockSpec((1, 2, pairs, pw, pw), lambda i, s: (s[_SCHED_SEQ, i], 0, 0, 0, 0))
    in_specs += [pl.BlockSpec((2, 6, CH, pw), lambda i, s: (0, 0, 0, 0)),
                 pl.BlockSpec((pairs, 1, pw), lambda i, s: (0, 0, 0)),
                 pl.BlockSpec((pairs, 1, pw), lambda i, s: (0, 0, 0)),
                 state_spec]
    args += [masks, gng, gnb, s0t]
    grid_spec = pltpu.PrefetchScalarGridSpec(
        num_scalar_prefetch=1,
        grid=(sched.shape[1],),
        in_specs=in_specs,
        out_specs=[pl.BlockSpec((blk_rows, D_MIX), lambda i, s: (s[_SCHED_FWD, i], 0)),
                   pl.BlockSpec((blk_rows, D_MIX), lambda i, s: (s[_SCHED_BWD, i], 0)),
                   pl.BlockSpec((1, 2, HEADS, HD, HD), lambda i, s: (s[_SCHED_SEQ, i], 0, 0, 0, 0))],
        scratch_shapes=[pltpu.VMEM((2, pairs, pw, pw), F32)],
    )
    return pl.pallas_call(
        functools.partial(_rwkv_pair_kernel, RWKV_CHUNKS),
        grid_spec=grid_spec,
        out_shape=[jax.ShapeDtypeStruct((n, D_MIX), F32), jax.ShapeDtypeStruct((n, D_MIX), F32),
                   jax.ShapeDtypeStruct((n_seq, 2, HEADS, HD, HD), F32)],
        compiler_params=_cparams(("arbitrary",), VMEM_LIMIT),
        name="rwkv_scan",
    )(jnp.asarray(sched), *args)


def rwkv_pack_state(s):
    n = s.shape[0]
    st = jnp.swapaxes(s, -1, -2).reshape(n, 2, HEADS // 2, 2, HD, HD)
    eye2 = jnp.eye(2, dtype=s.dtype)
    return jnp.einsum('ndpakv,ab->ndpakbv', st, eye2).reshape(n, 2, HEADS // 2, 2 * HD, 2 * HD)


def _ctx_attn_kernel(q_ref, k_ref, v_ref, y_ref, ko_ref, vo_ref):
    scale = HD ** -0.5
    for h in range(HEADS):
        sl = slice(h * HD, (h + 1) * HD)
        q = q_ref[:, sl]
        k = k_ref[:, sl]
        v = v_ref[:, sl]
        ko_ref[0, h] = k
        vo_ref[0, h] = v
        s = _dot_nt(q.astype(BF16), k.astype(BF16)) * scale
        m = jnp.max(s, axis=-1, keepdims=True)
        e = jnp.exp(s - m)
        p = e / jnp.sum(e, axis=-1, keepdims=True)
        y_ref[:, sl] = _dot(p.astype(BF16), v.astype(BF16))


def ctx_attention(z, bsz, t_len):
    qb = A_PAD // D_MIX
    return pl.pallas_call(
        _ctx_attn_kernel,
        grid=(bsz,),
        in_specs=[pl.BlockSpec((t_len, D_MIX), lambda b: (b, qb)),
                  pl.BlockSpec((t_len, D_MIX), lambda b: (b, qb + 1)),
                  pl.BlockSpec((t_len, D_MIX), lambda b: (b, qb + 2))],
        out_specs=[pl.BlockSpec((t_len, D_MIX), lambda b: (b, 0)),
                   pl.BlockSpec((1, HEADS, t_len, HD), lambda b: (b, 0, 0, 0)),
                   pl.BlockSpec((1, HEADS, t_len, HD), lambda b: (b, 0, 0, 0))],
        out_shape=[jax.ShapeDtypeStruct((bsz * t_len, D_MIX), F32),
                   jax.ShapeDtypeStruct((bsz, HEADS, t_len, HD), F32),
                   jax.ShapeDtypeStruct((bsz, HEADS, t_len, HD), F32)],
        compiler_params=_cparams(("arbitrary",)),
        name="ctx_attention",
    )(z, z, z)


def _na_kernel(rows, nblk, *refs):
    q_ref = refs[0]
    k_refs = refs[1:1 + nblk]
    v_refs = refs[1 + nblk:1 + 2 * nblk]
    kc_ref, vc_ref, tab_ref, y_ref = refs[1 + 2 * nblk:]
    r0 = pl.program_id(1) * NA_QROWS
    u0 = jnp.clip(r0 - WIN_R // 2, 0, rows - NA_KROWS)
    scale = HD ** -0.5
    kwin = jnp.concatenate([kr[...] for kr in k_refs], axis=0).astype(BF16)
    vwin = jnp.concatenate([vr[...] for vr in v_refs], axis=0).astype(BF16)
    left = lax.broadcasted_iota(jnp.int32, (1, 2 * GRID_W), 1) < GRID_W
    bias = []
    for i in range(NA_QROWS):
        r = r0 + i
        rs = jnp.clip(r - WIN_R // 2, 0, rows - WIN_R)
        per_pair = []
        for jp in range(NA_KROWS // 2):
            kr = u0 + 2 * jp
            off = [jnp.where(jnp.logical_and(kr + e >= rs, kr + e < rs + WIN_R), 0.0, NEG) for e in range(2)]
            per_pair.append((jnp.clip(kr - r + WIN_R, 0, 2 * WIN_R - 1), jnp.where(left, off[0], off[1])))
        bias.append(per_pair)
    for h in range(HEADS):
        sl = slice(h * HD, (h + 1) * HD)
        q = q_ref[:, sl].astype(BF16)
        s_raw = _dot_nt(q, kwin[:, sl]) * scale
        s_loc = jnp.concatenate([
            jnp.concatenate([s_raw[i * GRID_W:(i + 1) * GRID_W, jp * 2 * GRID_W:(jp + 1) * 2 * GRID_W]
                             + tab_ref[h, bias[i][jp][0]] + bias[i][jp][1]
                             for jp in range(NA_KROWS // 2)], axis=1)
            for i in range(NA_QROWS)], axis=0)
        s_ctx = _dot_nt(q, kc_ref[0, 0, h].astype(BF16)) * scale
        m = jnp.maximum(jnp.max(s_loc, axis=-1, keepdims=True), jnp.max(s_ctx, axis=-1, keepdims=True))
        e_loc = jnp.exp(s_loc - m)
        e_ctx = jnp.exp(s_ctx - m)
        den = jnp.sum(e_loc, axis=-1, keepdims=True) + jnp.sum(e_ctx, axis=-1, keepdims=True)
        p_loc = (e_loc / den).astype(BF16)
        p_ctx = (e_ctx / den).astype(BF16)
        y_ref[:, sl] = _dot(p_loc, vwin[:, sl]) + _dot(p_ctx, vc_ref[0, 0, h].astype(BF16))


def na_bias_table(rpb):
    cq = np.arange(GRID_W)[:, None]
    ck = np.arange(GRID_W)[None, :]
    cs = np.clip(cq - WIN_C // 2, 0, GRID_W - WIN_C)
    col_bias = np.where((ck >= cs) & (ck < cs + WIN_C), 0.0, NEG).astype(np.float32)
    col_idx = np.clip(ck - cq + WIN_C - 1, 0, 2 * WIN_C - 2)
    rpb_col = rpb.astype(F32)[:, :, col_idx] + col_bias
    padded = jnp.pad(rpb_col, ((0, 0), (1, 1), (0, 0), (0, 0)), constant_values=NEG)
    return jnp.concatenate([padded[:, :-1], padded[:, 1:]], axis=-1)


def na_attention(z, k_ctx, v_ctx, layer, bias_tab, bsz, t_len):
    rows = t_len // GRID_W
    assert rows >= NA_KROWS and rows % NA_QROWS == 0, "latent grid too small for the row-group tiling"
    qb = A_PAD // D_MIX
    past = k_ctx.shape[3]
    blk_rows = WIN_R // 2
    nblk = NA_KROWS // blk_rows
    blk_tok = blk_rows * GRID_W
    q_tok = NA_QROWS * GRID_W

    def win_spec(j, col):
        def index(b, g):
            u0 = jnp.clip(g * NA_QROWS - WIN_R // 2, 0, rows - NA_KROWS)
            return (b * (rows // blk_rows) + u0 // blk_rows + j, col)
        return pl.BlockSpec((blk_tok, D_MIX), index)

    in_specs = ([pl.BlockSpec((q_tok, D_MIX), lambda b, g: (b * (rows // NA_QROWS) + g, qb))]
                + [win_spec(j, qb + 1) for j in range(nblk)]
                + [win_spec(j, qb + 2) for j in range(nblk)]
                + [pl.BlockSpec((1, 1, HEADS, past, HD), lambda b, g: (b, layer, 0, 0, 0)),
                   pl.BlockSpec((1, 1, HEADS, past, HD), lambda b, g: (b, layer, 0, 0, 0)),
                   pl.BlockSpec(bias_tab.shape, lambda b, g: (0, 0, 0, 0))])
    return pl.pallas_call(
        functools.partial(_na_kernel, rows, nblk),
        grid=(bsz, rows // NA_QROWS),
        in_specs=in_specs,
        out_specs=pl.BlockSpec((q_tok, D_MIX), lambda b, g: (b * (rows // NA_QROWS) + g, 0)),
        out_shape=jax.ShapeDtypeStruct((bsz * t_len, D_MIX), F32),
        compiler_params=_cparams(("arbitrary", "arbitrary"), VMEM_LIMIT),
        name="na_attention",
    )(*([z] * (1 + 2 * nblk)), k_ctx, v_ctx, bias_tab)


def _s5_kernel(bsz, tb, uf_ref, ub_ref, bm_ref, cm_ref, ar_ref, ai_ref, x0_ref, yf_ref, yb_ref, xf_ref,
               x_scr, xo_scr, carry_scr):
    i = pl.program_id(1)

    @pl.when(i == 0)
    def _():
        carry_scr[...] = x0_ref[0]

    n_re = S5_GW // LANE
    n_rows = S5_GB * bsz
    assert n_rows == SUBLANE
    blk = bsz * tb
    group = min(n_re, S5_CARRY_VREGS // 2)
    ch = D_MIX // S5_GB
    lane = lambda c: slice(c * LANE, (c + 1) * LANE)
    steps_of = lambda row: pl.ds(row, tb, stride=n_rows)
    rows_of = lambda t: pl.ds(pl.multiple_of(t * n_rows, SUBLANE), n_rows)
    for d, (u_ref, y_ref) in enumerate(((uf_ref, yf_ref), (ub_ref, yb_ref))):
        u = u_ref[...].reshape(blk, D_MIX).astype(BF16)
        for k in range(S5_GB):
            bu = _dot(u[:, k * ch:(k + 1) * ch], bm_ref[d, k])
            for c in range(2 * n_re):
                for b in range(bsz):
                    x_scr[c, steps_of(k * bsz + b), :] = bu[b * tb:(b + 1) * tb, lane(c)]
        for c0 in range(0, n_re, group):
            tiles = list(range(c0, c0 + group))
            ar = [ar_ref[d, :, lane(c)] for c in tiles]
            ai = [ai_ref[d, :, lane(c)] for c in tiles]

            def step(s, x, d=d, tiles=tiles, ar=ar, ai=ai):
                t = s if d == 0 else tb - 1 - s
                rows = rows_of(t)
                out = []
                for c, a_r, a_i, (xr, xi) in zip(tiles, ar, ai, x):
                    nr = a_r * xr - a_i * xi + x_scr[c, rows, :]
                    ni = a_r * xi + a_i * xr + x_scr[n_re + c, rows, :]
                    xo_scr[c, rows, :] = nr
                    xo_scr[n_re + c, rows, :] = ni
                    out.append((nr, ni))
                return tuple(out)

            init = tuple((carry_scr[d, :, lane(c)], carry_scr[d, :, lane(n_re + c)]) for c in tiles)
            fin = lax.fori_loop(0, tb, step, init, unroll=4)
            for c, (xr, xi) in zip(tiles, fin):
                carry_scr[d, :, lane(c)] = xr
                carry_scr[d, :, lane(n_re + c)] = xi
        ys = []
        for k in range(S5_GB):
            xs = jnp.concatenate(
                [jnp.concatenate([xo_scr[c, steps_of(k * bsz + b), :] for b in range(bsz)], axis=0)
                 for c in range(2 * n_re)], axis=-1)
            ys.append(_dot(xs.astype(BF16), cm_ref[k]))
        y_ref[...] = jnp.concatenate(ys, axis=-1).reshape(bsz, tb, D_MIX)

    @pl.when(i == pl.num_programs(1) - 1)
    def _():
        xf_ref[0] = carry_scr[...]


def s5_scan(z3, bmat, cmat, a_re, a_im, x0, tb, grp):
    bsz, t_len, _ = z3.shape
    n_t = t_len // tb
    n_rows = S5_GB * grp
    ucol = (A_PAD + 3 * D_MIX) // D_MIX
    full = lambda shape: pl.BlockSpec(shape, lambda g, i: (0,) * len(shape))
    state_spec = pl.BlockSpec((1, 2, n_rows, 2 * S5_GW), lambda g, i: (g, 0, 0, 0))
    y_shape = jax.ShapeDtypeStruct((bsz, t_len, D_MIX), F32)
    return pl.pallas_call(
        functools.partial(_s5_kernel, grp, tb),
        grid=(bsz // grp, n_t),
        in_specs=[pl.BlockSpec((grp, tb, D_MIX), lambda g, i: (g, i, ucol)),
                  pl.BlockSpec((grp, tb, D_MIX), lambda g, i: (g, n_t - 1 - i, ucol)),
                  full((2, S5_GB, D_MIX // S5_GB, 2 * S5_GW)), full((S5_GB, 2 * S5_GW, D_MIX // S5_GB)),
                  full((2, n_rows, S5_GW)), full((2, n_rows, S5_GW)), state_spec],
        out_specs=[pl.BlockSpec((grp, tb, D_MIX), lambda g, i: (g, i, 0)),
                   pl.BlockSpec((grp, tb, D_MIX), lambda g, i: (g, n_t - 1 - i, 0)),
                   state_spec],
        out_shape=[y_shape, y_shape, jax.ShapeDtypeStruct((bsz // grp, 2, n_rows, 2 * S5_GW), F32)],
        scratch_shapes=[pltpu.VMEM((2 * S5_GW // LANE, n_rows * tb, LANE), F32),
                        pltpu.VMEM((2 * S5_GW // LANE, n_rows * tb, LANE), F32),
                        pltpu.VMEM((2, n_rows, 2 * S5_GW), F32)],
        compiler_params=_cparams(("arbitrary", "arbitrary"), VMEM_LIMIT),
        name="s5_scan",
    )(z3, z3, bmat, cmat, a_re, a_im, x0)


def s5_rows(x, grp):
    n_g = x.shape[1] // grp
    return (x.reshape(2, n_g, grp, S5_GB, S5_GW).transpose(1, 0, 3, 2, 4)
            .reshape(n_g, 2, S5_GB * grp, S5_GW))


def s5_unrows(x, grp):
    n_g = x.shape[0]
    return x.reshape(n_g, 2, S5_GB, grp, S5_GW).transpose(1, 0, 3, 2, 4).reshape(2, n_g * grp, S5_N)


def _merge_kernel(gnf_ref, gnb_ref, bonus_ref, g_ref, yb_ref, ycf_ref, ycb_ref, u_ref, zg0_ref, zg1_ref, zg2_ref,
                  x_ref, mod_ref, s5d_ref, wglu_ref, bglu_ref, wb_ref, wout_ref, lng_ref, lnb_ref, x1_ref, h2_ref):
    ya = (gnf_ref[...] + gnb_ref[...] + bonus_ref[...]) * g_ref[...]
    yc = ycf_ref[...] + ycb_ref[...] + s5d_ref[...] * u_ref[...]
    yc = 0.5 * yc * (1.0 + jnp.tanh(math.sqrt(2.0 / math.pi) * (yc + 0.044715 * (yc * yc * yc))))
    yc = yc * _sigmoid(_dot(yc.astype(BF16), wglu_ref[...]) + bglu_ref[...])
    merged = (_dot(ya.astype(BF16), wb_ref[0]) * _sigmoid(zg0_ref[...])
              + _dot(yb_ref[...].astype(BF16), wb_ref[1]) * _sigmoid(zg1_ref[...])
              + _dot(yc.astype(BF16), wb_ref[2]) * _sigmoid(zg2_ref[...]))
    mo = _dot(merged.astype(BF16), wout_ref[...])
    m = mod_ref[0]
    x1 = _layer_norm(DN_ALPHA * x_ref[...] + m[2:3] * mo, lng_ref[...], lnb_ref[...])
    x1_ref[...] = x1
    h2_ref[...] = (x1 * (1.0 + m[4:5]) + m[3:4]).astype(BF16)


def merge_branches(gn, bonus, g, yb, yc, z, x, mods, mod_of_tile, p):
    n = x.shape[0]
    row = lambda w, col=0: pl.BlockSpec((TM, w), lambda i, col=col: (i, col))
    full = lambda shape: pl.BlockSpec(shape, lambda i: (0,) * len(shape))
    gb = (A_PAD + 4 * D_MIX) // D_MODEL
    return pl.pallas_call(
        _merge_kernel,
        grid=(n // TM,),
        in_specs=[row(D_MIX), row(D_MIX), row(D_MIX), row(D_MIX), row(D_MIX), row(D_MIX), row(D_MIX),
                  row(D_MIX, A_PAD // D_MIX + 3),
                  row(D_MODEL, gb), row(D_MODEL, gb + 1), row(D_MODEL, gb + 2),
                  row(D_MODEL),
                  pl.BlockSpec((1, SUBLANE, D_MODEL), lambda i: (mod_of_tile(i, TM), 0, 0)),
                  full((1, D_MIX)), full((D_MIX, D_MIX)), full((1, D_MIX)),
                  full((3, D_MIX, D_MODEL)), full((D_MODEL, D_MODEL)),
                  full((1, D_MODEL)), full((1, D_MODEL))],
        out_specs=[row(D_MODEL), row(D_MODEL)],
        out_shape=[jax.ShapeDtypeStruct((n, D_MODEL), F32), jax.ShapeDtypeStruct((n, D_MODEL), BF16)],
        compiler_params=_cparams(("arbitrary",), VMEM_LIMIT),
        name="merge_branches",
    )(gn[0], gn[1], bonus, g, yb, yc[0], yc[1], z, z, z, z, x, mods,
      p['s5_d'], p['w_glu'], p['b_glu'], p['w_branch'], p['w_out'], p['ln1_g'], p['ln1_b'])


def _first_max(val, idx, big):
    m = jnp.max(jnp.max(val, axis=1, keepdims=True), axis=0, keepdims=True)
    cand = jnp.where(val == m, idx, big)
    first = jnp.min(jnp.min(cand, axis=1, keepdims=True), axis=0, keepdims=True)
    return m, idx == first


def _router_kernel(h_ref, wt_ref, bias_ref, gates_ref):
    per = N_EXPERTS // N_GROUPS
    logits = _dot_nt(wt_ref[...], h_ref[...])
    n = logits.shape[1]
    scores = _sigmoid(logits).reshape(N_GROUPS, per, n)
    sel = scores + bias_ref[...]
    e_idx = (lax.broadcasted_iota(jnp.int32, (N_GROUPS, per, n), 0) * per
             + lax.broadcasted_iota(jnp.int32, (N_GROUPS, per, n), 1))
    in_grp = lax.broadcasted_iota(jnp.int32, (N_GROUPS, per, n), 1)
    m1 = jnp.max(sel, axis=1, keepdims=True)
    first = jnp.min(jnp.where(sel == m1, in_grp, per), axis=1, keepdims=True)
    m2 = jnp.max(jnp.where(in_grp == first, -jnp.inf, sel), axis=1, keepdims=True)
    grp = m1 + m2
    g_idx = lax.broadcasted_iota(jnp.int32, (N_GROUPS, 1, n), 0)
    gmask = jnp.zeros((N_GROUPS, 1, n), F32)
    for _ in range(TOPK_GROUPS):
        _, hit = _first_max(grp, g_idx, N_GROUPS)
        gmask = jnp.where(hit, 1.0, gmask)
        grp = jnp.where(hit, -jnp.inf, grp)
    cur = jnp.where(jnp.broadcast_to(gmask, sel.shape) > 0.0, sel, NEG)
    w = jnp.zeros((N_GROUPS, per, n), F32)
    for _ in range(TOP_K):
        _, hit = _first_max(cur, e_idx, N_EXPERTS)
        w = jnp.where(hit, scores, w)
        cur = jnp.where(hit, -jnp.inf, cur)
    tot = jnp.sum(jnp.sum(w, axis=1, keepdims=True), axis=0, keepdims=True)
    gates = (ROUTED_SCALE * w / tot).reshape(N_EXPERTS, n)
    gates_ref[...] = gates.T


def moe_router(h2, router_wt, router_bias):
    n = h2.shape[0]
    return pl.pallas_call(
        _router_kernel,
        grid=(n // TM,),
        in_specs=[pl.BlockSpec((TM, D_MODEL), lambda i: (i, 0)),
                  pl.BlockSpec((N_EXPERTS, D_MODEL), lambda i: (0, 0)),
                  pl.BlockSpec((N_GROUPS, N_EXPERTS // N_GROUPS, 1), lambda i: (0, 0, 0))],
        out_specs=pl.BlockSpec((TM, N_EXPERTS), lambda i: (i, 0)),
        out_shape=jax.ShapeDtypeStruct((n, N_EXPERTS), F32),
        compiler_params=_cparams(("arbitrary",)),
        name="moe_router",
    )(h2, router_wt, router_bias)


def _moe_kernel(ec, h_ref, gates_ref, x1_ref, mod_ref, wg_ref, wu_ref, wd_ref,
                sg_ref, su_ref, sd_ref, lng_ref, lnb_ref, out_ref, acc_scr):
    j = pl.program_id(1)
    h = h_ref[...]

    @pl.when(j == 0)
    def _():
        sh = _dot(h, sg_ref[...])
        sh = sh * _sigmoid(sh) * _dot(h, su_ref[...])
        acc_scr[...] = _dot(sh.astype(BF16), sd_ref[...])

    gcols = gates_ref[0]
    gexp = jnp.concatenate([jnp.broadcast_to(gcols[:, e:e + 1], (gcols.shape[0], D_EXPERT)) for e in range(ec)],
                           axis=1)
    wg = jnp.concatenate([wg_ref[e] for e in range(ec)], axis=1)
    wu = jnp.concatenate([wu_ref[e] for e in range(ec)], axis=1)
    hg = _dot(h, wg)
    hu = _dot(h, wu)
    act = hg * _sigmoid(hg) * hu * gexp
    acc_scr[...] += _dot(act.astype(BF16), wd_ref[...])

    @pl.when(j == pl.num_programs(1) - 1)
    def _():
        m = mod_ref[0]
        out_ref[...] = _layer_norm(DN_ALPHA * x1_ref[...] + m[5:6] * acc_scr[...], lng_ref[...], lnb_ref[...])


def moe_ffn(h2, gates, x1, mods, mod_of_tile, p, tm=1024, ec=MOE_EC):
    n = h2.shape[0]
    wcols = ec * D_EXPERT
    n_j = N_EXPERTS // ec
    full = lambda shape: pl.BlockSpec(shape, lambda i, j: (0,) * len(shape))
    return pl.pallas_call(
        functools.partial(_moe_kernel, ec),
        grid=(n // tm, n_j),
        in_specs=[pl.BlockSpec((tm, D_MODEL), lambda i, j: (i, 0)),
                  pl.BlockSpec((1, tm, ec), lambda i, j: (j, i, 0)),
                  pl.BlockSpec((tm, D_MODEL), lambda i, j: (i, 0)),
                  pl.BlockSpec((1, SUBLANE, D_MODEL), lambda i, j: (mod_of_tile(i, tm), 0, 0)),
                  pl.BlockSpec((ec, D_MODEL, D_EXPERT), lambda i, j: (j, 0, 0)),
                  pl.BlockSpec((ec, D_MODEL, D_EXPERT), lambda i, j: (j, 0, 0)),
                  pl.BlockSpec((wcols, D_MODEL), lambda i, j: (j, 0)),
                  full((D_MODEL, D_SHARED)), full((D_MODEL, D_SHARED)), full((D_SHARED, D_MODEL)),
                  full((1, D_MODEL)), full((1, D_MODEL))],
        out_specs=pl.BlockSpec((tm, D_MODEL), lambda i, j: (i, 0)),
        out_shape=jax.ShapeDtypeStruct((n, D_MODEL), F32),
        scratch_shapes=[pltpu.VMEM((tm, D_MODEL), F32)],
        compiler_params=_cparams(("arbitrary", "arbitrary"), VMEM_LIMIT),
        name="moe_ffn",
    )(h2, gates.reshape(n, n_j, ec).transpose(1, 0, 2), x1, mods, p['wg'], p['wu'], p['wd'],
      p['sh_g'], p['sh_u'], p['sh_d'], p['ln2_g'], p['ln2_b'])


def _block_diag2(m):
    z = jnp.zeros_like(m[0])
    return jnp.concatenate([jnp.concatenate([m[0], z], axis=1), jnp.concatenate([z, m[1]], axis=1)], axis=0)


_SMALL_PARAMS = ('rwkv_mu', 'rwkv_w2', 'rwkv_a2', 'rwkv_w0', 'rwkv_a0', 'rwkv_kk', 'rwkv_ka', 'rwkv_rk',
                 'rwkv_gn_g', 'rwkv_gn_b', 's5_a_re', 's5_a_im', 's5_log_dt', 's5_b_re', 's5_b_im', 's5_c_re',
                 's5_c_im', 's5_d', 's5_b_glu', 'ln1_g', 'ln1_b', 'router_bias', 'ln2_g', 'ln2_b', 'na_rpb')


def _small_params(g):
    p = {}
    p['mu'] = jnp.concatenate([g['rwkv_mu'], jnp.zeros((A_PAD - A_COLS,), F32)])[None, :]
    p['w2bd'] = _block_diag2(g['rwkv_w2']).astype(BF16)
    p['a2bd'] = _block_diag2(g['rwkv_a2']).astype(BF16)
    p['w0'] = g['rwkv_w0'].reshape(1, 2 * D_MIX)
    p['a0'] = g['rwkv_a0'].reshape(1, 2 * D_MIX)
    p['kkp'] = g['rwkv_kk'][None, :]
    p['ka'] = g['rwkv_ka'][None, :]
    p['rk'] = g['rwkv_rk'][None, :]
    p['gng'] = g['rwkv_gn_g'].reshape(HEADS // 2, 1, 2 * HD)
    p['gnb'] = g['rwkv_gn_b'].reshape(HEADS // 2, 1, 2 * HD)
    p['na_bias'] = na_bias_table(g['na_rpb'])
    a = lax.complex(g['s5_a_re'], g['s5_a_im'])
    dt = jnp.exp(g['s5_log_dt'])[..., None]
    a_bar = jnp.exp(dt * a)
    b_bar = ((a_bar - 1.0) / a)[..., None] * lax.complex(g['s5_b_re'], g['s5_b_im'])
    eye_g = jnp.eye(C_GROUPS // S5_GB, dtype=F32)
    gpb = C_GROUPS // S5_GB
    bd = lambda m: jnp.einsum('dkgph,gj->dkghjp', m.reshape(2, S5_GB, gpb, C_STATE, C_GROUP),
                              eye_g).reshape(2, S5_GB, D_MIX // S5_GB, S5_GW)
    p['s5_bmat'] = jnp.concatenate([bd(b_bar.real), bd(b_bar.imag)], axis=-1).astype(BF16)
    cd = lambda m: jnp.einsum('kghp,gj->kjpgh', m.reshape(S5_GB, gpb, C_GROUP, C_STATE),
                              eye_g).reshape(S5_GB, S5_GW, D_MIX // S5_GB)
    p['s5_cmat'] = jnp.concatenate([cd(g['s5_c_re']), -cd(g['s5_c_im'])], axis=1).astype(BF16)
    p['s5_ar'] = a_bar.real.reshape(2, 1, S5_N)
    p['s5_ai'] = a_bar.imag.reshape(2, 1, S5_N)
    p['s5_d'] = g['s5_d'][None, :]
    p['b_glu'] = g['s5_b_glu'][None, :]
    p['ln1_g'] = g['ln1_g'][None, :]
    p['ln1_b'] = g['ln1_b'][None, :]
    p['router_bias'] = g['router_bias'].reshape(N_GROUPS, N_EXPERTS // N_GROUPS, 1)
    p['ln2_g'] = g['ln2_g'][None, :]
    p['ln2_b'] = g['ln2_b'][None, :]
    return p


def _layer_weights(P, l):
    g = lambda name: P[name][l]
    w_in = g('w_in')
    pad = jnp.zeros((D_MODEL, A_PAD - A_COLS), F32)
    p = {}
    p['w_in'] = jnp.concatenate([w_in[:, :A_COLS], pad, w_in[:, A_COLS:]], axis=1).astype(BF16)
    p['g2'] = g('rwkv_g2').astype(BF16)
    hid = np.arange(D_MIX) // HD
    p['seg_ones'] = jnp.asarray((hid[:, None] == hid[None, :]).astype(np.float32), dtype=BF16)
    p['w_glu'] = g('s5_w_glu').astype(BF16)
    p['w_branch'] = g('w_branch').astype(BF16)
    p['w_out'] = g('w_out').astype(BF16)
    p['router_wt'] = g('router_w').T.astype(BF16)
    ecols = N_EXPERTS * D_EXPERT
    p['wg'] = g('exp_w_gate').astype(BF16)
    p['wu'] = g('exp_w_up').astype(BF16)
    p['wd'] = g('exp_w_down').reshape(ecols, D_MODEL).astype(BF16)
    p['sh_g'] = g('sh_w_gate').astype(BF16)
    p['sh_u'] = g('sh_w_up').astype(BF16)
    p['sh_d'] = g('sh_w_down').astype(BF16)
    return p


def kernel(x_prompt, x_sample, c, cache_na_k, cache_na_v, state_rwkv, state_s5_re, state_s5_im, c_ctx, w_ada, b_ada, w_in, rwkv_mu, rwkv_w0, rwkv_w2, rwkv_a0, rwkv_a2, rwkv_g2, rwkv_kk, rwkv_ka, rwkv_rk, rwkv_gn_g, rwkv_gn_b, na_rpb, s5_a_re, s5_a_im, s5_log_dt, s5_b_re, s5_b_im, s5_c_re, s5_c_im, s5_d, s5_w_glu, s5_b_glu, w_branch, w_out, ln1_g, ln1_b, router_w, router_bias, exp_w_gate, exp_w_up, exp_w_down, sh_w_gate, sh_w_up, sh_w_down, ln2_g, ln2_b):
    P = dict(w_in=w_in, rwkv_mu=rwkv_mu, rwkv_w0=rwkv_w0, rwkv_w2=rwkv_w2, rwkv_a0=rwkv_a0, rwkv_a2=rwkv_a2,
             rwkv_g2=rwkv_g2, rwkv_kk=rwkv_kk, rwkv_ka=rwkv_ka, rwkv_rk=rwkv_rk, rwkv_gn_g=rwkv_gn_g,
             rwkv_gn_b=rwkv_gn_b, s5_a_re=s5_a_re, s5_a_im=s5_a_im, s5_log_dt=s5_log_dt, s5_b_re=s5_b_re,
             s5_b_im=s5_b_im, s5_c_re=s5_c_re, s5_c_im=s5_c_im, s5_d=s5_d, s5_w_glu=s5_w_glu,
             s5_b_glu=s5_b_glu, w_branch=w_branch, w_out=w_out, ln1_g=ln1_g, ln1_b=ln1_b, router_w=router_w,
             router_bias=router_bias, exp_w_gate=exp_w_gate, exp_w_up=exp_w_up, exp_w_down=exp_w_down,
             sh_w_gate=sh_w_gate, sh_w_up=sh_w_up, sh_w_down=sh_w_down, ln2_g=ln2_g, ln2_b=ln2_b)
    bc, tc, _ = x_prompt.shape
    bl, tl, _ = x_sample.shape
    depth = w_in.shape[0]

    cond = jnp.concatenate([c_ctx[None, :], c, jnp.zeros((SUBLANE - 1 - bl, D_MODEL), F32)], axis=0)
    ada = ada_modulation(cond, w_ada, b_ada)
    masks = jnp.asarray(_chunk_masks())

    paths = {
        'ctx': dict(bsz=bc, t=tc, x=x_prompt.reshape(bc * tc, D_MODEL), mod_of_tile=lambda i, tm: 0),
        'lat': dict(bsz=bl, t=tl, x=x_sample.reshape(bl * tl, D_MODEL),
                    mod_of_tile=lambda i, tm: 1 + (i * tm) // tl),
    }
    for q in paths.values():
        tiles = q['t'] // TM
        pos = np.arange(q['bsz'] * tiles) % tiles
        q['seq_tiles'] = jnp.asarray(np.concatenate([np.full_like(pos, tiles), pos]).astype(np.int32))
        q['sched'] = rwkv_schedule([q['t']] * q['bsz'])
        q['s5_tb'] = min(S5_ROWS // S5_GRP, q['t'])
        assert q['bsz'] % S5_GRP == 0 and q['s5_tb'] % SUBLANE == 0 and q['t'] % q['s5_tb'] == 0

    P['na_rpb'] = na_rpb
    small = jax.vmap(_small_params)({k: P[k] for k in _SMALL_PARAMS})
    mods_all = jnp.concatenate([ada[:, :1 + bl].reshape(depth, 1 + bl, 6, D_MODEL),
                                jnp.zeros((depth, 1 + bl, SUBLANE - 6, D_MODEL), F32)], axis=2)
    new_k, new_v, new_rwkv, new_s5 = [], [], [], []
    for l in range(depth):
        p = _layer_weights(P, l)
        p.update({k: v[l] for k, v in small.items()})
        mods = mods_all[l]
        bias_tab = p['na_bias']
        for name, q in paths.items():
            bsz, t_len, x, mod_of_tile = q['bsz'], q['t'], q['x'], q['mod_of_tile']
            z = in_projection(x, mods, p['w_in'], mod_of_tile)

            r, v, kk, ld, kd, bd, g, bonus = rwkv_prep(z, q['seq_tiles'], p)
            if name == 'ctx':
                s0t = jnp.zeros((bsz, 2, HEADS // 2, 2 * HD, 2 * HD), F32)
            else:
                s0t = rwkv_pack_state(state_rwkv[:, l])
            gn_f, gn_b, s_fin = rwkv_scan((r, v, kk, ld, kd, bd), masks, p['gng'], p['gnb'], s0t, q['sched'])

            if name == 'ctx':
                yb, k_h, v_h = ctx_attention(z, bsz, t_len)
            else:
                yb = na_attention(z, cache_na_k, cache_na_v, l, bias_tab, bsz, t_len)

            if name == 'ctx':
                x0 = jnp.zeros((bsz // S5_GRP, 2, S5_GB * S5_GRP, 2 * S5_GW), F32)
            else:
                x0 = jnp.concatenate(
                    [s5_rows(jnp.swapaxes(s[:, l].reshape(bsz, 2, S5_N), 0, 1), S5_GRP)
                     for s in (state_s5_re, state_s5_im)], axis=-1)
            a_re, a_im = (s5_rows(jnp.broadcast_to(a, (2, S5_GRP, S5_N)), S5_GRP)[0]
                          for a in (p['s5_ar'], p['s5_ai']))
            yc_f, yc_b, x_fin = s5_scan(z.reshape(bsz, t_len, IN_COLS_P), p['s5_bmat'], p['s5_cmat'],
                                        a_re, a_im, x0, q['s5_tb'], S5_GRP)
            yc = (yc_f.reshape(bsz * t_len, D_MIX), yc_b.reshape(bsz * t_len, D_MIX))

            x1, h2 = merge_branches((gn_f, gn_b), bonus, g, yb, yc, z, x, mods, mod_of_tile, p)
            gates = moe_router(h2, p['router_wt'], p['router_bias'])
            q['x'] = moe_ffn(h2, gates, x1, mods, mod_of_tile, p)
            if name == 'ctx':
                new_k.append(k_h)
                new_v.append(v_h)
                new_rwkv.append(jnp.swapaxes(s_fin, -1, -2))
                new_s5.append([jnp.swapaxes(s5_unrows(part, S5_GRP), 0, 1)
                               for part in (x_fin[..., :S5_GW], x_fin[..., S5_GW:])])

    s5_re, s5_im = (jnp.stack([layer[part] for layer in new_s5], axis=1).reshape(bc, depth, 2, C_GROUPS, C_STATE)
                    for part in range(2))
    return (paths['ctx']['x'].reshape(bc, tc, D_MODEL), paths['lat']['x'].reshape(bl, tl, D_MODEL),
            jnp.stack(new_k, axis=1), jnp.stack(new_v, axis=1), jnp.stack(new_rwkv, axis=1), s5_re, s5_im)
```
